```python
import functools
import jax, jax.numpy as jnp
from jax import lax
import numpy as np

D_MODEL = 1024
BATCH = 4
SEQ = 4096
DEPTH = 1
DEC_BATCH = 32
DEC_SEQ = 4
PAST_LEN = 16384
PAGE_SIZE = 128

N_HEADS = 8
HEAD_DIM = 64
D_ATTN = N_HEADS * HEAD_DIM
Q_BLOCK = 128
FORGET_BIAS_MIN = 2.0
FORGET_BIAS_MAX = 12.0
D_GMLP = D_MODEL // 2
GMLP_GROUPS = 4
GMLP_GROUP_DIM = D_GMLP // GMLP_GROUPS
CHUNK = 128
N_GROUPS = 4
EXPERTS_PER_GROUP = 4
N_EXPERTS = N_GROUPS * EXPERTS_PER_GROUP
TOP_K = 2
D_EXPERT = D_MODEL // 4
ALPHA = (2.0 * DEPTH) ** 0.25
BETA = (8.0 * DEPTH) ** -0.25
LN_EPS = 1e-5
ADA_SCALE = 0.2
SPLIT_SIZES = (D_ATTN, D_ATTN, D_ATTN, N_HEADS, D_GMLP, D_GMLP, D_MODEL, D_MODEL)
SPLIT_POINTS = tuple(int(s) for s in np.cumsum(SPLIT_SIZES)[:-1])
D_IN = int(sum(SPLIT_SIZES))

kernel_name = 'fox_gmlp_hmoe_deepnorm_adaln_step'


def _layernorm(x, g, b):
    xf = x.astype(jnp.float32)
    mu = jnp.mean(xf, axis=-1, keepdims=True)
    var = jnp.mean(jnp.square(xf - mu), axis=-1, keepdims=True)
    return ((xf - mu) * lax.rsqrt(var + LN_EPS)).astype(x.dtype) * g + b


def _ada(c, w_ada, b_ada):
    m = jax.nn.silu(c) @ w_ada + b_ada
    return jnp.split(m[:, None, :], 6, axis=-1)


def _mixer_inputs(h, w_in, b_f, gln_g, gln_b):
    z = h @ w_in
    q, k, v, f, u, gv, ga, gg = jnp.split(z, SPLIT_POINTS, axis=-1)
    shp = h.shape[:-1] + (N_HEADS, HEAD_DIM)
    logf = jax.nn.log_sigmoid((f + b_f).astype(jnp.float32))
    u = jax.nn.gelu(u)
    gv = _layernorm(jax.nn.gelu(gv), gln_g, gln_b)
    return q.reshape(shp), k.reshape(shp), v.reshape(shp), logf, u, gv, ga, gg


def _fox_prompt(q, k, v, logf):
    b, s = q.shape[:2]
    nb = s // Q_BLOCK
    scale = HEAD_DIM ** -0.5
    cum = jnp.cumsum(logf, axis=1)
    cum_k = cum.transpose(0, 2, 1)[:, :, None, :]
    kpos = jnp.arange(s)

    def block(args):
        qb, cb, qpos = args
        sc = jnp.einsum('bqhd,bkhd->bhqk', qb, k).astype(jnp.float32) * scale
        sc = sc + cb.transpose(0, 2, 1)[..., None] - cum_k
        sc = jnp.where(kpos[None, :] <= qpos[:, None], sc, -jnp.inf)
        p = jax.nn.softmax(sc, axis=-1).astype(v.dtype)
        return jnp.einsum('bhqk,bkhd->bqhd', p, v)

    qb = q.reshape(b, nb, Q_BLOCK, N_HEADS, HEAD_DIM).transpose(1, 0, 2, 3, 4)
    cb = cum.reshape(b, nb, Q_BLOCK, N_HEADS).transpose(1, 0, 2, 3)
    qpos = kpos.reshape(nb, Q_BLOCK)
    o = lax.map(block, (qb, cb, qpos))
    return o.transpose(1, 0, 2, 3, 4).reshape(b, s, D_ATTN)


def _fox_sample(q, k, v, logf, k_past, v_past, logf_past):
    n = q.shape[1]
    p_len = k_past.shape[1]
    scale = HEAD_DIM ** -0.5
    lp = logf_past.astype(jnp.float32)
    suffix = lax.cumsum(lp, axis=1, reverse=True) - lp
    cn = jnp.cumsum(logf, axis=1)
    cn_t = cn.transpose(0, 2, 1)
    suf_t = suffix.transpose(0, 2, 1)
    s_past = jnp.einsum('bqhd,bkhd->bhqk', q, k_past).astype(jnp.float32) * scale
    s_past = s_past + cn_t[..., :, None] + suf_t[..., None, :]
    s_new = jnp.einsum('bqhd,bkhd->bhqk', q, k).astype(jnp.float32) * scale
    s_new = s_new + cn_t[..., :, None] - cn_t[..., None, :]
    causal = jnp.tril(jnp.ones((n, n), dtype=bool))
    s_new = jnp.where(causal, s_new, -jnp.inf)
    p = jax.nn.softmax(jnp.concatenate([s_past, s_new], axis=-1), axis=-1).astype(v.dtype)
    o = (jnp.einsum('bhqk,bkhd->bqhd', p[..., :p_len], v_past)
         + jnp.einsum('bhqk,bkhd->bqhd', p[..., p_len:], v))
    return o.reshape(q.shape[0], n, D_ATTN)


def _gmlp_prompt(u, gv, w_s, b_s):
    b, s, _ = gv.shape
    mask = jnp.tril(jnp.ones((CHUNK, CHUNK), dtype=bool))
    ws = jnp.where(mask, w_s, jnp.zeros_like(w_s))
    vg = gv.reshape(b, s // CHUNK, CHUNK, GMLP_GROUPS, GMLP_GROUP_DIM)
    sp = jnp.einsum('gts,bnsgc->bntgc', ws, vg) + b_s.T[:, :, None]
    return u * sp.reshape(b, s, D_GMLP)


def _gmlp_sample(u, gv, w_s, b_s):
    b, n, _ = gv.shape
    mask = jnp.tril(jnp.ones((n, n), dtype=bool))
    ws = w_s[:, :n, :n]
    ws = jnp.where(mask, ws, jnp.zeros_like(ws))
    vg = gv.reshape(b, n, GMLP_GROUPS, GMLP_GROUP_DIM)
    sp = jnp.einsum('gts,bsgc->btgc', ws, vg) + b_s[:, :n].T[:, :, None]
    return u * sp.reshape(b, n, D_GMLP)


def _moe(h, w_gr, b_gr, w_er, b_er, w_eg, w_eu, w_ed):
    shp = h.shape
    t = h.reshape(-1, D_MODEL)
    pg = jax.nn.softmax((t @ w_gr + b_gr).astype(jnp.float32), axis=-1)
    gi = jnp.argmax(pg, axis=-1)
    pg_top = jnp.take_along_axis(pg, gi[:, None], axis=-1)
    el = (t @ w_er + b_er).astype(jnp.float32).reshape(-1, N_GROUPS, EXPERTS_PER_GROUP)
    el = jnp.take_along_axis(el, gi[:, None, None], axis=1)[:, 0]
    tw, ti = lax.top_k(jax.nn.softmax(el, axis=-1), TOP_K)
    tw = tw / jnp.sum(tw, axis=-1, keepdims=True) * pg_top
    eid = gi[:, None] * EXPERTS_PER_GROUP + ti
    comb = jnp.sum(jax.nn.one_hot(eid, N_EXPERTS, dtype=jnp.float32) * tw[..., None], axis=1)
    comb = comb.astype(h.dtype)
    hg = jnp.einsum('nd,edf->nef', t, w_eg)
    hu = jnp.einsum('nd,edf->nef', t, w_eu)
    a = jax.nn.silu(hg) * hu * comb[..., None]
    out = jnp.einsum('nef,efd->nd', a, w_ed)
    return out.reshape(shp)


def _layer(x, c, fox_fn, gmlp_fn, w_ada, b_ada, w_in, b_f, gln_g, gln_b, w_ao, w_go, w_o,
           ln1_g, ln1_b, w_gr, b_gr, w_er, b_er, w_eg, w_eu, w_ed, ln2_g, ln2_b):
    sh1, sc1, g1, sh2, sc2, g2 = _ada(c, w_ada, b_ada)
    h = x * (1.0 + sc1) + sh1
    q, k, v, logf, u, gv, ga, gg = _mixer_inputs(h, w_in, b_f, gln_g, gln_b)
    a = fox_fn(q, k, v, logf)
    gm = gmlp_fn(u, gv)
    m = (jax.nn.sigmoid(ga) * (a @ w_ao) + jax.nn.sigmoid(gg) * (gm @ w_go)) @ w_o
    x = _layernorm(ALPHA * x + (1.0 + g1) * m, ln1_g, ln1_b)
    h2 = x * (1.0 + sc2) + sh2
    f = _moe(h2, w_gr, b_gr, w_er, b_er, w_eg, w_eu, w_ed)
    x = _layernorm(ALPHA * x + (1.0 + g2) * f, ln2_g, ln2_b)
    return x, k, v, logf, gv


def setup_inputs(seed: int = 0) -> dict:
    key = jax.random.key(seed)
    ks = iter(jax.random.split(key, 40))
    nrm = lambda shape: jax.random.normal(next(ks), shape, dtype=jnp.float32)
    n_pages = PAST_LEN // PAGE_SIZE
    n_pool = (5 * DEC_BATCH * n_pages) // 4
    page_table = jax.random.permutation(next(ks), n_pool)[:DEC_BATCH * n_pages]
    page_table = page_table.reshape(DEC_BATCH, n_pages).astype(jnp.int32)
    col_scale = jnp.concatenate([jnp.ones((2 * D_ATTN,), jnp.float32),
                                 jnp.full((D_ATTN,), BETA, jnp.float32),
                                 jnp.ones((D_IN - 3 * D_ATTN,), jnp.float32)])
    head_bias = jnp.linspace(FORGET_BIAS_MIN, FORGET_BIAS_MAX, N_HEADS, dtype=jnp.float32)
    return {
        'x_prompt': nrm((BATCH, SEQ, D_MODEL)),
        'x_sample': nrm((DEC_BATCH, DEC_SEQ, D_MODEL)),
        'c_prompt': nrm((BATCH, D_MODEL)),
        'c_sample': nrm((DEC_BATCH, D_MODEL)),
        'cache_k': nrm((DEPTH, n_pool, PAGE_SIZE, N_HEADS, HEAD_DIM)),
        'cache_v': nrm((DEPTH, n_pool, PAGE_SIZE, N_HEADS, HEAD_DIM)) * BETA,
        'cache_logf': jax.nn.log_sigmoid(head_bias + 0.5 * nrm((DEPTH, n_pool, PAGE_SIZE, N_HEADS))),
        'page_table': page_table,
        'w_ada': nrm((DEPTH, D_MODEL, 6 * D_MODEL)) * (ADA_SCALE * D_MODEL ** -0.5),
        'b_ada': nrm((DEPTH, 6 * D_MODEL)) * 0.01,
        'w_in': nrm((DEPTH, D_MODEL, D_IN)) * (D_MODEL ** -0.5) * col_scale,
        'b_f': head_bias[None, :] + 0.5 * nrm((DEPTH, N_HEADS)),
        'gmlp_ln_g': 1.0 + 0.05 * nrm((DEPTH, D_GMLP)),
        'gmlp_ln_b': 0.02 * nrm((DEPTH, D_GMLP)),
        'w_s': nrm((DEPTH, GMLP_GROUPS, CHUNK, CHUNK)) * (CHUNK ** -0.5),
        'b_s': 1.0 + 0.1 * nrm((DEPTH, GMLP_GROUPS, CHUNK)),
        'w_attn_out': nrm((DEPTH, D_ATTN, D_MODEL)) * (BETA * D_ATTN ** -0.5),
        'w_gmlp_out': nrm((DEPTH, D_GMLP, D_MODEL)) * (BETA * D_GMLP ** -0.5),
        'w_o': nrm((DEPTH, D_MODEL, D_MODEL)) * (BETA * D_MODEL ** -0.5),
        'ln1_g': 1.0 + 0.05 * nrm((DEPTH, D_MODEL)),
        'ln1_b': 0.02 * nrm((DEPTH, D_MODEL)),
        'w_group_router': nrm((DEPTH, D_MODEL, N_GROUPS)) * (D_MODEL ** -0.5),
        'b_group_router': 0.01 * nrm((DEPTH, N_GROUPS)),
        'w_expert_router': nrm((DEPTH, D_MODEL, N_EXPERTS)) * (D_MODEL ** -0.5),
        'b_expert_router': 0.01 * nrm((DEPTH, N_EXPERTS)),
        'w_exp_gate': nrm((DEPTH, N_EXPERTS, D_MODEL, D_EXPERT)) * (D_MODEL ** -0.5),
        'w_exp_up': nrm((DEPTH, N_EXPERTS, D_MODEL, D_EXPERT)) * (D_MODEL ** -0.5),
        'w_exp_down': nrm((DEPTH, N_EXPERTS, D_EXPERT, D_MODEL)) * (BETA * D_EXPERT ** -0.5),
        'ln2_g': 1.0 + 0.05 * nrm((DEPTH, D_MODEL)),
        'ln2_b': 0.02 * nrm((DEPTH, D_MODEL)),
    }


def reference(x_prompt, x_sample, c_prompt, c_sample, cache_k, cache_v, cache_logf, page_table,
              w_ada, b_ada, w_in, b_f, gmlp_ln_g, gmlp_ln_b, w_s, b_s, w_attn_out, w_gmlp_out,
              w_o, ln1_g, ln1_b, w_group_router, b_group_router, w_expert_router,
              b_expert_router, w_exp_gate, w_exp_up, w_exp_down, ln2_g, ln2_b):
    db = x_sample.shape[0]
    xp, xs = x_prompt, x_sample
    kp_l, vp_l, lfp_l, ks_l, vs_l, lfs_l, gvs_l = [], [], [], [], [], [], []
    for l in range(DEPTH):
        lw = (w_ada[l], b_ada[l], w_in[l], b_f[l], gmlp_ln_g[l], gmlp_ln_b[l], w_attn_out[l],
              w_gmlp_out[l], w_o[l], ln1_g[l], ln1_b[l], w_group_router[l], b_group_router[l],
              w_expert_router[l], b_expert_router[l], w_exp_gate[l], w_exp_up[l],
              w_exp_down[l], ln2_g[l], ln2_b[l])
        gp = functools.partial(_gmlp_prompt, w_s=w_s[l], b_s=b_s[l])
        xp, kp, vp, lfp, _ = _layer(xp, c_prompt, _fox_prompt, gp, *lw)
        k_past = cache_k[l][page_table].reshape(db, -1, N_HEADS, HEAD_DIM)
        v_past = cache_v[l][page_table].reshape(db, -1, N_HEADS, HEAD_DIM)
        lf_past = cache_logf[l][page_table].reshape(db, -1, N_HEADS)
        fs = functools.partial(_fox_sample, k_past=k_past, v_past=v_past, logf_past=lf_past)
        gs = functools.partial(_gmlp_sample, w_s=w_s[l], b_s=b_s[l])
        xs, ksn, vsn, lfs, gvs = _layer(xs, c_sample, fs, gs, *lw)
        kp_l.append(kp); vp_l.append(vp); lfp_l.append(lfp)
        ks_l.append(ksn); vs_l.append(vsn); lfs_l.append(lfs); gvs_l.append(gvs)
    return (xp, xs, jnp.stack(kp_l), jnp.stack(vp_l), jnp.stack(lfp_l),
            jnp.stack(ks_l), jnp.stack(vs_l), jnp.stack(lfs_l), jnp.stack(gvs_l))
```

```python
import functools

import numpy as np
import jax
import jax.numpy as jnp
from jax import lax
from jax.experimental import pallas as pl
from jax.experimental.pallas import tpu as pltpu

F32 = jnp.float32
BF16 = jnp.bfloat16
HIGHEST = lax.Precision.HIGHEST

N_HEADS = 8
HEAD_DIM = 64
D_ATTN = N_HEADS * HEAD_DIM
PAGE_SIZE = 128
CHUNK = 128
GMLP_GROUPS = 4
N_GROUPS = 4
EXPERTS_PER_GROUP = 4
N_EXPERTS = N_GROUPS * EXPERTS_PER_GROUP
LN_EPS = 1e-5
LANES = 128
NEG_INF = float("-inf")

VMEM_LIMIT_BYTES = 56 * 1024 * 1024


def _params(*sem):
    return pltpu.CompilerParams(dimension_semantics=sem, vmem_limit_bytes=VMEM_LIMIT_BYTES)


def _full(shape):
    n = len(shape)
    return pl.BlockSpec(shape, lambda *_: (0,) * n)


def _ln(x, g, b):
    mu = jnp.mean(x, axis=-1, keepdims=True)
    xc = x - mu
    var = jnp.mean(xc * xc, axis=-1, keepdims=True)
    return xc * lax.rsqrt(var + LN_EPS) * g + b


def _mod(ref):
    return ref[0] if len(ref.shape) == 3 else ref[...]


def _mod_spec(m, tm, rows_per_mod):
    if m.ndim == 3:
        return pl.BlockSpec((1, 1, m.shape[-1]), lambda i: (i * tm // rows_per_mod, 0, 0))
    return pl.BlockSpec((tm, m.shape[-1]), lambda i: (i, 0))


def _dot(a, b, precise):
    if precise:
        return jnp.dot(a, b, precision=HIGHEST, preferred_element_type=F32)
    return jnp.dot(a.astype(BF16), b.astype(BF16), preferred_element_type=F32)


def _ada_kernel(c_ref, w_ref, b_ref, o_ref):
    c = c_ref[...]
    o_ref[...] = _dot(c * jax.nn.sigmoid(c), w_ref[...], True) + b_ref[...]


def _ada(c, w, b, *, tn=1024):
    n, d = c.shape
    dout = w.shape[1]
    return pl.pallas_call(
        _ada_kernel,
        out_shape=jax.ShapeDtypeStruct((n, dout), F32),
        grid=(dout // tn,),
        in_specs=[_full((n, d)), pl.BlockSpec((d, tn), lambda j: (0, j)), pl.BlockSpec((1, tn), lambda j: (0, j))],
        out_specs=pl.BlockSpec((n, tn), lambda j: (0, j)),
        compiler_params=_params("parallel"),
        name="ada",
    )(c, w, b)


_OFF_Q, _OFF_K, _OFF_V, _OFF_U, _OFF_GV = 0, 512, 1024, 1536, 2048
_OFF_GA, _OFF_GG, _OFF_F, _W_COLS = 2560, 3584, 4608, 4736


def _mixer_epilogue(zq, zk, zv, zf, zu, zgv, zga, zgg, bf, glg, glb):
    q = zq * (HEAD_DIM ** -0.5)
    logf = jax.nn.log_sigmoid(zf + bf)
    u = jax.nn.gelu(zu)
    gv = _ln(jax.nn.gelu(zgv), glg, glb)
    return q, zk, zv, logf, u, gv, jax.nn.sigmoid(zga), jax.nn.sigmoid(zgg)


def _prompt_in_kernel(x_ref, sc_ref, sh_ref, w_ref, bf_ref, glg_ref, glb_ref, ws_ref, bs_ref, tri_ref,
                      q_ref, k32_ref, kb_ref, v32_ref, vb_ref, logf_ref, cum_ref, cumt_ref, gm_ref, sga_ref,
                      sgg_ref, carry_ref, sp_ref, *, tiles_per_seq):
    tm = x_ref.shape[0]
    h = (x_ref[...] * (1.0 + sc_ref[0]) + sh_ref[0]).astype(BF16)

    def proj(off, width):
        return jnp.dot(h, w_ref[:, off:off + width], preferred_element_type=F32)

    q, k, v, logf, u, gv, sga, sgg = _mixer_epilogue(
        proj(_OFF_Q, 512), proj(_OFF_K, 512), proj(_OFF_V, 512), proj(_OFF_F, LANES), proj(_OFF_U, 512),
        proj(_OFF_GV, 512), proj(_OFF_GA, 1024), proj(_OFF_GG, 1024), bf_ref[...], glg_ref[...], glb_ref[...])
    q_ref[...] = q.astype(BF16)
    k32_ref[...] = k
    kb_ref[...] = k.astype(BF16)
    v32_ref[...] = v
    vb_ref[...] = v.astype(BF16)
    sga_ref[...] = sga.astype(BF16)
    sgg_ref[...] = sgg.astype(BF16)
    logf_ref[...] = logf[:, :N_HEADS]

    @pl.when(pl.program_id(0) % tiles_per_seq == 0)
    def _():
        carry_ref[...] = jnp.zeros_like(carry_ref)

    carry = carry_ref[...]
    parts = []
    for n in range(tm // CHUNK):
        c = _dot(tri_ref[...], logf[n * CHUNK:(n + 1) * CHUNK], True) + carry
        parts.append(c)
        carry = c[CHUNK - 1:CHUNK]
    carry_ref[...] = carry
    cum = jnp.concatenate(parts, axis=0)
    cum_ref[...] = cum[:, :N_HEADS]
    cumt_ref[0] = cum.T[:N_HEADS]

    gvb = gv.astype(BF16)
    gw = gv.shape[1] // GMLP_GROUPS
    for n in range(tm // CHUNK):
        for g in range(GMLP_GROUPS):
            sp_ref[n * CHUNK:(n + 1) * CHUNK, g * gw:(g + 1) * gw] = jnp.dot(
                ws_ref[g], gvb[n * CHUNK:(n + 1) * CHUNK, g * gw:(g + 1) * gw], preferred_element_type=F32)
    bs = jnp.concatenate([bs_ref[...]] * (tm // CHUNK), axis=0)
    gm_ref[...] = (u * (sp_ref[...] + bs)).astype(BF16)


def _prompt_in(x, sc, sh, w, bf, glg, glb, ws, bs, tri, *, seq, tm=512):
    n, d = x.shape
    nb = n // seq
    tps = seq // tm
    row = lambda c: pl.BlockSpec((tm, c), lambda i: (i, 0))
    mod = pl.BlockSpec((1, 1, d), lambda i: (i // tps, 0, 0))
    outs = [((n, 512), BF16), ((n, 512), F32), ((n, 512), BF16), ((n, 512), F32), ((n, 512), BF16),
            ((n, N_HEADS), F32), ((n, N_HEADS), F32), ((nb, N_HEADS, seq), F32), ((n, 512), BF16),
            ((n, 1024), BF16), ((n, 1024), BF16)]
    out_specs = [row(512)] * 5 + [row(N_HEADS)] * 2 + [
        pl.BlockSpec((1, N_HEADS, tm), lambda i: (i // tps, 0, i % tps)), row(512), row(1024), row(1024)]
    return pl.pallas_call(
        functools.partial(_prompt_in_kernel, tiles_per_seq=tps),
        out_shape=[jax.ShapeDtypeStruct(s, t) for s, t in outs],
        grid=(n // tm,),
        in_specs=[row(d), mod, mod, _full(w.shape), _full(bf.shape), _full(glg.shape), _full(glb.shape),
                  _full(ws.shape), _full(bs.shape), _full(tri.shape)],
        out_specs=out_specs,
        scratch_shapes=[pltpu.VMEM((1, LANES), F32), pltpu.VMEM((tm, 512), F32)],
        compiler_params=_params("arbitrary"),
        name="prompt_in",
    )(x, sc, sh, w, bf, glg, glb, ws, bs, tri)


def _fox_prompt_kernel(q_ref, k_ref, v_ref, cum_ref, cumt_ref, o_ref, m_ref, acc_ref, *, tq):
    qi = pl.program_id(1)
    lane = lax.broadcasted_iota(jnp.int32, (tq, LANES), 1)
    lo = lane < HEAD_DIM
    own = [jnp.where(lo, 1.0, 0.0).astype(BF16), jnp.where(lo, 0.0, 1.0).astype(BF16)]
    ones_col = [jnp.where(lane == HEAD_DIM, 1.0, 0.0).astype(BF16), jnp.where(lane == 0, 1.0, 0.0).astype(BF16)]
    m_ref[...] = jnp.full_like(m_ref, NEG_INF)
    acc_ref[...] = jnp.zeros_like(acc_ref)
    cb = cum_ref[...]
    qh_all = [q_ref[:, (h // 2) * LANES:(h // 2 + 1) * LANES] * own[h % 2] for h in range(N_HEADS)]

    def step(j, masked):
        ks = pl.multiple_of(j * tq, tq)
        kt = k_ref[pl.ds(ks, tq), :]
        vt = v_ref[pl.ds(ks, tq), :]
        if masked:
            rows = lax.broadcasted_iota(jnp.int32, (tq, tq), 0)
            cols = lax.broadcasted_iota(jnp.int32, (tq, tq), 1)
            keep = cols <= rows
        for h in range(N_HEADS):
            p2, odd = divmod(h, 2)
            sl = slice(p2 * LANES, (p2 + 1) * LANES)
            s = lax.dot_general(qh_all[h], kt[:, sl], (((1,), (1,)), ((), ())), preferred_element_type=F32)
            s = s + cb[:, h:h + 1] - cumt_ref[h:h + 1, pl.ds(ks, tq)]
            if masked:
                s = jnp.where(keep, s, NEG_INF)
            m_old = m_ref[h]
            m_new = jnp.maximum(m_old, jnp.max(s, axis=-1, keepdims=True))
            p = jnp.exp(s - m_new).astype(BF16)
            vh = vt[:, sl] * own[odd] + ones_col[odd]
            acc_ref[h] = acc_ref[h] * jnp.exp(m_old - m_new) + jnp.dot(p, vh, preferred_element_type=F32)
            m_ref[h] = m_new

    lax.fori_loop(0, qi, lambda j, c: (step(j, False), c)[1], 0)
    step(qi, True)

    outs = []
    for p2 in range(N_HEADS // 2):
        ae, ao = acc_ref[2 * p2], acc_ref[2 * p2 + 1]
        outs.append(jnp.where(lo, ae / ae[:, HEAD_DIM:HEAD_DIM + 1], ao / ao[:, 0:1]))
    o_ref[...] = jnp.concatenate(outs, axis=-1).astype(o_ref.dtype)


def _fox_prompt(q, k, v, cum, cumt, *, seq, tq=512):
    n = q.shape[0]
    nb, nq = n // seq, seq // tq
    return pl.pallas_call(
        functools.partial(_fox_prompt_kernel, tq=tq),
        out_shape=jax.ShapeDtypeStruct((n, D_ATTN), BF16),
        grid=(nb, nq),
        in_specs=[pl.BlockSpec((tq, D_ATTN), lambda b, i: (b * nq + i, 0)),
                  pl.BlockSpec((seq, D_ATTN), lambda b, i: (b, 0)),
                  pl.BlockSpec((seq, D_ATTN), lambda b, i: (b, 0)),
                  pl.BlockSpec((tq, N_HEADS), lambda b, i: (b * nq + i, 0)),
                  pl.BlockSpec((None, N_HEADS, seq), lambda b, i: (b, 0, 0))],
        out_specs=pl.BlockSpec((tq, D_ATTN), lambda b, i: (b * nq + i, 0)),
        scratch_shapes=[pltpu.VMEM((N_HEADS, tq, 1), F32), pltpu.VMEM((N_HEADS, tq, LANES), F32)],
        compiler_params=_params("parallel", "arbitrary"),
        name="fox_prompt",
    )(q, k, v, cum, cumt)


def _route(logits):
    lane = lax.broadcasted_iota(jnp.int32, logits.shape, 1)
    big = jnp.int32(LANES)
    is_g = (lane >= N_EXPERTS) & (lane < N_EXPERTS + N_GROUPS)
    gl = jnp.where(is_g, logits, NEG_INF)
    gmax = jnp.max(gl, axis=-1, keepdims=True)
    gi = jnp.min(jnp.where(gl == gmax, lane, big), axis=-1, keepdims=True) - N_EXPERTS
    pg_top = 1.0 / jnp.sum(jnp.exp(gl - gmax), axis=-1, keepdims=True)
    in_g = (lane >= gi * EXPERTS_PER_GROUP) & (lane < (gi + 1) * EXPERTS_PER_GROUP)
    el = jnp.where(in_g, logits, NEG_INF)
    m1 = jnp.max(el, axis=-1, keepdims=True)
    i1 = jnp.min(jnp.where(el == m1, lane, big), axis=-1, keepdims=True)
    el2 = jnp.where(lane == i1, NEG_INF, el)
    m2 = jnp.max(el2, axis=-1, keepdims=True)
    i2 = jnp.min(jnp.where(el2 == m2, lane, big), axis=-1, keepdims=True)
    e2 = jnp.exp(m2 - m1)
    w1 = pg_top / (1.0 + e2)
    return jnp.where(lane == i1, w1, jnp.where(lane == i2, w1 * e2, 0.0))


def _merge_kernel(a_ref, gm_ref, sga_ref, sgg_ref, x_ref, g1_ref, sc2_ref, sh2_ref, wao_ref, wgo_ref, wo_ref,
                  l1g_ref, l1b_ref, wr_ref, br_ref, x1_ref, h2_ref, comb_ref, *, alpha, precise):
    t = (sga_ref[...].astype(F32) * _dot(a_ref[...], wao_ref[...], precise)
         + sgg_ref[...].astype(F32) * _dot(gm_ref[...], wgo_ref[...], precise))
    m = _dot(t, wo_ref[...], precise)
    x1 = _ln(alpha * x_ref[...] + (1.0 + _mod(g1_ref)) * m, l1g_ref[...], l1b_ref[...])
    h2 = x1 * (1.0 + _mod(sc2_ref)) + _mod(sh2_ref)
    x1_ref[...] = x1
    h2_ref[...] = h2.astype(h2_ref.dtype)
    comb_ref[...] = _route(_dot(h2, wr_ref[...], precise) + br_ref[...])


def _merge(a, gm, sga, sgg, x, g1, sc2, sh2, wao, wgo, wo, l1g, l1b, wr, br, *, rows_per_mod, tm, alpha, precise):
    n, d = x.shape
    row = lambda c: pl.BlockSpec((tm, c), lambda i: (i, 0))
    mod = _mod_spec(g1, tm, rows_per_mod)
    return pl.pallas_call(
        functools.partial(_merge_kernel, alpha=alpha, precise=precise),
        out_shape=[jax.ShapeDtypeStruct((n, d), F32), jax.ShapeDtypeStruct((n, d), BF16),
                   jax.ShapeDtypeStruct((n, LANES), F32)],
        grid=(n // tm,),
        in_specs=[row(D_ATTN), row(gm.shape[1]), row(d), row(d), row(d), mod, mod, mod, _full(wao.shape),
                  _full(wgo.shape), _full(wo.shape), _full(l1g.shape), _full(l1b.shape), _full(wr.shape),
                  _full(br.shape)],
        out_specs=[row(d), row(d), row(LANES)],
        compiler_params=_params("parallel"),
        name="merge_precise" if precise else "merge",
    )(a, gm, sga, sgg, x, g1, sc2, sh2, wao, wgo, wo, l1g, l1b, wr, br)


def _moe_kernel(h2_ref, comb_ref, x1_ref, g2_ref, wg_ref, wu_ref, wd_ref, l2g_ref, l2b_ref, o_ref, *, alpha):
    h2 = h2_ref[...]
    comb = comb_ref[...]
    acc = jnp.zeros(o_ref.shape, F32)
    for e in range(N_EXPERTS):
        hg = jnp.dot(h2, wg_ref[e], preferred_element_type=F32)
        hu = jnp.dot(h2, wu_ref[e], preferred_element_type=F32)
        a = hg * jax.nn.sigmoid(hg) * hu * comb[:, e:e + 1]
        acc = acc + jnp.dot(a.astype(BF16), wd_ref[e], preferred_element_type=F32)
    o_ref[...] = _ln(alpha * x1_ref[...] + (1.0 + _mod(g2_ref)) * acc, l2g_ref[...], l2b_ref[...])


def _moe(h2, comb, x1, g2, wg, wu, wd, l2g, l2b, *, rows_per_mod, tm, alpha):
    n, d = x1.shape
    row = lambda c: pl.BlockSpec((tm, c), lambda i: (i, 0))
    mod = _mod_spec(g2, tm, rows_per_mod)
    once =lambda s: pl.BlockSpec(s, lambda i: (0,) * len(s), pipeline_mode=pl.Buffered(1))
    return pl.pallas_call(
        functools.partial(_moe_kernel, alpha=alpha),
        out_shape=jax.ShapeDtypeStruct((n, d), F32),
        grid=(n // tm,),
        in_specs=[row(d), row(LANES), row(d), mod, once(wg.shape), once(wu.shape), once(wd.shape),
                  _full(l2g.shape), _full(l2b.shape)],
        out_specs=row(d),
        compiler_params=_params("parallel"),
        name="moe",
    )(h2, comb, x1, g2, wg, wu, wd, l2g, l2b)


def _sample_proj_kernel(x_ref, sc_ref, sh_ref, w_ref, z_ref):
    z_ref[...] = _dot(x_ref[...] * (1.0 + sc_ref[...]) + sh_ref[...], w_ref[...], True)


def _sample_proj(x, sc, sh, w, *, tn):
    n, d = x.shape
    return pl.pallas_call(
        _sample_proj_kernel,
        out_shape=jax.ShapeDtypeStruct((n, w.shape[1]), F32),
        grid=(w.shape[1] // tn,),
        in_specs=[_full((n, d))] * 3 + [pl.BlockSpec((d, tn), lambda j: (0, j))],
        out_specs=pl.BlockSpec((n, tn), lambda j: (0, j)),
        compiler_params=_params("parallel"),
        name="sample_proj",
    )(x, sc, sh, w)


def _sample_mix_kernel(z_ref, bf_ref, glg_ref, glb_ref, mg_ref, bs_ref, mc_ref,
                       q_ref, k_ref, v_ref, logf_ref, cn_ref, gv_ref, gm_ref, sga_ref, sgg_ref):
    z = z_ref[...]
    sec = lambda off, width: z[:, off:off + width]
    q, k, v, logf, u, gv, sga, sgg = _mixer_epilogue(
        sec(_OFF_Q, 512), sec(_OFF_K, 512), sec(_OFF_V, 512), sec(_OFF_F, LANES), sec(_OFF_U, 512),
        sec(_OFF_GV, 512), sec(_OFF_GA, 1024), sec(_OFF_GG, 1024), bf_ref[...], glg_ref[...], glb_ref[...])
    q_ref[...] = q
    k_ref[...] = k
    v_ref[...] = v
    logf_ref[...] = logf
    cn_ref[...] = _dot(mc_ref[...], logf, True)
    gv_ref[...] = gv
    sga_ref[...] = sga
    sgg_ref[...] = sgg
    gw = gv.shape[1] // GMLP_GROUPS
    sp = jnp.concatenate([_dot(mg_ref[g], gv[:, g * gw:(g + 1) * gw], True) for g in range(GMLP_GROUPS)], axis=-1)
    gm_ref[...] = u * (sp + bs_ref[...])


def _sample_mix(z, bf, glg, glb, mg, bs, mc):
    n = z.shape[0]
    shapes = [(n, 512)] * 3 + [(n, LANES)] * 2 + [(n, 512)] * 2 + [(n, 1024)] * 2
    return pl.pallas_call(
        _sample_mix_kernel,
        out_shape=[jax.ShapeDtypeStruct(s, F32) for s in shapes],
        compiler_params=pltpu.CompilerParams(vmem_limit_bytes=VMEM_LIMIT_BYTES),
        name="sample_mix",
    )(z, bf, glg, glb, mg, bs, mc)


PAGES_PER_STEP = 16


def _suffix_kernel(pt_ref, *refs):
    del pt_ref
    pages = refs[:PAGES_PER_STEP]
    ux_ref, later_ref, o_ref, carry_ref = refs[PAGES_PER_STEP:]

    @pl.when(pl.program_id(1) == 0)
    def _():
        carry_ref[...] = jnp.zeros_like(carry_ref)

    x = jnp.concatenate([p[0] for p in pages], axis=0)
    within = _dot(x, ux_ref[...], True)
    tot = jnp.broadcast_to(jnp.sum(x, axis=-1, keepdims=True), x.shape)
    later = _dot(later_ref[...], tot, True)
    carry = carry_ref[...]
    o_ref[0] = (within + later + jnp.concatenate([carry] * PAGES_PER_STEP, axis=0)).reshape(o_ref.shape[1:])
    carry_ref[...] = carry + later[:N_HEADS] + tot[:N_HEADS]


def _suffix(page_table, logf_pool, ux, later_m):
    nb, n_pages = page_table.shape
    P = PAGES_PER_STEP
    n_chunks = n_pages // P
    rev = lambda c: n_chunks - 1 - c

    def page_spec(r):
        return pl.BlockSpec((1, N_HEADS, PAGE_SIZE), lambda b, c, pt: (pt[b, rev(c) * P + r], 0, 0))

    grid_spec = pltpu.PrefetchScalarGridSpec(
        num_scalar_prefetch=1,
        grid=(nb, n_chunks),
        in_specs=[page_spec(r) for r in range(P)] + [
            pl.BlockSpec(ux.shape, lambda b, c, pt: (0, 0)), pl.BlockSpec(later_m.shape, lambda b, c, pt: (0, 0))],
        out_specs=pl.BlockSpec((1, P, N_HEADS, PAGE_SIZE), lambda b, c, pt: (b, rev(c), 0, 0)),
        scratch_shapes=[pltpu.VMEM((N_HEADS, PAGE_SIZE), F32)],
    )
    return pl.pallas_call(
        _suffix_kernel,
        out_shape=jax.ShapeDtypeStruct((nb, n_pages, N_HEADS, PAGE_SIZE), F32),
        grid_spec=grid_spec,
        compiler_params=_params("parallel", "arbitrary"),
        name="suffix",
    )(page_table, *([logf_pool] * P), ux, later_m)


def _fox_sample_kernel(pt_ref, *refs, n_chunks, n_new):
    del pt_ref
    P = PAGES_PER_STEP
    kp, vp = refs[:P], refs[P:2 * P]
    (qbd_ref, suf_ref, cn_ref, kn_ref, vn_ref, cnb_ref, o_ref, m_ref, l_ref, acc_ref) = refs[2 * P:]
    c = pl.program_id(1)
    nrow = n_new * N_HEADS

    @pl.when(c == 0)
    def _():
        m_ref[...] = jnp.full_like(m_ref, NEG_INF)
        l_ref[...] = jnp.zeros_like(l_ref)
        acc_ref[...] = jnp.zeros_like(acc_ref)

    qbd = qbd_ref[0]
    cn = cn_ref[0]

    def update(s, vs):
        m_old = m_ref[...]
        m_new = m_old
        for sb in s:
            m_new = jnp.maximum(m_new, jnp.max(sb, axis=-1, keepdims=True))
        alpha = jnp.exp(m_old - m_new)
        l_new = l_ref[...] * alpha
        acc = acc_ref[...] * alpha
        for sb, vb in zip(s, vs):
            p = jnp.exp(sb - m_new)
            l_new = l_new + jnp.sum(p, axis=-1, keepdims=True)
            acc = acc + jnp.dot(p, vb, preferred_element_type=F32)
        m_ref[...] = m_new
        l_ref[...] = l_new
        acc_ref[...] = acc

    scores = []
    for r in range(P):
        st = jnp.dot(kp[r][0], qbd, preferred_element_type=F32).T[:nrow]
        bias = jnp.concatenate([suf_ref[0, r]] * n_new, axis=0)
        scores.append(st + bias + cn)
    update(scores, [vp[r][0] for r in range(P)])

    @pl.when(c == n_chunks - 1)
    def _():
        st = jnp.dot(kn_ref[0], qbd, preferred_element_type=F32).T[:nrow] + cn - cnb_ref[0]
        col = lax.broadcasted_iota(jnp.int32, st.shape, 1)
        row = lax.broadcasted_iota(jnp.int32, st.shape, 0)
        update([jnp.where(col <= row // N_HEADS, st, NEG_INF)], [vn_ref[0]])
        full = acc_ref[...] / l_ref[...]
        lane_head = lax.broadcasted_iota(jnp.int32, full.shape, 1) // HEAD_DIM
        row_head = lax.broadcasted_iota(jnp.int32, full.shape, 0) % N_HEADS
        own = jnp.where(lane_head == row_head, full, 0.0)
        rows = [jnp.sum(own[q * N_HEADS:(q + 1) * N_HEADS], axis=0, keepdims=True) for q in range(n_new)]
        o_ref[0] = jnp.concatenate(rows + [jnp.zeros((o_ref.shape[1] - n_new, D_ATTN), F32)], axis=0)


def _fox_sample(page_table, cache_k, cache_v, qbd, suf, cn, kn, vn, cnb, *, n_new):
    nb, n_pages = page_table.shape
    P = PAGES_PER_STEP
    n_chunks = n_pages // P
    nrow = n_new * N_HEADS

    def page_spec(r):
        return pl.BlockSpec((1, PAGE_SIZE, D_ATTN), lambda b, c, pt: (pt[b, c * P + r], 0, 0))

    per_b = lambda s: pl.BlockSpec((1,) + s, lambda b, c, pt: (b,) + (0,) * len(s))
    grid_spec = pltpu.PrefetchScalarGridSpec(
        num_scalar_prefetch=1,
        grid=(nb, n_chunks),
        in_specs=[page_spec(r) for r in range(P)] + [page_spec(r) for r in range(P)] + [
            per_b((D_ATTN, LANES)),
            pl.BlockSpec((1, P, N_HEADS, PAGE_SIZE), lambda b, c, pt: (b, c, 0, 0)),
            per_b((nrow, 1)), per_b((LANES, D_ATTN)), per_b((LANES, D_ATTN)), per_b((nrow, LANES))],
        out_specs=per_b((8, D_ATTN)),
        scratch_shapes=[pltpu.VMEM((nrow, 1), F32), pltpu.VMEM((nrow, 1), F32), pltpu.VMEM((nrow, D_ATTN), F32)],
    )
    return pl.pallas_call(
        functools.partial(_fox_sample_kernel, n_chunks=n_chunks, n_new=n_new),
        out_shape=jax.ShapeDtypeStruct((nb, 8, D_ATTN), F32),
        grid_spec=grid_spec,
        compiler_params=_params("parallel", "arbitrary"),
        name="fox_sample",
    )(page_table, *([cache_k] * P), *([cache_v] * P), qbd, suf, cn, kn, vn, cnb)


def _reorder_w_in(w_in):
    d = w_in.shape[0]
    f0 = 3 * D_ATTN
    pad = jnp.zeros((d, LANES - N_HEADS), w_in.dtype)
    return jnp.concatenate([w_in[:, :f0], w_in[:, f0 + N_HEADS:], w_in[:, f0:f0 + N_HEADS], pad], axis=1)


def kernel(x_prompt, x_sample, c_prompt, c_sample, cache_k, cache_v, cache_logf, page_table, w_ada, b_ada, w_in,
           b_f, gmlp_ln_g, gmlp_ln_b, w_s, b_s, w_attn_out, w_gmlp_out, w_o, ln1_g, ln1_b, w_group_router,
           b_group_router, w_expert_router, b_expert_router, w_exp_gate, w_exp_up, w_exp_down, ln2_g, ln2_b):
    depth = w_ada.shape[0]
    assert depth == 1
    nbp, seq, d = x_prompt.shape
    nbs, n_new, _ = x_sample.shape
    alpha = (2.0 * depth) ** 0.25
    n_pool = cache_k.shape[1]
    d_gmlp = gmlp_ln_g.shape[1]
    gw = d_gmlp // GMLP_GROUPS

    w_in_r = _reorder_w_in(w_in[0])
    w_in_b = w_in_r.astype(BF16)
    bf = jnp.pad(b_f[0], (0, LANES - N_HEADS))[None]
    glg, glb = gmlp_ln_g[0][None], gmlp_ln_b[0][None]
    tril = jnp.tril(jnp.ones((CHUNK, CHUNK), F32))
    ws = jnp.where(tril > 0, w_s[0], 0.0)
    bs_tile = jnp.repeat(b_s[0].T, gw, axis=1)
    w_r = jnp.pad(jnp.concatenate([w_expert_router[0], w_group_router[0]], axis=1),
                  ((0, 0), (0, LANES - N_EXPERTS - N_GROUPS)))
    b_r = jnp.pad(jnp.concatenate([b_expert_router[0], b_group_router[0]]), (0, LANES - N_EXPERTS - N_GROUPS))[None]
    l1g, l1b, l2g, l2b = ln1_g[0][None], ln1_b[0][None], ln2_g[0][None], ln2_b[0][None]
    wg_b, wu_b, wd_b = w_exp_gate[0].astype(BF16), w_exp_up[0].astype(BF16), w_exp_down[0].astype(BF16)

    nc = nbp + nbs
    c_all = jnp.pad(jnp.concatenate([c_prompt, c_sample], axis=0), ((0, -nc % 8), (0, 0)))
    mod = _ada(c_all, w_ada[0], b_ada[0][None])
    modp = [m[:, None, :] for m in jnp.split(mod[:nbp], 6, axis=-1)]
    mods = [m[:, None, :] for m in jnp.split(mod[nbp:nc], 6, axis=-1)]

    xp = x_prompt.reshape(nbp * seq, d)
    q, k32, kb, v32, vb, logf_p, cum, cumt, gm, sga, sgg = _prompt_in(
        xp, modp[1], modp[0], w_in_b, bf, glg, glb, ws.astype(BF16), bs_tile, tril, seq=seq)
    a = _fox_prompt(q, kb, vb, cum, cumt, seq=seq)
    x1, h2, comb = _merge(a, gm, sga, sgg, xp, modp[2], modp[4], modp[3], w_attn_out[0].astype(BF16),
                          w_gmlp_out[0].astype(BF16), w_o[0].astype(BF16), l1g, l1b, w_r.astype(BF16), b_r,
                          rows_per_mod=seq, tm=512, alpha=alpha, precise=False)
    yp = _moe(h2, comb, x1, modp[5], wg_b, wu_b, wd_b, l2g, l2b, rows_per_mod=seq, tm=256, alpha=alpha)

    ns = nbs * n_new
    xs = x_sample.reshape(ns, d)
    rep = lambda m: jnp.repeat(m[:, 0, :], n_new, axis=0)
    w_in_s = jnp.pad(w_in_r, ((0, 0), (0, -w_in_r.shape[1] % 896)))
    z = _sample_proj(xs, rep(mods[1]), rep(mods[0]), w_in_s, tn=896)
    eye_b = jnp.eye(nbs, dtype=F32)
    mg = jnp.stack([jnp.kron(eye_b, ws[g, :n_new, :n_new]) for g in range(GMLP_GROUPS)])
    bs_rows = jnp.tile(bs_tile[:n_new], (nbs, 1))
    mc = jnp.kron(eye_b, tril[:n_new, :n_new])
    qs, ks, vs, logf_s, cn, gv_s, gm_s, sga_s, sgg_s = _sample_mix(z, bf, glg, glb, mg, bs_rows, mc)

    logf_pool = cache_logf[0].transpose(0, 2, 1)
    ux = jnp.triu(jnp.ones((PAGE_SIZE, PAGE_SIZE), F32), 1).T
    later_m = jnp.kron(jnp.triu(jnp.ones((PAGES_PER_STEP, PAGES_PER_STEP), F32), 1), jnp.eye(N_HEADS, dtype=F32))
    suf = _suffix(page_table, logf_pool, ux, later_m)

    nrow = n_new * N_HEADS
    q4 = qs.reshape(nbs, n_new, N_HEADS, HEAD_DIM)
    qbd = jnp.einsum("bqhd,hg->bhdqg", q4, jnp.eye(N_HEADS, dtype=F32)).reshape(nbs, D_ATTN, nrow)
    qbd = jnp.pad(qbd, ((0, 0), (0, 0), (0, LANES - nrow)))
    cn3 = cn[:, :N_HEADS].reshape(nbs, n_new, N_HEADS)
    cn_col = cn3.reshape(nbs, nrow, 1)
    cnb = jnp.pad(jnp.tile(cn3.transpose(0, 2, 1), (1, n_new, 1)), ((0, 0), (0, 0), (0, LANES - n_new)))
    pad_new = lambda t: jnp.pad(t.reshape(nbs, n_new, D_ATTN), ((0, 0), (0, LANES - n_new), (0, 0)))
    a_s = _fox_sample(page_table, cache_k[0].reshape(n_pool, PAGE_SIZE, D_ATTN),
                      cache_v[0].reshape(n_pool, PAGE_SIZE, D_ATTN), qbd, suf, cn_col, pad_new(ks), pad_new(vs),
                      cnb, n_new=n_new)
    a_s = a_s[:, :n_new].reshape(ns, D_ATTN)
    x1s, h2s, comb_s = _merge(a_s, gm_s, sga_s, sgg_s, xs, rep(mods[2]), rep(mods[4]), rep(mods[3]),
                              w_attn_out[0], w_gmlp_out[0], w_o[0], l1g, l1b, w_r, b_r,
                              rows_per_mod=ns, tm=ns, alpha=alpha, precise=True)
    ys = _moe(h2s, comb_s, x1s, rep(mods[5]), wg_b, wu_b, wd_b, l2g, l2b, rows_per_mod=ns, tm=ns, alpha=alpha)

    hs = (N_HEADS, HEAD_DIM)
    return (yp.reshape(nbp, seq, d), ys.reshape(nbs, n_new, d),
            k32.reshape(1, nbp, seq, *hs), v32.reshape(1, nbp, seq, *hs), logf_p.reshape(1, nbp, seq, N_HEADS),
            ks.reshape(1, nbs, n_new, *hs), vs.reshape(1, nbs, n_new, *hs),
            logf_s[:, :N_HEADS].reshape(1, nbs, n_new, N_HEADS), gv_s.reshape(1, nbs, n_new, d_gmlp))
```

```python
import functools

import numpy as np
import jax
import jax.numpy as jnp
from jax import lax
from jax.experimental import pallas as pl
from jax.experimental.pallas import tpu as pltpu

F32 = jnp.float32
BF16 = jnp.bfloat16
HIGHEST = lax.Precision.HIGHEST

N_HEADS = 8
HEAD_DIM = 64
D_ATTN = N_HEADS * HEAD_DIM
PAGE_SIZE = 128
CHUNK = 128
GMLP_GROUPS = 4
N_GROUPS = 4
EXPERTS_PER_GROUP = 4
N_EXPERTS = N_GROUPS * EXPERTS_PER_GROUP
LN_EPS = 1e-5
LANES = 128
NEG_INF = float("-inf")
LOG2E = 1.4426950408889634

VMEM_LIMIT_BYTES = 56 * 1024 * 1024


def _params(*sem):
    return pltpu.CompilerParams(dimension_semantics=sem, vmem_limit_bytes=VMEM_LIMIT_BYTES)


def _full(shape):
    n = len(shape)
    return pl.BlockSpec(shape, lambda *_: (0,) * n)


def _ln(x, g, b):
    mu = jnp.mean(x, axis=-1, keepdims=True)
    xc = x - mu
    var = jnp.mean(xc * xc, axis=-1, keepdims=True)
    return xc * lax.rsqrt(var + LN_EPS) * g + b


def _mod(ref):
    return ref[0] if len(ref.shape) == 3 else ref[...]


def _mod_spec(m, tm, rows_per_mod):
    if m.ndim == 3:
        return pl.BlockSpec((1, 1, m.shape[-1]), lambda i: (i * tm // rows_per_mod, 0, 0))
    return pl.BlockSpec((tm, m.shape[-1]), lambda i: (i, 0))


def _dot(a, b, precise):
    if precise:
        return jnp.dot(a, b, precision=HIGHEST, preferred_element_type=F32)
    return jnp.dot(a.astype(BF16), b.astype(BF16), preferred_element_type=F32)


def _ada_kernel(c_ref, w_ref, b_ref, o_ref):
    c = c_ref[...]
    o_ref[...] = _dot(c * jax.nn.sigmoid(c), w_ref[...], True) + b_ref[...]


def _ada(c, w, b, *, tn=1024):
    n, d = c.shape
    dout = w.shape[1]
    return pl.pallas_call(
        _ada_kernel,
        out_shape=jax.ShapeDtypeStruct((n, dout), F32),
        grid=(dout // tn,),
        in_specs=[_full((n, d)), pl.BlockSpec((d, tn), lambda j: (0, j)), pl.BlockSpec((1, tn), lambda j: (0, j))],
        out_specs=pl.BlockSpec((n, tn), lambda j: (0, j)),
        compiler_params=_params("parallel"),
        name="ada",
    )(c, w, b)


_OFF_Q, _OFF_K, _OFF_V, _OFF_U, _OFF_GV = 0, 512, 1024, 1536, 2048
_OFF_GA, _OFF_GG, _OFF_F, _W_COLS = 2560, 3584, 4608, 4736


def _mixer_epilogue(zq, zk, zv, zf, zu, zgv, zga, zgg, bf, glg, glb):
    q = zq * (HEAD_DIM ** -0.5)
    logf = jax.nn.log_sigmoid(zf + bf)
    u = jax.nn.gelu(zu)
    gv = _ln(jax.nn.gelu(zgv), glg, glb)
    return q, zk, zv, logf, u, gv, jax.nn.sigmoid(zga), jax.nn.sigmoid(zgg)


def _prompt_in_kernel(x_ref, sc_ref, sh_ref, w_ref, bf_ref, glg_ref, glb_ref, ws_ref, bs_ref, tri_ref,
                      q_ref, kt32_ref, ktb_ref, vt32_ref, vb_ref, logft_ref, cum_ref, cumt_ref, gm_ref, sga_ref,
                      sgg_ref, carry_ref, sp_ref, *, tiles_per_seq):
    tm = x_ref.shape[0]
    h = (x_ref[...] * (1.0 + sc_ref[0]) + sh_ref[0]).astype(BF16)

    def proj(off, width):
        return jnp.dot(h, w_ref[:, off:off + width], preferred_element_type=F32)

    q, k, v, logf, u, gv, sga, sgg = _mixer_epilogue(
        proj(_OFF_Q, 512), proj(_OFF_K, 512), proj(_OFF_V, 512), proj(_OFF_F, LANES), proj(_OFF_U, 512),
        proj(_OFF_GV, 512), proj(_OFF_GA, 1024), proj(_OFF_GG, 1024), bf_ref[...], glg_ref[...], glb_ref[...])
    q_ref[...] = (q * LOG2E).astype(BF16)
    kt = k.T
    kt32_ref[0] = kt
    ktb_ref[0] = kt.astype(BF16)
    vt32_ref[0] = v.T
    vb_ref[...] = v.astype(BF16)
    sga_ref[...] = sga.astype(BF16)
    sgg_ref[...] = sgg.astype(BF16)
    logft_ref[0] = logf.T[:N_HEADS]

    @pl.when(pl.program_id(0) % tiles_per_seq == 0)
    def _():
        carry_ref[...] = jnp.zeros_like(carry_ref)

    carry = carry_ref[...]
    parts = []
    for n in range(tm // CHUNK):
        c = _dot(tri_ref[...], logf[n * CHUNK:(n + 1) * CHUNK], True) + carry
        parts.append(c)
        carry = c[CHUNK - 1:CHUNK]
    carry_ref[...] = carry
    cum = jnp.concatenate(parts, axis=0) * LOG2E
    cum_ref[...] = cum[:, :N_HEADS]
    cumt_ref[0] = cum.T[:N_HEADS]

    gvb = gv.astype(BF16)
    gw = gv.shape[1] // GMLP_GROUPS
    for n in range(tm // CHUNK):
        for g in range(GMLP_GROUPS):
            sp_ref[n * CHUNK:(n + 1) * CHUNK, g * gw:(g + 1) * gw] = jnp.dot(
                ws_ref[g], gvb[n * CHUNK:(n + 1) * CHUNK, g * gw:(g + 1) * gw], preferred_element_type=F32)
    bs = jnp.concatenate([bs_ref[...]] * (tm // CHUNK), axis=0)
    gm_ref[...] = (u * (sp_ref[...] + bs)).astype(BF16)


def _prompt_in(x, sc, sh, w, bf, glg, glb, ws, bs, tri, *, seq, tm=512):
    n, d = x.shape
    nb = n // seq
    tps = seq // tm
    row = lambda c: pl.BlockSpec((tm, c), lambda i: (i, 0))
    mod = pl.BlockSpec((1, 1, d), lambda i: (i // tps, 0, 0))
    tr = lambda r: pl.BlockSpec((1, r, tm), lambda i: (i // tps, 0, i % tps))
    outs = [((n, 512), BF16), ((nb, 512, seq), F32), ((nb, 512, seq), BF16), ((nb, 512, seq), F32),
            ((n, 512), BF16), ((nb, N_HEADS, seq), F32), ((n, N_HEADS), F32), ((nb, N_HEADS, seq), F32),
            ((n, 512), BF16), ((n, 1024), BF16), ((n, 1024), BF16)]
    out_specs = [row(512), tr(512), tr(512), tr(512), row(512), tr(N_HEADS), row(N_HEADS), tr(N_HEADS),
                 row(512), row(1024), row(1024)]
    return pl.pallas_call(
        functools.partial(_prompt_in_kernel, tiles_per_seq=tps),
        out_shape=[jax.ShapeDtypeStruct(s, t) for s, t in outs],
        grid=(n // tm,),
        in_specs=[row(d), mod, mod, _full(w.shape), _full(bf.shape), _full(glg.shape), _full(glb.shape),
                  _full(ws.shape), _full(bs.shape), _full(tri.shape)],
        out_specs=out_specs,
        scratch_shapes=[pltpu.VMEM((1, LANES), F32), pltpu.VMEM((tm, 512), F32)],
        compiler_params=_params("arbitrary"),
        name="prompt_in",
    )(x, sc, sh, w, bf, glg, glb, ws, bs, tri)


def _fox_prompt_kernel(q_ref, kt_ref, v_ref, cum_ref, cumt_ref, o_ref, m_ref, acc_ref, *, tq, tk):
    qi = pl.program_id(1)

    def lane_masks(rows):
        lane = lax.broadcasted_iota(jnp.int32, (rows, LANES), 1)
        lo = lane < HEAD_DIM
        own = [jnp.where(lo, 1.0, 0.0).astype(BF16), jnp.where(lo, 0.0, 1.0).astype(BF16)]
        ones = [jnp.where(lane == HEAD_DIM, 1.0, 0.0).astype(BF16), jnp.where(lane == 0, 1.0, 0.0).astype(BF16)]
        return lo, own, ones

    lo, own_q, _ = lane_masks(tq)
    _, own_v, ones_v = lane_masks(tk)
    m_ref[...] = jnp.full_like(m_ref, NEG_INF)
    acc_ref[...] = jnp.zeros_like(acc_ref)
    cb = cum_ref[...]
    qh_all = [q_ref[:, (h // 2) * LANES:(h // 2 + 1) * LANES] * own_q[h % 2] for h in range(N_HEADS)]
    q0 = qi * tq

    def step(j, masked):
        ks = pl.multiple_of(j * tk, tk)
        vt = v_ref[pl.ds(ks, tk), :]
        if masked:
            rows = lax.broadcasted_iota(jnp.int32, (tq, tk), 0) + q0
            cols = lax.broadcasted_iota(jnp.int32, (tq, tk), 1) + ks
            keep = cols <= rows
        def qk(h):
            sl = slice((h // 2) * LANES, (h // 2 + 1) * LANES)
            return jnp.dot(qh_all[h], kt_ref[sl, pl.ds(ks, tk)], preferred_element_type=F32)

        ahead = 2
        raw = [qk(h) for h in range(ahead)]
        for h in range(N_HEADS):
            p2, odd = divmod(h, 2)
            sl = slice(p2 * LANES, (p2 + 1) * LANES)
            if h + ahead < N_HEADS:
                raw.append(qk(h + ahead))
            s = raw[h] + cb[:, h:h + 1] - cumt_ref[h:h + 1, pl.ds(ks, tk)]
            if masked:
                s = jnp.where(keep, s, NEG_INF)
            m_old = m_ref[h]
            m_new = jnp.maximum(m_old, jnp.max(s, axis=-1, keepdims=True))
            p = jnp.exp2(s - m_new).astype(BF16)
            vh = vt[:, sl] * own_v[odd] + ones_v[odd]
            acc_ref[h] = acc_ref[h] * jnp.exp2(m_old - m_new) + jnp.dot(p, vh, preferred_element_type=F32)
            m_ref[h] = m_new

    jd = q0 // tk
    lax.fori_loop(0, jd, lambda j, c: (step(j, False), c)[1], 0)
    step(jd, True)

    outs = []
    for p2 in range(N_HEADS // 2):
        ae, ao = acc_ref[2 * p2], acc_ref[2 * p2 + 1]
        outs.append(jnp.where(lo, ae / ae[:, HEAD_DIM:HEAD_DIM + 1], ao / ao[:, 0:1]))
    o_ref[...] = jnp.concatenate(outs, axis=-1).astype(o_ref.dtype)


def _fox_prompt(q, kt, v, cum, cumt, *, seq, tq=256, tk=512):
    n = q.shape[0]
    nb, nq = n // seq, seq // tq
    return pl.pallas_call(
        functools.partial(_fox_prompt_kernel, tq=tq, tk=tk),
        out_shape=jax.ShapeDtypeStruct((n, D_ATTN), BF16),
        grid=(nb, nq),
        in_specs=[pl.BlockSpec((tq, D_ATTN), lambda b, i: (b * nq + i, 0)),
                  pl.BlockSpec((None, D_ATTN, seq), lambda b, i: (b, 0, 0)),
                  pl.BlockSpec((seq, D_ATTN), lambda b, i: (b, 0)),
                  pl.BlockSpec((tq, N_HEADS), lambda b, i: (b * nq + i, 0)),
                  pl.BlockSpec((None, N_HEADS, seq), lambda b, i: (b, 0, 0))],
        out_specs=pl.BlockSpec((tq, D_ATTN), lambda b, i: (b * nq + i, 0)),
        scratch_shapes=[pltpu.VMEM((N_HEADS, tq, 1), F32), pltpu.VMEM((N_HEADS, tq, LANES), F32)],
        compiler_params=_params("parallel", "arbitrary"),
        name="fox_prompt",
    )(q, kt, v, cum, cumt)


def _route(logits):
    lane = lax.broadcasted_iota(jnp.int32, logits.shape, 1)
    big = jnp.int32(LANES)
    is_g = (lane >= N_EXPERTS) & (lane < N_EXPERTS + N_GROUPS)
    gl = jnp.where(is_g, logits, NEG_INF)
    gmax = jnp.max(gl, axis=-1, keepdims=True)
    gi = jnp.min(jnp.where(gl == gmax, lane, big), axis=-1, keepdims=True) - N_EXPERTS
    pg_top = 1.0 / jnp.sum(jnp.exp(gl - gmax), axis=-1, keepdims=True)
    in_g = (lane >= gi * EXPERTS_PER_GROUP) & (lane < (gi + 1) * EXPERTS_PER_GROUP)
    el = jnp.where(in_g, logits, NEG_INF)
    m1 = jnp.max(el, axis=-1, keepdims=True)
    i1 = jnp.min(jnp.where(el == m1, lane, big), axis=-1, keepdims=True)
    el2 = jnp.where(lane == i1, NEG_INF, el)
    m2 = jnp.max(el2, axis=-1, keepdims=True)
    i2 = jnp.min(jnp.where(el2 == m2, lane, big), axis=-1, keepdims=True)
    e2 = jnp.exp(m2 - m1)
    w1 = pg_top / (1.0 + e2)
    return jnp.where(lane == i1, w1, jnp.where(lane == i2, w1 * e2, 0.0))


def _merge_kernel(a_ref, gm_ref, sga_ref, sgg_ref, x_ref, g1_ref, sc2_ref, sh2_ref, wao_ref, wgo_ref, wo_ref,
                  l1g_ref, l1b_ref, wr_ref, br_ref, x1_ref, h2_ref, comb_ref, *, alpha, precise):
    t = (sga_ref[...].astype(F32) * _dot(a_ref[...], wao_ref[...], precise)
         + sgg_ref[...].astype(F32) * _dot(gm_ref[...], wgo_ref[...], precise))
    m = _dot(t, wo_ref[...], precise)
    x1 = _ln(alpha * x_ref[...] + (1.0 + _mod(g1_ref)) * m, l1g_ref[...], l1b_ref[...])
    h2 = x1 * (1.0 + _mod(sc2_ref)) + _mod(sh2_ref)
    x1_ref[...] = x1
    h2_ref[...] = h2.astype(h2_ref.dtype)
    comb_ref[...] = _route(_dot(h2, wr_ref[...], precise) + br_ref[...])


def _merge(a, gm, sga, sgg, x, g1, sc2, sh2, wao, wgo, wo, l1g, l1b, wr, br, *, rows_per_mod, tm, alpha, precise):
    n, d = x.shape
    row = lambda c: pl.BlockSpec((tm, c), lambda i: (i, 0))
    mod = _mod_spec(g1, tm, rows_per_mod)
    return pl.pallas_call(
        functools.partial(_merge_kernel, alpha=alpha, precise=precise),
        out_shape=[jax.ShapeDtypeStruct((n, d), F32), jax.ShapeDtypeStruct((n, d), BF16),
                   jax.ShapeDtypeStruct((n, LANES), F32)],
        grid=(n // tm,),
        in_specs=[row(D_ATTN), row(gm.shape[1]), row(d), row(d), row(d), mod, mod, mod, _full(wao.shape),
                  _full(wgo.shape), _full(wo.shape), _full(l1g.shape), _full(l1b.shape), _full(wr.shape),
                  _full(br.shape)],
        out_specs=[row(d), row(d), row(LANES)],
        compiler_params=_params("parallel"),
        name="merge_precise" if precise else "merge",
    )(a, gm, sga, sgg, x, g1, sc2, sh2, wao, wgo, wo, l1g, l1b, wr, br)


def _moe_kernel(h2_ref, comb_ref, x1_ref, g2_ref, wg_ref, wu_ref, wd_ref, l2g_ref, l2b_ref, o_ref, *, alpha):
    h2 = h2_ref[...]
    comb = comb_ref[...]
    acc = jnp.zeros(o_ref.shape, F32)
    for e in range(N_EXPERTS):
        hg = jnp.dot(h2, wg_ref[e], preferred_element_type=F32)
        hu = jnp.dot(h2, wu_ref[e], preferred_element_type=F32)
        a = hg * jax.nn.sigmoid(hg) * hu * comb[:, e:e + 1]
        acc = acc + jnp.dot(a.astype(BF16), wd_ref[e], preferred_element_type=F32)
    o_ref[...] = _ln(alpha * x1_ref[...] + (1.0 + _mod(g2_ref)) * acc, l2g_ref[...], l2b_ref[...])


def _moe(h2, comb, x1, g2, wg, wu, wd, l2g, l2b, *, rows_per_mod, tm, alpha):
    n, d = x1.shape
    row = lambda c: pl.BlockSpec((tm, c), lambda i: (i, 0))
    mod = _mod_spec(g2, tm, rows_per_mod)
    once =lambda s: pl.BlockSpec(s, lambda i: (0,) * len(s), pipeline_mode=pl.Buffered(1))
    return pl.pallas_call(
        functools.partial(_moe_kernel, alpha=alpha),
        out_shape=jax.ShapeDtypeStruct((n, d), F32),
        grid=(n // tm,),
        in_specs=[row(d), row(LANES), row(d), mod, once(wg.shape), once(wu.shape), once(wd.shape),
                  _full(l2g.shape), _full(l2b.shape)],
        out_specs=row(d),
        compiler_params=_params("parallel"),
        name="moe",
    )(h2, comb, x1, g2, wg, wu, wd, l2g, l2b)


def _sample_proj_kernel(x_ref, sc_ref, sh_ref, w_ref, z_ref):
    z_ref[...] = _dot(x_ref[...] * (1.0 + sc_ref[...]) + sh_ref[...], w_ref[...], True)


def _sample_proj(x, sc, sh, w, *, tn):
    n, d = x.shape
    return pl.pallas_call(
        _sample_proj_kernel,
        out_shape=jax.ShapeDtypeStruct((n, w.shape[1]), F32),
        grid=(w.shape[1] // tn,),
        in_specs=[_full((n, d))] * 3 + [pl.BlockSpec((d, tn), lambda j: (0, j))],
        out_specs=pl.BlockSpec((n, tn), lambda j: (0, j)),
        compiler_params=_params("parallel"),
        name="sample_proj",
    )(x, sc, sh, w)


def _sample_mix_kernel(z_ref, bf_ref, glg_ref, glb_ref, mg_ref, bs_ref, mc_ref,
                       q_ref, k_ref, v_ref, logf_ref, cn_ref, gv_ref, gm_ref, sga_ref, sgg_ref):
    z = z_ref[...]
    sec = lambda off, width: z[:, off:off + width]
    q, k, v, logf, u, gv, sga, sgg = _mixer_epilogue(
        sec(_OFF_Q, 512), sec(_OFF_K, 512), sec(_OFF_V, 512), sec(_OFF_F, LANES), sec(_OFF_U, 512),
        sec(_OFF_GV, 512), sec(_OFF_GA, 1024), sec(_OFF_GG, 1024), bf_ref[...], glg_ref[...], glb_ref[...])
    q_ref[...] = q
    k_ref[...] = k
    v_ref[...] = v
    logf_ref[...] = logf
    cn_ref[...] = _dot(mc_ref[...], logf, True)
    gv_ref[...] = gv
    sga_ref[...] = sga
    sgg_ref[...] = sgg
    gw = gv.shape[1] // GMLP_GROUPS
    sp = jnp.concatenate([_dot(mg_ref[g], gv[:, g * gw:(g + 1) * gw], True) for g in range(GMLP_GROUPS)], axis=-1)
    gm_ref[...] = u * (sp + bs_ref[...])


def _sample_mix(z, bf, glg, glb, mg, bs, mc):
    n = z.shape[0]
    shapes = [(n, 512)] * 3 + [(n, LANES)] * 2 + [(n, 512)] * 2 + [(n, 1024)] * 2
    return pl.pallas_call(
        _sample_mix_kernel,
        out_shape=[jax.ShapeDtypeStruct(s, F32) for s in shapes],
        compiler_params=pltpu.CompilerParams(vmem_limit_bytes=VMEM_LIMIT_BYTES),
        name="sample_mix",
    )(z, bf, glg, glb, mg, bs, mc)


PAGES_PER_STEP = 16


def _fox_sample_kernel(pt_ref, *refs, n_chunks, n_new):
    del pt_ref
    P = PAGES_PER_STEP
    kp, vp, lp = refs[:P], refs[P:2 * P], refs[2 * P:3 * P]
    (qbd_ref, cn_ref, kn_ref, vn_ref, cnb_ref, o_ref, m_ref, l_ref, acc_ref, later_ref) = refs[3 * P:]
    c = pl.program_id(1)

    @pl.when(c == 0)
    def _():
        m_ref[...] = jnp.full_like(m_ref, NEG_INF)
        l_ref[...] = jnp.zeros_like(l_ref)
        acc_ref[...] = jnp.zeros_like(acc_ref)
        later_ref[...] = jnp.zeros_like(later_ref)

    qbd = qbd_ref[0]
    cn = cn_ref[0]
    nt = (((1,), (1,)), ((), ()))

    def update(s, vs):
        m_old = m_ref[...]
        m_new = m_old
        for sb in s:
            m_new = jnp.maximum(m_new, jnp.max(sb, axis=-1, keepdims=True))
        alpha = jnp.exp(m_old - m_new)
        l_new = l_ref[...] * alpha
        acc = acc_ref[...] * alpha
        for sb, vb in zip(s, vs):
            p = jnp.exp(sb - m_new)
            l_new = l_new + jnp.sum(p, axis=-1, keepdims=True)
            acc = acc + lax.dot_general(p, vb, nt, preferred_element_type=F32)
        m_ref[...] = m_new
        l_ref[...] = l_new
        acc_ref[...] = acc

    lane = lax.broadcasted_iota(jnp.int32, (N_HEADS, PAGE_SIZE), 1)
    later = later_ref[...]
    bias = [None] * P
    for r in reversed(range(P)):
        y = lp[r][0]
        for k in range(7):
            sh = 1 << k
            y = y + jnp.where(lane < PAGE_SIZE - sh, pltpu.roll(y, PAGE_SIZE - sh, 1), 0.0)
        bias[r] = jnp.where(lane < PAGE_SIZE - 1, pltpu.roll(y, PAGE_SIZE - 1, 1), 0.0) + later
        later = later + y[:, 0:1]
    later_ref[...] = later

    scores = []
    for r in range(P):
        st = jnp.dot(qbd, kp[r][0].reshape(D_ATTN, PAGE_SIZE), preferred_element_type=F32)
        scores.append(st + jnp.concatenate([bias[r]] * n_new, axis=0) + cn)
    update(scores, [vp[r][0].reshape(D_ATTN, PAGE_SIZE) for r in range(P)])

    @pl.when(c == n_chunks - 1)
    def _():
        st = jnp.dot(qbd, kn_ref[0], preferred_element_type=F32) + cn - cnb_ref[0]
        col = lax.broadcasted_iota(jnp.int32, st.shape, 1)
        row = lax.broadcasted_iota(jnp.int32, st.shape, 0)
        update([jnp.where(col <= row // N_HEADS, st, NEG_INF)], [vn_ref[0]])
        full = acc_ref[...] / l_ref[...]
        lane_head = lax.broadcasted_iota(jnp.int32, full.shape, 1) // HEAD_DIM
        row_head = lax.broadcasted_iota(jnp.int32, full.shape, 0) % N_HEADS
        own = jnp.where(lane_head == row_head, full, 0.0)
        rows = [jnp.sum(own[q * N_HEADS:(q + 1) * N_HEADS], axis=0, keepdims=True) for q in range(n_new)]
        o_ref[0] = jnp.concatenate(rows + [jnp.zeros((o_ref.shape[1] - n_new, D_ATTN), F32)], axis=0)


def _fox_sample(page_table, kt_pool, vt_pool, logf_pool, qbd, cn, kn, vn, cnb, *, n_new):
    nb, n_pages = page_table.shape
    P = PAGES_PER_STEP
    n_chunks = n_pages // P
    nrow = n_new * N_HEADS

    def page_spec(shape):
        def one(r):
            return pl.BlockSpec((1,) + shape,
                                lambda b, c, pt: (pt[b, (n_chunks - 1 - c) * P + r],) + (0,) * len(shape))
        return [one(r) for r in range(P)]

    per_b = lambda s: pl.BlockSpec((1,) + s, lambda b, c, pt: (b,) + (0,) * len(s))
    kv_page = (N_HEADS, HEAD_DIM, PAGE_SIZE)
    grid_spec = pltpu.PrefetchScalarGridSpec(
        num_scalar_prefetch=1,
        grid=(nb, n_chunks),
        in_specs=page_spec(kv_page) + page_spec(kv_page) + page_spec((N_HEADS, PAGE_SIZE)) + [
            per_b((nrow, D_ATTN)), per_b((nrow, 1)), per_b((D_ATTN, LANES)), per_b((D_ATTN, LANES)),
            per_b((nrow, LANES))],
        out_specs=per_b((8, D_ATTN)),
        scratch_shapes=[pltpu.VMEM((nrow, 1), F32), pltpu.VMEM((nrow, 1), F32), pltpu.VMEM((nrow, D_ATTN), F32),
                        pltpu.VMEM((N_HEADS, PAGE_SIZE), F32)],
    )
    return pl.pallas_call(
        functools.partial(_fox_sample_kernel, n_chunks=n_chunks, n_new=n_new),
        out_shape=jax.ShapeDtypeStruct((nb, 8, D_ATTN), F32),
        grid_spec=grid_spec,
        compiler_params=_params("parallel", "arbitrary"),
        name="fox_sample",
    )(page_table, *([kt_pool] * P), *([vt_pool] * P), *([logf_pool] * P), qbd, cn, kn, vn, cnb)


def _reorder_w_in(w_in):
    d = w_in.shape[0]
    f0 = 3 * D_ATTN
    pad = jnp.zeros((d, LANES - N_HEADS), w_in.dtype)
    return jnp.concatenate([w_in[:, :f0], w_in[:, f0 + N_HEADS:], w_in[:, f0:f0 + N_HEADS], pad], axis=1)


def kernel(x_prompt, x_sample, c_prompt, c_sample, cache_k, cache_v, cache_logf, page_table, w_ada, b_ada, w_in,
           b_f, gmlp_ln_g, gmlp_ln_b, w_s, b_s, w_attn_out, w_gmlp_out, w_o, ln1_g, ln1_b, w_group_router,
           b_group_router, w_expert_router, b_expert_router, w_exp_gate, w_exp_up, w_exp_down, ln2_g, ln2_b):
    depth = w_ada.shape[0]
    assert depth == 1
    nbp, seq, d = x_prompt.shape
    nbs, n_new, _ = x_sample.shape
    alpha = (2.0 * depth) ** 0.25
    n_pool = cache_k.shape[1]
    d_gmlp = gmlp_ln_g.shape[1]
    gw = d_gmlp // GMLP_GROUPS

    w_in_r = _reorder_w_in(w_in[0])
    w_in_b = w_in_r.astype(BF16)
    bf = jnp.pad(b_f[0], (0, LANES - N_HEADS))[None]
    glg, glb = gmlp_ln_g[0][None], gmlp_ln_b[0][None]
    tril = jnp.tril(jnp.ones((CHUNK, CHUNK), F32))
    ws = jnp.where(tril > 0, w_s[0], 0.0)
    bs_tile = jnp.repeat(b_s[0].T, gw, axis=1)
    w_r = jnp.pad(jnp.concatenate([w_expert_router[0], w_group_router[0]], axis=1),
                  ((0, 0), (0, LANES - N_EXPERTS - N_GROUPS)))
    b_r = jnp.pad(jnp.concatenate([b_expert_router[0], b_group_router[0]]), (0, LANES - N_EXPERTS - N_GROUPS))[None]
    l1g, l1b, l2g, l2b = ln1_g[0][None], ln1_b[0][None], ln2_g[0][None], ln2_b[0][None]
    wg_b, wu_b, wd_b = w_exp_gate[0].astype(BF16), w_exp_up[0].astype(BF16), w_exp_down[0].astype(BF16)

    nc = nbp + nbs
    c_all = jnp.pad(jnp.concatenate([c_prompt, c_sample], axis=0), ((0, -nc % 8), (0, 0)))
    mod = _ada(c_all, w_ada[0], b_ada[0][None])
    modp = [m[:, None, :] for m in jnp.split(mod[:nbp], 6, axis=-1)]
    mods = [m[:, None, :] for m in jnp.split(mod[nbp:nc], 6, axis=-1)]

    xp = x_prompt.reshape(nbp * seq, d)
    q, kt32, ktb, vt32, vb, logft, cum, cumt, gm, sga, sgg = _prompt_in(
        xp, modp[1], modp[0], w_in_b, bf, glg, glb, ws.astype(BF16), bs_tile, tril, seq=seq)
    a = _fox_prompt(q, ktb, vb, cum, cumt, seq=seq)
    x1, h2, comb = _merge(a, gm, sga, sgg, xp, modp[2], modp[4], modp[3], w_attn_out[0].astype(BF16),
                          w_gmlp_out[0].astype(BF16), w_o[0].astype(BF16), l1g, l1b, w_r.astype(BF16), b_r,
                          rows_per_mod=seq, tm=512, alpha=alpha, precise=False)
    yp = _moe(h2, comb, x1, modp[5], wg_b, wu_b, wd_b, l2g, l2b, rows_per_mod=seq, tm=256, alpha=alpha)

    ns = nbs * n_new
    xs = x_sample.reshape(ns, d)
    rep = lambda m: jnp.repeat(m[:, 0, :], n_new, axis=0)
    w_in_s = jnp.pad(w_in_r, ((0, 0), (0, -w_in_r.shape[1] % 896)))
    z = _sample_proj(xs, rep(mods[1]), rep(mods[0]), w_in_s, tn=896)
    eye_b = jnp.eye(nbs, dtype=F32)
    mg = jnp.stack([jnp.kron(eye_b, ws[g, :n_new, :n_new]) for g in range(GMLP_GROUPS)])
    bs_rows = jnp.tile(bs_tile[:n_new], (nbs, 1))
    mc = jnp.kron(eye_b, tril[:n_new, :n_new])
    qs, ks, vs, logf_s, cn, gv_s, gm_s, sga_s, sgg_s = _sample_mix(z, bf, glg, glb, mg, bs_rows, mc)

    kt_pool = cache_k[0].transpose(0, 2, 3, 1)
    vt_pool = cache_v[0].transpose(0, 2, 3, 1)
    logf_pool = cache_logf[0].transpose(0, 2, 1)
    nrow = n_new * N_HEADS
    q4 = qs.reshape(nbs, n_new, N_HEADS, HEAD_DIM)
    qbd = jnp.einsum("bqhd,hg->bqhgd", q4, jnp.eye(N_HEADS, dtype=F32)).reshape(nbs, nrow, D_ATTN)
    cn3 = cn[:, :N_HEADS].reshape(nbs, n_new, N_HEADS)
    cn_col = cn3.reshape(nbs, nrow, 1)
    cnb = jnp.pad(jnp.tile(cn3.transpose(0, 2, 1), (1, n_new, 1)), ((0, 0), (0, 0), (0, LANES - n_new)))
    new_t = lambda t: jnp.pad(t.reshape(nbs, n_new, D_ATTN).transpose(0, 2, 1), ((0, 0), (0, 0), (0, LANES - n_new)))
    a_s = _fox_sample(page_table, kt_pool, vt_pool, logf_pool, qbd, cn_col, new_t(ks), new_t(vs), cnb, n_new=n_new)
    a_s = a_s[:, :n_new].reshape(ns, D_ATTN)
    x1s, h2s, comb_s = _merge(a_s, gm_s, sga_s, sgg_s, xs, rep(mods[2]), rep(mods[4]), rep(mods[3]),
                              w_attn_out[0], w_gmlp_out[0], w_o[0], l1g, l1b, w_r, b_r,
                              rows_per_mod=ns, tm=ns, alpha=alpha, precise=True)
    ys = _moe(h2s, comb_s, x1s, rep(mods[5]), wg_b, wu_b, wd_b, l2g, l2b, rows_per_mod=ns, tm=ns, alpha=alpha)

    hs = (N_HEADS, HEAD_DIM)
    untr = lambda t: t.reshape(1, nbp, *hs, seq).transpose(0, 1, 4, 2, 3)
    return (yp.reshape(nbp, seq, d), ys.reshape(nbs, n_new, d), untr(kt32), untr(vt32),
            logft.reshape(1, nbp, N_HEADS, seq).transpose(0, 1, 3, 2),
            ks.reshape(1, nbs, n_new, *hs), vs.reshape(1, nbs, n_new, *hs),
            logf_s[:, :N_HEADS].reshape(1, nbs, n_new, N_HEADS), gv_s.reshape(1, nbs, n_new, d_gmlp))
```

```python
import functools

import numpy as np
import jax
import jax.numpy as jnp
from jax import lax
from jax.experimental import pallas as pl
from jax.experimental.pallas import tpu as pltpu

F32 = jnp.float32
BF16 = jnp.bfloat16
HIGHEST = lax.Precision.HIGHEST

N_HEADS = 8
HEAD_DIM = 64
D_ATTN = N_HEADS * HEAD_DIM
PAGE_SIZE = 128
CHUNK = 128
GMLP_GROUPS = 4
N_GROUPS = 4
EXPERTS_PER_GROUP = 4
N_EXPERTS = N_GROUPS * EXPERTS_PER_GROUP
LN_EPS = 1e-5
LANES = 128
NEG_INF = float("-inf")
LOG2E = 1.4426950408889634

VMEM_LIMIT_BYTES = 56 * 1024 * 1024


def _params(*sem):
    return pltpu.CompilerParams(dimension_semantics=sem, vmem_limit_bytes=VMEM_LIMIT_BYTES)


def _full(shape):
    n = len(shape)
    return pl.BlockSpec(shape, lambda *_: (0,) * n)


def _ln(x, g, b):
    mu = jnp.mean(x, axis=-1, keepdims=True)
    xc = x - mu
    var = jnp.mean(xc * xc, axis=-1, keepdims=True)
    return xc * lax.rsqrt(var + LN_EPS) * g + b


def _mod(ref):
    return ref[0] if len(ref.shape) == 3 else ref[...]


def _mod_spec(m, tm, rows_per_mod):
    if m.ndim == 3:
        return pl.BlockSpec((1, 1, m.shape[-1]), lambda i: (i * tm // rows_per_mod, 0, 0))
    return pl.BlockSpec((tm, m.shape[-1]), lambda i: (i, 0))


def _dot(a, b, precise):
    if precise:
        return jnp.dot(a, b, precision=HIGHEST, preferred_element_type=F32)
    return jnp.dot(a.astype(BF16), b.astype(BF16), preferred_element_type=F32)


def _ada_kernel(c_ref, w_ref, b_ref, o_ref):
    c = c_ref[...]
    o_ref[...] = _dot(c * jax.nn.sigmoid(c), w_ref[...], True) + b_ref[...]


def _ada(c, w, b, *, tn=1024):
    n, d = c.shape
    dout = w.shape[1]
    return pl.pallas_call(
        _ada_kernel,
        out_shape=jax.ShapeDtypeStruct((n, dout), F32),
        grid=(dout // tn,),
        in_specs=[_full((n, d)), pl.BlockSpec((d, tn), lambda j: (0, j)), pl.BlockSpec((1, tn), lambda j: (0, j))],
        out_specs=pl.BlockSpec((n, tn), lambda j: (0, j)),
        compiler_params=_params("parallel"),
        name="ada",
    )(c, w, b)


_OFF_Q, _OFF_K, _OFF_V, _OFF_U, _OFF_GV = 0, 512, 1024, 1536, 2048
_OFF_GA, _OFF_GG, _OFF_F, _W_COLS = 2560, 3584, 4608, 4736


def _mixer_epilogue(zq, zk, zv, zf, zu, zgv, zga, zgg, bf, glg, glb):
    q = zq * (HEAD_DIM ** -0.5)
    logf = jax.nn.log_sigmoid(zf + bf)
    u = jax.nn.gelu(zu)
    gv = _ln(jax.nn.gelu(zgv), glg, glb)
    return q, zk, zv, logf, u, gv, jax.nn.sigmoid(zga), jax.nn.sigmoid(zgg)


def _prompt_in_kernel(x_ref, sc_ref, sh_ref, w_ref, bf_ref, glg_ref, glb_ref, ws_ref, bs_ref, tri_ref,
                      qt_ref, kt32_ref, kaug_ref, vt32_ref, vtaug_ref, logft_ref, cumt_ref, reft_ref, gm_ref,
                      sga_ref, sgg_ref, carry_ref, sp_ref, *, tiles_per_seq):
    tm = x_ref.shape[0]
    h = (x_ref[...] * (1.0 + sc_ref[0]) + sh_ref[0]).astype(BF16)

    def proj(off, width):
        return jnp.dot(h, w_ref[:, off:off + width], preferred_element_type=F32)

    q, k, v, logf, u, gv, sga, sgg = _mixer_epilogue(
        proj(_OFF_Q, 512), proj(_OFF_K, 512), proj(_OFF_V, 512), proj(_OFF_F, LANES), proj(_OFF_U, 512),
        proj(_OFF_GV, 512), proj(_OFF_GA, 1024), proj(_OFF_GG, 1024), bf_ref[...], glg_ref[...], glb_ref[...])
    qt_ref[0] = (q * LOG2E).T.astype(BF16)
    kt32_ref[0] = k.T
    vt = v.T
    vt32_ref[0] = vt
    sga_ref[...] = sga.astype(BF16)
    sgg_ref[...] = sgg.astype(BF16)
    logft_ref[0] = logf.T[:N_HEADS]

    @pl.when(pl.program_id(0) % tiles_per_seq == 0)
    def _():
        carry_ref[...] = jnp.zeros_like(carry_ref)

    carry_in = carry_ref[...]
    carry = carry_in
    parts = []
    for n in range(tm // CHUNK):
        c = _dot(tri_ref[...], logf[n * CHUNK:(n + 1) * CHUNK], True) + carry
        parts.append(c)
        carry = c[CHUNK - 1:CHUNK]
    carry_ref[...] = carry
    cum = jnp.concatenate(parts, axis=0)
    cumt_ref[0] = (cum * LOG2E).T[:N_HEADS]
    reft_ref[0] = (jnp.broadcast_to(carry_in, cum.shape) * LOG2E).T[:N_HEADS]

    neg = (carry_in - cum) * LOG2E
    hi = neg.astype(BF16).astype(F32)
    mid = (neg - hi).astype(BF16).astype(F32)
    lo = (neg - hi - mid).astype(BF16).astype(F32)
    lane = lax.broadcasted_iota(jnp.int32, (tm, LANES), 1)
    for hd in range(N_HEADS):
        src = k[:, (hd // 2) * LANES:(hd // 2 + 1) * LANES]
        if hd % 2:
            src = pltpu.roll(src, HEAD_DIM, 1)
        blk = jnp.where(lane < HEAD_DIM, src, 0.0)
        for i, piece in enumerate((hi, mid, lo)):
            blk = jnp.where(lane == HEAD_DIM + i, piece[:, hd:hd + 1], blk)
        kaug_ref[:, hd * LANES:(hd + 1) * LANES] = blk.astype(BF16)

    ones_rows = jnp.where(lax.broadcasted_iota(jnp.int32, (HEAD_DIM, tm), 0) == 0, 1.0, 0.0)
    for hd in range(N_HEADS):
        grp = jnp.concatenate([vt[hd * HEAD_DIM:(hd + 1) * HEAD_DIM], ones_rows], axis=0)
        vtaug_ref[0, hd * LANES:(hd + 1) * LANES, :] = grp.astype(BF16)

    gvb = gv.astype(BF16)
    gw = gv.shape[1] // GMLP_GROUPS
    for n in range(tm // CHUNK):
        for g in range(GMLP_GROUPS):
            sp_ref[n * CHUNK:(n + 1) * CHUNK, g * gw:(g + 1) * gw] = jnp.dot(
                ws_ref[g], gvb[n * CHUNK:(n + 1) * CHUNK, g * gw:(g + 1) * gw], preferred_element_type=F32)
    bs = jnp.concatenate([bs_ref[...]] * (tm // CHUNK), axis=0)
    gm_ref[...] = (u * (sp_ref[...] + bs)).astype(BF16)


def _prompt_in(x, sc, sh, w, bf, glg, glb, ws, bs, tri, *, seq, tm):
    n, d = x.shape
    nb = n // seq
    tps = seq // tm
    row = lambda c: pl.BlockSpec((tm, c), lambda i: (i, 0))
    mod = pl.BlockSpec((1, 1, d), lambda i: (i // tps, 0, 0))
    tr = lambda r: pl.BlockSpec((1, r, tm), lambda i: (i // tps, 0, i % tps))
    aug = N_HEADS * LANES
    outs = [((nb, 512, seq), BF16), ((nb, 512, seq), F32), ((n, aug), BF16), ((nb, 512, seq), F32),
            ((nb, aug, seq), BF16), ((nb, N_HEADS, seq), F32), ((nb, N_HEADS, seq), F32),
            ((nb, N_HEADS, seq), F32), ((n, 512), BF16), ((n, 1024), BF16), ((n, 1024), BF16)]
    out_specs = [tr(512), tr(512), row(aug), tr(512), tr(aug), tr(N_HEADS), tr(N_HEADS), tr(N_HEADS),
                 row(512), row(1024), row(1024)]
    return pl.pallas_call(
        functools.partial(_prompt_in_kernel, tiles_per_seq=tps),
        out_shape=[jax.ShapeDtypeStruct(s, t) for s, t in outs],
        grid=(n // tm,),
        in_specs=[row(d), mod, mod, _full(w.shape), _full(bf.shape), _full(glg.shape), _full(glb.shape),
                  _full(ws.shape), _full(bs.shape), _full(tri.shape)],
        out_specs=out_specs,
        scratch_shapes=[pltpu.VMEM((1, LANES), F32), pltpu.VMEM((tm, 512), F32)],
        compiler_params=_params("arbitrary"),
        name="prompt_in",
    )(x, sc, sh, w, bf, glg, glb, ws, bs, tri)


ATTN_TILE = 512
QK_AHEAD = 2


def _fox_prompt_kernel(qt_ref, kaug_ref, vtaug_ref, cumt_ref, reft_ref, o_ref, m_ref, acc_ref):
    t = ATTN_TILE
    qi = pl.program_id(1)
    q0 = pl.multiple_of(qi * t, t)
    m_ref[...] = jnp.full_like(m_ref, NEG_INF)
    acc_ref[...] = jnp.zeros_like(acc_ref)
    ones3 = jnp.where(lax.broadcasted_iota(jnp.int32, (HEAD_DIM, t), 0) < 3, 1.0, 0.0).astype(BF16)
    qaug = [jnp.concatenate([qt_ref[h * HEAD_DIM:(h + 1) * HEAD_DIM, :], ones3], axis=0) for h in range(N_HEADS)]
    cb = [cumt_ref[h:h + 1, pl.ds(q0, t)] for h in range(N_HEADS)]

    def step(j, masked):
        ks = pl.multiple_of(j * t, t)
        if masked:
            keep = lax.broadcasted_iota(jnp.int32, (t, t), 0) <= lax.broadcasted_iota(jnp.int32, (t, t), 1)
        def scores(h):
            return jnp.dot(kaug_ref[pl.ds(ks, t), h * LANES:(h + 1) * LANES], qaug[h], preferred_element_type=F32)

        raw = [scores(h) for h in range(QK_AHEAD)]
        for h in range(N_HEADS):
            if h + QK_AHEAD < N_HEADS:
                raw.append(scores(h + QK_AHEAD))
            grp = slice(h * LANES, (h + 1) * LANES)
            st = raw[h]
            if masked:
                st = jnp.where(keep, st, NEG_INF)
            ref_j = jnp.concatenate([reft_ref[h:h + 1, pl.ds(ks, LANES)]] * (t // LANES), axis=1)
            crow = cb[h] - ref_j
            m_old = m_ref[h]
            m_new = jnp.maximum(m_old, jnp.max(st, axis=0, keepdims=True) + crow)
            pt = jnp.exp2(st - (m_new - crow)).astype(BF16)
            acc_ref[h] = acc_ref[h] * jnp.exp2(m_old - m_new) + jnp.dot(
                vtaug_ref[grp, pl.ds(ks, t)], pt, preferred_element_type=F32)
            m_ref[h] = m_new

    lax.fori_loop(0, qi, lambda j, c: (step(j, False), c)[1], 0)
    step(qi, True)

    outs = []
    for h in range(N_HEADS):
        acc = acc_ref[h]
        outs.append(acc[:HEAD_DIM] / acc[HEAD_DIM:HEAD_DIM + 1])
    o_ref[...] = jnp.concatenate(outs, axis=0).T.astype(o_ref.dtype)


def _fox_prompt(qt, kaug, vtaug, cumt, reft, *, seq):
    nb = qt.shape[0]
    t = ATTN_TILE
    nq = seq // t
    aug = N_HEADS * LANES
    per_b = lambda r: pl.BlockSpec((None, r, seq), lambda b, i: (b, 0, 0))
    return pl.pallas_call(
        _fox_prompt_kernel,
        out_shape=jax.ShapeDtypeStruct((nb * seq, D_ATTN), BF16),
        grid=(nb, nq),
        in_specs=[pl.BlockSpec((None, D_ATTN, t), lambda b, i: (b, 0, i)),
                  pl.BlockSpec((seq, aug), lambda b, i: (b, 0)),
                  per_b(aug), per_b(N_HEADS), per_b(N_HEADS)],
        out_specs=pl.BlockSpec((t, D_ATTN), lambda b, i: (b * nq + i, 0)),
        scratch_shapes=[pltpu.VMEM((N_HEADS, 1, t), F32), pltpu.VMEM((N_HEADS, LANES, t), F32)],
        compiler_params=_params("parallel", "arbitrary"),
        name="fox_prompt",
    )(qt, kaug, vtaug, cumt, reft)


def _route(logits):
    lane = lax.broadcasted_iota(jnp.int32, logits.shape, 1)
    big = jnp.int32(LANES)
    is_g = (lane >= N_EXPERTS) & (lane < N_EXPERTS + N_GROUPS)
    gl = jnp.where(is_g, logits, NEG_INF)
    gmax = jnp.max(gl, axis=-1, keepdims=True)
    gi = jnp.min(jnp.where(gl == gmax, lane, big), axis=-1, keepdims=True) - N_EXPERTS
    pg_top = 1.0 / jnp.sum(jnp.exp(gl - gmax), axis=-1, keepdims=True)
    in_g = (lane >= gi * EXPERTS_PER_GROUP) & (lane < (gi + 1) * EXPERTS_PER_GROUP)
    el = jnp.where(in_g, logits, NEG_INF)
    m1 = jnp.max(el, axis=-1, keepdims=True)
    i1 = jnp.min(jnp.where(el == m1, lane, big), axis=-1, keepdims=True)
    el2 = jnp.where(lane == i1, NEG_INF, el)
    m2 = jnp.max(el2, axis=-1, keepdims=True)
    i2 = jnp.min(jnp.where(el2 == m2, lane, big), axis=-1, keepdims=True)
    e2 = jnp.exp(m2 - m1)
    w1 = pg_top / (1.0 + e2)
    return jnp.where(lane == i1, w1, jnp.where(lane == i2, w1 * e2, 0.0))


def _merge_kernel(a_ref, gm_ref, sga_ref, sgg_ref, x_ref, g1_ref, sc2_ref, sh2_ref, wao_ref, wgo_ref, wo_ref,
                  l1g_ref, l1b_ref, wr_ref, br_ref, x1_ref, h2_ref, comb_ref, *, alpha, precise):
    t = (sga_ref[...].astype(F32) * _dot(a_ref[...], wao_ref[...], precise)
         + sgg_ref[...].astype(F32) * _dot(gm_ref[...], wgo_ref[...], precise))
    m = _dot(t, wo_ref[...], precise)
    x1 = _ln(alpha * x_ref[...] + (1.0 + _mod(g1_ref)) * m, l1g_ref[...], l1b_ref[...])
    h2 = x1 * (1.0 + _mod(sc2_ref)) + _mod(sh2_ref)
    x1_ref[...] = x1
    h2_ref[...] = h2.astype(h2_ref.dtype)
    comb_ref[...] = _route(_dot(h2, wr_ref[...], precise) + br_ref[...])


def _merge(a, gm, sga, sgg, x, g1, sc2, sh2, wao, wgo, wo, l1g, l1b, wr, br, *, rows_per_mod, tm, alpha, precise):
    n, d = x.shape
    row = lambda c: pl.BlockSpec((tm, c), lambda i: (i, 0))
    mod = _mod_spec(g1, tm, rows_per_mod)
    return pl.pallas_call(
        functools.partial(_merge_kernel, alpha=alpha, precise=precise),
        out_shape=[jax.ShapeDtypeStruct((n, d), F32), jax.ShapeDtypeStruct((n, d), BF16),
                   jax.ShapeDtypeStruct((n, LANES), F32)],
        grid=(n // tm,),
        in_specs=[row(D_ATTN), row(gm.shape[1]), row(d), row(d), row(d), mod, mod, mod, _full(wao.shape),
                  _full(wgo.shape), _full(wo.shape), _full(l1g.shape), _full(l1b.shape), _full(wr.shape),
                  _full(br.shape)],
        out_specs=[row(d), row(d), row(LANES)],
        compiler_params=_params("parallel"),
        name="merge_precise" if precise else "merge",
    )(a, gm, sga, sgg, x, g1, sc2, sh2, wao, wgo, wo, l1g, l1b, wr, br)


def _moe_kernel(h2_ref, comb_ref, x1_ref, g2_ref, wg_ref, wu_ref, wd_ref, l2g_ref, l2b_ref, o_ref, *, alpha):
    h2 = h2_ref[...]
    comb = comb_ref[...]
    acc = jnp.zeros(o_ref.shape, F32)
    for e in range(N_EXPERTS):
        hg = jnp.dot(h2, wg_ref[e], preferred_element_type=F32)
        hu = jnp.dot(h2, wu_ref[e], preferred_element_type=F32)
        a = hg * jax.nn.sigmoid(hg) * hu * comb[:, e:e + 1]
        acc = acc + jnp.dot(a.astype(BF16), wd_ref[e], preferred_element_type=F32)
    o_ref[...] = _ln(alpha * x1_ref[...] + (1.0 + _mod(g2_ref)) * acc, l2g_ref[...], l2b_ref[...])


def _moe(h2, comb, x1, g2, wg, wu, wd, l2g, l2b, *, rows_per_mod, tm, alpha):
    n, d = x1.shape
    row = lambda c: pl.BlockSpec((tm, c), lambda i: (i, 0))
    mod = _mod_spec(g2, tm, rows_per_mod)
    once =lambda s: pl.BlockSpec(s, lambda i: (0,) * len(s), pipeline_mode=pl.Buffered(1))
    return pl.pallas_call(
        functools.partial(_moe_kernel, alpha=alpha),
        out_shape=jax.ShapeDtypeStruct((n, d), F32),
        grid=(n // tm,),
        in_specs=[row(d), row(LANES), row(d), mod, once(wg.shape), once(wu.shape), once(wd.shape),
                  _full(l2g.shape), _full(l2b.shape)],
        out_specs=row(d),
        compiler_params=_params("parallel"),
        name="moe",
    )(h2, comb, x1, g2, wg, wu, wd, l2g, l2b)


def _sample_proj_kernel(x_ref, sc_ref, sh_ref, w_ref, z_ref):
    z_ref[...] = _dot(x_ref[...] * (1.0 + sc_ref[...]) + sh_ref[...], w_ref[...], True)


def _sample_proj(x, sc, sh, w, *, tn):
    n, d = x.shape
    return pl.pallas_call(
        _sample_proj_kernel,
        out_shape=jax.ShapeDtypeStruct((n, w.shape[1]), F32),
        grid=(w.shape[1] // tn,),
        in_specs=[_full((n, d))] * 3 + [pl.BlockSpec((d, tn), lambda j: (0, j))],
        out_specs=pl.BlockSpec((n, tn), lambda j: (0, j)),
        compiler_params=_params("parallel"),
        name="sample_proj",
    )(x, sc, sh, w)


def _sample_mix_kernel(z_ref, bf_ref, glg_ref, glb_ref, mg_ref, bs_ref, mc_ref,
                       q_ref, k_ref, v_ref, logf_ref, cn_ref, gv_ref, gm_ref, sga_ref, sgg_ref):
    z = z_ref[...]
    sec = lambda off, width: z[:, off:off + width]
    q, k, v, logf, u, gv, sga, sgg = _mixer_epilogue(
        sec(_OFF_Q, 512), sec(_OFF_K, 512), sec(_OFF_V, 512), sec(_OFF_F, LANES), sec(_OFF_U, 512),
        sec(_OFF_GV, 512), sec(_OFF_GA, 1024), sec(_OFF_GG, 1024), bf_ref[...], glg_ref[...], glb_ref[...])
    q_ref[...] = q
    k_ref[...] = k
    v_ref[...] = v
    logf_ref[...] = logf
    cn_ref[...] = _dot(mc_ref[...], logf, True)
    gv_ref[...] = gv
    sga_ref[...] = sga
    sgg_ref[...] = sgg
    gw = gv.shape[1] // GMLP_GROUPS
    sp = jnp.concatenate([_dot(mg_ref[g], gv[:, g * gw:(g + 1) * gw], True) for g in range(GMLP_GROUPS)], axis=-1)
    gm_ref[...] = u * (sp + bs_ref[...])


def _sample_mix(z, bf, glg, glb, mg, bs, mc):
    n = z.shape[0]
    shapes = [(n, 512)] * 3 + [(n, LANES)] * 2 + [(n, 512)] * 2 + [(n, 1024)] * 2
    return pl.pallas_call(
        _sample_mix_kernel,
        out_shape=[jax.ShapeDtypeStruct(s, F32) for s in shapes],
        compiler_params=pltpu.CompilerParams(vmem_limit_bytes=VMEM_LIMIT_BYTES),
        name="sample_mix",
    )(z, bf, glg, glb, mg, bs, mc)


PAGES_PER_STEP = 16


def _fox_sample_kernel(pt_ref, *refs, n_chunks, n_new):
    del pt_ref
    P = PAGES_PER_STEP
    kp, vp, lp = refs[:P], refs[P:2 * P], refs[2 * P:3 * P]
    (qbd_ref, cn_ref, kn_ref, vn_ref, cnb_ref, o_ref, m_ref, l_ref, acc_ref, later_ref) = refs[3 * P:]
    c = pl.program_id(1)

    @pl.when(c == 0)
    def _():
        m_ref[...] = jnp.full_like(m_ref, NEG_INF)
        l_ref[...] = jnp.zeros_like(l_ref)
        acc_ref[...] = jnp.zeros_like(acc_ref)
        later_ref[...] = jnp.zeros_like(later_ref)

    qbd = qbd_ref[0]
    cn = cn_ref[0]
    nt = (((1,), (1,)), ((), ()))

    def update(s, vs):
        m_old = m_ref[...]
        m_new = m_old
        for sb in s:
            m_new = jnp.maximum(m_new, jnp.max(sb, axis=-1, keepdims=True))
        alpha = jnp.exp(m_old - m_new)
        l_new = l_ref[...] * alpha
        acc = acc_ref[...] * alpha
        for sb, vb in zip(s, vs):
            p = jnp.exp(sb - m_new)
            l_new = l_new + jnp.sum(p, axis=-1, keepdims=True)
            acc = acc + lax.dot_general(p, vb, nt, preferred_element_type=F32)
        m_ref[...] = m_new
        l_ref[...] = l_new
        acc_ref[...] = acc

    lane = lax.broadcasted_iota(jnp.int32, (N_HEADS, PAGE_SIZE), 1)
    later = later_ref[...]
    bias = [None] * P
    for r in reversed(range(P)):
        y = lp[r][0]
        for k in range(7):
            sh = 1 << k
            y = y + jnp.where(lane < PAGE_SIZE - sh, pltpu.roll(y, PAGE_SIZE - sh, 1), 0.0)
        bias[r] = jnp.where(lane < PAGE_SIZE - 1, pltpu.roll(y, PAGE_SIZE - 1, 1), 0.0) + later
        later = later + y[:, 0:1]
    later_ref[...] = later

    scores = []
    for r in range(P):
        st = jnp.dot(qbd, kp[r][0].reshape(D_ATTN, PAGE_SIZE), preferred_element_type=F32)
        scores.append(st + jnp.concatenate([bias[r]] * n_new, axis=0) + cn)
    update(scores, [vp[r][0].reshape(D_ATTN, PAGE_SIZE) for r in range(P)])

    @pl.when(c == n_chunks - 1)
    def _():
        st = jnp.dot(qbd, kn_ref[0], preferred_element_type=F32) + cn - cnb_ref[0]
        col = lax.broadcasted_iota(jnp.int32, st.shape, 1)
        row = lax.broadcasted_iota(jnp.int32, st.shape, 0)
        update([jnp.where(col <= row // N_HEADS, st, NEG_INF)], [vn_ref[0]])
        full = acc_ref[...] / l_ref[...]
        lane_head = lax.broadcasted_iota(jnp.int32, full.shape, 1) // HEAD_DIM
        row_head = lax.broadcasted_iota(jnp.int32, full.shape, 0) % N_HEADS
        own = jnp.where(lane_head == row_head, full, 0.0)
        rows = [jnp.sum(own[q * N_HEADS:(q + 1) * N_HEADS], axis=0, keepdims=True) for q in range(n_new)]
        o_ref[0] = jnp.concatenate(rows + [jnp.zeros((o_ref.shape[1] - n_new, D_ATTN), F32)], axis=0)


def _fox_sample(page_table, kt_pool, vt_pool, logf_pool, qbd, cn, kn, vn, cnb, *, n_new):
    nb, n_pages = page_table.shape
    P = PAGES_PER_STEP
    n_chunks = n_pages // P
    nrow = n_new * N_HEADS

    def page_spec(shape):
        def one(r):
            return pl.BlockSpec((1,) + shape,
                                lambda b, c, pt: (pt[b, (n_chunks - 1 - c) * P + r],) + (0,) * len(shape))
        return [one(r) for r in range(P)]

    per_b = lambda s: pl.BlockSpec((1,) + s, lambda b, c, pt: (b,) + (0,) * len(s))
    kv_page = (N_HEADS, HEAD_DIM, PAGE_SIZE)
    grid_spec = pltpu.PrefetchScalarGridSpec(
        num_scalar_prefetch=1,
        grid=(nb, n_chunks),
        in_specs=page_spec(kv_page) + page_spec(kv_page) + page_spec((N_HEADS, PAGE_SIZE)) + [
            per_b((nrow, D_ATTN)), per_b((nrow, 1)), per_b((D_ATTN, LANES)), per_b((D_ATTN, LANES)),
            per_b((nrow, LANES))],
        out_specs=per_b((8, D_ATTN)),
        scratch_shapes=[pltpu.VMEM((nrow, 1), F32), pltpu.VMEM((nrow, 1), F32), pltpu.VMEM((nrow, D_ATTN), F32),
                        pltpu.VMEM((N_HEADS, PAGE_SIZE), F32)],
    )
    return pl.pallas_call(
        functools.partial(_fox_sample_kernel, n_chunks=n_chunks, n_new=n_new),
        out_shape=jax.ShapeDtypeStruct((nb, 8, D_ATTN), F32),
        grid_spec=grid_spec,
        compiler_params=_params("parallel", "arbitrary"),
        name="fox_sample",
    )(page_table, *([kt_pool] * P), *([vt_pool] * P), *([logf_pool] * P), qbd, cn, kn, vn, cnb)


def _reorder_w_in(w_in):
    d = w_in.shape[0]
    f0 = 3 * D_ATTN
    pad = jnp.zeros((d, LANES - N_HEADS), w_in.dtype)
    return jnp.concatenate([w_in[:, :f0], w_in[:, f0 + N_HEADS:], w_in[:, f0:f0 + N_HEADS], pad], axis=1)


def kernel(x_prompt, x_sample, c_prompt, c_sample, cache_k, cache_v, cache_logf, page_table, w_ada, b_ada, w_in,
           b_f, gmlp_ln_g, gmlp_ln_b, w_s, b_s, w_attn_out, w_gmlp_out, w_o, ln1_g, ln1_b, w_group_router,
           b_group_router, w_expert_router, b_expert_router, w_exp_gate, w_exp_up, w_exp_down, ln2_g, ln2_b):
    depth = w_ada.shape[0]
    assert depth == 1
    nbp, seq, d = x_prompt.shape
    nbs, n_new, _ = x_sample.shape
    alpha = (2.0 * depth) ** 0.25
    n_pool = cache_k.shape[1]
    d_gmlp = gmlp_ln_g.shape[1]
    gw = d_gmlp // GMLP_GROUPS

    w_in_r = _reorder_w_in(w_in[0])
    w_in_b = w_in_r.astype(BF16)
    bf = jnp.pad(b_f[0], (0, LANES - N_HEADS))[None]
    glg, glb = gmlp_ln_g[0][None], gmlp_ln_b[0][None]
    tril = jnp.tril(jnp.ones((CHUNK, CHUNK), F32))
    ws = jnp.where(tril > 0, w_s[0], 0.0)
    bs_tile = jnp.repeat(b_s[0].T, gw, axis=1)
    w_r = jnp.pad(jnp.concatenate([w_expert_router[0], w_group_router[0]], axis=1),
                  ((0, 0), (0, LANES - N_EXPERTS - N_GROUPS)))
    b_r = jnp.pad(jnp.concatenate([b_expert_router[0], b_group_router[0]]), (0, LANES - N_EXPERTS - N_GROUPS))[None]
    l1g, l1b, l2g, l2b = ln1_g[0][None], ln1_b[0][None], ln2_g[0][None], ln2_b[0][None]
    wg_b, wu_b, wd_b = w_exp_gate[0].astype(BF16), w_exp_up[0].astype(BF16), w_exp_down[0].astype(BF16)

    nc = nbp + nbs
    c_all = jnp.pad(jnp.concatenate([c_prompt, c_sample], axis=0), ((0, -nc % 8), (0, 0)))
    mod = _ada(c_all, w_ada[0], b_ada[0][None])
    modp = [m[:, None, :] for m in jnp.split(mod[:nbp], 6, axis=-1)]
    mods = [m[:, None, :] for m in jnp.split(mod[nbp:nc], 6, axis=-1)]

    xp = x_prompt.reshape(nbp * seq, d)
    qt, kt32, kaug, vt32, vtaug, logft, cumt, reft, gm, sga, sgg = _prompt_in(
        xp, modp[1], modp[0], w_in_b, bf, glg, glb, ws.astype(BF16), bs_tile, tril, seq=seq, tm=ATTN_TILE)
    a = _fox_prompt(qt, kaug, vtaug, cumt, reft, seq=seq)
    x1, h2, comb = _merge(a, gm, sga, sgg, xp, modp[2], modp[4], modp[3], w_attn_out[0].astype(BF16),
                          w_gmlp_out[0].astype(BF16), w_o[0].astype(BF16), l1g, l1b, w_r.astype(BF16), b_r,
                          rows_per_mod=seq, tm=512, alpha=alpha, precise=False)
    yp = _moe(h2, comb, x1, modp[5], wg_b, wu_b, wd_b, l2g, l2b, rows_per_mod=seq, tm=256, alpha=alpha)

    ns = nbs * n_new
    xs = x_sample.reshape(ns, d)
    rep = lambda m: jnp.repeat(m[:, 0, :], n_new, axis=0)
    w_in_s = jnp.pad(w_in_r, ((0, 0), (0, -w_in_r.shape[1] % 896)))
    z = _sample_proj(xs, rep(mods[1]), rep(mods[0]), w_in_s, tn=896)
    eye_b = jnp.eye(nbs, dtype=F32)
    mg = jnp.stack([jnp.kron(eye_b, ws[g, :n_new, :n_new]) for g in range(GMLP_GROUPS)])
    bs_rows = jnp.tile(bs_tile[:n_new], (nbs, 1))
    mc = jnp.kron(eye_b, tril[:n_new, :n_new])
    qs, ks, vs, logf_s, cn, gv_s, gm_s, sga_s, sgg_s = _sample_mix(z, bf, glg, glb, mg, bs_rows, mc)

    kt_pool = cache_k[0].transpose(0, 2, 3, 1)
    vt_pool = cache_v[0].transpose(0, 2, 3, 1)
    logf_pool = cache_logf[0].transpose(0, 2, 1)
    nrow = n_new * N_HEADS
    q4 = qs.reshape(nbs, n_new, N_HEADS, HEAD_DIM)
    qbd = jnp.einsum("bqhd,hg->bqhgd", q4, jnp.eye(N_HEADS, dtype=F32)).reshape(nbs, nrow, D_ATTN)
    cn3 = cn[:, :N_HEADS].reshape(nbs, n_new, N_HEADS)
    cn_col = cn3.reshape(nbs, nrow, 1)
    cnb = jnp.pad(jnp.tile(cn3.transpose(0, 2, 1), (1, n_new, 1)), ((0, 0), (0, 0), (0, LANES - n_new)))
    new_t = lambda t: jnp.pad(t.reshape(nbs, n_new, D_ATTN).transpose(0, 2, 1), ((0, 0), (0, 0), (0, LANES - n_new)))
    a_s = _fox_sample(page_table, kt_pool, vt_pool, logf_pool, qbd, cn_col, new_t(ks), new_t(vs), cnb, n_new=n_new)
    a_s = a_s[:, :n_new].reshape(ns, D_ATTN)
    x1s, h2s, comb_s = _merge(a_s, gm_s, sga_s, sgg_s, xs, rep(mods[2]), rep(mods[4]), rep(mods[3]),
                              w_attn_out[0], w_gmlp_out[0], w_o[0], l1g, l1b, w_r, b_r,
                              rows_per_mod=ns, tm=ns, alpha=alpha, precise=True)
    ys = _moe(h2s, comb_s, x1s, rep(mods[5]), wg_b, wu_b, wd_b, l2g, l2b, rows_per_mod=ns, tm=ns, alpha=alpha)

    hs = (N_HEADS, HEAD_DIM)
    untr = lambda t: t.reshape(1, nbp, *hs, seq).transpose(0, 1, 4, 2, 3)
    return (yp.reshape(nbp, seq, d), ys.reshape(nbs, n_new, d), untr(kt32), untr(vt32),
            logft.reshape(1, nbp, N_HEADS, seq).transpose(0, 1, 3, 2),
            ks.reshape(1, nbs, n_new, *hs), vs.reshape(1, nbs, n_new, *hs),
            logf_s[:, :N_HEADS].reshape(1, nbs, n_new, N_HEADS), gv_s.reshape(1, nbs, n_new, d_gmlp))
```

```python
import functools

import numpy as np
import jax
import jax.numpy as jnp
from jax import lax
from jax.experimental import pallas as pl
from jax.experimental.pallas import tpu as pltpu

F32 = jnp.float32
BF16 = jnp.bfloat16
HIGHEST = lax.Precision.HIGHEST

N_HEADS = 8
HEAD_DIM = 64
D_ATTN = N_HEADS * HEAD_DIM
PAGE_SIZE = 128
CHUNK = 128
GMLP_GROUPS = 4
N_GROUPS = 4
EXPERTS_PER_GROUP = 4
N_EXPERTS = N_GROUPS * EXPERTS_PER_GROUP
LN_EPS = 1e-5
LANES = 128
NEG_INF = float("-inf")
LOG2E = 1.4426950408889634

VMEM_LIMIT_BYTES = 56 * 1024 * 1024


def _params(*sem):
    return pltpu.CompilerParams(dimension_semantics=sem, vmem_limit_bytes=VMEM_LIMIT_BYTES)


def _full(shape):
    n = len(shape)
    return pl.BlockSpec(shape, lambda *_: (0,) * n)


def _ln(x, g, b):
    mu = jnp.mean(x, axis=-1, keepdims=True)
    xc = x - mu
    var = jnp.mean(xc * xc, axis=-1, keepdims=True)
    return xc * lax.rsqrt(var + LN_EPS) * g + b


def _mod(ref):
    return ref[0] if len(ref.shape) == 3 else ref[...]


def _mod_spec(m, tm, rows_per_mod):
    if m.ndim == 3:
        return pl.BlockSpec((1, 1, m.shape[-1]), lambda i: (i * tm // rows_per_mod, 0, 0))
    return pl.BlockSpec((tm, m.shape[-1]), lambda i: (i, 0))


def _dot(a, b, precise):
    if precise:
        return jnp.dot(a, b, precision=HIGHEST, preferred_element_type=F32)
    return jnp.dot(a.astype(BF16), b.astype(BF16), preferred_element_type=F32)


def _ada_kernel(c_ref, w_ref, b_ref, o_ref):
    c = c_ref[...]
    o_ref[...] = _dot(c * jax.nn.sigmoid(c), w_ref[...], True) + b_ref[...]


def _ada(c, w, b, *, tn=1024):
    n, d = c.shape
    dout = w.shape[1]
    return pl.pallas_call(
        _ada_kernel,
        out_shape=jax.ShapeDtypeStruct((n, dout), F32),
        grid=(dout // tn,),
        in_specs=[_full((n, d)), pl.BlockSpec((d, tn), lambda j: (0, j)), pl.BlockSpec((1, tn), lambda j: (0, j))],
        out_specs=pl.BlockSpec((n, tn), lambda j: (0, j)),
        compiler_params=_params("parallel"),
        name="ada",
    )(c, w, b)


_OFF_Q, _OFF_K, _OFF_V, _OFF_U, _OFF_GV = 0, 512, 1024, 1536, 2048
_OFF_GA, _OFF_GG, _OFF_F, _W_COLS = 2560, 3584, 4608, 4736


def _mixer_epilogue(zq, zk, zv, zf, zu, zgv, zga, zgg, bf, glg, glb):
    q = zq * (HEAD_DIM ** -0.5)
    logf = jax.nn.log_sigmoid(zf + bf)
    u = jax.nn.gelu(zu)
    gv = _ln(jax.nn.gelu(zgv), glg, glb)
    return q, zk, zv, logf, u, gv, jax.nn.sigmoid(zga), jax.nn.sigmoid(zgg)


def _prompt_in_kernel(x_ref, sc_ref, sh_ref, w_ref, bf_ref, glg_ref, glb_ref, ws_ref, bs_ref, tri_ref,
                      qt_ref, kt32_ref, kaug_ref, vt32_ref, vtaug_ref, logft_ref, cumt_ref, reft_ref, gm_ref,
                      sga_ref, sgg_ref, carry_ref, sp_ref, *, tiles_per_seq):
    tm = x_ref.shape[0]
    h = (x_ref[...] * (1.0 + sc_ref[0]) + sh_ref[0]).astype(BF16)

    def proj(off, width):
        return jnp.dot(h, w_ref[:, off:off + width], preferred_element_type=F32)

    q, k, v, logf, u, gv, sga, sgg = _mixer_epilogue(
        proj(_OFF_Q, 512), proj(_OFF_K, 512), proj(_OFF_V, 512), proj(_OFF_F, LANES), proj(_OFF_U, 512),
        proj(_OFF_GV, 512), proj(_OFF_GA, 1024), proj(_OFF_GG, 1024), bf_ref[...], glg_ref[...], glb_ref[...])
    qt_ref[0] = (q * LOG2E).T.astype(BF16)
    kt32_ref[0] = k.T
    vt = v.T
    vt32_ref[0] = vt
    sga_ref[...] = sga.astype(BF16)
    sgg_ref[...] = sgg.astype(BF16)
    logft_ref[0] = logf.T[:N_HEADS]

    @pl.when(pl.program_id(0) % tiles_per_seq == 0)
    def _():
        carry_ref[...] = jnp.zeros_like(carry_ref)

    carry_in = carry_ref[...]
    carry = carry_in
    parts = []
    for n in range(tm // CHUNK):
        c = _dot(tri_ref[...], logf[n * CHUNK:(n + 1) * CHUNK], True) + carry
        parts.append(c)
        carry = c[CHUNK - 1:CHUNK]
    carry_ref[...] = carry
    cum = jnp.concatenate(parts, axis=0)
    cumt_ref[0] = (cum * LOG2E).T[:N_HEADS]
    reft_ref[0] = (jnp.broadcast_to(carry_in, cum.shape) * LOG2E).T[:N_HEADS]

    neg = (carry_in - cum) * LOG2E
    hi = neg.astype(BF16).astype(F32)
    mid = (neg - hi).astype(BF16).astype(F32)
    lo = (neg - hi - mid).astype(BF16).astype(F32)
    lane = lax.broadcasted_iota(jnp.int32, (tm, LANES), 1)
    for hd in range(N_HEADS):
        src = k[:, (hd // 2) * LANES:(hd // 2 + 1) * LANES]
        if hd % 2:
            src = pltpu.roll(src, HEAD_DIM, 1)
        blk = jnp.where(lane < HEAD_DIM, src, 0.0)
        for i, piece in enumerate((hi, mid, lo)):
            blk = jnp.where(lane == HEAD_DIM + i, piece[:, hd:hd + 1], blk)
        kaug_ref[:, hd * LANES:(hd + 1) * LANES] = blk.astype(BF16)

    ones_rows = jnp.where(lax.broadcasted_iota(jnp.int32, (HEAD_DIM, tm), 0) == 0, 1.0, 0.0)
    for hd in range(N_HEADS):
        grp = jnp.concatenate([vt[hd * HEAD_DIM:(hd + 1) * HEAD_DIM], ones_rows], axis=0)
        vtaug_ref[0, hd * LANES:(hd + 1) * LANES, :] = grp.astype(BF16)

    gvb = gv.astype(BF16)
    gw = gv.shape[1] // GMLP_GROUPS
    for n in range(tm // CHUNK):
        for g in range(GMLP_GROUPS):
            sp_ref[n * CHUNK:(n + 1) * CHUNK, g * gw:(g + 1) * gw] = jnp.dot(
                ws_ref[g], gvb[n * CHUNK:(n + 1) * CHUNK, g * gw:(g + 1) * gw], preferred_element_type=F32)
    bs = jnp.concatenate([bs_ref[...]] * (tm // CHUNK), axis=0)
    gm_ref[...] = (u * (sp_ref[...] + bs)).astype(BF16)


def _prompt_in(x, sc, sh, w, bf, glg, glb, ws, bs, tri, *, seq, tm):
    n, d = x.shape
    nb = n // seq
    tps = seq // tm
    row = lambda c: pl.BlockSpec((tm, c), lambda i: (i, 0))
    mod = pl.BlockSpec((1, 1, d), lambda i: (i // tps, 0, 0))
    tr = lambda r: pl.BlockSpec((1, r, tm), lambda i: (i // tps, 0, i % tps))
    aug = N_HEADS * LANES
    outs = [((nb, 512, seq), BF16), ((nb, 512, seq), F32), ((n, aug), BF16), ((nb, 512, seq), F32),
            ((nb, aug, seq), BF16), ((nb, N_HEADS, seq), F32), ((nb, N_HEADS, seq), F32),
            ((nb, N_HEADS, seq), F32), ((n, 512), BF16), ((n, 1024), BF16), ((n, 1024), BF16)]
    out_specs = [tr(512), tr(512), row(aug), tr(512), tr(aug), tr(N_HEADS), tr(N_HEADS), tr(N_HEADS),
                 row(512), row(1024), row(1024)]
    return pl.pallas_call(
        functools.partial(_prompt_in_kernel, tiles_per_seq=tps),
        out_shape=[jax.ShapeDtypeStruct(s, t) for s, t in outs],
        grid=(n // tm,),
        in_specs=[row(d), mod, mod, _full(w.shape), _full(bf.shape), _full(glg.shape), _full(glb.shape),
                  _full(ws.shape), _full(bs.shape), _full(tri.shape)],
        out_specs=out_specs,
        scratch_shapes=[pltpu.VMEM((1, LANES), F32), pltpu.VMEM((tm, 512), F32)],
        compiler_params=_params("arbitrary"),
        name="prompt_in",
    )(x, sc, sh, w, bf, glg, glb, ws, bs, tri)


ATTN_TILE = 512
QK_AHEAD = 2


def _fox_prompt_kernel(qt_ref, kaug_ref, vtaug_ref, cumt_ref, reft_ref, o_ref, m_ref, acc_ref):
    t = ATTN_TILE
    qi = pl.program_id(1)
    q0 = pl.multiple_of(qi * t, t)
    m_ref[...] = jnp.full_like(m_ref, NEG_INF)
    acc_ref[...] = jnp.zeros_like(acc_ref)
    ones3 = jnp.where(lax.broadcasted_iota(jnp.int32, (HEAD_DIM, t), 0) < 3, 1.0, 0.0).astype(BF16)
    qaug = [jnp.concatenate([qt_ref[h * HEAD_DIM:(h + 1) * HEAD_DIM, :], ones3], axis=0) for h in range(N_HEADS)]
    cb = [cumt_ref[h:h + 1, pl.ds(q0, t)] for h in range(N_HEADS)]

    def step(j, masked):
        ks = pl.multiple_of(j * t, t)
        if masked:
            keep = lax.broadcasted_iota(jnp.int32, (t, t), 0) <= lax.broadcasted_iota(jnp.int32, (t, t), 1)
        def scores(h):
            return jnp.dot(kaug_ref[pl.ds(ks, t), h * LANES:(h + 1) * LANES], qaug[h], preferred_element_type=F32)

        raw = [scores(h) for h in range(QK_AHEAD)]
        for h in range(N_HEADS):
            if h + QK_AHEAD < N_HEADS:
                raw.append(scores(h + QK_AHEAD))
            grp = slice(h * LANES, (h + 1) * LANES)
            st = raw[h]
            if masked:
                st = jnp.where(keep, st, NEG_INF)
            ref_j = jnp.concatenate([reft_ref[h:h + 1, pl.ds(ks, LANES)]] * (t // LANES), axis=1)
            crow = cb[h] - ref_j
            m_old = m_ref[h]
            m_new = jnp.maximum(m_old, jnp.max(st, axis=0, keepdims=True) + crow)
            pt = jnp.exp2(st - (m_new - crow)).astype(BF16)
            acc_ref[h] = acc_ref[h] * jnp.exp2(m_old - m_new) + jnp.dot(
                vtaug_ref[grp, pl.ds(ks, t)], pt, preferred_element_type=F32)
            m_ref[h] = m_new

    lax.fori_loop(0, qi, lambda j, c: (step(j, False), c)[1], 0)
    step(qi, True)

    outs = []
    for h in range(N_HEADS):
        acc = acc_ref[h]
        outs.append(acc[:HEAD_DIM] / acc[HEAD_DIM:HEAD_DIM + 1])
    o_ref[...] = jnp.concatenate(outs, axis=0).T.astype(o_ref.dtype)


def _fox_prompt(qt, kaug, vtaug, cumt, reft, *, seq):
    nb = qt.shape[0]
    t = ATTN_TILE
    nq = seq // t
    aug = N_HEADS * LANES
    per_b = lambda r: pl.BlockSpec((None, r, seq), lambda b, i: (b, 0, 0))
    return pl.pallas_call(
        _fox_prompt_kernel,
        out_shape=jax.ShapeDtypeStruct((nb * seq, D_ATTN), BF16),
        grid=(nb, nq),
        in_specs=[pl.BlockSpec((None, D_ATTN, t), lambda b, i: (b, 0, i)),
                  pl.BlockSpec((seq, aug), lambda b, i: (b, 0)),
                  per_b(aug), per_b(N_HEADS), per_b(N_HEADS)],
        out_specs=pl.BlockSpec((t, D_ATTN), lambda b, i: (b * nq + i, 0)),
        scratch_shapes=[pltpu.VMEM((N_HEADS, 1, t), F32), pltpu.VMEM((N_HEADS, LANES, t), F32)],
        compiler_params=_params("parallel", "arbitrary"),
        name="fox_prompt",
    )(qt, kaug, vtaug, cumt, reft)


def _route(logits):
    lane = lax.broadcasted_iota(jnp.int32, logits.shape, 1)
    big = jnp.int32(LANES)
    is_g = (lane >= N_EXPERTS) & (lane < N_EXPERTS + N_GROUPS)
    gl = jnp.where(is_g, logits, NEG_INF)
    gmax = jnp.max(gl, axis=-1, keepdims=True)
    gi = jnp.min(jnp.where(gl == gmax, lane, big), axis=-1, keepdims=True) - N_EXPERTS
    pg_top = 1.0 / jnp.sum(jnp.exp(gl - gmax), axis=-1, keepdims=True)
    in_g = (lane >= gi * EXPERTS_PER_GROUP) & (lane < (gi + 1) * EXPERTS_PER_GROUP)
    el = jnp.where(in_g, logits, NEG_INF)
    m1 = jnp.max(el, axis=-1, keepdims=True)
    i1 = jnp.min(jnp.where(el == m1, lane, big), axis=-1, keepdims=True)
    el2 = jnp.where(lane == i1, NEG_INF, el)
    m2 = jnp.max(el2, axis=-1, keepdims=True)
    i2 = jnp.min(jnp.where(el2 == m2, lane, big), axis=-1, keepdims=True)
    e2 = jnp.exp(m2 - m1)
    w1 = pg_top / (1.0 + e2)
    return jnp.where(lane == i1, w1, jnp.where(lane == i2, w1 * e2, 0.0)), gi


ROUTE_CHUNK = 32
SORT_BLOCK = 512
SORTED_ROWS = 768
ROUTE_TILE = 256


def _sort_block(h2b, comb, gi, ls_ref, xs_ref, cs_ref, pos_ref, cnt_ref):
    tm = h2b.shape[0]
    lane = lax.broadcasted_iota(jnp.int32, (tm, LANES), 1)
    onehot = jnp.where(lane == gi, 1.0, 0.0)
    before = jnp.dot(ls_ref[...], onehot.astype(BF16), preferred_element_type=F32)
    tot = before[tm - 1:tm] + onehot[tm - 1:tm]
    cnt_ref[0] = tot
    padded = jnp.floor((tot + (ROUTE_CHUNK - 1)) * (1.0 / ROUTE_CHUNK)) * ROUTE_CHUNK
    lane1 = lax.broadcasted_iota(jnp.int32, (1, LANES), 1)
    start = jnp.zeros_like(padded)
    for s in range(1, N_GROUPS):
        start = start + jnp.where(lane1 >= s, pltpu.roll(padded, s, 1), 0.0)
    pos = jnp.sum(onehot * (start + before), axis=-1, keepdims=True)
    pos_ref[...] = pos
    pos_row = jnp.broadcast_to(pos, (tm, LANES)).T[0:1]
    rows = lax.broadcasted_iota(jnp.int32, (SORTED_ROWS, tm), 0).astype(F32)
    perm = jnp.where(rows == pos_row, 1.0, 0.0).astype(BF16)
    xs_ref[...] = jnp.dot(perm, h2b, preferred_element_type=F32).astype(BF16)
    rel = jnp.zeros_like(comb)
    for g in range(N_GROUPS):
        moved = comb if g == 0 else pltpu.roll(comb, LANES - g * EXPERTS_PER_GROUP, 1)
        rel = rel + jnp.where(gi == g, moved, 0.0)
    rel = jnp.where(lane < EXPERTS_PER_GROUP, rel, 0.0)
    hi = rel.astype(BF16)
    lo = (rel - hi.astype(F32)).astype(BF16)
    cs_ref[...] = (jnp.dot(perm, hi, preferred_element_type=F32) + jnp.dot(perm, lo, preferred_element_type=F32))


def _merge_kernel(a_ref, gm_ref, sga_ref, sgg_ref, x_ref, g1_ref, sc2_ref, sh2_ref, wao_ref, wgo_ref, wo_ref,
                  l1g_ref, l1b_ref, wr_ref, br_ref, *rest, alpha, precise, sort):
    t = (sga_ref[...].astype(F32) * _dot(a_ref[...], wao_ref[...], precise)
         + sgg_ref[...].astype(F32) * _dot(gm_ref[...], wgo_ref[...], precise))
    m = _dot(t, wo_ref[...], precise)
    x1 = _ln(alpha * x_ref[...] + (1.0 + _mod(g1_ref)) * m, l1g_ref[...], l1b_ref[...])
    h2 = x1 * (1.0 + _mod(sc2_ref)) + _mod(sh2_ref)
    comb, gi = _route(_dot(h2, wr_ref[...], precise) + br_ref[...])
    if sort:
        ls_ref, x1_ref, xs_ref, cs_ref, pos_ref, cnt_ref = rest
        _sort_block(h2.astype(BF16), comb, gi, ls_ref, xs_ref, cs_ref, pos_ref, cnt_ref)
    else:
        x1_ref, h2_ref, comb_ref = rest
        h2_ref[...] = h2.astype(h2_ref.dtype)
        comb_ref[...] = comb
    x1_ref[...] = x1


def _merge(a, gm, sga, sgg, x, g1, sc2, sh2, wao, wgo, wo, l1g, l1b, wr, br, *, rows_per_mod, tm, alpha, precise,
           sort):
    n, d = x.shape
    row = lambda c: pl.BlockSpec((tm, c), lambda i: (i, 0))
    mod = _mod_spec(g1, tm, rows_per_mod)
    ins = [a, gm, sga, sgg, x, g1, sc2, sh2, wao, wgo, wo, l1g, l1b, wr, br]
    in_specs = [row(D_ATTN), row(gm.shape[1]), row(d), row(d), row(d), mod, mod, mod, _full(wao.shape),
                _full(wgo.shape), _full(wo.shape), _full(l1g.shape), _full(l1b.shape), _full(wr.shape),
                _full(br.shape)]
    if sort:
        assert tm == SORT_BLOCK
        nblk = n // tm
        ls = jnp.tril(jnp.ones((tm, tm), BF16), -1)
        ins.append(ls)
        in_specs.append(_full(ls.shape))
        srow = lambda c: pl.BlockSpec((SORTED_ROWS, c), lambda i: (i, 0))
        out_shape = [((n, d), F32), ((nblk * SORTED_ROWS, d), BF16), ((nblk * SORTED_ROWS, LANES), F32),
                     ((n, 1), F32), ((nblk, 1, LANES), F32)]
        out_specs = [row(d), srow(d), srow(LANES), row(1), pl.BlockSpec((1, 1, LANES), lambda i: (i, 0, 0))]
    else:
        out_shape = [((n, d), F32), ((n, d), BF16), ((n, LANES), F32)]
        out_specs = [row(d), row(d), row(LANES)]
    return pl.pallas_call(
        functools.partial(_merge_kernel, alpha=alpha, precise=precise, sort=sort),
        out_shape=[jax.ShapeDtypeStruct(s, t) for s, t in out_shape],
        grid=(n // tm,),
        in_specs=in_specs,
        out_specs=out_specs,
        compiler_params=_params("parallel"),
        name="merge_precise" if precise else "merge",
    )(*ins)


def _route_tables(counts, nblk):
    ch, cpb, cpt = ROUTE_CHUNK, SORTED_ROWS // ROUTE_CHUNK, ROUTE_TILE // ROUTE_CHUNK
    max_tiles = (nblk * (SORT_BLOCK + N_GROUPS * (ch - 1))) // ROUTE_TILE + N_GROUPS + 1
    cnt = counts[:, 0, :N_GROUPS].astype(jnp.int32)
    nch = (cnt + ch - 1) // ch
    loc = jnp.cumsum(nch, axis=1) - nch
    earlier = jnp.cumsum(nch, axis=0) - nch
    tiles_g = (nch.sum(0) + cpt - 1) // cpt
    tile_off = jnp.cumsum(tiles_g) - tiles_g
    n_tiles = tiles_g.sum()
    k = jnp.arange(cpb, dtype=jnp.int32)[None, None, :]
    used = (k >= loc[:, :, None]) & (k < (loc + nch)[:, :, None])
    glob = (tile_off * cpt)[None, :, None] + earlier[:, :, None] + k - loc[:, :, None]
    dst = jnp.sum(jnp.where(used, glob, 0), axis=1)
    blk_chunk = jnp.arange(nblk, dtype=jnp.int32)[:, None] * cpb + k[0]
    scatter_to = jnp.where(used.any(axis=1), dst, max_tiles * cpt)
    src = jnp.full((max_tiles * cpt,), cpb - 1, jnp.int32).at[scatter_to.reshape(-1)].set(
        blk_chunk.reshape(-1), mode="drop")
    t = jnp.arange(max_tiles, dtype=jnp.int32)
    tile_group = jnp.minimum(jnp.sum(t[:, None] >= (tile_off + tiles_g)[None, :], axis=1), N_GROUPS - 1)
    return src, dst.reshape(-1), tile_group.astype(jnp.int32), n_tiles.reshape(1).astype(jnp.int32), max_tiles


def _experts_kernel(src_ref, grp_ref, nt_ref, *refs):
    del src_ref, grp_ref
    cpt = ROUTE_TILE // ROUTE_CHUNK
    xr, cr = refs[:cpt], refs[cpt:2 * cpt]
    wg_ref, wu_ref, wd_ref, o_ref = refs[2 * cpt:]
    live = pl.program_id(0) < nt_ref[0]

    @pl.when(live)
    def _():
        x = jnp.concatenate([r[...] for r in xr], axis=0)
        c = jnp.concatenate([r[...] for r in cr], axis=0)
        hg = jnp.dot(x, wg_ref[0], preferred_element_type=F32)
        hu = jnp.dot(x, wu_ref[0], preferred_element_type=F32)
        de = hg.shape[1] // EXPERTS_PER_GROUP
        a = jnp.concatenate([(hg[:, e * de:(e + 1) * de] * jax.nn.sigmoid(hg[:, e * de:(e + 1) * de])
                              * hu[:, e * de:(e + 1) * de] * c[:, e:e + 1]).astype(BF16)
                             for e in range(EXPERTS_PER_GROUP)], axis=1)
        o_ref[...] = jnp.dot(a, wd_ref[0], preferred_element_type=F32).astype(o_ref.dtype)

    @pl.when(jnp.logical_not(live))
    def _():
        o_ref[...] = jnp.zeros_like(o_ref)


def _experts(src, tile_group, n_tiles, max_tiles, xs, cs, wg4, wu4, wd4):
    d = xs.shape[1]
    cpt = ROUTE_TILE // ROUTE_CHUNK

    def chunk(width):
        def one(r):
            return pl.BlockSpec((ROUTE_CHUNK, width), lambda t, src, grp, nt: (src[t * cpt + r], 0))
        return [one(r) for r in range(cpt)]

    wspec = lambda w: pl.BlockSpec((1,) + w.shape[1:], lambda t, src, grp, nt: (grp[t], 0, 0))
    grid_spec = pltpu.PrefetchScalarGridSpec(
        num_scalar_prefetch=3,
        grid=(max_tiles,),
        in_specs=chunk(d) + chunk(LANES) + [wspec(wg4), wspec(wu4), wspec(wd4)],
        out_specs=pl.BlockSpec((ROUTE_TILE, d), lambda t, src, grp, nt: (t, 0)),
    )
    return pl.pallas_call(
        _experts_kernel,
        out_shape=jax.ShapeDtypeStruct((max_tiles * ROUTE_TILE, d), BF16),
        grid_spec=grid_spec,
        compiler_params=_params("arbitrary"),
        name="experts",
    )(src, tile_group, n_tiles, *([xs] * cpt), *([cs] * cpt), wg4, wu4, wd4)


def _unsort_kernel(dst_ref, *refs, alpha):
    del dst_ref
    cpb = SORTED_ROWS // ROUTE_CHUNK
    yr = refs[:cpb]
    pos_ref, x1_ref, g2_ref, l2g_ref, l2b_ref, o_ref = refs[cpb:]
    ys = jnp.concatenate([r[...] for r in yr], axis=0)
    tm = pos_ref.shape[0]
    cols = lax.broadcasted_iota(jnp.int32, (tm, SORTED_ROWS), 1).astype(F32)
    perm_t = jnp.where(cols == pos_ref[...], 1.0, 0.0).astype(BF16)
    f = jnp.dot(perm_t, ys, preferred_element_type=F32)
    o_ref[...] = _ln(alpha * x1_ref[...] + (1.0 + _mod(g2_ref)) * f, l2g_ref[...], l2b_ref[...])


def _unsort(dst, y, pos, x1, g2, l2g, l2b, *, rows_per_mod, alpha):
    n, d = x1.shape
    tm = SORT_BLOCK
    cpb = SORTED_ROWS // ROUTE_CHUNK

    def chunk(r):
        return pl.BlockSpec((ROUTE_CHUNK, d), lambda i, dst: (dst[i * cpb + r], 0))

    row = lambda c: pl.BlockSpec((tm, c), lambda i, dst: (i, 0))
    full = lambda s: pl.BlockSpec(s, lambda i, dst: (0,) * len(s))
    grid_spec = pltpu.PrefetchScalarGridSpec(
        num_scalar_prefetch=1,
        grid=(n // tm,),
        in_specs=[chunk(r) for r in range(cpb)] + [
            row(1), row(d), pl.BlockSpec((1, 1, d), lambda i, dst: (i * tm // rows_per_mod, 0, 0)),
            full(l2g.shape), full(l2b.shape)],
        out_specs=row(d),
    )
    return pl.pallas_call(
        functools.partial(_unsort_kernel, alpha=alpha),
        out_shape=jax.ShapeDtypeStruct((n, d), F32),
        grid_spec=grid_spec,
        compiler_params=_params("parallel"),
        name="unsort",
    )(dst, *([y] * cpb), pos, x1, g2, l2g, l2b)


def _moe_kernel(h2_ref, comb_ref, x1_ref, g2_ref, wg_ref, wu_ref, wd_ref, l2g_ref, l2b_ref, o_ref, *, alpha):
    h2 = h2_ref[...]
    comb = comb_ref[...]
    acc = jnp.zeros(o_ref.shape, F32)
    for e in range(N_EXPERTS):
        hg = jnp.dot(h2, wg_ref[e], preferred_element_type=F32)
        hu = jnp.dot(h2, wu_ref[e], preferred_element_type=F32)
        a = hg * jax.nn.sigmoid(hg) * hu * comb[:, e:e + 1]
        acc = acc + jnp.dot(a.astype(BF16), wd_ref[e], preferred_element_type=F32)
    o_ref[...] = _ln(alpha * x1_ref[...] + (1.0 + _mod(g2_ref)) * acc, l2g_ref[...], l2b_ref[...])


def _moe(h2, comb, x1, g2, wg, wu, wd, l2g, l2b, *, rows_per_mod, tm, alpha):
    n, d = x1.shape
    row = lambda c: pl.BlockSpec((tm, c), lambda i: (i, 0))
    mod = _mod_spec(g2, tm, rows_per_mod)
    once =lambda s: pl.BlockSpec(s, lambda i: (0,) * len(s), pipeline_mode=pl.Buffered(1))
    return pl.pallas_call(
        functools.partial(_moe_kernel, alpha=alpha),
        out_shape=jax.ShapeDtypeStruct((n, d), F32),
        grid=(n // tm,),
        in_specs=[row(d), row(LANES), row(d), mod, once(wg.shape), once(wu.shape), once(wd.shape),
                  _full(l2g.shape), _full(l2b.shape)],
        out_specs=row(d),
        compiler_params=_params("parallel"),
        name="moe",
    )(h2, comb, x1, g2, wg, wu, wd, l2g, l2b)


def _sample_proj_kernel(x_ref, sc_ref, sh_ref, w_ref, z_ref):
    z_ref[...] = _dot(x_ref[...] * (1.0 + sc_ref[...]) + sh_ref[...], w_ref[...], True)


def _sample_proj(x, sc, sh, w, *, tn):
    n, d = x.shape
    return pl.pallas_call(
        _sample_proj_kernel,
        out_shape=jax.ShapeDtypeStruct((n, w.shape[1]), F32),
        grid=(w.shape[1] // tn,),
        in_specs=[_full((n, d))] * 3 + [pl.BlockSpec((d, tn), lambda j: (0, j))],
        out_specs=pl.BlockSpec((n, tn), lambda j: (0, j)),
        compiler_params=_params("parallel"),
        name="sample_proj",
    )(x, sc, sh, w)


def _sample_mix_kernel(z_ref, bf_ref, glg_ref, glb_ref, mg_ref, bs_ref, mc_ref,
                       q_ref, k_ref, v_ref, logf_ref, cn_ref, gv_ref, gm_ref, sga_ref, sgg_ref):
    z = z_ref[...]
    sec = lambda off, width: z[:, off:off + width]
    q, k, v, logf, u, gv, sga, sgg = _mixer_epilogue(
        sec(_OFF_Q, 512), sec(_OFF_K, 512), sec(_OFF_V, 512), sec(_OFF_F, LANES), sec(_OFF_U, 512),
        sec(_OFF_GV, 512), sec(_OFF_GA, 1024), sec(_OFF_GG, 1024), bf_ref[...], glg_ref[...], glb_ref[...])
    q_ref[...] = q
    k_ref[...] = k
    v_ref[...] = v
    logf_ref[...] = logf
    cn_ref[...] = _dot(mc_ref[...], logf, True)
    gv_ref[...] = gv
    sga_ref[...] = sga
    sgg_ref[...] = sgg
    gw = gv.shape[1] // GMLP_GROUPS
    sp = jnp.concatenate([_dot(mg_ref[g], gv[:, g * gw:(g + 1) * gw], True) for g in range(GMLP_GROUPS)], axis=-1)
    gm_ref[...] = u * (sp + bs_ref[...])


def _sample_mix(z, bf, glg, glb, mg, bs, mc):
    n = z.shape[0]
    shapes = [(n, 512)] * 3 + [(n, LANES)] * 2 + [(n, 512)] * 2 + [(n, 1024)] * 2
    return pl.pallas_call(
        _sample_mix_kernel,
        out_shape=[jax.ShapeDtypeStruct(s, F32) for s in shapes],
        compiler_params=pltpu.CompilerParams(vmem_limit_bytes=VMEM_LIMIT_BYTES),
        name="sample_mix",
    )(z, bf, glg, glb, mg, bs, mc)


PAGES_PER_STEP = 16


def _fox_sample_kernel(pt_ref, *refs, n_chunks, n_new):
    del pt_ref
    P = PAGES_PER_STEP
    kp, vp, lp = refs[:P], refs[P:2 * P], refs[2 * P:3 * P]
    (qbd_ref, cn_ref, kn_ref, vn_ref, cnb_ref, o_ref, m_ref, l_ref, acc_ref, later_ref) = refs[3 * P:]
    c = pl.program_id(1)

    @pl.when(c == 0)
    def _():
        m_ref[...] = jnp.full_like(m_ref, NEG_INF)
        l_ref[...] = jnp.zeros_like(l_ref)
        acc_ref[...] = jnp.zeros_like(acc_ref)
        later_ref[...] = jnp.zeros_like(later_ref)

    qbd = qbd_ref[0]
    cn = cn_ref[0]
    nt = (((1,), (1,)), ((), ()))

    def update(s, vs):
        m_old = m_ref[...]
        m_new = m_old
        for sb in s:
            m_new = jnp.maximum(m_new, jnp.max(sb, axis=-1, keepdims=True))
        alpha = jnp.exp(m_old - m_new)
        l_new = l_ref[...] * alpha
        acc = acc_ref[...] * alpha
        for sb, vb in zip(s, vs):
            p = jnp.exp(sb - m_new)
            l_new = l_new + jnp.sum(p, axis=-1, keepdims=True)
            acc = acc + lax.dot_general(p, vb, nt, preferred_element_type=F32)
        m_ref[...] = m_new
        l_ref[...] = l_new
        acc_ref[...] = acc

    lane = lax.broadcasted_iota(jnp.int32, (N_HEADS, PAGE_SIZE), 1)
    later = later_ref[...]
    bias = [None] * P
    for r in reversed(range(P)):
        y = lp[r][0]
        for k in range(7):
            sh = 1 << k
            y = y + jnp.where(lane < PAGE_SIZE - sh, pltpu.roll(y, PAGE_SIZE - sh, 1), 0.0)
        bias[r] = jnp.where(lane < PAGE_SIZE - 1, pltpu.roll(y, PAGE_SIZE - 1, 1), 0.0) + later
        later = later + y[:, 0:1]
    later_ref[...] = later

    scores = []
    for r in range(P):
        st = jnp.dot(qbd, kp[r][0].reshape(D_ATTN, PAGE_SIZE), preferred_element_type=F32)
        scores.append(st + jnp.concatenate([bias[r]] * n_new, axis=0) + cn)
    update(scores, [vp[r][0].reshape(D_ATTN, PAGE_SIZE) for r in range(P)])

    @pl.when(c == n_chunks - 1)
    def _():
        st = jnp.dot(qbd, kn_ref[0], preferred_element_type=F32) + cn - cnb_ref[0]
        col = lax.broadcasted_iota(jnp.int32, st.shape, 1)
        row = lax.broadcasted_iota(jnp.int32, st.shape, 0)
        update([jnp.where(col <= row // N_HEADS, st, NEG_INF)], [vn_ref[0]])
        full = acc_ref[...] / l_ref[...]
        lane_head = lax.broadcasted_iota(jnp.int32, full.shape, 1) // HEAD_DIM
        row_head = lax.broadcasted_iota(jnp.int32, full.shape, 0) % N_HEADS
        own = jnp.where(lane_head == row_head, full, 0.0)
        rows = [jnp.sum(own[q * N_HEADS:(q + 1) * N_HEADS], axis=0, keepdims=True) for q in range(n_new)]
        o_ref[0] = jnp.concatenate(rows + [jnp.zeros((o_ref.shape[1] - n_new, D_ATTN), F32)], axis=0)


def _fox_sample(page_table, kt_pool, vt_pool, logf_pool, qbd, cn, kn, vn, cnb, *, n_new):
    nb, n_pages = page_table.shape
    P = PAGES_PER_STEP
    n_chunks = n_pages // P
    nrow = n_new * N_HEADS

    def page_spec(shape):
        def one(r):
            return pl.BlockSpec((1,) + shape,
                                lambda b, c, pt: (pt[b, (n_chunks - 1 - c) * P + r],) + (0,) * len(shape))
        return [one(r) for r in range(P)]

    per_b = lambda s: pl.BlockSpec((1,) + s, lambda b, c, pt: (b,) + (0,) * len(s))
    kv_page = (N_HEADS, HEAD_DIM, PAGE_SIZE)
    grid_spec = pltpu.PrefetchScalarGridSpec(
        num_scalar_prefetch=1,
        grid=(nb, n_chunks),
        in_specs=page_spec(kv_page) + page_spec(kv_page) + page_spec((N_HEADS, PAGE_SIZE)) + [
            per_b((nrow, D_ATTN)), per_b((nrow, 1)), per_b((D_ATTN, LANES)), per_b((D_ATTN, LANES)),
            per_b((nrow, LANES))],
        out_specs=per_b((8, D_ATTN)),
        scratch_shapes=[pltpu.VMEM((nrow, 1), F32), pltpu.VMEM((nrow, 1), F32), pltpu.VMEM((nrow, D_ATTN), F32),
                        pltpu.VMEM((N_HEADS, PAGE_SIZE), F32)],
    )
    return pl.pallas_call(
        functools.partial(_fox_sample_kernel, n_chunks=n_chunks, n_new=n_new),
        out_shape=jax.ShapeDtypeStruct((nb, 8, D_ATTN), F32),
        grid_spec=grid_spec,
        compiler_params=_params("parallel", "arbitrary"),
        name="fox_sample",
    )(page_table, *([kt_pool] * P), *([vt_pool] * P), *([logf_pool] * P), qbd, cn, kn, vn, cnb)


def _reorder_w_in(w_in):
    d = w_in.shape[0]
    f0 = 3 * D_ATTN
    pad = jnp.zeros((d, LANES - N_HEADS), w_in.dtype)
    return jnp.concatenate([w_in[:, :f0], w_in[:, f0 + N_HEADS:], w_in[:, f0:f0 + N_HEADS], pad], axis=1)


def kernel(x_prompt, x_sample, c_prompt, c_sample, cache_k, cache_v, cache_logf, page_table, w_ada, b_ada, w_in,
           b_f, gmlp_ln_g, gmlp_ln_b, w_s, b_s, w_attn_out, w_gmlp_out, w_o, ln1_g, ln1_b, w_group_router,
           b_group_router, w_expert_router, b_expert_router, w_exp_gate, w_exp_up, w_exp_down, ln2_g, ln2_b):
    depth = w_ada.shape[0]
    assert depth == 1
    nbp, seq, d = x_prompt.shape
    nbs, n_new, _ = x_sample.shape
    alpha = (2.0 * depth) ** 0.25
    n_pool = cache_k.shape[1]
    d_gmlp = gmlp_ln_g.shape[1]
    gw = d_gmlp // GMLP_GROUPS

    w_in_r = _reorder_w_in(w_in[0])
    w_in_b = w_in_r.astype(BF16)
    bf = jnp.pad(b_f[0], (0, LANES - N_HEADS))[None]
    glg, glb = gmlp_ln_g[0][None], gmlp_ln_b[0][None]
    tril = jnp.tril(jnp.ones((CHUNK, CHUNK), F32))
    ws = jnp.where(tril > 0, w_s[0], 0.0)
    bs_tile = jnp.repeat(b_s[0].T, gw, axis=1)
    w_r = jnp.pad(jnp.concatenate([w_expert_router[0], w_group_router[0]], axis=1),
                  ((0, 0), (0, LANES - N_EXPERTS - N_GROUPS)))
    b_r = jnp.pad(jnp.concatenate([b_expert_router[0], b_group_router[0]]), (0, LANES - N_EXPERTS - N_GROUPS))[None]
    l1g, l1b, l2g, l2b = ln1_g[0][None], ln1_b[0][None], ln2_g[0][None], ln2_b[0][None]
    wg_b, wu_b, wd_b = w_exp_gate[0].astype(BF16), w_exp_up[0].astype(BF16), w_exp_down[0].astype(BF16)

    def group_weights(wg, wu, wd):
        de = wg.shape[2]
        side = lambda w: w.reshape(N_GROUPS, EXPERTS_PER_GROUP, d, de).transpose(0, 2, 1, 3).reshape(
            N_GROUPS, d, EXPERTS_PER_GROUP * de)
        return side(wg), side(wu), wd.reshape(N_GROUPS, EXPERTS_PER_GROUP * de, d)

    nc = nbp + nbs
    c_all = jnp.pad(jnp.concatenate([c_prompt, c_sample], axis=0), ((0, -nc % 8), (0, 0)))
    mod = _ada(c_all, w_ada[0], b_ada[0][None])
    modp = [m[:, None, :] for m in jnp.split(mod[:nbp], 6, axis=-1)]
    mods = [m[:, None, :] for m in jnp.split(mod[nbp:nc], 6, axis=-1)]

    xp = x_prompt.reshape(nbp * seq, d)
    qt, kt32, kaug, vt32, vtaug, logft, cumt, reft, gm, sga, sgg = _prompt_in(
        xp, modp[1], modp[0], w_in_b, bf, glg, glb, ws.astype(BF16), bs_tile, tril, seq=seq, tm=ATTN_TILE)
    a = _fox_prompt(qt, kaug, vtaug, cumt, reft, seq=seq)
    x1, xs_sorted, cs_sorted, pos, counts = _merge(
        a, gm, sga, sgg, xp, modp[2], modp[4], modp[3], w_attn_out[0].astype(BF16), w_gmlp_out[0].astype(BF16),
        w_o[0].astype(BF16), l1g, l1b, w_r.astype(BF16), b_r, rows_per_mod=seq, tm=SORT_BLOCK, alpha=alpha,
        precise=False, sort=True)
    src, dst, tile_group, n_tiles, max_tiles = _route_tables(counts, nbp * seq // SORT_BLOCK)
    y_sorted = _experts(src, tile_group, n_tiles, max_tiles, xs_sorted, cs_sorted, *group_weights(wg_b, wu_b, wd_b))
    yp = _unsort(dst, y_sorted, pos, x1, modp[5], l2g, l2b, rows_per_mod=seq, alpha=alpha)

    ns = nbs * n_new
    xs = x_sample.reshape(ns, d)
    rep = lambda m: jnp.repeat(m[:, 0, :], n_new, axis=0)
    w_in_s = jnp.pad(w_in_r, ((0, 0), (0, -w_in_r.shape[1] % 896)))
    z = _sample_proj(xs, rep(mods[1]), rep(mods[0]), w_in_s, tn=896)
    eye_b = jnp.eye(nbs, dtype=F32)
    mg = jnp.stack([jnp.kron(eye_b, ws[g, :n_new, :n_new]) for g in range(GMLP_GROUPS)])
    bs_rows = jnp.tile(bs_tile[:n_new], (nbs, 1))
    mc = jnp.kron(eye_b, tril[:n_new, :n_new])
    qs, ks, vs, logf_s, cn, gv_s, gm_s, sga_s, sgg_s = _sample_mix(z, bf, glg, glb, mg, bs_rows, mc)

    kt_pool = cache_k[0].transpose(0, 2, 3, 1)
    vt_pool = cache_v[0].transpose(0, 2, 3, 1)
    logf_pool = cache_logf[0].transpose(0, 2, 1)
    nrow = n_new * N_HEADS
    q4 = qs.reshape(nbs, n_new, N_HEADS, HEAD_DIM)
    qbd = jnp.einsum("bqhd,hg->bqhgd", q4, jnp.eye(N_HEADS, dtype=F32)).reshape(nbs, nrow, D_ATTN)
    cn3 = cn[:, :N_HEADS].reshape(nbs, n_new, N_HEADS)
    cn_col = cn3.reshape(nbs, nrow, 1)
    cnb = jnp.pad(jnp.tile(cn3.transpose(0, 2, 1), (1, n_new, 1)), ((0, 0), (0, 0), (0, LANES - n_new)))
    new_t = lambda t: jnp.pad(t.reshape(nbs, n_new, D_ATTN).transpose(0, 2, 1), ((0, 0), (0, 0), (0, LANES - n_new)))
    a_s = _fox_sample(page_table, kt_pool, vt_pool, logf_pool, qbd, cn_col, new_t(ks), new_t(vs), cnb, n_new=n_new)
    a_s = a_s[:, :n_new].reshape(ns, D_ATTN)
    x1s, h2s, comb_s = _merge(a_s, gm_s, sga_s, sgg_s, xs, rep(mods[2]), rep(mods[4]), rep(mods[3]),
                              w_attn_out[0], w_gmlp_out[0], w_o[0], l1g, l1b, w_r, b_r,
                              rows_per_mod=ns, tm=ns, alpha=alpha, precise=True, sort=False)
    ys = _moe(h2s, comb_s, x1s, rep(mods[5]), wg_b, wu_b, wd_b, l2g, l2b, rows_per_mod=ns, tm=ns, alpha=alpha)

    hs = (N_HEADS, HEAD_DIM)
    untr = lambda t: t.reshape(1, nbp, *hs, seq).transpose(0, 1, 4, 2, 3)
    return (yp.reshape(nbp, seq, d), ys.reshape(nbs, n_new, d), untr(kt32), untr(vt32),
            logft.reshape(1, nbp, N_HEADS, seq).transpose(0, 1, 3, 2),
            ks.reshape(1, nbs, n_new, *hs), vs.reshape(1, nbs, n_new, *hs),
            logf_s[:, :N_HEADS].reshape(1, nbs, n_new, N_HEADS), gv_s.reshape(1, nbs, n_new, d_gmlp))
```

```python
import functools

import numpy as np
import jax
import jax.numpy as jnp
from jax import lax
from jax.experimental import pallas as pl
from jax.experimental.pallas import tpu as pltpu

F32 = jnp.float32
BF16 = jnp.bfloat16
HIGHEST = lax.Precision.HIGHEST

N_HEADS = 8
HEAD_DIM = 64
D_ATTN = N_HEADS * HEAD_DIM
PAGE_SIZE = 128
CHUNK = 128
GMLP_GROUPS = 4
N_GROUPS = 4
EXPERTS_PER_GROUP = 4
N_EXPERTS = N_GROUPS * EXPERTS_PER_GROUP
LN_EPS = 1e-5
LANES = 128
NEG_INF = float("-inf")
LOG2E = 1.4426950408889634

VMEM_LIMIT_BYTES = 56 * 1024 * 1024


def _params(*sem):
    return pltpu.CompilerParams(dimension_semantics=sem, vmem_limit_bytes=VMEM_LIMIT_BYTES)


def _full(shape):
    n = len(shape)
    return pl.BlockSpec(shape, lambda *_: (0,) * n)


def _ln(x, g, b):
    mu = jnp.mean(x, axis=-1, keepdims=True)
    xc = x - mu
    var = jnp.mean(xc * xc, axis=-1, keepdims=True)
    return xc * lax.rsqrt(var + LN_EPS) * g + b


def _mod(ref):
    return ref[0] if len(ref.shape) == 3 else ref[...]


def _mod_spec(m, tm, rows_per_mod):
    if m.ndim == 3:
        return pl.BlockSpec((1, 1, m.shape[-1]), lambda i: (i * tm // rows_per_mod, 0, 0))
    return pl.BlockSpec((tm, m.shape[-1]), lambda i: (i, 0))


def _dot(a, b, precise):
    if precise:
        return jnp.dot(a, b, precision=HIGHEST, preferred_element_type=F32)
    return jnp.dot(a.astype(BF16), b.astype(BF16), preferred_element_type=F32)


def _ada_kernel(c_ref, w_ref, b_ref, o_ref):
    c = c_ref[...]
    o_ref[...] = _dot(c * jax.nn.sigmoid(c), w_ref[...], True) + b_ref[...]


def _ada(c, w, b, *, tn=1024):
    n, d = c.shape
    dout = w.shape[1]
    return pl.pallas_call(
        _ada_kernel,
        out_shape=jax.ShapeDtypeStruct((n, dout), F32),
        grid=(dout // tn,),
        in_specs=[_full((n, d)), pl.BlockSpec((d, tn), lambda j: (0, j)), pl.BlockSpec((1, tn), lambda j: (0, j))],
        out_specs=pl.BlockSpec((n, tn), lambda j: (0, j)),
        compiler_params=_params("parallel"),
        name="ada",
    )(c, w, b)


_OFF_Q, _OFF_K, _OFF_V, _OFF_U, _OFF_GV = 0, 512, 1024, 1536, 2048
_OFF_GA, _OFF_GG, _OFF_F, _W_COLS = 2560, 3584, 4608, 4736
SAMPLE_PROJ_TN = 896


def _mixer_epilogue(zq, zk, zv, zf, zu, zgv, zga, zgg, bf, glg, glb):
    q = zq * (HEAD_DIM ** -0.5)
    logf = jax.nn.log_sigmoid(zf + bf)
    u = jax.nn.gelu(zu)
    gv = _ln(jax.nn.gelu(zgv), glg, glb)
    return q, zk, zv, logf, u, gv, jax.nn.sigmoid(zga), jax.nn.sigmoid(zgg)


def _prompt_in_kernel(x_ref, sc_ref, sh_ref, w_ref, bf_ref, glg_ref, glb_ref, ws_ref, bs_ref, tri_ref,
                      qt_ref, kt32_ref, kaug_ref, vt32_ref, vtaug_ref, logft_ref, cumt_ref, reft_ref, gm_ref,
                      sga_ref, sgg_ref, carry_ref, sp_ref, *, tiles_per_seq):
    tm = x_ref.shape[0]
    h = (x_ref[...] * (1.0 + sc_ref[0]) + sh_ref[0]).astype(BF16)

    def proj(off, width):
        return jnp.dot(h, w_ref[:, off:off + width], preferred_element_type=F32)

    q, k, v, logf, u, gv, sga, sgg = _mixer_epilogue(
        proj(_OFF_Q, 512), proj(_OFF_K, 512), proj(_OFF_V, 512), proj(_OFF_F, LANES), proj(_OFF_U, 512),
        proj(_OFF_GV, 512), proj(_OFF_GA, 1024), proj(_OFF_GG, 1024), bf_ref[...], glg_ref[...], glb_ref[...])
    qt_ref[0] = (q * LOG2E).T.astype(BF16)
    kt32_ref[0] = k.T
    vt = v.T
    vt32_ref[0] = vt
    sga_ref[...] = sga.astype(BF16)
    sgg_ref[...] = sgg.astype(BF16)
    logft_ref[0] = logf.T[:N_HEADS]

    @pl.when(pl.program_id(0) % tiles_per_seq == 0)
    def _():
        carry_ref[...] = jnp.zeros_like(carry_ref)

    carry_in = carry_ref[...]
    carry = carry_in
    parts = []
    for n in range(tm // CHUNK):
        c = _dot(tri_ref[...], logf[n * CHUNK:(n + 1) * CHUNK], True) + carry
        parts.append(c)
        carry = c[CHUNK - 1:CHUNK]
    carry_ref[...] = carry
    cum = jnp.concatenate(parts, axis=0)
    cumt_ref[0] = (cum * LOG2E).T[:N_HEADS]
    reft_ref[0] = (jnp.broadcast_to(carry_in, cum.shape) * LOG2E).T[:N_HEADS]

    neg = (carry_in - cum) * LOG2E
    hi = neg.astype(BF16).astype(F32)
    mid = (neg - hi).astype(BF16).astype(F32)
    lo = (neg - hi - mid).astype(BF16).astype(F32)
    lane = lax.broadcasted_iota(jnp.int32, (tm, LANES), 1)
    for hd in range(N_HEADS):
        src = k[:, (hd // 2) * LANES:(hd // 2 + 1) * LANES]
        if hd % 2:
            src = pltpu.roll(src, HEAD_DIM, 1)
        blk = jnp.where(lane < HEAD_DIM, src, 0.0)
        for i, piece in enumerate((hi, mid, lo)):
            blk = jnp.where(lane == HEAD_DIM + i, piece[:, hd:hd + 1], blk)
        kaug_ref[:, hd * LANES:(hd + 1) * LANES] = blk.astype(BF16)

    ones_rows = jnp.where(lax.broadcasted_iota(jnp.int32, (HEAD_DIM, tm), 0) == 0, 1.0, 0.0)
    for hd in range(N_HEADS):
        grp = jnp.concatenate([vt[hd * HEAD_DIM:(hd + 1) * HEAD_DIM], ones_rows], axis=0)
        vtaug_ref[0, hd * LANES:(hd + 1) * LANES, :] = grp.astype(BF16)

    gvb = gv.astype(BF16)
    gw = gv.shape[1] // GMLP_GROUPS
    for n in range(tm // CHUNK):
        for g in range(GMLP_GROUPS):
            sp_ref[n * CHUNK:(n + 1) * CHUNK, g * gw:(g + 1) * gw] = jnp.dot(
                ws_ref[g], gvb[n * CHUNK:(n + 1) * CHUNK, g * gw:(g + 1) * gw], preferred_element_type=F32)
    bs = jnp.concatenate([bs_ref[...]] * (tm // CHUNK), axis=0)
    gm_ref[...] = (u * (sp_ref[...] + bs)).astype(BF16)


def _prompt_in(x, sc, sh, w, bf, glg, glb, ws, bs, tri, *, seq, tm):
    n, d = x.shape
    nb = n // seq
    tps = seq // tm
    row = lambda c: pl.BlockSpec((tm, c), lambda i: (i, 0))
    mod = pl.BlockSpec((1, 1, d), lambda i: (i // tps, 0, 0))
    tr = lambda r: pl.BlockSpec((1, r, tm), lambda i: (i // tps, 0, i % tps))
    aug = N_HEADS * LANES
    outs = [((nb, 512, seq), BF16), ((nb, 512, seq), F32), ((n, aug), BF16), ((nb, 512, seq), F32),
            ((nb, aug, seq), BF16), ((nb, N_HEADS, seq), F32), ((nb, N_HEADS, seq), F32),
            ((nb, N_HEADS, seq), F32), ((n, 512), BF16), ((n, 1024), BF16), ((n, 1024), BF16)]
    out_specs = [tr(512), tr(512), row(aug), tr(512), tr(aug), tr(N_HEADS), tr(N_HEADS), tr(N_HEADS),
                 row(512), row(1024), row(1024)]
    return pl.pallas_call(
        functools.partial(_prompt_in_kernel, tiles_per_seq=tps),
        out_shape=[jax.ShapeDtypeStruct(s, t) for s, t in outs],
        grid=(n // tm,),
        in_specs=[row(d), mod, mod, _full(w.shape), _full(bf.shape), _full(glg.shape), _full(glb.shape),
                  _full(ws.shape), _full(bs.shape), _full(tri.shape)],
        out_specs=out_specs,
        scratch_shapes=[pltpu.VMEM((1, LANES), F32), pltpu.VMEM((tm, 512), F32)],
        compiler_params=_params("arbitrary"),
        name="prompt_in",
    )(x, sc, sh, w, bf, glg, glb, ws, bs, tri)


ATTN_TILE = 512
QK_AHEAD = 2


def _fox_prompt_kernel(qt_ref, kaug_ref, vtaug_ref, cumt_ref, reft_ref, o_ref, m_ref, acc_ref):
    t = ATTN_TILE
    qi = pl.program_id(1)
    q0 = pl.multiple_of(qi * t, t)
    m_ref[...] = jnp.full_like(m_ref, NEG_INF)
    acc_ref[...] = jnp.zeros_like(acc_ref)
    ones3 = jnp.where(lax.broadcasted_iota(jnp.int32, (HEAD_DIM, t), 0) < 3, 1.0, 0.0).astype(BF16)
    qaug = [jnp.concatenate([qt_ref[h * HEAD_DIM:(h + 1) * HEAD_DIM, :], ones3], axis=0) for h in range(N_HEADS)]
    cb = [cumt_ref[h:h + 1, pl.ds(q0, t)] for h in range(N_HEADS)]

    def step(j, masked):
        ks = pl.multiple_of(j * t, t)
        if masked:
            keep = lax.broadcasted_iota(jnp.int32, (t, t), 0) <= lax.broadcasted_iota(jnp.int32, (t, t), 1)
        def scores(h):
            return jnp.dot(kaug_ref[pl.ds(ks, t), h * LANES:(h + 1) * LANES], qaug[h], preferred_element_type=F32)

        raw = [scores(h) for h in range(QK_AHEAD)]
        for h in range(N_HEADS):
            if h + QK_AHEAD < N_HEADS:
                raw.append(scores(h + QK_AHEAD))
            grp = slice(h * LANES, (h + 1) * LANES)
            st = raw[h]
            if masked:
                st = jnp.where(keep, st, NEG_INF)
            ref_j = jnp.concatenate([reft_ref[h:h + 1, pl.ds(ks, LANES)]] * (t // LANES), axis=1)
            crow = cb[h] - ref_j
            m_old = m_ref[h]
            m_new = jnp.maximum(m_old, jnp.max(st, axis=0, keepdims=True) + crow)
            pt = jnp.exp2(st - (m_new - crow)).astype(BF16)
            acc_ref[h] = acc_ref[h] * jnp.exp2(m_old - m_new) + jnp.dot(
                vtaug_ref[grp, pl.ds(ks, t)], pt, preferred_element_type=F32)
            m_ref[h] = m_new

    lax.fori_loop(0, qi, lambda j, c: (step(j, False), c)[1], 0)
    step(qi, True)

    outs = []
    for h in range(N_HEADS):
        acc = acc_ref[h]
        outs.append(acc[:HEAD_DIM] / acc[HEAD_DIM:HEAD_DIM + 1])
    o_ref[...] = jnp.concatenate(outs, axis=0).T.astype(o_ref.dtype)


def _fox_prompt(qt, kaug, vtaug, cumt, reft, *, seq):
    nb = qt.shape[0]
    t = ATTN_TILE
    nq = seq // t
    aug = N_HEADS * LANES
    per_b = lambda r: pl.BlockSpec((None, r, seq), lambda b, i: (b, 0, 0))
    return pl.pallas_call(
        _fox_prompt_kernel,
        out_shape=jax.ShapeDtypeStruct((nb * seq, D_ATTN), BF16),
        grid=(nb, nq),
        in_specs=[pl.BlockSpec((None, D_ATTN, t), lambda b, i: (b, 0, i)),
                  pl.BlockSpec((seq, aug), lambda b, i: (b, 0)),
                  per_b(aug), per_b(N_HEADS), per_b(N_HEADS)],
        out_specs=pl.BlockSpec((t, D_ATTN), lambda b, i: (b * nq + i, 0)),
        scratch_shapes=[pltpu.VMEM((N_HEADS, 1, t), F32), pltpu.VMEM((N_HEADS, LANES, t), F32)],
        compiler_params=_params("parallel", "arbitrary"),
        name="fox_prompt",
    )(qt, kaug, vtaug, cumt, reft)


def _route(logits):
    lane = lax.broadcasted_iota(jnp.int32, logits.shape, 1)
    big = jnp.int32(LANES)
    is_g = (lane >= N_EXPERTS) & (lane < N_EXPERTS + N_GROUPS)
    gl = jnp.where(is_g, logits, NEG_INF)
    gmax = jnp.max(gl, axis=-1, keepdims=True)
    gi = jnp.min(jnp.where(gl == gmax, lane, big), axis=-1, keepdims=True) - N_EXPERTS
    pg_top = 1.0 / jnp.sum(jnp.exp(gl - gmax), axis=-1, keepdims=True)
    in_g = (lane >= gi * EXPERTS_PER_GROUP) & (lane < (gi + 1) * EXPERTS_PER_GROUP)
    el = jnp.where(in_g, logits, NEG_INF)
    m1 = jnp.max(el, axis=-1, keepdims=True)
    i1 = jnp.min(jnp.where(el == m1, lane, big), axis=-1, keepdims=True)
    el2 = jnp.where(lane == i1, NEG_INF, el)
    m2 = jnp.max(el2, axis=-1, keepdims=True)
    i2 = jnp.min(jnp.where(el2 == m2, lane, big), axis=-1, keepdims=True)
    e2 = jnp.exp(m2 - m1)
    w1 = pg_top / (1.0 + e2)
    return jnp.where(lane == i1, w1, jnp.where(lane == i2, w1 * e2, 0.0)), gi


ROUTE_CHUNK = 32
SORT_BLOCK = 512
SORTED_ROWS = 768
ROUTE_TILE = 256


def _sort_block(h2b, comb, gi, ls_ref, xs_ref, cs_ref, pos_ref, cnt_ref):
    tm = h2b.shape[0]
    lane = lax.broadcasted_iota(jnp.int32, (tm, LANES), 1)
    onehot = jnp.where(lane == gi, 1.0, 0.0)
    before = jnp.dot(ls_ref[...], onehot.astype(BF16), preferred_element_type=F32)
    tot = before[tm - 1:tm] + onehot[tm - 1:tm]
    cnt_ref[0] = tot
    padded = jnp.floor((tot + (ROUTE_CHUNK - 1)) * (1.0 / ROUTE_CHUNK)) * ROUTE_CHUNK
    lane1 = lax.broadcasted_iota(jnp.int32, (1, LANES), 1)
    start = jnp.zeros_like(padded)
    for s in range(1, N_GROUPS):
        start = start + jnp.where(lane1 >= s, pltpu.roll(padded, s, 1), 0.0)
    pos = jnp.sum(onehot * (start + before), axis=-1, keepdims=True)
    pos_ref[...] = pos
    pos_row = jnp.broadcast_to(pos, (tm, LANES)).T[0:1]
    rows = lax.broadcasted_iota(jnp.int32, (SORTED_ROWS, tm), 0).astype(F32)
    perm = jnp.where(rows == pos_row, 1.0, 0.0).astype(BF16)
    xs_ref[...] = jnp.dot(perm, h2b, preferred_element_type=F32).astype(BF16)
    rel = jnp.zeros_like(comb)
    for g in range(N_GROUPS):
        moved = comb if g == 0 else pltpu.roll(comb, LANES - g * EXPERTS_PER_GROUP, 1)
        rel = rel + jnp.where(gi == g, moved, 0.0)
    rel = jnp.where(lane < EXPERTS_PER_GROUP, rel, 0.0)
    hi = rel.astype(BF16)
    lo = (rel - hi.astype(F32)).astype(BF16)
    cs_ref[...] = (jnp.dot(perm, hi, preferred_element_type=F32) + jnp.dot(perm, lo, preferred_element_type=F32))


def _merge_kernel(a_ref, gm_ref, sga_ref, sgg_ref, x_ref, g1_ref, sc2_ref, sh2_ref, wao_ref, wgo_ref, wo_ref,
                  l1g_ref, l1b_ref, wr_ref, br_ref, *rest, alpha, precise, sort):
    t = (sga_ref[...].astype(F32) * _dot(a_ref[...], wao_ref[...], precise)
         + sgg_ref[...].astype(F32) * _dot(gm_ref[...], wgo_ref[...], precise))
    m = _dot(t, wo_ref[...], precise)
    x1 = _ln(alpha * x_ref[...] + (1.0 + _mod(g1_ref)) * m, l1g_ref[...], l1b_ref[...])
    h2 = x1 * (1.0 + _mod(sc2_ref)) + _mod(sh2_ref)
    comb, gi = _route(_dot(h2, wr_ref[...], precise) + br_ref[...])
    if sort:
        ls_ref, x1_ref, xs_ref, cs_ref, pos_ref, cnt_ref = rest
        _sort_block(h2.astype(BF16), comb, gi, ls_ref, xs_ref, cs_ref, pos_ref, cnt_ref)
    else:
        x1_ref, h2_ref, comb_ref = rest
        h2_ref[...] = h2.astype(h2_ref.dtype)
        comb_ref[...] = comb
    x1_ref[...] = x1


def _merge(a, gm, sga, sgg, x, g1, sc2, sh2, wao, wgo, wo, l1g, l1b, wr, br, *, rows_per_mod, tm, alpha, precise,
           sort):
    n, d = x.shape
    row = lambda c: pl.BlockSpec((tm, c), lambda i: (i, 0))
    mod = _mod_spec(g1, tm, rows_per_mod)
    ins = [a, gm, sga, sgg, x, g1, sc2, sh2, wao, wgo, wo, l1g, l1b, wr, br]
    in_specs = [row(D_ATTN), row(gm.shape[1]), row(d), row(d), row(d), mod, mod, mod, _full(wao.shape),
                _full(wgo.shape), _full(wo.shape), _full(l1g.shape), _full(l1b.shape), _full(wr.shape),
                _full(br.shape)]
    if sort:
        assert tm == SORT_BLOCK
        nblk = n // tm
        ls = jnp.tril(jnp.ones((tm, tm), BF16), -1)
        ins.append(ls)
        in_specs.append(_full(ls.shape))
        srow = lambda c: pl.BlockSpec((SORTED_ROWS, c), lambda i: (i, 0))
        out_shape = [((n, d), F32), ((nblk * SORTED_ROWS, d), BF16), ((nblk * SORTED_ROWS, LANES), F32),
                     ((n, 1), F32), ((nblk, 1, LANES), F32)]
        out_specs = [row(d), srow(d), srow(LANES), row(1), pl.BlockSpec((1, 1, LANES), lambda i: (i, 0, 0))]
    else:
        out_shape = [((n, d), F32), ((n, d), BF16), ((n, LANES), F32)]
        out_specs = [row(d), row(d), row(LANES)]
    return pl.pallas_call(
        functools.partial(_merge_kernel, alpha=alpha, precise=precise, sort=sort),
        out_shape=[jax.ShapeDtypeStruct(s, t) for s, t in out_shape],
        grid=(n // tm,),
        in_specs=in_specs,
        out_specs=out_specs,
        compiler_params=_params("parallel"),
        name="merge_precise" if precise else "merge",
    )(*ins)


def _route_tables(counts, nblk):
    ch, cpb, cpt = ROUTE_CHUNK, SORTED_ROWS // ROUTE_CHUNK, ROUTE_TILE // ROUTE_CHUNK
    max_tiles = (nblk * (SORT_BLOCK + N_GROUPS * (ch - 1))) // ROUTE_TILE + N_GROUPS + 1
    cnt = counts[:, 0, :N_GROUPS].astype(jnp.int32)
    nch = (cnt + ch - 1) // ch
    loc = jnp.cumsum(nch, axis=1) - nch
    earlier = jnp.cumsum(nch, axis=0) - nch
    tiles_g = (nch.sum(0) + cpt - 1) // cpt
    tile_off = jnp.cumsum(tiles_g) - tiles_g
    n_tiles = tiles_g.sum()
    k = jnp.arange(cpb, dtype=jnp.int32)[None, None, :]
    used = (k >= loc[:, :, None]) & (k < (loc + nch)[:, :, None])
    glob = (tile_off * cpt)[None, :, None] + earlier[:, :, None] + k - loc[:, :, None]
    dst = jnp.sum(jnp.where(used, glob, 0), axis=1)
    blk_chunk = jnp.arange(nblk, dtype=jnp.int32)[:, None] * cpb + k[0]
    scatter_to = jnp.where(used.any(axis=1), dst, max_tiles * cpt)
    src = jnp.full((max_tiles * cpt,), cpb - 1, jnp.int32).at[scatter_to.reshape(-1)].set(
        blk_chunk.reshape(-1), mode="drop")
    t = jnp.arange(max_tiles, dtype=jnp.int32)
    tile_group = jnp.minimum(jnp.sum(t[:, None] >= (tile_off + tiles_g)[None, :], axis=1), N_GROUPS - 1)
    return src, dst.reshape(-1), tile_group.astype(jnp.int32), n_tiles.reshape(1).astype(jnp.int32), max_tiles


def _experts_kernel(src_ref, grp_ref, nt_ref, *refs):
    del src_ref
    cpt = ROUTE_TILE // ROUTE_CHUNK
    xr, cr = refs[:cpt], refs[cpt:2 * cpt]
    wg_ref, wu_ref, wd_ref, o_ref, wg_s, wu_s, wd_s = refs[2 * cpt:]
    t = pl.program_id(0)
    live = t < nt_ref[0]
    de = wg_ref.shape[3]

    @pl.when((t == 0) | (grp_ref[t] != grp_ref[jnp.maximum(t - 1, 0)]))
    def _():
        for e in range(EXPERTS_PER_GROUP):
            wg_s[:, e * de:(e + 1) * de] = wg_ref[0, e].astype(BF16)
            wu_s[:, e * de:(e + 1) * de] = wu_ref[0, e].astype(BF16)
            wd_s[e * de:(e + 1) * de, :] = wd_ref[0, e].astype(BF16)

    @pl.when(live)
    def _():
        x = jnp.concatenate([r[...] for r in xr], axis=0)
        c = jnp.concatenate([r[...] for r in cr], axis=0)
        hg = jnp.dot(x, wg_s[...], preferred_element_type=F32)
        hu = jnp.dot(x, wu_s[...], preferred_element_type=F32)
        a = jnp.concatenate([(hg[:, e * de:(e + 1) * de] * jax.nn.sigmoid(hg[:, e * de:(e + 1) * de])
                              * hu[:, e * de:(e + 1) * de] * c[:, e:e + 1]).astype(BF16)
                             for e in range(EXPERTS_PER_GROUP)], axis=1)
        o_ref[...] = jnp.dot(a, wd_s[...], preferred_element_type=F32).astype(o_ref.dtype)

    @pl.when(jnp.logical_not(live))
    def _():
        o_ref[...] = jnp.zeros_like(o_ref)


def _experts(src, tile_group, n_tiles, max_tiles, xs, cs, wg, wu, wd):
    d = xs.shape[1]
    de = wg.shape[2]
    cpt = ROUTE_TILE // ROUTE_CHUNK
    grouped = lambda w: w.reshape((N_GROUPS, EXPERTS_PER_GROUP) + w.shape[1:])

    def chunk(width):
        def one(r):
            return pl.BlockSpec((ROUTE_CHUNK, width), lambda t, src, grp, nt: (src[t * cpt + r], 0))
        return [one(r) for r in range(cpt)]

    wspec = lambda w: pl.BlockSpec((1, EXPERTS_PER_GROUP) + w.shape[1:], lambda t, src, grp, nt: (grp[t], 0, 0, 0))
    wide = EXPERTS_PER_GROUP * de
    grid_spec = pltpu.PrefetchScalarGridSpec(
        num_scalar_prefetch=3,
        grid=(max_tiles,),
        in_specs=chunk(d) + chunk(LANES) + [wspec(wg), wspec(wu), wspec(wd)],
        out_specs=pl.BlockSpec((ROUTE_TILE, d), lambda t, src, grp, nt: (t, 0)),
        scratch_shapes=[pltpu.VMEM((d, wide), BF16), pltpu.VMEM((d, wide), BF16), pltpu.VMEM((wide, d), BF16)],
    )
    return pl.pallas_call(
        _experts_kernel,
        out_shape=jax.ShapeDtypeStruct((max_tiles * ROUTE_TILE, d), BF16),
        grid_spec=grid_spec,
        compiler_params=_params("arbitrary"),
        name="experts",
    )(src, tile_group, n_tiles, *([xs] * cpt), *([cs] * cpt), grouped(wg), grouped(wu), grouped(wd))


def _unsort_kernel(dst_ref, *refs, alpha):
    del dst_ref
    cpb = SORTED_ROWS // ROUTE_CHUNK
    yr = refs[:cpb]
    pos_ref, x1_ref, g2_ref, l2g_ref, l2b_ref, o_ref = refs[cpb:]
    ys = jnp.concatenate([r[...] for r in yr], axis=0)
    tm = pos_ref.shape[0]
    cols = lax.broadcasted_iota(jnp.int32, (tm, SORTED_ROWS), 1).astype(F32)
    perm_t = jnp.where(cols == pos_ref[...], 1.0, 0.0).astype(BF16)
    f = jnp.dot(perm_t, ys, preferred_element_type=F32)
    o_ref[...] = _ln(alpha * x1_ref[...] + (1.0 + _mod(g2_ref)) * f, l2g_ref[...], l2b_ref[...])


def _unsort(dst, y, pos, x1, g2, l2g, l2b, *, rows_per_mod, alpha):
    n, d = x1.shape
    tm = SORT_BLOCK
    cpb = SORTED_ROWS // ROUTE_CHUNK

    def chunk(r):
        return pl.BlockSpec((ROUTE_CHUNK, d), lambda i, dst: (dst[i * cpb + r], 0))

    row = lambda c: pl.BlockSpec((tm, c), lambda i, dst: (i, 0))
    full = lambda s: pl.BlockSpec(s, lambda i, dst: (0,) * len(s))
    grid_spec = pltpu.PrefetchScalarGridSpec(
        num_scalar_prefetch=1,
        grid=(n // tm,),
        in_specs=[chunk(r) for r in range(cpb)] + [
            row(1), row(d), pl.BlockSpec((1, 1, d), lambda i, dst: (i * tm // rows_per_mod, 0, 0)),
            full(l2g.shape), full(l2b.shape)],
        out_specs=row(d),
    )
    return pl.pallas_call(
        functools.partial(_unsort_kernel, alpha=alpha),
        out_shape=jax.ShapeDtypeStruct((n, d), F32),
        grid_spec=grid_spec,
        compiler_params=_params("parallel"),
        name="unsort",
    )(dst, *([y] * cpb), pos, x1, g2, l2g, l2b)


def _moe_kernel(h2_ref, comb_ref, x1_ref, g2_ref, wg_ref, wu_ref, wd_ref, l2g_ref, l2b_ref, o_ref, *, alpha):
    e = pl.program_id(0)

    @pl.when(e == 0)
    def _():
        o_ref[...] = jnp.zeros_like(o_ref)

    h2 = h2_ref[...]
    comb = comb_ref[...]
    lane = lax.broadcasted_iota(jnp.int32, comb.shape, 1)
    w = jnp.sum(jnp.where(lane == e, comb, 0.0), axis=-1, keepdims=True)
    hg = jnp.dot(h2, wg_ref[0].astype(BF16), preferred_element_type=F32)
    hu = jnp.dot(h2, wu_ref[0].astype(BF16), preferred_element_type=F32)
    a = hg * jax.nn.sigmoid(hg) * hu * w
    o_ref[...] += jnp.dot(a.astype(BF16), wd_ref[0].astype(BF16), preferred_element_type=F32)

    @pl.when(e == pl.num_programs(0) - 1)
    def _():
        o_ref[...] = _ln(alpha * x1_ref[...] + (1.0 + g2_ref[...]) * o_ref[...], l2g_ref[...], l2b_ref[...])


def _moe(h2, comb, x1, g2, wg, wu, wd, l2g, l2b, *, alpha):
    n, d = x1.shape
    ne, _, de = wg.shape
    return pl.pallas_call(
        functools.partial(_moe_kernel, alpha=alpha),
        out_shape=jax.ShapeDtypeStruct((n, d), F32),
        grid=(ne,),
        in_specs=[_full((n, d)), _full((n, LANES)), _full((n, d)), _full((n, d)),
                  pl.BlockSpec((1, d, de), lambda e: (e, 0, 0)), pl.BlockSpec((1, d, de), lambda e: (e, 0, 0)),
                  pl.BlockSpec((1, de, d), lambda e: (e, 0, 0)), _full(l2g.shape), _full(l2b.shape)],
        out_specs=_full((n, d)),
        compiler_params=_params("arbitrary"),
        name="moe",
    )(h2, comb, x1, g2, wg, wu, wd, l2g, l2b)


def _sample_proj_kernel(x_ref, sc_ref, sh_ref, w_ref, z_ref):
    z_ref[...] = _dot(x_ref[...] * (1.0 + sc_ref[...]) + sh_ref[...], w_ref[...], True)


def _sample_proj(x, sc, sh, w, *, tn):
    n, d = x.shape
    return pl.pallas_call(
        _sample_proj_kernel,
        out_shape=jax.ShapeDtypeStruct((n, w.shape[1]), F32),
        grid=(w.shape[1] // tn,),
        in_specs=[_full((n, d))] * 3 + [pl.BlockSpec((d, tn), lambda j: (0, j))],
        out_specs=pl.BlockSpec((n, tn), lambda j: (0, j)),
        compiler_params=_params("parallel"),
        name="sample_proj",
    )(x, sc, sh, w)


def _sample_mix_kernel(z_ref, bf_ref, glg_ref, glb_ref, mg_ref, bs_ref, mc_ref,
                       q_ref, k_ref, v_ref, logf_ref, cn_ref, gv_ref, gm_ref, sga_ref, sgg_ref):
    z = z_ref[...]
    sec = lambda off, width: z[:, off:off + width]
    q, k, v, logf, u, gv, sga, sgg = _mixer_epilogue(
        sec(_OFF_Q, 512), sec(_OFF_K, 512), sec(_OFF_V, 512), sec(_OFF_F, LANES), sec(_OFF_U, 512),
        sec(_OFF_GV, 512), sec(_OFF_GA, 1024), sec(_OFF_GG, 1024), bf_ref[...], glg_ref[...], glb_ref[...])
    q_ref[...] = q
    k_ref[...] = k
    v_ref[...] = v
    logf_ref[...] = logf
    cn_ref[...] = _dot(mc_ref[...], logf, True)
    gv_ref[...] = gv
    sga_ref[...] = sga
    sgg_ref[...] = sgg
    gw = gv.shape[1] // GMLP_GROUPS
    sp = jnp.concatenate([_dot(mg_ref[g], gv[:, g * gw:(g + 1) * gw], True) for g in range(GMLP_GROUPS)], axis=-1)
    gm_ref[...] = u * (sp + bs_ref[...])


def _sample_mix(z, bf, glg, glb, mg, bs, mc):
    n = z.shape[0]
    shapes = [(n, 512)] * 3 + [(n, LANES)] * 2 + [(n, 512)] * 2 + [(n, 1024)] * 2
    return pl.pallas_call(
        _sample_mix_kernel,
        out_shape=[jax.ShapeDtypeStruct(s, F32) for s in shapes],
        compiler_params=pltpu.CompilerParams(vmem_limit_bytes=VMEM_LIMIT_BYTES),
        name="sample_mix",
    )(z, bf, glg, glb, mg, bs, mc)


PAGES_PER_STEP = 16


def _fox_sample_kernel(pt_ref, kt_hbm, vt_hbm, lf_hbm, qbd_ref, cn_ref, kn_ref, vn_ref, cnb_ref, o_ref,
                       kbuf, vbuf, lbuf, sem, m_ref, l_ref, acc_ref, later_ref, *, n_chunks, n_new):
    P = PAGES_PER_STEP
    b, c = pl.program_id(0), pl.program_id(1)
    step = b * n_chunks + c
    n_steps = pl.num_programs(0) * n_chunks
    slot = lax.rem(step, 2)

    def page_copies(seq, chunk, sl, lookup):
        out = []
        for r in range(P):
            page = pt_ref[seq, (n_chunks - 1 - chunk) * P + r] if lookup else 0
            out += [pltpu.make_async_copy(kt_hbm.at[page], kbuf.at[sl, r], sem.at[sl]),
                    pltpu.make_async_copy(vt_hbm.at[page], vbuf.at[sl, r], sem.at[sl]),
                    pltpu.make_async_copy(lf_hbm.at[page], lbuf.at[sl, r], sem.at[sl])]
        return out

    @pl.when(step == 0)
    def _():
        for cp in page_copies(b, c, slot, True):
            cp.start()

    @pl.when(step + 1 < n_steps)
    def _():
        nxt = step + 1
        for cp in page_copies(nxt // n_chunks, lax.rem(nxt, n_chunks), 1 - slot, True):
            cp.start()

    @pl.when(c == 0)
    def _():
        m_ref[...] = jnp.full_like(m_ref, NEG_INF)
        l_ref[...] = jnp.zeros_like(l_ref)
        acc_ref[...] = jnp.zeros_like(acc_ref)
        later_ref[...] = jnp.zeros_like(later_ref)

    qbd = qbd_ref[0]
    cn = cn_ref[0]
    nt = (((1,), (1,)), ((), ()))

    def update(s, vs):
        m_old = m_ref[...]
        m_new = m_old
        for sb in s:
            m_new = jnp.maximum(m_new, jnp.max(sb, axis=-1, keepdims=True))
        alpha = jnp.exp(m_old - m_new)
        l_new = l_ref[...] * alpha
        acc = acc_ref[...] * alpha
        for sb, vb in zip(s, vs):
            p = jnp.exp(sb - m_new)
            l_new = l_new + jnp.sum(p, axis=-1, keepdims=True)
            acc = acc + lax.dot_general(p, vb, nt, preferred_element_type=F32)
        m_ref[...] = m_new
        l_ref[...] = l_new
        acc_ref[...] = acc

    for cp in page_copies(b, c, slot, False):
        cp.wait()

    lane = lax.broadcasted_iota(jnp.int32, (N_HEADS, PAGE_SIZE), 1)
    later = later_ref[...]
    bias = [None] * P
    for r in reversed(range(P)):
        y = lbuf[slot, r]
        for k in range(7):
            sh = 1 << k
            y = y + jnp.where(lane < PAGE_SIZE - sh, pltpu.roll(y, PAGE_SIZE - sh, 1), 0.0)
        bias[r] = jnp.where(lane < PAGE_SIZE - 1, pltpu.roll(y, PAGE_SIZE - 1, 1), 0.0) + later
        later = later + y[:, 0:1]
    later_ref[...] = later

    scores = []
    for r in range(P):
        st = jnp.dot(qbd, kbuf[slot, r].reshape(D_ATTN, PAGE_SIZE), preferred_element_type=F32)
        scores.append(st + jnp.concatenate([bias[r]] * n_new, axis=0) + cn)
    update(scores, [vbuf[slot, r].reshape(D_ATTN, PAGE_SIZE) for r in range(P)])

    @pl.when(c == n_chunks - 1)
    def _():
        st = jnp.dot(qbd, kn_ref[0], preferred_element_type=F32) + cn - cnb_ref[0]
        col = lax.broadcasted_iota(jnp.int32, st.shape, 1)
        row = lax.broadcasted_iota(jnp.int32, st.shape, 0)
        update([jnp.where(col <= row // N_HEADS, st, NEG_INF)], [vn_ref[0]])
        full = acc_ref[...] / l_ref[...]
        lane_head = lax.broadcasted_iota(jnp.int32, full.shape, 1) // HEAD_DIM
        row_head = lax.broadcasted_iota(jnp.int32, full.shape, 0) % N_HEADS
        own = jnp.where(lane_head == row_head, full, 0.0)
        rows = [jnp.sum(own[q * N_HEADS:(q + 1) * N_HEADS], axis=0, keepdims=True) for q in range(n_new)]
        o_ref[0] = jnp.concatenate(rows + [jnp.zeros((o_ref.shape[1] - n_new, D_ATTN), F32)], axis=0)


def _fox_sample(page_table, kt_pool, vt_pool, logf_pool, qbd, cn, kn, vn, cnb, *, n_new):
    nb, n_pages = page_table.shape
    P = PAGES_PER_STEP
    n_chunks = n_pages // P
    nrow = n_new * N_HEADS
    per_b = lambda s: pl.BlockSpec((1,) + s, lambda b, c, pt: (b,) + (0,) * len(s))
    hbm = pl.BlockSpec(memory_space=pl.ANY)
    kv_page = (N_HEADS, HEAD_DIM, PAGE_SIZE)
    grid_spec = pltpu.PrefetchScalarGridSpec(
        num_scalar_prefetch=1,
        grid=(nb, n_chunks),
        in_specs=[hbm, hbm, hbm, per_b((nrow, D_ATTN)), per_b((nrow, 1)), per_b((D_ATTN, LANES)),
                  per_b((D_ATTN, LANES)), per_b((nrow, LANES))],
        out_specs=per_b((8, D_ATTN)),
        scratch_shapes=[pltpu.VMEM((2, P) + kv_page, F32), pltpu.VMEM((2, P) + kv_page, F32),
                        pltpu.VMEM((2, P, N_HEADS, PAGE_SIZE), F32), pltpu.SemaphoreType.DMA((2,)),
                        pltpu.VMEM((nrow, 1), F32), pltpu.VMEM((nrow, 1), F32), pltpu.VMEM((nrow, D_ATTN), F32),
                        pltpu.VMEM((N_HEADS, PAGE_SIZE), F32)],
    )
    return pl.pallas_call(
        functools.partial(_fox_sample_kernel, n_chunks=n_chunks, n_new=n_new),
        out_shape=jax.ShapeDtypeStruct((nb, 8, D_ATTN), F32),
        grid_spec=grid_spec,
        compiler_params=_params("arbitrary", "arbitrary"),
        name="fox_sample",
    )(page_table, kt_pool, vt_pool, logf_pool, qbd, cn, kn, vn, cnb)


def _reorder_w_in(w_in, dtype, width):
    d, cols = w_in.shape
    f0 = 3 * D_ATTN
    pad = jnp.zeros((d, width - cols), dtype)
    part = lambda a, b: w_in[:, a:b].astype(dtype)
    return jnp.concatenate([part(0, f0), part(f0 + N_HEADS, cols), part(f0, f0 + N_HEADS), pad], axis=1)


def kernel(x_prompt, x_sample, c_prompt, c_sample, cache_k, cache_v, cache_logf, page_table, w_ada, b_ada, w_in,
           b_f, gmlp_ln_g, gmlp_ln_b, w_s, b_s, w_attn_out, w_gmlp_out, w_o, ln1_g, ln1_b, w_group_router,
           b_group_router, w_expert_router, b_expert_router, w_exp_gate, w_exp_up, w_exp_down, ln2_g, ln2_b):
    depth = w_ada.shape[0]
    assert depth == 1
    nbp, seq, d = x_prompt.shape
    nbs, n_new, _ = x_sample.shape
    alpha = (2.0 * depth) ** 0.25
    n_pool = cache_k.shape[1]
    d_gmlp = gmlp_ln_g.shape[1]
    gw = d_gmlp // GMLP_GROUPS

    w_in_b = _reorder_w_in(w_in[0], BF16, _W_COLS)
    w_in_s = _reorder_w_in(w_in[0], F32, SAMPLE_PROJ_TN * pl.cdiv(_W_COLS, SAMPLE_PROJ_TN))
    bf =jnp.pad(b_f[0], (0, LANES - N_HEADS))[None]
    glg, glb = gmlp_ln_g[0][None], gmlp_ln_b[0][None]
    tril = jnp.tril(jnp.ones((CHUNK, CHUNK), F32))
    ws = jnp.where(tril > 0, w_s[0], 0.0)
    bs_tile = jnp.repeat(b_s[0].T, gw, axis=1)
    w_r = jnp.pad(jnp.concatenate([w_expert_router[0], w_group_router[0]], axis=1),
                  ((0, 0), (0, LANES - N_EXPERTS - N_GROUPS)))
    b_r = jnp.pad(jnp.concatenate([b_expert_router[0], b_group_router[0]]), (0, LANES - N_EXPERTS - N_GROUPS))[None]
    l1g, l1b, l2g, l2b = ln1_g[0][None], ln1_b[0][None], ln2_g[0][None], ln2_b[0][None]

    nc = nbp + nbs
    c_all = jnp.pad(jnp.concatenate([c_prompt, c_sample], axis=0), ((0, -nc % 8), (0, 0)))
    mod = _ada(c_all, w_ada[0], b_ada[0][None])
    modp = [m[:, None, :] for m in jnp.split(mod[:nbp], 6, axis=-1)]
    mods = [m[:, None, :] for m in jnp.split(mod[nbp:nc], 6, axis=-1)]

    xp = x_prompt.reshape(nbp * seq, d)
    qt, kt32, kaug, vt32, vtaug, logft, cumt, reft, gm, sga, sgg = _prompt_in(
        xp, modp[1], modp[0], w_in_b, bf, glg, glb, ws.astype(BF16), bs_tile, tril, seq=seq, tm=ATTN_TILE)
    a = _fox_prompt(qt, kaug, vtaug, cumt, reft, seq=seq)
    x1, xs_sorted, cs_sorted, pos, counts = _merge(
        a, gm, sga, sgg, xp, modp[2], modp[4], modp[3], w_attn_out[0].astype(BF16), w_gmlp_out[0].astype(BF16),
        w_o[0].astype(BF16), l1g, l1b, w_r.astype(BF16), b_r, rows_per_mod=seq, tm=SORT_BLOCK, alpha=alpha,
        precise=False, sort=True)
    src, dst, tile_group, n_tiles, max_tiles = _route_tables(counts, nbp * seq // SORT_BLOCK)
    y_sorted = _experts(src, tile_group, n_tiles, max_tiles, xs_sorted, cs_sorted, w_exp_gate[0], w_exp_up[0],
                        w_exp_down[0])
    yp = _unsort(dst, y_sorted, pos, x1, modp[5], l2g, l2b, rows_per_mod=seq, alpha=alpha)

    ns = nbs * n_new
    xs = x_sample.reshape(ns, d)
    rep = lambda m: jnp.repeat(m[:, 0, :], n_new, axis=0)
    z = _sample_proj(xs, rep(mods[1]), rep(mods[0]), w_in_s, tn=SAMPLE_PROJ_TN)
    eye_b = jnp.eye(nbs, dtype=F32)
    mg = jnp.stack([jnp.kron(eye_b, ws[g, :n_new, :n_new]) for g in range(GMLP_GROUPS)])
    bs_rows = jnp.tile(bs_tile[:n_new], (nbs, 1))
    mc = jnp.kron(eye_b, tril[:n_new, :n_new])
    qs, ks, vs, logf_s, cn, gv_s, gm_s, sga_s, sgg_s = _sample_mix(z, bf, glg, glb, mg, bs_rows, mc)

    kt_pool = cache_k[0].transpose(0, 2, 3, 1)
    vt_pool = cache_v[0].transpose(0, 2, 3, 1)
    logf_pool = cache_logf[0].transpose(0, 2, 1)
    nrow = n_new * N_HEADS
    q4 = qs.reshape(nbs, n_new, N_HEADS, HEAD_DIM)
    qbd = jnp.einsum("bqhd,hg->bqhgd", q4, jnp.eye(N_HEADS, dtype=F32)).reshape(nbs, nrow, D_ATTN)
    cn3 = cn[:, :N_HEADS].reshape(nbs, n_new, N_HEADS)
    cn_col = cn3.reshape(nbs, nrow, 1)
    cnb = jnp.pad(jnp.tile(cn3.transpose(0, 2, 1), (1, n_new, 1)), ((0, 0), (0, 0), (0, LANES - n_new)))
    new_t = lambda t: jnp.pad(t.reshape(nbs, n_new, D_ATTN).transpose(0, 2, 1), ((0, 0), (0, 0), (0, LANES - n_new)))
    a_s = _fox_sample(page_table, kt_pool, vt_pool, logf_pool, qbd, cn_col, new_t(ks), new_t(vs), cnb, n_new=n_new)
    a_s = a_s[:, :n_new].reshape(ns, D_ATTN)
    x1s, h2s, comb_s = _merge(a_s, gm_s, sga_s, sgg_s, xs, rep(mods[2]), rep(mods[4]), rep(mods[3]),
                              w_attn_out[0], w_gmlp_out[0], w_o[0], l1g, l1b, w_r, b_r,
                              rows_per_mod=ns, tm=ns, alpha=alpha, precise=True, sort=False)
    ys = _moe(h2s, comb_s, x1s, rep(mods[5]), w_exp_gate[0], w_exp_up[0], w_exp_down[0], l2g, l2b, alpha=alpha)

    hs = (N_HEADS, HEAD_DIM)
    untr = lambda t: t.reshape(1, nbp, *hs, seq).transpose(0, 1, 4, 2, 3)
    return (yp.reshape(nbp, seq, d), ys.reshape(nbs, n_new, d), untr(kt32), untr(vt32),
            logft.reshape(1, nbp, N_HEADS, seq).transpose(0, 1, 3, 2),
            ks.reshape(1, nbs, n_new, *hs), vs.reshape(1, nbs, n_new, *hs),
            logf_s[:, :N_HEADS].reshape(1, nbs, n_new, N_HEADS), gv_s.reshape(1, nbs, n_new, d_gmlp))
```

```python
import functools

import numpy as np
import jax
import jax.numpy as jnp
from jax import lax
from jax.experimental import pallas as pl
from jax.experimental.pallas import tpu as pltpu

F32 = jnp.float32
BF16 = jnp.bfloat16
HIGHEST = lax.Precision.HIGHEST

N_HEADS = 8
HEAD_DIM = 64
D_ATTN = N_HEADS * HEAD_DIM
PAGE_SIZE = 128
CHUNK = 128
GMLP_GROUPS = 4
N_GROUPS = 4
EXPERTS_PER_GROUP = 4
N_EXPERTS = N_GROUPS * EXPERTS_PER_GROUP
LN_EPS = 1e-5
LANES = 128
NEG_INF = float("-inf")
LOG2E = 1.4426950408889634

VMEM_LIMIT_BYTES = 56 * 1024 * 1024


def _params(*sem):
    return pltpu.CompilerParams(dimension_semantics=sem, vmem_limit_bytes=VMEM_LIMIT_BYTES)


def _full(shape):
    n = len(shape)
    return pl.BlockSpec(shape, lambda *_: (0,) * n)


def _ln(x, g, b):
    mu = jnp.mean(x, axis=-1, keepdims=True)
    xc = x - mu
    var = jnp.mean(xc * xc, axis=-1, keepdims=True)
    return xc * lax.rsqrt(var + LN_EPS) * g + b


def _mod(ref):
    return ref[0] if len(ref.shape) == 3 else ref[...]


def _mod_spec(m, tm, rows_per_mod):
    if m.ndim == 3:
        return pl.BlockSpec((1, 1, m.shape[-1]), lambda i: (i * tm // rows_per_mod, 0, 0))
    return pl.BlockSpec((tm, m.shape[-1]), lambda i: (i, 0))


def _dot(a, b, precise):
    if precise:
        return jnp.dot(a, b, precision=HIGHEST, preferred_element_type=F32)
    return jnp.dot(a.astype(BF16), b.astype(BF16), preferred_element_type=F32)


def _ada_kernel(c_ref, w_ref, b_ref, o_ref):
    c = c_ref[...]
    o_ref[...] = _dot(c * jax.nn.sigmoid(c), w_ref[...], True) + b_ref[...]


def _ada(c, w, b, *, tn=1024):
    n, d = c.shape
    dout = w.shape[1]
    return pl.pallas_call(
        _ada_kernel,
        out_shape=jax.ShapeDtypeStruct((n, dout), F32),
        grid=(dout // tn,),
        in_specs=[_full((n, d)), pl.BlockSpec((d, tn), lambda j: (0, j)), pl.BlockSpec((1, tn), lambda j: (0, j))],
        out_specs=pl.BlockSpec((n, tn), lambda j: (0, j)),
        compiler_params=_params("parallel"),
        name="ada",
    )(c, w, b)


_OFF_Q, _OFF_K, _OFF_V, _OFF_U, _OFF_GV = 0, 512, 1024, 1536, 2048
_OFF_GA, _OFF_GG, _OFF_F, _W_COLS = 2560, 3584, 4608, 4736


def _mixer_epilogue(zq, zk, zv, zf, zu, zgv, zga, zgg, bf, glg, glb):
    q = zq * (HEAD_DIM ** -0.5)
    logf = jax.nn.log_sigmoid(zf + bf)
    u = jax.nn.gelu(zu)
    gv = _ln(jax.nn.gelu(zgv), glg, glb)
    return q, zk, zv, logf, u, gv, jax.nn.sigmoid(zga), jax.nn.sigmoid(zgg)


def _prompt_in_kernel(x_ref, sc_ref, sh_ref, wa_ref, wb_ref, wf_ref, bf_ref, glg_ref, glb_ref, ws_ref, bs_ref, tri_ref,
                      qt_ref, kt32_ref, kaug_ref, vt32_ref, vtaug_ref, logft_ref, cumt_ref, reft_ref, gm_ref,
                      sga_ref, sgg_ref, carry_ref, sp_ref, *, tiles_per_seq):
    tm = x_ref.shape[0]
    h = (x_ref[...] * (1.0 + sc_ref[0]) + sh_ref[0]).astype(BF16)

    def proj(off, width):
        ref, base = (wa_ref, 0) if off < _OFF_U else (wb_ref, _OFF_U) if off < _OFF_F else (wf_ref, _OFF_F)
        return jnp.dot(h, ref[:, off - base:off - base + width], preferred_element_type=F32)

    q, k, v, logf, u, gv, sga, sgg = _mixer_epilogue(
        proj(_OFF_Q, 512), proj(_OFF_K, 512), proj(_OFF_V, 512), proj(_OFF_F, LANES), proj(_OFF_U, 512),
        proj(_OFF_GV, 512), proj(_OFF_GA, 1024), proj(_OFF_GG, 1024), bf_ref[...], glg_ref[...], glb_ref[...])
    qt_ref[0] = (q * LOG2E).T.astype(BF16)
    kt32_ref[0] = k.T
    vt = v.T
    vt32_ref[0] = vt
    sga_ref[...] = sga.astype(BF16)
    sgg_ref[...] = sgg.astype(BF16)
    logft_ref[0] = logf.T[:N_HEADS]

    @pl.when(pl.program_id(0) % tiles_per_seq == 0)
    def _():
        carry_ref[...] = jnp.zeros_like(carry_ref)

    carry_in = carry_ref[...]
    carry = carry_in
    parts = []
    for n in range(tm // CHUNK):
        c = _dot(tri_ref[...], logf[n * CHUNK:(n + 1) * CHUNK], True) + carry
        parts.append(c)
        carry = c[CHUNK - 1:CHUNK]
    carry_ref[...] = carry
    cum = jnp.concatenate(parts, axis=0)
    cumt_ref[0] = (cum * LOG2E).T[:N_HEADS]
    reft_ref[0] = (jnp.broadcast_to(carry_in, cum.shape) * LOG2E).T[:N_HEADS]

    neg = (carry_in - cum) * LOG2E
    hi = neg.astype(BF16).astype(F32)
    mid = (neg - hi).astype(BF16).astype(F32)
    lo = (neg - hi - mid).astype(BF16).astype(F32)
    lane = lax.broadcasted_iota(jnp.int32, (tm, LANES), 1)
    for hd in range(N_HEADS):
        src = k[:, (hd // 2) * LANES:(hd // 2 + 1) * LANES]
        if hd % 2:
            src = pltpu.roll(src, HEAD_DIM, 1)
        blk = jnp.where(lane < HEAD_DIM, src, 0.0)
        for i, piece in enumerate((hi, mid, lo)):
            blk = jnp.where(lane == HEAD_DIM + i, piece[:, hd:hd + 1], blk)
        kaug_ref[:, hd * LANES:(hd + 1) * LANES] = blk.astype(BF16)

    ones_rows = jnp.where(lax.broadcasted_iota(jnp.int32, (HEAD_DIM, tm), 0) == 0, 1.0, 0.0)
    for hd in range(N_HEADS):
        grp = jnp.concatenate([vt[hd * HEAD_DIM:(hd + 1) * HEAD_DIM], ones_rows], axis=0)
        vtaug_ref[0, hd * LANES:(hd + 1) * LANES, :] = grp.astype(BF16)

    gvb = gv.astype(BF16)
    gw = gv.shape[1] // GMLP_GROUPS
    for n in range(tm // CHUNK):
        for g in range(GMLP_GROUPS):
            sp_ref[n * CHUNK:(n + 1) * CHUNK, g * gw:(g + 1) * gw] = jnp.dot(
                ws_ref[g], gvb[n * CHUNK:(n + 1) * CHUNK, g * gw:(g + 1) * gw], preferred_element_type=F32)
    bs = jnp.concatenate([bs_ref[...]] * (tm // CHUNK), axis=0)
    gm_ref[...] = (u * (sp_ref[...] + bs)).astype(BF16)


def _prompt_in(x, sc, sh, wa, wb, wf, bf, glg, glb, ws, bs, tri, *, seq, tm):
    n, d = x.shape
    nb = n // seq
    tps = seq // tm
    row = lambda c: pl.BlockSpec((tm, c), lambda i: (i, 0))
    mod = pl.BlockSpec((1, 1, d), lambda i: (i // tps, 0, 0))
    tr = lambda r: pl.BlockSpec((1, r, tm), lambda i: (i // tps, 0, i % tps))
    aug = N_HEADS * LANES
    outs = [((nb, 512, seq), BF16), ((nb, 512, seq), F32), ((n, aug), BF16), ((nb, 512, seq), F32),
            ((nb, aug, seq), BF16), ((nb, N_HEADS, seq), F32), ((nb, N_HEADS, seq), F32),
            ((nb, N_HEADS, seq), F32), ((n, 512), BF16), ((n, 1024), BF16), ((n, 1024), BF16)]
    out_specs = [tr(512), tr(512), row(aug), tr(512), tr(aug), tr(N_HEADS), tr(N_HEADS), tr(N_HEADS),
                 row(512), row(1024), row(1024)]
    return pl.pallas_call(
        functools.partial(_prompt_in_kernel, tiles_per_seq=tps),
        out_shape=[jax.ShapeDtypeStruct(s, t) for s, t in outs],
        grid=(n // tm,),
        in_specs=[row(d), mod, mod, _full(wa.shape), _full(wb.shape), _full(wf.shape), _full(bf.shape),
                  _full(glg.shape), _full(glb.shape), _full(ws.shape), _full(bs.shape), _full(tri.shape)],
        out_specs=out_specs,
        scratch_shapes=[pltpu.VMEM((1, LANES), F32), pltpu.VMEM((tm, 512), F32)],
        compiler_params=_params("arbitrary"),
        name="prompt_in",
    )(x, sc, sh, wa, wb, wf, bf, glg, glb, ws, bs, tri)


ATTN_TILE = 512
QK_AHEAD = 2


def _fox_prompt_kernel(qt_ref, kaug_ref, vtaug_ref, cumt_ref, reft_ref, o_ref, m_ref, acc_ref):
    t = ATTN_TILE
    qi = pl.program_id(1)
    q0 = pl.multiple_of(qi * t, t)
    m_ref[...] = jnp.full_like(m_ref, NEG_INF)
    acc_ref[...] = jnp.zeros_like(acc_ref)
    ones3 = jnp.where(lax.broadcasted_iota(jnp.int32, (HEAD_DIM, t), 0) < 3, 1.0, 0.0).astype(BF16)
    qaug = [jnp.concatenate([qt_ref[h * HEAD_DIM:(h + 1) * HEAD_DIM, :], ones3], axis=0) for h in range(N_HEADS)]
    cb = [cumt_ref[h:h + 1, pl.ds(q0, t)] for h in range(N_HEADS)]

    def run(tiles):
        starts = [pl.multiple_of(j * t, t) for j, _ in tiles]
        rounds = [(i, h) for i in range(len(tiles)) for h in range(N_HEADS)]

        def scores(i, h):
            return jnp.dot(kaug_ref[pl.ds(starts[i], t), h * LANES:(h + 1) * LANES], qaug[h],
                           preferred_element_type=F32)

        raw = [scores(*r) for r in rounds[:QK_AHEAD]]
        for n, (i, h) in enumerate(rounds):
            if n + QK_AHEAD < len(rounds):
                raw.append(scores(*rounds[n + QK_AHEAD]))
            ks, grp = starts[i], slice(h * LANES, (h + 1) * LANES)
            st = raw[n]
            if tiles[i][1]:
                st = jnp.where(lax.broadcasted_iota(jnp.int32, (t, t), 0) <= lax.broadcasted_iota(jnp.int32, (t, t), 1),
                               st, NEG_INF)
            ref_j = jnp.concatenate([reft_ref[h:h + 1, pl.ds(ks, LANES)]] * (t // LANES), axis=1)
            crow = cb[h] - ref_j
            m_old = m_ref[h]
            m_new = jnp.maximum(m_old, jnp.max(st, axis=0, keepdims=True) + crow)
            pt = jnp.exp2(st - (m_new - crow)).astype(BF16)
            acc_ref[h] = acc_ref[h] * jnp.exp2(m_old - m_new) + jnp.dot(
                vtaug_ref[grp, pl.ds(ks, t)], pt, preferred_element_type=F32)
            m_ref[h] = m_new

    lax.fori_loop(0, qi // 2, lambda jj, c: (run([(2 * jj, False), (2 * jj + 1, False)]), c)[1], 0)

    @pl.when(qi % 2 == 1)
    def _():
        run([(qi - 1, False), (qi, True)])

    @pl.when(qi % 2 == 0)
    def _():
        run([(qi, True)])

    outs = []
    for h in range(N_HEADS):
        acc = acc_ref[h]
        outs.append(acc[:HEAD_DIM] / acc[HEAD_DIM:HEAD_DIM + 1])
    o_ref[...] = jnp.concatenate(outs, axis=0).T.astype(o_ref.dtype)


def _fox_prompt(qt, kaug, vtaug, cumt, reft, *, seq):
    nb = qt.shape[0]
    t = ATTN_TILE
    nq = seq // t
    aug = N_HEADS * LANES
    per_b = lambda r: pl.BlockSpec((None, r, seq), lambda b, i: (b, 0, 0))
    return pl.pallas_call(
        _fox_prompt_kernel,
        out_shape=jax.ShapeDtypeStruct((nb * seq, D_ATTN), BF16),
        grid=(nb, nq),
        in_specs=[pl.BlockSpec((None, D_ATTN, t), lambda b, i: (b, 0, i)),
                  pl.BlockSpec((seq, aug), lambda b, i: (b, 0)),
                  per_b(aug), per_b(N_HEADS), per_b(N_HEADS)],
        out_specs=pl.BlockSpec((t, D_ATTN), lambda b, i: (b * nq + i, 0)),
        scratch_shapes=[pltpu.VMEM((N_HEADS, 1, t), F32), pltpu.VMEM((N_HEADS, LANES, t), F32)],
        compiler_params=_params("parallel", "arbitrary"),
        name="fox_prompt",
    )(qt, kaug, vtaug, cumt, reft)


def _route(logits):
    lane = lax.broadcasted_iota(jnp.int32, logits.shape, 1)
    big = jnp.int32(LANES)
    is_g = (lane >= N_EXPERTS) & (lane < N_EXPERTS + N_GROUPS)
    gl = jnp.where(is_g, logits, NEG_INF)
    gmax = jnp.max(gl, axis=-1, keepdims=True)
    gi = jnp.min(jnp.where(gl == gmax, lane, big), axis=-1, keepdims=True) - N_EXPERTS
    pg_top = 1.0 / jnp.sum(jnp.exp(gl - gmax), axis=-1, keepdims=True)
    in_g = (lane >= gi * EXPERTS_PER_GROUP) & (lane < (gi + 1) * EXPERTS_PER_GROUP)
    el = jnp.where(in_g, logits, NEG_INF)
    m1 = jnp.max(el, axis=-1, keepdims=True)
    i1 = jnp.min(jnp.where(el == m1, lane, big), axis=-1, keepdims=True)
    el2 = jnp.where(lane == i1, NEG_INF, el)
    m2 = jnp.max(el2, axis=-1, keepdims=True)
    i2 = jnp.min(jnp.where(el2 == m2, lane, big), axis=-1, keepdims=True)
    e2 = jnp.exp(m2 - m1)
    w1 = pg_top / (1.0 + e2)
    return jnp.where(lane == i1, w1, jnp.where(lane == i2, w1 * e2, 0.0)), gi


ROUTE_CHUNK = 32
SORT_BLOCK = 512
SORTED_ROWS = 768
ROUTE_TILE = 256


def _sort_block(h2b, comb, gi, ls_ref, xs_ref, cs_ref, pos_ref, cnt_ref):
    tm = h2b.shape[0]
    lane = lax.broadcasted_iota(jnp.int32, (tm, LANES), 1)
    onehot = jnp.where(lane == gi, 1.0, 0.0)
    before = jnp.dot(ls_ref[...], onehot.astype(BF16), preferred_element_type=F32)
    tot = before[tm - 1:tm] + onehot[tm - 1:tm]
    cnt_ref[0] = tot
    padded = jnp.floor((tot + (ROUTE_CHUNK - 1)) * (1.0 / ROUTE_CHUNK)) * ROUTE_CHUNK
    lane1 = lax.broadcasted_iota(jnp.int32, (1, LANES), 1)
    start = jnp.zeros_like(padded)
    for s in range(1, N_GROUPS):
        start = start + jnp.where(lane1 >= s, pltpu.roll(padded, s, 1), 0.0)
    pos = jnp.sum(onehot * (start + before), axis=-1, keepdims=True)
    pos_ref[...] = pos
    pos_row = jnp.broadcast_to(pos, (tm, LANES)).T[0:1]
    rows = lax.broadcasted_iota(jnp.int32, (SORTED_ROWS, tm), 0).astype(F32)
    perm = jnp.where(rows == pos_row, 1.0, 0.0).astype(BF16)
    xs_ref[...] = jnp.dot(perm, h2b, preferred_element_type=F32).astype(BF16)
    rel = jnp.zeros_like(comb)
    for g in range(N_GROUPS):
        moved = comb if g == 0 else pltpu.roll(comb, LANES - g * EXPERTS_PER_GROUP, 1)
        rel = rel + jnp.where(gi == g, moved, 0.0)
    rel = jnp.where(lane < EXPERTS_PER_GROUP, rel, 0.0)
    hi = rel.astype(BF16)
    lo = (rel - hi.astype(F32)).astype(BF16)
    cs_ref[...] = (jnp.dot(perm, hi, preferred_element_type=F32) + jnp.dot(perm, lo, preferred_element_type=F32))


def _merge_kernel(a_ref, gm_ref, sga_ref, sgg_ref, x_ref, g1_ref, sc2_ref, sh2_ref, wao_ref, wgo_ref, wo_ref,
                  l1g_ref, l1b_ref, wr_ref, br_ref, *rest, alpha, precise, sort):
    t = (sga_ref[...].astype(F32) * _dot(a_ref[...], wao_ref[...], precise)
         + sgg_ref[...].astype(F32) * _dot(gm_ref[...], wgo_ref[...], precise))
    m = _dot(t, wo_ref[...], precise)
    x1 = _ln(alpha * x_ref[...] + (1.0 + _mod(g1_ref)) * m, l1g_ref[...], l1b_ref[...])
    h2 = x1 * (1.0 + _mod(sc2_ref)) + _mod(sh2_ref)
    comb, gi = _route(_dot(h2, wr_ref[...], precise) + br_ref[...])
    if sort:
        ls_ref, x1_ref, xs_ref, cs_ref, pos_ref, cnt_ref = rest
        _sort_block(h2.astype(BF16), comb, gi, ls_ref, xs_ref, cs_ref, pos_ref, cnt_ref)
    else:
        x1_ref, h2_ref, comb_ref = rest
        h2_ref[...] = h2.astype(h2_ref.dtype)
        comb_ref[...] = comb
    x1_ref[...] = x1


def _merge(a, gm, sga, sgg, x, g1, sc2, sh2, wao, wgo, wo, l1g, l1b, wr, br, *, rows_per_mod, tm, alpha, precise,
           sort):
    n, d = x.shape
    row = lambda c: pl.BlockSpec((tm, c), lambda i: (i, 0))
    mod = _mod_spec(g1, tm, rows_per_mod)
    ins = [a, gm, sga, sgg, x, g1, sc2, sh2, wao, wgo, wo, l1g, l1b, wr, br]
    in_specs = [row(D_ATTN), row(gm.shape[1]), row(d), row(d), row(d), mod, mod, mod, _full(wao.shape),
                _full(wgo.shape), _full(wo.shape), _full(l1g.shape), _full(l1b.shape), _full(wr.shape),
                _full(br.shape)]
    if sort:
        assert tm == SORT_BLOCK
        nblk = n // tm
        ls = jnp.tril(jnp.ones((tm, tm), BF16), -1)
        ins.append(ls)
        in_specs.append(_full(ls.shape))
        srow = lambda c: pl.BlockSpec((SORTED_ROWS, c), lambda i: (i, 0))
        out_shape = [((n, d), F32), ((nblk * SORTED_ROWS, d), BF16), ((nblk * SORTED_ROWS, LANES), F32),
                     ((n, 1), F32), ((nblk, 1, LANES), F32)]
        out_specs = [row(d), srow(d), srow(LANES), row(1), pl.BlockSpec((1, 1, LANES), lambda i: (i, 0, 0))]
    else:
        out_shape = [((n, d), F32), ((n, d), BF16), ((n, LANES), F32)]
        out_specs = [row(d), row(d), row(LANES)]
    return pl.pallas_call(
        functools.partial(_merge_kernel, alpha=alpha, precise=precise, sort=sort),
        out_shape=[jax.ShapeDtypeStruct(s, t) for s, t in out_shape],
        grid=(n // tm,),
        in_specs=in_specs,
        out_specs=out_specs,
        compiler_params=_params("parallel"),
        name="merge_precise" if precise else "merge",
    )(*ins)


def _route_tables(counts, nblk):
    ch, cpb, cpt = ROUTE_CHUNK, SORTED_ROWS // ROUTE_CHUNK, ROUTE_TILE // ROUTE_CHUNK
    max_tiles = (nblk * (SORT_BLOCK + N_GROUPS * (ch - 1))) // ROUTE_TILE + N_GROUPS + 1
    cnt = counts[:, 0, :N_GROUPS].astype(jnp.int32)
    nch = (cnt + ch - 1) // ch
    loc = jnp.cumsum(nch, axis=1) - nch
    earlier = jnp.cumsum(nch, axis=0) - nch
    tiles_g = (nch.sum(0) + cpt - 1) // cpt
    tile_off = jnp.cumsum(tiles_g) - tiles_g
    n_tiles = tiles_g.sum()
    k = jnp.arange(cpb, dtype=jnp.int32)[None, None, :]
    used = (k >= loc[:, :, None]) & (k < (loc + nch)[:, :, None])
    glob = (tile_off * cpt)[None, :, None] + earlier[:, :, None] + k - loc[:, :, None]
    dst = jnp.sum(jnp.where(used, glob, 0), axis=1)
    blk_chunk = jnp.arange(nblk, dtype=jnp.int32)[:, None] * cpb + k[0]
    scatter_to = jnp.where(used.any(axis=1), dst, max_tiles * cpt)
    src = jnp.full((max_tiles * cpt,), cpb - 1, jnp.int32).at[scatter_to.reshape(-1)].set(
        blk_chunk.reshape(-1), mode="drop")
    t = jnp.arange(max_tiles, dtype=jnp.int32)
    tile_group = jnp.minimum(jnp.sum(t[:, None] >= (tile_off + tiles_g)[None, :], axis=1), N_GROUPS - 1)
    return src, dst.reshape(-1), tile_group.astype(jnp.int32), n_tiles.reshape(1).astype(jnp.int32), max_tiles


def _experts_kernel(src_ref, grp_ref, nt_ref, *refs):
    del src_ref
    cpt = ROUTE_TILE // ROUTE_CHUNK
    xr, cr = refs[:cpt], refs[cpt:2 * cpt]
    wg_ref, wu_ref, wd_ref, o_ref, wg_s, wu_s, wd_s = refs[2 * cpt:]
    t = pl.program_id(0)
    live = t < nt_ref[0]
    de = wg_ref.shape[3]

    @pl.when((t == 0) | (grp_ref[t] != grp_ref[jnp.maximum(t - 1, 0)]))
    def _():
        for e in range(EXPERTS_PER_GROUP):
            wg_s[:, e * de:(e + 1) * de] = wg_ref[0, e].astype(BF16)
            wu_s[:, e * de:(e + 1) * de] = wu_ref[0, e].astype(BF16)
            wd_s[e * de:(e + 1) * de, :] = wd_ref[0, e].astype(BF16)

    @pl.when(live)
    def _():
        x = jnp.concatenate([r[...] for r in xr], axis=0)
        c = jnp.concatenate([r[...] for r in cr], axis=0)
        hg = jnp.dot(x, wg_s[...], preferred_element_type=F32)
        hu = jnp.dot(x, wu_s[...], preferred_element_type=F32)
        a = jnp.concatenate([(hg[:, e * de:(e + 1) * de] * jax.nn.sigmoid(hg[:, e * de:(e + 1) * de])
                              * hu[:, e * de:(e + 1) * de] * c[:, e:e + 1]).astype(BF16)
                             for e in range(EXPERTS_PER_GROUP)], axis=1)
        o_ref[...] = jnp.dot(a, wd_s[...], preferred_element_type=F32).astype(o_ref.dtype)

    @pl.when(jnp.logical_not(live))
    def _():
        o_ref[...] = jnp.zeros_like(o_ref)


def _experts(src, tile_group, n_tiles, max_tiles, xs, cs, wg, wu, wd):
    d = xs.shape[1]
    de = wg.shape[2]
    cpt = ROUTE_TILE // ROUTE_CHUNK
    grouped = lambda w: w.reshape((N_GROUPS, EXPERTS_PER_GROUP) + w.shape[1:])

    def chunk(width):
        def one(r):
            return pl.BlockSpec((ROUTE_CHUNK, width), lambda t, src, grp, nt: (src[t * cpt + r], 0))
        return [one(r) for r in range(cpt)]

    wspec = lambda w: pl.BlockSpec((1, EXPERTS_PER_GROUP) + w.shape[1:], lambda t, src, grp, nt: (grp[t], 0, 0, 0))
    wide = EXPERTS_PER_GROUP * de
    grid_spec = pltpu.PrefetchScalarGridSpec(
        num_scalar_prefetch=3,
        grid=(max_tiles,),
        in_specs=chunk(d) + chunk(LANES) + [wspec(wg), wspec(wu), wspec(wd)],
        out_specs=pl.BlockSpec((ROUTE_TILE, d), lambda t, src, grp, nt: (t, 0)),
        scratch_shapes=[pltpu.VMEM((d, wide), BF16), pltpu.VMEM((d, wide), BF16), pltpu.VMEM((wide, d), BF16)],
    )
    return pl.pallas_call(
        _experts_kernel,
        out_shape=jax.ShapeDtypeStruct((max_tiles * ROUTE_TILE, d), BF16),
        grid_spec=grid_spec,
        compiler_params=_params("arbitrary"),
        name="experts",
    )(src, tile_group, n_tiles, *([xs] * cpt), *([cs] * cpt), grouped(wg), grouped(wu), grouped(wd))


def _unsort_kernel(dst_ref, *refs, alpha):
    del dst_ref
    cpb = SORTED_ROWS // ROUTE_CHUNK
    yr = refs[:cpb]
    pos_ref, x1_ref, g2_ref, l2g_ref, l2b_ref, o_ref = refs[cpb:]
    ys = jnp.concatenate([r[...] for r in yr], axis=0)
    tm = pos_ref.shape[0]
    cols = lax.broadcasted_iota(jnp.int32, (tm, SORTED_ROWS), 1).astype(F32)
    perm_t = jnp.where(cols == pos_ref[...], 1.0, 0.0).astype(BF16)
    f = jnp.dot(perm_t, ys, preferred_element_type=F32)
    o_ref[...] = _ln(alpha * x1_ref[...] + (1.0 + _mod(g2_ref)) * f, l2g_ref[...], l2b_ref[...])


def _unsort(dst, y, pos, x1, g2, l2g, l2b, *, rows_per_mod, alpha):
    n, d = x1.shape
    tm = SORT_BLOCK
    cpb = SORTED_ROWS // ROUTE_CHUNK

    def chunk(r):
        return pl.BlockSpec((ROUTE_CHUNK, d), lambda i, dst: (dst[i * cpb + r], 0))

    row = lambda c: pl.BlockSpec((tm, c), lambda i, dst: (i, 0))
    full = lambda s: pl.BlockSpec(s, lambda i, dst: (0,) * len(s))
    grid_spec = pltpu.PrefetchScalarGridSpec(
        num_scalar_prefetch=1,
        grid=(n // tm,),
        in_specs=[chunk(r) for r in range(cpb)] + [
            row(1), row(d), pl.BlockSpec((1, 1, d), lambda i, dst: (i * tm // rows_per_mod, 0, 0)),
            full(l2g.shape), full(l2b.shape)],
        out_specs=row(d),
    )
    return pl.pallas_call(
        functools.partial(_unsort_kernel, alpha=alpha),
        out_shape=jax.ShapeDtypeStruct((n, d), F32),
        grid_spec=grid_spec,
        compiler_params=_params("parallel"),
        name="unsort",
    )(dst, *([y] * cpb), pos, x1, g2, l2g, l2b)


def _moe_kernel(h2_ref, comb_ref, x1_ref, g2_ref, wg_ref, wu_ref, wd_ref, l2g_ref, l2b_ref, o_ref, *, alpha):
    e = pl.program_id(0)

    @pl.when(e == 0)
    def _():
        o_ref[...] = jnp.zeros_like(o_ref)

    h2 = h2_ref[...]
    comb = comb_ref[...]
    lane = lax.broadcasted_iota(jnp.int32, comb.shape, 1)
    w = jnp.sum(jnp.where(lane == e, comb, 0.0), axis=-1, keepdims=True)
    hg = jnp.dot(h2, wg_ref[0].astype(BF16), preferred_element_type=F32)
    hu = jnp.dot(h2, wu_ref[0].astype(BF16), preferred_element_type=F32)
    a = hg * jax.nn.sigmoid(hg) * hu * w
    o_ref[...] += jnp.dot(a.astype(BF16), wd_ref[0].astype(BF16), preferred_element_type=F32)

    @pl.when(e == pl.num_programs(0) - 1)
    def _():
        o_ref[...] = _ln(alpha * x1_ref[...] + (1.0 + g2_ref[...]) * o_ref[...], l2g_ref[...], l2b_ref[...])


def _moe(h2, comb, x1, g2, wg, wu, wd, l2g, l2b, *, alpha):
    n, d = x1.shape
    ne, _, de = wg.shape
    return pl.pallas_call(
        functools.partial(_moe_kernel, alpha=alpha),
        out_shape=jax.ShapeDtypeStruct((n, d), F32),
        grid=(ne,),
        in_specs=[_full((n, d)), _full((n, LANES)), _full((n, d)), _full((n, d)),
                  pl.BlockSpec((1, d, de), lambda e: (e, 0, 0)), pl.BlockSpec((1, d, de), lambda e: (e, 0, 0)),
                  pl.BlockSpec((1, de, d), lambda e: (e, 0, 0)), _full(l2g.shape), _full(l2b.shape)],
        out_specs=_full((n, d)),
        compiler_params=_params("arbitrary"),
        name="moe",
    )(h2, comb, x1, g2, wg, wu, wd, l2g, l2b)


def _sample_proj_kernel(x_ref, sc_ref, sh_ref, wt_ref, z_ref):
    h = x_ref[...] * (1.0 + sc_ref[...]) + sh_ref[...]
    z_ref[...] = lax.dot_general(h, wt_ref[...], (((1,), (1,)), ((), ())), precision=HIGHEST,
                                 preferred_element_type=F32)


def _sample_proj(x, sc, sh, wt):
    n, d = x.shape
    return pl.pallas_call(
        _sample_proj_kernel,
        out_shape=jax.ShapeDtypeStruct((n, wt.shape[0]), F32),
        compiler_params=pltpu.CompilerParams(vmem_limit_bytes=VMEM_LIMIT_BYTES),
        name="sample_proj",
    )(x, sc, sh, wt)


def _sample_mix_kernel(z_ref, bf_ref, glg_ref, glb_ref, mg_ref, bs_ref, mc_ref,
                       q_ref, k_ref, v_ref, logf_ref, cn_ref, gv_ref, gm_ref, sga_ref, sgg_ref):
    z = z_ref[...]
    sec = lambda off, width: z[:, off:off + width]
    q, k, v, logf, u, gv, sga, sgg = _mixer_epilogue(
        sec(_OFF_Q, 512), sec(_OFF_K, 512), sec(_OFF_V, 512), sec(_OFF_F, LANES), sec(_OFF_U, 512),
        sec(_OFF_GV, 512), sec(_OFF_GA, 1024), sec(_OFF_GG, 1024), bf_ref[...], glg_ref[...], glb_ref[...])
    q_ref[...] = q
    k_ref[...] = k
    v_ref[...] = v
    logf_ref[...] = logf
    cn_ref[...] = _dot(mc_ref[...], logf, True)
    gv_ref[...] = gv
    sga_ref[...] = sga
    sgg_ref[...] = sgg
    gw = gv.shape[1] // GMLP_GROUPS
    sp = jnp.concatenate([_dot(mg_ref[g], gv[:, g * gw:(g + 1) * gw], True) for g in range(GMLP_GROUPS)], axis=-1)
    gm_ref[...] = u * (sp + bs_ref[...])


def _sample_mix(z, bf, glg, glb, mg, bs, mc):
    n = z.shape[0]
    shapes = [(n, 512)] * 3 + [(n, LANES)] * 2 + [(n, 512)] * 2 + [(n, 1024)] * 2
    return pl.pallas_call(
        _sample_mix_kernel,
        out_shape=[jax.ShapeDtypeStruct(s, F32) for s in shapes],
        compiler_params=pltpu.CompilerParams(vmem_limit_bytes=VMEM_LIMIT_BYTES),
        name="sample_mix",
    )(z, bf, glg, glb, mg, bs, mc)


PAGES_PER_STEP = 16


def _fox_sample_kernel(pt_ref, kt_hbm, vt_hbm, lf_hbm, qbd_ref, cn_ref, kn_ref, vn_ref, cnb_ref, o_ref,
                       kbuf, vbuf, lbuf, sem, m_ref, l_ref, acc_ref, later_ref, *, n_chunks, n_new):
    P = PAGES_PER_STEP
    b, c = pl.program_id(0), pl.program_id(1)
    step = b * n_chunks + c
    n_steps = pl.num_programs(0) * n_chunks
    slot = lax.rem(step, 2)

    def page_copies(seq, chunk, sl, lookup):
        out = []
        for r in range(P):
            page = pt_ref[seq, (n_chunks - 1 - chunk) * P + r] if lookup else 0
            out += [pltpu.make_async_copy(kt_hbm.at[page], kbuf.at[sl, r], sem.at[sl]),
                    pltpu.make_async_copy(vt_hbm.at[page], vbuf.at[sl, r], sem.at[sl]),
                    pltpu.make_async_copy(lf_hbm.at[page], lbuf.at[sl, r], sem.at[sl])]
        return out

    def start_all(cps):
        for i, cp in enumerate(cps):
            cp.start(priority=1 if i % 3 == 1 else 0)

    @pl.when(step == 0)
    def _():
        start_all(page_copies(b, c, slot, True))

    @pl.when(step + 1 < n_steps)
    def _():
        nxt = step + 1
        start_all(page_copies(nxt // n_chunks, lax.rem(nxt, n_chunks), 1 - slot, True))

    @pl.when(c == 0)
    def _():
        m_ref[...] = jnp.full_like(m_ref, NEG_INF)
        l_ref[...] = jnp.zeros_like(l_ref)
        acc_ref[...] = jnp.zeros_like(acc_ref)
        later_ref[...] = jnp.zeros_like(later_ref)

    qbd = qbd_ref[0]
    cn = cn_ref[0]
    nt = (((1,), (1,)), ((), ()))

    def update(s, vs):
        m_old = m_ref[...]
        m_new = m_old
        for sb in s:
            m_new = jnp.maximum(m_new, jnp.max(sb, axis=-1, keepdims=True))
        alpha = jnp.exp(m_old - m_new)
        l_new = l_ref[...] * alpha
        acc = acc_ref[...] * alpha
        for sb, vb in zip(s, vs):
            p = jnp.exp(sb - m_new)
            l_new = l_new + jnp.sum(p, axis=-1, keepdims=True)
            acc = acc + lax.dot_general(p, vb, nt, preferred_element_type=F32)
        m_ref[...] = m_new
        l_ref[...] = l_new
        acc_ref[...] = acc

    for cp in page_copies(b, c, slot, False):
        cp.wait()

    lane = lax.broadcasted_iota(jnp.int32, (N_HEADS, PAGE_SIZE), 1)
    later = later_ref[...]
    bias = [None] * P
    for r in reversed(range(P)):
        y = lbuf[slot, r]
        for k in range(7):
            sh = 1 << k
            y = y + jnp.where(lane < PAGE_SIZE - sh, pltpu.roll(y, PAGE_SIZE - sh, 1), 0.0)
        bias[r] = jnp.where(lane < PAGE_SIZE - 1, pltpu.roll(y, PAGE_SIZE - 1, 1), 0.0) + later
        later = later + y[:, 0:1]
    later_ref[...] = later

    scores = []
    for r in range(P):
        st = jnp.dot(qbd, kbuf[slot, r].reshape(D_ATTN, PAGE_SIZE), preferred_element_type=F32)
        scores.append(st + jnp.concatenate([bias[r]] * n_new, axis=0) + cn)
    update(scores, [vbuf[slot, r].reshape(D_ATTN, PAGE_SIZE) for r in range(P)])

    @pl.when(c == n_chunks - 1)
    def _():
        st = jnp.dot(qbd, kn_ref[0], preferred_element_type=F32) + cn - cnb_ref[0]
        col = lax.broadcasted_iota(jnp.int32, st.shape, 1)
        row = lax.broadcasted_iota(jnp.int32, st.shape, 0)
        update([jnp.where(col <= row // N_HEADS, st, NEG_INF)], [vn_ref[0]])
        full = acc_ref[...] / l_ref[...]
        lane_head = lax.broadcasted_iota(jnp.int32, full.shape, 1) // HEAD_DIM
        row_head = lax.broadcasted_iota(jnp.int32, full.shape, 0) % N_HEADS
        own = jnp.where(lane_head == row_head, full, 0.0)
        rows = [jnp.sum(own[q * N_HEADS:(q + 1) * N_HEADS], axis=0, keepdims=True) for q in range(n_new)]
        o_ref[0] = jnp.concatenate(rows + [jnp.zeros((o_ref.shape[1] - n_new, D_ATTN), F32)], axis=0)


def _fox_sample(page_table, kt_pool, vt_pool, logf_pool, qbd, cn, kn, vn, cnb, *, n_new):
    nb, n_pages = page_table.shape
    P = PAGES_PER_STEP
    n_chunks = n_pages // P
    nrow = n_new * N_HEADS
    per_b = lambda s: pl.BlockSpec((1,) + s, lambda b, c, pt: (b,) + (0,) * len(s))
    hbm = pl.BlockSpec(memory_space=pl.ANY)
    kv_page = (N_HEADS, HEAD_DIM, PAGE_SIZE)
    grid_spec = pltpu.PrefetchScalarGridSpec(
        num_scalar_prefetch=1,
        grid=(nb, n_chunks),
        in_specs=[hbm, hbm, hbm, per_b((nrow, D_ATTN)), per_b((nrow, 1)), per_b((D_ATTN, LANES)),
                  per_b((D_ATTN, LANES)), per_b((nrow, LANES))],
        out_specs=per_b((8, D_ATTN)),
        scratch_shapes=[pltpu.VMEM((2, P) + kv_page, F32), pltpu.VMEM((2, P) + kv_page, F32),
                        pltpu.VMEM((2, P, N_HEADS, PAGE_SIZE), F32), pltpu.SemaphoreType.DMA((2,)),
                        pltpu.VMEM((nrow, 1), F32), pltpu.VMEM((nrow, 1), F32), pltpu.VMEM((nrow, D_ATTN), F32),
                        pltpu.VMEM((N_HEADS, PAGE_SIZE), F32)],
    )
    return pl.pallas_call(
        functools.partial(_fox_sample_kernel, n_chunks=n_chunks, n_new=n_new),
        out_shape=jax.ShapeDtypeStruct((nb, 8, D_ATTN), F32),
        grid_spec=grid_spec,
        compiler_params=_params("arbitrary", "arbitrary"),
        name="fox_sample",
    )(page_table, kt_pool, vt_pool, logf_pool, qbd, cn, kn, vn, cnb)


def _split_cols(z):
    f0 = 3 * D_ATTN
    return z[:, :f0], z[:, f0 + N_HEADS:], jnp.pad(z[:, f0:f0 + N_HEADS], ((0, 0), (0, LANES - N_HEADS)))


def kernel(x_prompt, x_sample, c_prompt, c_sample, cache_k, cache_v, cache_logf, page_table, w_ada, b_ada, w_in,
           b_f, gmlp_ln_g, gmlp_ln_b, w_s, b_s, w_attn_out, w_gmlp_out, w_o, ln1_g, ln1_b, w_group_router,
           b_group_router, w_expert_router, b_expert_router, w_exp_gate, w_exp_up, w_exp_down, ln2_g, ln2_b):
    depth = w_ada.shape[0]
    assert depth == 1
    nbp, seq, d = x_prompt.shape
    nbs, n_new, _ = x_sample.shape
    alpha = (2.0 * depth) ** 0.25
    n_pool = cache_k.shape[1]
    d_gmlp = gmlp_ln_g.shape[1]
    gw = d_gmlp // GMLP_GROUPS

    wa_b, wb_b, wf_b = _split_cols(w_in[0].astype(BF16))
    bf =jnp.pad(b_f[0], (0, LANES - N_HEADS))[None]
    glg, glb = gmlp_ln_g[0][None], gmlp_ln_b[0][None]
    tril = jnp.tril(jnp.ones((CHUNK, CHUNK), F32))
    ws = jnp.where(tril > 0, w_s[0], 0.0)
    bs_tile = jnp.repeat(b_s[0].T, gw, axis=1)
    w_r = jnp.pad(jnp.concatenate([w_expert_router[0], w_group_router[0]], axis=1),
                  ((0, 0), (0, LANES - N_EXPERTS - N_GROUPS)))
    b_r = jnp.pad(jnp.concatenate([b_expert_router[0], b_group_router[0]]), (0, LANES - N_EXPERTS - N_GROUPS))[None]
    l1g, l1b, l2g, l2b = ln1_g[0][None], ln1_b[0][None], ln2_g[0][None], ln2_b[0][None]

    nc = nbp + nbs
    c_all = jnp.pad(jnp.concatenate([c_prompt, c_sample], axis=0), ((0, -nc % 8), (0, 0)))
    mod = _ada(c_all, w_ada[0], b_ada[0][None])
    modp = [m[:, None, :] for m in jnp.split(mod[:nbp], 6, axis=-1)]
    mods = [m[:, None, :] for m in jnp.split(mod[nbp:nc], 6, axis=-1)]

    xp = x_prompt.reshape(nbp * seq, d)
    qt, kt32, kaug, vt32, vtaug, logft, cumt, reft, gm, sga, sgg = _prompt_in(
        xp, modp[1], modp[0], wa_b, wb_b, wf_b, bf, glg, glb, ws.astype(BF16), bs_tile, tril, seq=seq, tm=ATTN_TILE)
    a = _fox_prompt(qt, kaug, vtaug, cumt, reft, seq=seq)
    x1, xs_sorted, cs_sorted, pos, counts = _merge(
        a, gm, sga, sgg, xp, modp[2], modp[4], modp[3], w_attn_out[0].astype(BF16), w_gmlp_out[0].astype(BF16),
        w_o[0].astype(BF16), l1g, l1b, w_r.astype(BF16), b_r, rows_per_mod=seq, tm=SORT_BLOCK, alpha=alpha,
        precise=False, sort=True)
    src, dst, tile_group, n_tiles, max_tiles = _route_tables(counts, nbp * seq // SORT_BLOCK)
    y_sorted = _experts(src, tile_group, n_tiles, max_tiles, xs_sorted, cs_sorted, w_exp_gate[0], w_exp_up[0],
                        w_exp_down[0])
    yp = _unsort(dst, y_sorted, pos, x1, modp[5], l2g, l2b, rows_per_mod=seq, alpha=alpha)

    ns = nbs * n_new
    xs = x_sample.reshape(ns, d)
    rep = lambda m: jnp.repeat(m[:, 0, :], n_new, axis=0)
    z = jnp.concatenate(_split_cols(_sample_proj(xs, rep(mods[1]), rep(mods[0]), w_in[0].T)), axis=1)
    eye_b = jnp.eye(nbs, dtype=F32)
    mg = jnp.stack([jnp.kron(eye_b, ws[g, :n_new, :n_new]) for g in range(GMLP_GROUPS)])
    bs_rows = jnp.tile(bs_tile[:n_new], (nbs, 1))
    mc = jnp.kron(eye_b, tril[:n_new, :n_new])
    qs, ks, vs, logf_s, cn, gv_s, gm_s, sga_s, sgg_s = _sample_mix(z, bf, glg, glb, mg, bs_rows, mc)

    kt_pool = cache_k[0].transpose(0, 2, 3, 1)
    vt_pool = cache_v[0].transpose(0, 2, 3, 1)
    logf_pool = cache_logf[0].transpose(0, 2, 1)
    nrow = n_new * N_HEADS
    q4 = qs.reshape(nbs, n_new, N_HEADS, HEAD_DIM)
    qbd = jnp.einsum("bqhd,hg->bqhgd", q4, jnp.eye(N_HEADS, dtype=F32)).reshape(nbs, nrow, D_ATTN)
    cn3 = cn[:, :N_HEADS].reshape(nbs, n_new, N_HEADS)
    cn_col = cn3.reshape(nbs, nrow, 1)
    cnb = jnp.pad(jnp.tile(cn3.transpose(0, 2, 1), (1, n_new, 1)), ((0, 0), (0, 0), (0, LANES - n_new)))
    new_t = lambda t: jnp.pad(t.reshape(nbs, n_new, D_ATTN).transpose(0, 2, 1), ((0, 0), (0, 0), (0, LANES - n_new)))
    a_s = _fox_sample(page_table, kt_pool, vt_pool, logf_pool, qbd, cn_col, new_t(ks), new_t(vs), cnb, n_new=n_new)
    a_s = a_s[:, :n_new].reshape(ns, D_ATTN)
    x1s, h2s, comb_s = _merge(a_s, gm_s, sga_s, sgg_s, xs, rep(mods[2]), rep(mods[4]), rep(mods[3]),
                              w_attn_out[0], w_gmlp_out[0], w_o[0], l1g, l1b, w_r, b_r,
                              rows_per_mod=ns, tm=ns, alpha=alpha, precise=True, sort=False)
    ys = _moe(h2s, comb_s, x1s, rep(mods[5]), w_exp_gate[0], w_exp_up[0], w_exp_down[0], l2g, l2b, alpha=alpha)

    hs = (N_HEADS, HEAD_DIM)
    untr = lambda t: t.reshape(1, nbp, *hs, seq).transpose(0, 1, 4, 2, 3)
    return (yp.reshape(nbp, seq, d), ys.reshape(nbs, n_new, d), untr(kt32), untr(vt32),
            logft.reshape(1, nbp, N_HEADS, seq).transpose(0, 1, 3, 2),
            ks.reshape(1, nbs, n_new, *hs), vs.reshape(1, nbs, n_new, *hs),
            logf_s[:, :N_HEADS].reshape(1, nbs, n_new, N_HEADS), gv_s.reshape(1, nbs, n_new, d_gmlp))
```

```python
import functools

import numpy as np
import jax
import jax.numpy as jnp
from jax import lax
from jax.experimental import pallas as pl
from jax.experimental.pallas import tpu as pltpu

F32 = jnp.float32
BF16 = jnp.bfloat16
HIGHEST = lax.Precision.HIGHEST

N_HEADS = 8
HEAD_DIM = 64
D_ATTN = N_HEADS * HEAD_DIM
PAGE_SIZE = 128
CHUNK = 128
GMLP_GROUPS = 4
N_GROUPS = 4
EXPERTS_PER_GROUP = 4
N_EXPERTS = N_GROUPS * EXPERTS_PER_GROUP
LN_EPS = 1e-5
LANES = 128
NEG_INF = float("-inf")
LOG2E = 1.4426950408889634

VMEM_LIMIT_BYTES = 56 * 1024 * 1024


def _params(*sem):
    return pltpu.CompilerParams(dimension_semantics=sem, vmem_limit_bytes=VMEM_LIMIT_BYTES)


def _full(shape):
    n = len(shape)
    return pl.BlockSpec(shape, lambda *_: (0,) * n)


def _ln(x, g, b):
    mu = jnp.mean(x, axis=-1, keepdims=True)
    xc = x - mu
    var = jnp.mean(xc * xc, axis=-1, keepdims=True)
    return xc * lax.rsqrt(var + LN_EPS) * g + b


def _mod(ref):
    return ref[0] if len(ref.shape) == 3 else ref[...]


def _mod_spec(m, tm, rows_per_mod):
    if m.ndim == 3:
        return pl.BlockSpec((1, 1, m.shape[-1]), lambda i: (i * tm // rows_per_mod, 0, 0))
    return pl.BlockSpec((tm, m.shape[-1]), lambda i: (i, 0))


def _dot(a, b, precise):
    if precise:
        return jnp.dot(a, b, precision=HIGHEST, preferred_element_type=F32)
    return jnp.dot(a.astype(BF16), b.astype(BF16), preferred_element_type=F32)


def _ada_kernel(c_ref, w_ref, b_ref, o_ref):
    c = c_ref[...]
    o_ref[...] = _dot(c * jax.nn.sigmoid(c), w_ref[...], True) + b_ref[...]


def _ada(c, w, b, *, tn=1024):
    n, d = c.shape
    dout = w.shape[1]
    return pl.pallas_call(
        _ada_kernel,
        out_shape=jax.ShapeDtypeStruct((n, dout), F32),
        grid=(dout // tn,),
        in_specs=[_full((n, d)), pl.BlockSpec((d, tn), lambda j: (0, j)), pl.BlockSpec((1, tn), lambda j: (0, j))],
        out_specs=pl.BlockSpec((n, tn), lambda j: (0, j)),
        compiler_params=_params("parallel"),
        name="ada",
    )(c, w, b)


_OFF_Q, _OFF_K, _OFF_V, _OFF_U, _OFF_GV = 0, 512, 1024, 1536, 2048
_OFF_GA, _OFF_GG, _OFF_F, _W_COLS = 2560, 3584, 4608, 4736


def _mixer_epilogue(zq, zk, zv, zf, zu, zgv, zga, zgg, bf, glg, glb):
    q = zq * (HEAD_DIM ** -0.5)
    logf = jax.nn.log_sigmoid(zf + bf)
    u = jax.nn.gelu(zu)
    gv = _ln(jax.nn.gelu(zgv), glg, glb)
    return q, zk, zv, logf, u, gv, jax.nn.sigmoid(zga), jax.nn.sigmoid(zgg)


def _prompt_in_kernel(x_ref, sc_ref, sh_ref, wa_ref, wb_ref, wf_ref, bf_ref, glg_ref, glb_ref, ws_ref, bs_ref, tri_ref,
                      qt_ref, kt32_ref, kaug_ref, vt32_ref, vtaug_ref, logft_ref, cumt_ref, reft_ref, gm_ref,
                      sga_ref, sgg_ref, carry_ref, sp_ref, *, tiles_per_seq):
    tm = x_ref.shape[0]
    h = (x_ref[...] * (1.0 + sc_ref[0]) + sh_ref[0]).astype(BF16)

    def proj(off, width):
        ref, base = (wa_ref, 0) if off < _OFF_U else (wb_ref, _OFF_U) if off < _OFF_F else (wf_ref, _OFF_F)
        return jnp.dot(h, ref[:, off - base:off - base + width], preferred_element_type=F32)

    q, k, v, logf, u, gv, sga, sgg = _mixer_epilogue(
        proj(_OFF_Q, 512), proj(_OFF_K, 512), proj(_OFF_V, 512), proj(_OFF_F, LANES), proj(_OFF_U, 512),
        proj(_OFF_GV, 512), proj(_OFF_GA, 1024), proj(_OFF_GG, 1024), bf_ref[...], glg_ref[...], glb_ref[...])
    qt_ref[0] = (q * LOG2E).T.astype(BF16)
    kt32_ref[0] = k.T
    vt = v.T
    vt32_ref[0] = vt
    sga_ref[...] = sga.astype(BF16)
    sgg_ref[...] = sgg.astype(BF16)
    logft_ref[0] = logf.T[:N_HEADS]

    @pl.when(pl.program_id(0) % tiles_per_seq == 0)
    def _():
        carry_ref[...] = jnp.zeros_like(carry_ref)

    carry_in = carry_ref[...]
    carry = carry_in
    parts = []
    for n in range(tm // CHUNK):
        c = _dot(tri_ref[...], logf[n * CHUNK:(n + 1) * CHUNK], True) + carry
        parts.append(c)
        carry = c[CHUNK - 1:CHUNK]
    carry_ref[...] = carry
    cum = jnp.concatenate(parts, axis=0)
    cumt_ref[0] = (cum * LOG2E).T[:N_HEADS]
    reft_ref[0] = (jnp.broadcast_to(carry_in, cum.shape) * LOG2E).T[:N_HEADS]

    neg = (carry_in - cum) * LOG2E
    hi = neg.astype(BF16).astype(F32)
    mid = (neg - hi).astype(BF16).astype(F32)
    lo = (neg - hi - mid).astype(BF16).astype(F32)
    lane = lax.broadcasted_iota(jnp.int32, (tm, LANES), 1)
    for hd in range(N_HEADS):
        src = k[:, (hd // 2) * LANES:(hd // 2 + 1) * LANES]
        if hd % 2:
            src = pltpu.roll(src, HEAD_DIM, 1)
        blk = jnp.where(lane < HEAD_DIM, src, 0.0)
        for i, piece in enumerate((hi, mid, lo)):
            blk = jnp.where(lane == HEAD_DIM + i, piece[:, hd:hd + 1], blk)
        kaug_ref[:, hd * LANES:(hd + 1) * LANES] = blk.astype(BF16)

    ones_rows = jnp.where(lax.broadcasted_iota(jnp.int32, (HEAD_DIM, tm), 0) == 0, 1.0, 0.0)
    for hd in range(N_HEADS):
        grp = jnp.concatenate([vt[hd * HEAD_DIM:(hd + 1) * HEAD_DIM], ones_rows], axis=0)
        vtaug_ref[0, hd * LANES:(hd + 1) * LANES, :] = grp.astype(BF16)

    gvb = gv.astype(BF16)
    gw = gv.shape[1] // GMLP_GROUPS
    for n in range(tm // CHUNK):
        for g in range(GMLP_GROUPS):
            sp_ref[n * CHUNK:(n + 1) * CHUNK, g * gw:(g + 1) * gw] = jnp.dot(
                ws_ref[g], gvb[n * CHUNK:(n + 1) * CHUNK, g * gw:(g + 1) * gw], preferred_element_type=F32)
    bs = jnp.concatenate([bs_ref[...]] * (tm // CHUNK), axis=0)
    gm_ref[...] = (u * (sp_ref[...] + bs)).astype(BF16)


def _prompt_in(x, sc, sh, wa, wb, wf, bf, glg, glb, ws, bs, tri, *, seq, tm):
    n, d = x.shape
    nb = n // seq
    tps = seq // tm
    row = lambda c: pl.BlockSpec((tm, c), lambda i: (i, 0))
    mod = pl.BlockSpec((1, 1, d), lambda i: (i // tps, 0, 0))
    tr = lambda r: pl.BlockSpec((1, r, tm), lambda i: (i // tps, 0, i % tps))
    aug = N_HEADS * LANES
    outs = [((nb, 512, seq), BF16), ((nb, 512, seq), F32), ((n, aug), BF16), ((nb, 512, seq), F32),
            ((nb, aug, seq), BF16), ((nb, N_HEADS, seq), F32), ((nb, N_HEADS, seq), F32),
            ((nb, N_HEADS, seq), F32), ((n, 512), BF16), ((n, 1024), BF16), ((n, 1024), BF16)]
    out_specs = [tr(512), tr(512), row(aug), tr(512), tr(aug), tr(N_HEADS), tr(N_HEADS), tr(N_HEADS),
                 row(512), row(1024), row(1024)]
    return pl.pallas_call(
        functools.partial(_prompt_in_kernel, tiles_per_seq=tps),
        out_shape=[jax.ShapeDtypeStruct(s, t) for s, t in outs],
        grid=(n // tm,),
        in_specs=[row(d), mod, mod, _full(wa.shape), _full(wb.shape), _full(wf.shape), _full(bf.shape),
                  _full(glg.shape), _full(glb.shape), _full(ws.shape), _full(bs.shape), _full(tri.shape)],
        out_specs=out_specs,
        scratch_shapes=[pltpu.VMEM((1, LANES), F32), pltpu.VMEM((tm, 512), F32)],
        compiler_params=_params("arbitrary"),
        name="prompt_in",
    )(x, sc, sh, wa, wb, wf, bf, glg, glb, ws, bs, tri)


ATTN_TILE = 512
QK_AHEAD = 2


def _fox_prompt_kernel(qt_ref, kaug_ref, vtaug_ref, cumt_ref, reft_ref, o_ref, m_ref, acc_ref):
    t = ATTN_TILE
    qi = pl.program_id(1)
    q0 = pl.multiple_of(qi * t, t)
    m_ref[...] = jnp.full_like(m_ref, NEG_INF)
    acc_ref[...] = jnp.zeros_like(acc_ref)
    ones3 = jnp.where(lax.broadcasted_iota(jnp.int32, (HEAD_DIM, t), 0) < 3, 1.0, 0.0).astype(BF16)
    qaug = [jnp.concatenate([qt_ref[h * HEAD_DIM:(h + 1) * HEAD_DIM, :], ones3], axis=0) for h in range(N_HEADS)]
    cb = [cumt_ref[h:h + 1, pl.ds(q0, t)] for h in range(N_HEADS)]

    def run(tiles):
        starts = [pl.multiple_of(j * t, t) for j, _ in tiles]
        rounds = [(i, h) for i in range(len(tiles)) for h in range(N_HEADS)]

        def scores(i, h):
            return jnp.dot(kaug_ref[pl.ds(starts[i], t), h * LANES:(h + 1) * LANES], qaug[h],
                           preferred_element_type=F32)

        raw = [scores(*r) for r in rounds[:QK_AHEAD]]
        for n, (i, h) in enumerate(rounds):
            if n + QK_AHEAD < len(rounds):
                raw.append(scores(*rounds[n + QK_AHEAD]))
            ks, grp = starts[i], slice(h * LANES, (h + 1) * LANES)
            st = raw[n]
            if tiles[i][1]:
                st = jnp.where(lax.broadcasted_iota(jnp.int32, (t, t), 0) <= lax.broadcasted_iota(jnp.int32, (t, t), 1),
                               st, NEG_INF)
            ref_j = jnp.concatenate([reft_ref[h:h + 1, pl.ds(ks, LANES)]] * (t // LANES), axis=1)
            crow = cb[h] - ref_j
            m_old = m_ref[h]
            m_new = jnp.maximum(m_old, jnp.max(st, axis=0, keepdims=True) + crow)
            pt = jnp.exp2(st - (m_new - crow)).astype(BF16)
            acc_ref[h] = acc_ref[h] * jnp.exp2(m_old - m_new) + jnp.dot(
                vtaug_ref[grp, pl.ds(ks, t)], pt, preferred_element_type=F32)
            m_ref[h] = m_new

    lax.fori_loop(0, qi // 2, lambda jj, c: (run([(2 * jj, False), (2 * jj + 1, False)]), c)[1], 0)

    @pl.when(qi % 2 == 1)
    def _():
        run([(qi - 1, False), (qi, True)])

    @pl.when(qi % 2 == 0)
    def _():
        run([(qi, True)])

    outs = []
    for h in range(N_HEADS):
        acc = acc_ref[h]
        outs.append(acc[:HEAD_DIM] / acc[HEAD_DIM:HEAD_DIM + 1])
    o_ref[...] = jnp.concatenate(outs, axis=0).T.astype(o_ref.dtype)


def _fox_prompt(qt, kaug, vtaug, cumt, reft, *, seq):
    nb = qt.shape[0]
    t = ATTN_TILE
    nq = seq // t
    aug = N_HEADS * LANES
    per_b = lambda r: pl.BlockSpec((None, r, seq), lambda b, i: (b, 0, 0))
    return pl.pallas_call(
        _fox_prompt_kernel,
        out_shape=jax.ShapeDtypeStruct((nb * seq, D_ATTN), BF16),
        grid=(nb, nq),
        in_specs=[pl.BlockSpec((None, D_ATTN, t), lambda b, i: (b, 0, i)),
                  pl.BlockSpec((seq, aug), lambda b, i: (b, 0)),
                  per_b(aug), per_b(N_HEADS), per_b(N_HEADS)],
        out_specs=pl.BlockSpec((t, D_ATTN), lambda b, i: (b * nq + i, 0)),
        scratch_shapes=[pltpu.VMEM((N_HEADS, 1, t), F32), pltpu.VMEM((N_HEADS, LANES, t), F32)],
        compiler_params=_params("parallel", "arbitrary"),
        name="fox_prompt",
    )(qt, kaug, vtaug, cumt, reft)


def _route(logits):
    lane = lax.broadcasted_iota(jnp.int32, logits.shape, 1)
    big = jnp.int32(LANES)
    is_g = (lane >= N_EXPERTS) & (lane < N_EXPERTS + N_GROUPS)
    gl = jnp.where(is_g, logits, NEG_INF)
    gmax = jnp.max(gl, axis=-1, keepdims=True)
    gi = jnp.min(jnp.where(gl == gmax, lane, big), axis=-1, keepdims=True) - N_EXPERTS
    pg_top = 1.0 / jnp.sum(jnp.exp(gl - gmax), axis=-1, keepdims=True)
    in_g = (lane >= gi * EXPERTS_PER_GROUP) & (lane < (gi + 1) * EXPERTS_PER_GROUP)
    el = jnp.where(in_g, logits, NEG_INF)
    m1 = jnp.max(el, axis=-1, keepdims=True)
    i1 = jnp.min(jnp.where(el == m1, lane, big), axis=-1, keepdims=True)
    el2 = jnp.where(lane == i1, NEG_INF, el)
    m2 = jnp.max(el2, axis=-1, keepdims=True)
    i2 = jnp.min(jnp.where(el2 == m2, lane, big), axis=-1, keepdims=True)
    e2 = jnp.exp(m2 - m1)
    w1 = pg_top / (1.0 + e2)
    return jnp.where(lane == i1, w1, jnp.where(lane == i2, w1 * e2, 0.0)), gi


ROUTE_CHUNK = 32
SORT_BLOCK = 512
SORTED_ROWS = 768
ROUTE_TILE = 256


def _sort_block(h2b, comb, gi, ls_ref, xs_ref, cs_ref, pos_ref, cnt_ref):
    tm = h2b.shape[0]
    lane = lax.broadcasted_iota(jnp.int32, (tm, LANES), 1)
    onehot = jnp.where(lane == gi, 1.0, 0.0)
    before = jnp.dot(ls_ref[...], onehot.astype(BF16), preferred_element_type=F32)
    tot = before[tm - 1:tm] + onehot[tm - 1:tm]
    cnt_ref[0] = tot
    padded = jnp.floor((tot + (ROUTE_CHUNK - 1)) * (1.0 / ROUTE_CHUNK)) * ROUTE_CHUNK
    lane1 = lax.broadcasted_iota(jnp.int32, (1, LANES), 1)
    start = jnp.zeros_like(padded)
    for s in range(1, N_GROUPS):
        start = start + jnp.where(lane1 >= s, pltpu.roll(padded, s, 1), 0.0)
    pos = jnp.sum(onehot * (start + before), axis=-1, keepdims=True)
    pos_ref[...] = pos
    pos_row = jnp.broadcast_to(pos, (tm, LANES)).T[0:1]
    rows = lax.broadcasted_iota(jnp.int32, (SORTED_ROWS, tm), 0).astype(F32)
    perm = jnp.where(rows == pos_row, 1.0, 0.0).astype(BF16)
    xs_ref[...] = jnp.dot(perm, h2b, preferred_element_type=F32).astype(BF16)
    rel = jnp.zeros_like(comb)
    for g in range(N_GROUPS):
        moved = comb if g == 0 else pltpu.roll(comb, LANES - g * EXPERTS_PER_GROUP, 1)
        rel = rel + jnp.where(gi == g, moved, 0.0)
    rel = jnp.where(lane < EXPERTS_PER_GROUP, rel, 0.0)
    hi = rel.astype(BF16)
    lo = (rel - hi.astype(F32)).astype(BF16)
    cs_ref[...] = (jnp.dot(perm, hi, preferred_element_type=F32) + jnp.dot(perm, lo, preferred_element_type=F32))


def _merge_kernel(a_ref, gm_ref, sga_ref, sgg_ref, x_ref, g1_ref, sc2_ref, sh2_ref, wao_ref, wgo_ref, wo_ref,
                  l1g_ref, l1b_ref, wr_ref, br_ref, *rest, alpha, precise, sort):
    t = (sga_ref[...].astype(F32) * _dot(a_ref[...], wao_ref[...], precise)
         + sgg_ref[...].astype(F32) * _dot(gm_ref[...], wgo_ref[...], precise))
    m = _dot(t, wo_ref[...], precise)
    x1 = _ln(alpha * x_ref[...] + (1.0 + _mod(g1_ref)) * m, l1g_ref[...], l1b_ref[...])
    h2 = x1 * (1.0 + _mod(sc2_ref)) + _mod(sh2_ref)
    comb, gi = _route(_dot(h2, wr_ref[...], precise) + br_ref[...])
    if sort:
        ls_ref, x1_ref, xs_ref, cs_ref, pos_ref, cnt_ref = rest
        _sort_block(h2.astype(BF16), comb, gi, ls_ref, xs_ref, cs_ref, pos_ref, cnt_ref)
    else:
        x1_ref, h2_ref, comb_ref = rest
        h2_ref[...] = h2.astype(h2_ref.dtype)
        comb_ref[...] = comb
    x1_ref[...] = x1


def _merge(a, gm, sga, sgg, x, g1, sc2, sh2, wao, wgo, wo, l1g, l1b, wr, br, *, rows_per_mod, tm, alpha, precise,
           sort):
    n, d = x.shape
    row = lambda c: pl.BlockSpec((tm, c), lambda i: (i, 0))
    mod = _mod_spec(g1, tm, rows_per_mod)
    ins = [a, gm, sga, sgg, x, g1, sc2, sh2, wao, wgo, wo, l1g, l1b, wr, br]
    in_specs = [row(D_ATTN), row(gm.shape[1]), row(d), row(d), row(d), mod, mod, mod, _full(wao.shape),
                _full(wgo.shape), _full(wo.shape), _full(l1g.shape), _full(l1b.shape), _full(wr.shape),
                _full(br.shape)]
    if sort:
        assert tm == SORT_BLOCK
        nblk = n // tm
        ls = jnp.tril(jnp.ones((tm, tm), BF16), -1)
        ins.append(ls)
        in_specs.append(_full(ls.shape))
        srow = lambda c: pl.BlockSpec((SORTED_ROWS, c), lambda i: (i, 0))
        out_shape = [((n, d), F32), ((nblk * SORTED_ROWS, d), BF16), ((nblk * SORTED_ROWS, LANES), F32),
                     ((n, 1), F32), ((nblk, 1, LANES), F32)]
        out_specs = [row(d), srow(d), srow(LANES), row(1), pl.BlockSpec((1, 1, LANES), lambda i: (i, 0, 0))]
    else:
        out_shape = [((n, d), F32), ((n, d), BF16), ((n, LANES), F32)]
        out_specs = [row(d), row(d), row(LANES)]
    return pl.pallas_call(
        functools.partial(_merge_kernel, alpha=alpha, precise=precise, sort=sort),
        out_shape=[jax.ShapeDtypeStruct(s, t) for s, t in out_shape],
        grid=(n // tm,),
        in_specs=in_specs,
        out_specs=out_specs,
        compiler_params=_params("parallel"),
        name="merge_precise" if precise else "merge",
    )(*ins)


def _route_tables(counts, nblk):
    ch, cpb, cpt = ROUTE_CHUNK, SORTED_ROWS // ROUTE_CHUNK, ROUTE_TILE // ROUTE_CHUNK
    max_tiles = (nblk * (SORT_BLOCK + N_GROUPS * (ch - 1))) // ROUTE_TILE + N_GROUPS + 1
    cnt = counts[:, 0, :N_GROUPS].astype(jnp.int32)
    nch = (cnt + ch - 1) // ch
    loc = jnp.cumsum(nch, axis=1) - nch
    earlier = jnp.cumsum(nch, axis=0) - nch
    tiles_g = (nch.sum(0) + cpt - 1) // cpt
    tile_off = jnp.cumsum(tiles_g) - tiles_g
    n_tiles = tiles_g.sum()
    k = jnp.arange(cpb, dtype=jnp.int32)[None, None, :]
    used = (k >= loc[:, :, None]) & (k < (loc + nch)[:, :, None])
    glob = (tile_off * cpt)[None, :, None] + earlier[:, :, None] + k - loc[:, :, None]
    dst = jnp.sum(jnp.where(used, glob, 0), axis=1)
    blk_chunk = jnp.arange(nblk, dtype=jnp.int32)[:, None] * cpb + k[0]
    scatter_to = jnp.where(used.any(axis=1), dst, max_tiles * cpt)
    src = jnp.full((max_tiles * cpt,), cpb - 1, jnp.int32).at[scatter_to.reshape(-1)].set(
        blk_chunk.reshape(-1), mode="drop")
    t = jnp.arange(max_tiles, dtype=jnp.int32)
    tile_group = jnp.minimum(jnp.sum(t[:, None] >= (tile_off + tiles_g)[None, :], axis=1), N_GROUPS - 1)
    return src, dst.reshape(-1), tile_group.astype(jnp.int32), n_tiles.reshape(1).astype(jnp.int32), max_tiles


def _chunk_ring(table_ref, srcs, bufs, sem, step, n_steps, per_step):
    slot = lax.rem(step, 2)

    def copies(s, sl, lookup):
        out = []
        for r in range(per_step):
            idx = table_ref[s * per_step + r] if lookup else 0
            out += [pltpu.make_async_copy(src.at[idx], buf.at[sl, r], sem.at[sl]) for src, buf in zip(srcs, bufs)]
        return out

    @pl.when(step == 0)
    def _():
        for cp in copies(step, slot, True):
            cp.start()

    @pl.when(step + 1 < n_steps)
    def _():
        for cp in copies(step + 1, 1 - slot, True):
            cp.start()

    for cp in copies(step, slot, False):
        cp.wait()
    return slot


def _experts_kernel(src_ref, grp_ref, nt_ref, xs_hbm, cs_hbm, wg_ref, wu_ref, wd_ref, o_ref,
                    xbuf, cbuf, sem, wg_s, wu_s, wd_s):
    cpt = ROUTE_TILE // ROUTE_CHUNK
    t = pl.program_id(0)
    slot = _chunk_ring(src_ref, (xs_hbm, cs_hbm), (xbuf, cbuf), sem, t, pl.num_programs(0), cpt)
    live = t < nt_ref[0]
    de = wg_ref.shape[3]

    @pl.when((t == 0) | (grp_ref[t] != grp_ref[jnp.maximum(t - 1, 0)]))
    def _():
        for e in range(EXPERTS_PER_GROUP):
            wg_s[:, e * de:(e + 1) * de] = wg_ref[0, e].astype(BF16)
            wu_s[:, e * de:(e + 1) * de] = wu_ref[0, e].astype(BF16)
            wd_s[e * de:(e + 1) * de, :] = wd_ref[0, e].astype(BF16)

    @pl.when(live)
    def _():
        x = xbuf[slot].reshape(ROUTE_TILE, xbuf.shape[-1])
        c = cbuf[slot].reshape(ROUTE_TILE, LANES)
        hg = jnp.dot(x, wg_s[...], preferred_element_type=F32)
        hu = jnp.dot(x, wu_s[...], preferred_element_type=F32)
        a = jnp.concatenate([(hg[:, e * de:(e + 1) * de] * jax.nn.sigmoid(hg[:, e * de:(e + 1) * de])
                              * hu[:, e * de:(e + 1) * de] * c[:, e:e + 1]).astype(BF16)
                             for e in range(EXPERTS_PER_GROUP)], axis=1)
        o_ref[...] = jnp.dot(a, wd_s[...], preferred_element_type=F32).astype(o_ref.dtype)

    @pl.when(jnp.logical_not(live))
    def _():
        o_ref[...] = jnp.zeros_like(o_ref)


def _experts(src, tile_group, n_tiles, max_tiles, xs, cs, wg, wu, wd):
    d = xs.shape[1]
    de = wg.shape[2]
    cpt = ROUTE_TILE // ROUTE_CHUNK
    grouped = lambda w: w.reshape((N_GROUPS, EXPERTS_PER_GROUP) + w.shape[1:])
    chunked = lambda a: a.reshape(-1, ROUTE_CHUNK, a.shape[1])
    hbm = pl.BlockSpec(memory_space=pl.ANY)
    wspec = lambda w: pl.BlockSpec((1, EXPERTS_PER_GROUP) + w.shape[1:], lambda t, src, grp, nt: (grp[t], 0, 0, 0))
    wide = EXPERTS_PER_GROUP * de
    grid_spec = pltpu.PrefetchScalarGridSpec(
        num_scalar_prefetch=3,
        grid=(max_tiles,),
        in_specs=[hbm, hbm, wspec(wg), wspec(wu), wspec(wd)],
        out_specs=pl.BlockSpec((ROUTE_TILE, d), lambda t, src, grp, nt: (t, 0)),
        scratch_shapes=[pltpu.VMEM((2, cpt, ROUTE_CHUNK, d), BF16), pltpu.VMEM((2, cpt, ROUTE_CHUNK, LANES), F32),
                        pltpu.SemaphoreType.DMA((2,)),
                        pltpu.VMEM((d, wide), BF16), pltpu.VMEM((d, wide), BF16), pltpu.VMEM((wide, d), BF16)],
    )
    return pl.pallas_call(
        _experts_kernel,
        out_shape=jax.ShapeDtypeStruct((max_tiles * ROUTE_TILE, d), BF16),
        grid_spec=grid_spec,
        compiler_params=_params("arbitrary"),
        name="experts",
    )(src, tile_group, n_tiles, chunked(xs), chunked(cs), grouped(wg), grouped(wu), grouped(wd))


def _unsort_kernel(dst_ref, y_hbm, pos_ref, x1_ref, g2_ref, l2g_ref, l2b_ref, o_ref, ybuf, sem, *, alpha):
    cpb = SORTED_ROWS // ROUTE_CHUNK
    slot = _chunk_ring(dst_ref, (y_hbm,), (ybuf,), sem, pl.program_id(0), pl.num_programs(0), cpb)
    ys = ybuf[slot].reshape(SORTED_ROWS, ybuf.shape[-1])
    tm = pos_ref.shape[0]
    cols = lax.broadcasted_iota(jnp.int32, (tm, SORTED_ROWS), 1).astype(F32)
    perm_t = jnp.where(cols == pos_ref[...], 1.0, 0.0).astype(BF16)
    f = jnp.dot(perm_t, ys, preferred_element_type=F32)
    o_ref[...] = _ln(alpha * x1_ref[...] + (1.0 + _mod(g2_ref)) * f, l2g_ref[...], l2b_ref[...])


def _unsort(dst, y, pos, x1, g2, l2g, l2b, *, rows_per_mod, alpha):
    n, d = x1.shape
    tm = SORT_BLOCK
    cpb = SORTED_ROWS // ROUTE_CHUNK
    row = lambda c: pl.BlockSpec((tm, c), lambda i, dst: (i, 0))
    full = lambda s: pl.BlockSpec(s, lambda i, dst: (0,) * len(s))
    grid_spec = pltpu.PrefetchScalarGridSpec(
        num_scalar_prefetch=1,
        grid=(n // tm,),
        in_specs=[pl.BlockSpec(memory_space=pl.ANY), row(1), row(d),
                  pl.BlockSpec((1, 1, d), lambda i, dst: (i * tm // rows_per_mod, 0, 0)),
                  full(l2g.shape), full(l2b.shape)],
        out_specs=row(d),
        scratch_shapes=[pltpu.VMEM((2, cpb, ROUTE_CHUNK, d), BF16), pltpu.SemaphoreType.DMA((2,))],
    )
    return pl.pallas_call(
        functools.partial(_unsort_kernel, alpha=alpha),
        out_shape=jax.ShapeDtypeStruct((n, d), F32),
        grid_spec=grid_spec,
        compiler_params=_params("arbitrary"),
        name="unsort",
    )(dst, y.reshape(-1, ROUTE_CHUNK, d), pos, x1, g2, l2g, l2b)


def _moe_kernel(h2_ref, comb_ref, x1_ref, g2_ref, wg_ref, wu_ref, wd_ref, l2g_ref, l2b_ref, o_ref, *, alpha):
    e = pl.program_id(0)

    @pl.when(e == 0)
    def _():
        o_ref[...] = jnp.zeros_like(o_ref)

    h2 = h2_ref[...]
    comb = comb_ref[...]
    lane = lax.broadcasted_iota(jnp.int32, comb.shape, 1)
    w = jnp.sum(jnp.where(lane == e, comb, 0.0), axis=-1, keepdims=True)
    hg = jnp.dot(h2, wg_ref[0].astype(BF16), preferred_element_type=F32)
    hu = jnp.dot(h2, wu_ref[0].astype(BF16), preferred_element_type=F32)
    a = hg * jax.nn.sigmoid(hg) * hu * w
    o_ref[...] += jnp.dot(a.astype(BF16), wd_ref[0].astype(BF16), preferred_element_type=F32)

    @pl.when(e == pl.num_programs(0) - 1)
    def _():
        o_ref[...] = _ln(alpha * x1_ref[...] + (1.0 + g2_ref[...]) * o_ref[...], l2g_ref[...], l2b_ref[...])


def _moe(h2, comb, x1, g2, wg, wu, wd, l2g, l2b, *, alpha):
    n, d = x1.shape
    ne, _, de = wg.shape
    return pl.pallas_call(
        functools.partial(_moe_kernel, alpha=alpha),
        out_shape=jax.ShapeDtypeStruct((n, d), F32),
        grid=(ne,),
        in_specs=[_full((n, d)), _full((n, LANES)), _full((n, d)), _full((n, d)),
                  pl.BlockSpec((1, d, de), lambda e: (e, 0, 0)), pl.BlockSpec((1, d, de), lambda e: (e, 0, 0)),
                  pl.BlockSpec((1, de, d), lambda e: (e, 0, 0)), _full(l2g.shape), _full(l2b.shape)],
        out_specs=_full((n, d)),
        compiler_params=_params("arbitrary"),
        name="moe",
    )(h2, comb, x1, g2, wg, wu, wd, l2g, l2b)


def _sample_proj_kernel(x_ref, sc_ref, sh_ref, wt_ref, z_ref):
    h = x_ref[...] * (1.0 + sc_ref[...]) + sh_ref[...]
    z_ref[...] = lax.dot_general(h, wt_ref[...], (((1,), (1,)), ((), ())), precision=HIGHEST,
                                 preferred_element_type=F32)


def _sample_proj(x, sc, sh, wt):
    n, d = x.shape
    return pl.pallas_call(
        _sample_proj_kernel,
        out_shape=jax.ShapeDtypeStruct((n, wt.shape[0]), F32),
        compiler_params=pltpu.CompilerParams(vmem_limit_bytes=VMEM_LIMIT_BYTES),
        name="sample_proj",
    )(x, sc, sh, wt)


def _sample_mix_kernel(z_ref, bf_ref, glg_ref, glb_ref, mg_ref, bs_ref, mc_ref,
                       q_ref, k_ref, v_ref, logf_ref, cn_ref, gv_ref, gm_ref, sga_ref, sgg_ref):
    z = z_ref[...]
    sec = lambda off, width: z[:, off:off + width]
    q, k, v, logf, u, gv, sga, sgg = _mixer_epilogue(
        sec(_OFF_Q, 512), sec(_OFF_K, 512), sec(_OFF_V, 512), sec(_OFF_F, LANES), sec(_OFF_U, 512),
        sec(_OFF_GV, 512), sec(_OFF_GA, 1024), sec(_OFF_GG, 1024), bf_ref[...], glg_ref[...], glb_ref[...])
    q_ref[...] = q
    k_ref[...] = k
    v_ref[...] = v
    logf_ref[...] = logf
    cn_ref[...] = _dot(mc_ref[...], logf, True)
    gv_ref[...] = gv
    sga_ref[...] = sga
    sgg_ref[...] = sgg
    gw = gv.shape[1] // GMLP_GROUPS
    sp = jnp.concatenate([_dot(mg_ref[g], gv[:, g * gw:(g + 1) * gw], True) for g in range(GMLP_GROUPS)], axis=-1)
    gm_ref[...] = u * (sp + bs_ref[...])


def _sample_mix(z, bf, glg, glb, mg, bs, mc):
    n = z.shape[0]
    shapes = [(n, 512)] * 3 + [(n, LANES)] * 2 + [(n, 512)] * 2 + [(n, 1024)] * 2
    return pl.pallas_call(
        _sample_mix_kernel,
        out_shape=[jax.ShapeDtypeStruct(s, F32) for s in shapes],
        compiler_params=pltpu.CompilerParams(vmem_limit_bytes=VMEM_LIMIT_BYTES),
        name="sample_mix",
    )(z, bf, glg, glb, mg, bs, mc)


PAGES_PER_STEP = 16


def _fox_sample_kernel(pt_ref, kt_hbm, vt_hbm, lf_hbm, qbd_ref, cn_ref, kn_ref, vn_ref, cnb_ref, o_ref,
                       kbuf, vbuf, lbuf, sem, m_ref, l_ref, acc_ref, later_ref, *, n_chunks, n_new):
    P = PAGES_PER_STEP
    b, c = pl.program_id(0), pl.program_id(1)
    step = b * n_chunks + c
    n_steps = pl.num_programs(0) * n_chunks
    slot = lax.rem(step, 2)

    def page_copies(seq, chunk, sl, lookup):
        out = []
        for r in range(P):
            page = pt_ref[seq, (n_chunks - 1 - chunk) * P + r] if lookup else 0
            out += [pltpu.make_async_copy(kt_hbm.at[page], kbuf.at[sl, r], sem.at[sl]),
                    pltpu.make_async_copy(vt_hbm.at[page], vbuf.at[sl, r], sem.at[sl]),
                    pltpu.make_async_copy(lf_hbm.at[page], lbuf.at[sl, r], sem.at[sl])]
        return out

    def start_all(cps):
        for i, cp in enumerate(cps):
            cp.start(priority=1 if i % 3 == 1 else 0)

    @pl.when(step == 0)
    def _():
        start_all(page_copies(b, c, slot, True))

    @pl.when(step + 1 < n_steps)
    def _():
        nxt = step + 1
        start_all(page_copies(nxt // n_chunks, lax.rem(nxt, n_chunks), 1 - slot, True))

    @pl.when(c == 0)
    def _():
        m_ref[...] = jnp.full_like(m_ref, NEG_INF)
        l_ref[...] = jnp.zeros_like(l_ref)
        acc_ref[...] = jnp.zeros_like(acc_ref)
        later_ref[...] = jnp.zeros_like(later_ref)

    qbd = qbd_ref[0]
    cn = cn_ref[0]
    nt = (((1,), (1,)), ((), ()))

    def update(s, vs):
        m_old = m_ref[...]
        m_new = m_old
        for sb in s:
            m_new = jnp.maximum(m_new, jnp.max(sb, axis=-1, keepdims=True))
        alpha = jnp.exp(m_old - m_new)
        l_new = l_ref[...] * alpha
        acc = acc_ref[...] * alpha
        for sb, vb in zip(s, vs):
            p = jnp.exp(sb - m_new)
            l_new = l_new + jnp.sum(p, axis=-1, keepdims=True)
            acc = acc + lax.dot_general(p, vb, nt, preferred_element_type=F32)
        m_ref[...] = m_new
        l_ref[...] = l_new
        acc_ref[...] = acc

    for cp in page_copies(b, c, slot, False):
        cp.wait()

    lane = lax.broadcasted_iota(jnp.int32, (N_HEADS, PAGE_SIZE), 1)
    later = later_ref[...]
    bias = [None] * P
    for r in reversed(range(P)):
        y = lbuf[slot, r]
        for k in range(7):
            sh = 1 << k
            y = y + jnp.where(lane < PAGE_SIZE - sh, pltpu.roll(y, PAGE_SIZE - sh, 1), 0.0)
        bias[r] = jnp.where(lane < PAGE_SIZE - 1, pltpu.roll(y, PAGE_SIZE - 1, 1), 0.0) + later
        later = later + y[:, 0:1]
    later_ref[...] = later

    scores = []
    for r in range(P):
        st = jnp.dot(qbd, kbuf[slot, r].reshape(D_ATTN, PAGE_SIZE), preferred_element_type=F32)
        scores.append(st + jnp.concatenate([bias[r]] * n_new, axis=0) + cn)
    update(scores, [vbuf[slot, r].reshape(D_ATTN, PAGE_SIZE) for r in range(P)])

    @pl.when(c == n_chunks - 1)
    def _():
        st = jnp.dot(qbd, kn_ref[0], preferred_element_type=F32) + cn - cnb_ref[0]
        col = lax.broadcasted_iota(jnp.int32, st.shape, 1)
        row = lax.broadcasted_iota(jnp.int32, st.shape, 0)
        update([jnp.where(col <= row // N_HEADS, st, NEG_INF)], [vn_ref[0]])
        full = acc_ref[...] / l_ref[...]
        lane_head = lax.broadcasted_iota(jnp.int32, full.shape, 1) // HEAD_DIM
        row_head = lax.broadcasted_iota(jnp.int32, full.shape, 0) % N_HEADS
        own = jnp.where(lane_head == row_head, full, 0.0)
        rows = [jnp.sum(own[q * N_HEADS:(q + 1) * N_HEADS], axis=0, keepdims=True) for q in range(n_new)]
        o_ref[0] = jnp.concatenate(rows + [jnp.zeros((o_ref.shape[1] - n_new, D_ATTN), F32)], axis=0)


def _fox_sample(page_table, kt_pool, vt_pool, logf_pool, qbd, cn, kn, vn, cnb, *, n_new):
    nb, n_pages = page_table.shape
    P = PAGES_PER_STEP
    n_chunks = n_pages // P
    nrow = n_new * N_HEADS
    per_b = lambda s: pl.BlockSpec((1,) + s, lambda b, c, pt: (b,) + (0,) * len(s))
    hbm = pl.BlockSpec(memory_space=pl.ANY)
    kv_page = (N_HEADS, HEAD_DIM, PAGE_SIZE)
    grid_spec = pltpu.PrefetchScalarGridSpec(
        num_scalar_prefetch=1,
        grid=(nb, n_chunks),
        in_specs=[hbm, hbm, hbm, per_b((nrow, D_ATTN)), per_b((nrow, 1)), per_b((D_ATTN, LANES)),
                  per_b((D_ATTN, LANES)), per_b((nrow, LANES))],
        out_specs=per_b((8, D_ATTN)),
        scratch_shapes=[pltpu.VMEM((2, P) + kv_page, F32), pltpu.VMEM((2, P) + kv_page, F32),
                        pltpu.VMEM((2, P, N_HEADS, PAGE_SIZE), F32), pltpu.SemaphoreType.DMA((2,)),
                        pltpu.VMEM((nrow, 1), F32), pltpu.VMEM((nrow, 1), F32), pltpu.VMEM((nrow, D_ATTN), F32),
                        pltpu.VMEM((N_HEADS, PAGE_SIZE), F32)],
    )
    return pl.pallas_call(
        functools.partial(_fox_sample_kernel, n_chunks=n_chunks, n_new=n_new),
        out_shape=jax.ShapeDtypeStruct((nb, 8, D_ATTN), F32),
        grid_spec=grid_spec,
        compiler_params=_params("arbitrary", "arbitrary"),
        name="fox_sample",
    )(page_table, kt_pool, vt_pool, logf_pool, qbd, cn, kn, vn, cnb)


def _split_cols(z):
    f0 = 3 * D_ATTN
    return z[:, :f0], z[:, f0 + N_HEADS:], jnp.pad(z[:, f0:f0 + N_HEADS], ((0, 0), (0, LANES - N_HEADS)))


def kernel(x_prompt, x_sample, c_prompt, c_sample, cache_k, cache_v, cache_logf, page_table, w_ada, b_ada, w_in,
           b_f, gmlp_ln_g, gmlp_ln_b, w_s, b_s, w_attn_out, w_gmlp_out, w_o, ln1_g, ln1_b, w_group_router,
           b_group_router, w_expert_router, b_expert_router, w_exp_gate, w_exp_up, w_exp_down, ln2_g, ln2_b):
    depth = w_ada.shape[0]
    assert depth == 1
    nbp, seq, d = x_prompt.shape
    nbs, n_new, _ = x_sample.shape
    alpha = (2.0 * depth) ** 0.25
    n_pool = cache_k.shape[1]
    d_gmlp = gmlp_ln_g.shape[1]
    gw = d_gmlp // GMLP_GROUPS

    wa_b, wb_b, wf_b = _split_cols(w_in[0].astype(BF16))
    bf =jnp.pad(b_f[0], (0, LANES - N_HEADS))[None]
    glg, glb = gmlp_ln_g[0][None], gmlp_ln_b[0][None]
    tril = jnp.tril(jnp.ones((CHUNK, CHUNK), F32))
    ws = jnp.where(tril > 0, w_s[0], 0.0)
    bs_tile = jnp.repeat(b_s[0].T, gw, axis=1)
    w_r = jnp.pad(jnp.concatenate([w_expert_router[0], w_group_router[0]], axis=1),
                  ((0, 0), (0, LANES - N_EXPERTS - N_GROUPS)))
    b_r = jnp.pad(jnp.concatenate([b_expert_router[0], b_group_router[0]]), (0, LANES - N_EXPERTS - N_GROUPS))[None]
    l1g, l1b, l2g, l2b = ln1_g[0][None], ln1_b[0][None], ln2_g[0][None], ln2_b[0][None]

    nc = nbp + nbs
    c_all = jnp.pad(jnp.concatenate([c_prompt, c_sample], axis=0), ((0, -nc % 8), (0, 0)))
    mod = _ada(c_all, w_ada[0], b_ada[0][None])
    modp = [m[:, None, :] for m in jnp.split(mod[:nbp], 6, axis=-1)]
    mods = [m[:, None, :] for m in jnp.split(mod[nbp:nc], 6, axis=-1)]

    xp = x_prompt.reshape(nbp * seq, d)
    qt, kt32, kaug, vt32, vtaug, logft, cumt, reft, gm, sga, sgg = _prompt_in(
        xp, modp[1], modp[0], wa_b, wb_b, wf_b, bf, glg, glb, ws.astype(BF16), bs_tile, tril, seq=seq, tm=ATTN_TILE)
    a = _fox_prompt(qt, kaug, vtaug, cumt, reft, seq=seq)
    x1, xs_sorted, cs_sorted, pos, counts = _merge(
        a, gm, sga, sgg, xp, modp[2], modp[4], modp[3], w_attn_out[0].astype(BF16), w_gmlp_out[0].astype(BF16),
        w_o[0].astype(BF16), l1g, l1b, w_r.astype(BF16), b_r, rows_per_mod=seq, tm=SORT_BLOCK, alpha=alpha,
        precise=False, sort=True)
    src, dst, tile_group, n_tiles, max_tiles = _route_tables(counts, nbp * seq // SORT_BLOCK)
    y_sorted = _experts(src, tile_group, n_tiles, max_tiles, xs_sorted, cs_sorted, w_exp_gate[0], w_exp_up[0],
                        w_exp_down[0])
    yp = _unsort(dst, y_sorted, pos, x1, modp[5], l2g, l2b, rows_per_mod=seq, alpha=alpha)

    ns = nbs * n_new
    xs = x_sample.reshape(ns, d)
    rep = lambda m: jnp.repeat(m[:, 0, :], n_new, axis=0)
    z = jnp.concatenate(_split_cols(_sample_proj(xs, rep(mods[1]), rep(mods[0]), w_in[0].T)), axis=1)
    eye_b = jnp.eye(nbs, dtype=F32)
    mg = jnp.stack([jnp.kron(eye_b, ws[g, :n_new, :n_new]) for g in range(GMLP_GROUPS)])
    bs_rows = jnp.tile(bs_tile[:n_new], (nbs, 1))
    mc = jnp.kron(eye_b, tril[:n_new, :n_new])
    qs, ks, vs, logf_s, cn, gv_s, gm_s, sga_s, sgg_s = _sample_mix(z, bf, glg, glb, mg, bs_rows, mc)

    kt_pool = cache_k[0].transpose(0, 2, 3, 1)
    vt_pool = cache_v[0].transpose(0, 2, 3, 1)
    logf_pool = cache_logf[0].transpose(0, 2, 1)
    nrow = n_new * N_HEADS
    q4 = qs.reshape(nbs, n_new, N_HEADS, HEAD_DIM)
    qbd = jnp.einsum("bqhd,hg->bqhgd", q4, jnp.eye(N_HEADS, dtype=F32)).reshape(nbs, nrow, D_ATTN)
    cn3 = cn[:, :N_HEADS].reshape(nbs, n_new, N_HEADS)
    cn_col = cn3.reshape(nbs, nrow, 1)
    cnb = jnp.pad(jnp.tile(cn3.transpose(0, 2, 1), (1, n_new, 1)), ((0, 0), (0, 0), (0, LANES - n_new)))
    new_t = lambda t: jnp.pad(t.reshape(nbs, n_new, D_ATTN).transpose(0, 2, 1), ((0, 0), (0, 0), (0, LANES - n_new)))
    a_s = _fox_sample(page_table, kt_pool, vt_pool, logf_pool, qbd, cn_col, new_t(ks), new_t(vs), cnb, n_new=n_new)
    a_s = a_s[:, :n_new].reshape(ns, D_ATTN)
    x1s, h2s, comb_s = _merge(a_s, gm_s, sga_s, sgg_s, xs, rep(mods[2]), rep(mods[4]), rep(mods[3]),
                              w_attn_out[0], w_gmlp_out[0], w_o[0], l1g, l1b, w_r, b_r,
                              rows_per_mod=ns, tm=ns, alpha=alpha, precise=True, sort=False)
    ys = _moe(h2s, comb_s, x1s, rep(mods[5]), w_exp_gate[0], w_exp_up[0], w_exp_down[0], l2g, l2b, alpha=alpha)

    hs = (N_HEADS, HEAD_DIM)
    untr = lambda t: t.reshape(1, nbp, *hs, seq).transpose(0, 1, 4, 2, 3)
    return (yp.reshape(nbp, seq, d), ys.reshape(nbs, n_new, d), untr(kt32), untr(vt32),
            logft.reshape(1, nbp, N_HEADS, seq).transpose(0, 1, 3, 2),
            ks.reshape(1, nbs, n_new, *hs), vs.reshape(1, nbs, n_new, *hs),
            logf_s[:, :N_HEADS].reshape(1, nbs, n_new, N_HEADS), gv_s.reshape(1, nbs, n_new, d_gmlp))
```

```python
import functools

import numpy as np
import jax
import jax.numpy as jnp
from jax import lax
from jax.experimental import pallas as pl
from jax.experimental.pallas import tpu as pltpu

F32 = jnp.float32
BF16 = jnp.bfloat16
HIGHEST = lax.Precision.HIGHEST

N_HEADS = 8
HEAD_DIM = 64
D_ATTN = N_HEADS * HEAD_DIM
PAGE_SIZE = 128
CHUNK = 128
GMLP_GROUPS = 4
N_GROUPS = 4
EXPERTS_PER_GROUP = 4
N_EXPERTS = N_GROUPS * EXPERTS_PER_GROUP
LN_EPS = 1e-5
LANES = 128
NEG_INF = float("-inf")
LOG2E = 1.4426950408889634

VMEM_LIMIT_BYTES = 56 * 1024 * 1024


def _params(*sem):
    return pltpu.CompilerParams(dimension_semantics=sem, vmem_limit_bytes=VMEM_LIMIT_BYTES)


def _full(shape):
    n = len(shape)
    return pl.BlockSpec(shape, lambda *_: (0,) * n)


def _ln(x, g, b):
    mu = jnp.mean(x, axis=-1, keepdims=True)
    xc = x - mu
    var = jnp.mean(xc * xc, axis=-1, keepdims=True)
    return xc * lax.rsqrt(var + LN_EPS) * g + b


def _mod(ref):
    return ref[0] if len(ref.shape) == 3 else ref[...]


def _mod_spec(m, tm, rows_per_mod):
    if m.ndim == 3:
        return pl.BlockSpec((1, 1, m.shape[-1]), lambda i: (i * tm // rows_per_mod, 0, 0))
    return pl.BlockSpec((tm, m.shape[-1]), lambda i: (i, 0))


def _dot(a, b, precise):
    if precise:
        return jnp.dot(a, b, precision=HIGHEST, preferred_element_type=F32)
    return jnp.dot(a.astype(BF16), b.astype(BF16), preferred_element_type=F32)


def _ada_kernel(c_ref, w_ref, b_ref, o_ref):
    c = c_ref[...]
    o_ref[...] = _dot(c * jax.nn.sigmoid(c), w_ref[...], True) + b_ref[...]


def _ada(c, w, b, *, tn=1024):
    n, d = c.shape
    dout = w.shape[1]
    return pl.pallas_call(
        _ada_kernel,
        out_shape=jax.ShapeDtypeStruct((n, dout), F32),
        grid=(dout // tn,),
        in_specs=[_full((n, d)), pl.BlockSpec((d, tn), lambda j: (0, j)), pl.BlockSpec((1, tn), lambda j: (0, j))],
        out_specs=pl.BlockSpec((n, tn), lambda j: (0, j)),
        compiler_params=_params("parallel"),
        name="ada",
    )(c, w, b)


_OFF_Q, _OFF_K, _OFF_V, _OFF_U, _OFF_GV = 0, 512, 1024, 1536, 2048
_OFF_GA, _OFF_GG, _OFF_F, _W_COLS = 2560, 3584, 4608, 4736


def _forget(zf, bf):
    return jax.nn.log_sigmoid(zf + bf)


def _gate_value(zgv, glg, glb):
    return _ln(jax.nn.gelu(zgv), glg, glb)


def _mixer_epilogue(zq, zk, zv, zf, zu, zgv, zga, zgg, bf, glg, glb):
    q = zq * (HEAD_DIM ** -0.5)
    return (q, zk, zv, _forget(zf, bf), jax.nn.gelu(zu), _gate_value(zgv, glg, glb), jax.nn.sigmoid(zga),
            jax.nn.sigmoid(zgg))


def _prompt_in_kernel(x_ref, sc_ref, sh_ref, wa_ref, wb_ref, wf_ref, bf_ref, glg_ref, glb_ref, ws_ref, bs_ref, tri_ref,
                      qt_ref, kt32_ref, kaug_ref, vt32_ref, vtaug_ref, logft_ref, cumt_ref, reft_ref, gm_ref,
                      sga_ref, sgg_ref, carry_ref, sp_ref, *, tiles_per_seq):
    tm = x_ref.shape[0]

    @pl.when(pl.program_id(0) % tiles_per_seq == 0)
    def _():
        carry_ref[...] = jnp.zeros_like(carry_ref)

    h = (x_ref[...] * (1.0 + sc_ref[0]) + sh_ref[0]).astype(BF16)

    def proj(off, width):
        ref, base = (wa_ref, 0) if off < _OFF_U else (wb_ref, _OFF_U) if off < _OFF_F else (wf_ref, _OFF_F)
        return jnp.dot(h, ref[:, off - base:off - base + width], preferred_element_type=F32)

    logf = _forget(proj(_OFF_F, LANES), bf_ref[...])
    gv = _gate_value(proj(_OFF_GV, 512), glg_ref[...], glb_ref[...])
    u = jax.nn.gelu(proj(_OFF_U, 512))
    q = proj(_OFF_Q, 512) * (HEAD_DIM ** -0.5 * LOG2E)
    k = proj(_OFF_K, 512)

    carry_in = carry_ref[...]
    carry = carry_in
    parts = []
    for n in range(tm // CHUNK):
        c = _dot(tri_ref[...], logf[n * CHUNK:(n + 1) * CHUNK], True) + carry
        parts.append(c)
        carry = c[CHUNK - 1:CHUNK]
    carry_ref[...] = carry
    cum = jnp.concatenate(parts, axis=0)

    gvb = gv.astype(BF16)
    gw = gv.shape[1] // GMLP_GROUPS
    for n in range(tm // CHUNK):
        for g in range(GMLP_GROUPS):
            sp_ref[n * CHUNK:(n + 1) * CHUNK, g * gw:(g + 1) * gw] = jnp.dot(
                ws_ref[g], gvb[n * CHUNK:(n + 1) * CHUNK, g * gw:(g + 1) * gw], preferred_element_type=F32)

    v = proj(_OFF_V, 512)
    sga_ref[...] = jax.nn.sigmoid(proj(_OFF_GA, 1024)).astype(BF16)
    sgg_ref[...] = jax.nn.sigmoid(proj(_OFF_GG, 1024)).astype(BF16)

    qt_ref[0] = q.T.astype(BF16)
    kt32_ref[0] = k.T
    vt = v.T
    vt32_ref[0] = vt
    logft_ref[0] = logf.T[:N_HEADS]
    cumt_ref[0] = (cum * LOG2E).T[:N_HEADS]
    reft_ref[0] = (jnp.broadcast_to(carry_in, cum.shape) * LOG2E).T[:N_HEADS]
    bs = jnp.concatenate([bs_ref[...]] * (tm // CHUNK), axis=0)
    gm_ref[...] = (u * (sp_ref[...] + bs)).astype(BF16)

    neg = (carry_in - cum) * LOG2E
    hi = neg.astype(BF16).astype(F32)
    mid = (neg - hi).astype(BF16).astype(F32)
    lo = (neg - hi - mid).astype(BF16).astype(F32)
    lane = lax.broadcasted_iota(jnp.int32, (tm, LANES), 1)
    for hd in range(N_HEADS):
        src = k[:, (hd // 2) * LANES:(hd // 2 + 1) * LANES]
        if hd % 2:
            src = pltpu.roll(src, HEAD_DIM, 1)
        blk = jnp.where(lane < HEAD_DIM, src, 0.0)
        for i, piece in enumerate((hi, mid, lo)):
            blk = jnp.where(lane == HEAD_DIM + i, piece[:, hd:hd + 1], blk)
        kaug_ref[:, hd * LANES:(hd + 1) * LANES] = blk.astype(BF16)

    ones_rows = jnp.where(lax.broadcasted_iota(jnp.int32, (HEAD_DIM, tm), 0) == 0, 1.0, 0.0)
    for hd in range(N_HEADS):
        grp = jnp.concatenate([vt[hd * HEAD_DIM:(hd + 1) * HEAD_DIM], ones_rows], axis=0)
        vtaug_ref[0, hd * LANES:(hd + 1) * LANES, :] = grp.astype(BF16)


def _prompt_in(x, sc, sh, wa, wb, wf, bf, glg, glb, ws, bs, tri, *, seq, tm):
    n, d = x.shape
    nb = n // seq
    tps = seq // tm
    row = lambda c: pl.BlockSpec((tm, c), lambda i: (i, 0))
    mod = pl.BlockSpec((1, 1, d), lambda i: (i // tps, 0, 0))
    tr = lambda r: pl.BlockSpec((1, r, tm), lambda i: (i // tps, 0, i % tps))
    aug = N_HEADS * LANES
    outs = [((nb, 512, seq), BF16), ((nb, 512, seq), F32), ((n, aug), BF16), ((nb, 512, seq), F32),
            ((nb, aug, seq), BF16), ((nb, N_HEADS, seq), F32), ((nb, N_HEADS, seq), F32),
            ((nb, N_HEADS, seq), F32), ((n, 512), BF16), ((n, 1024), BF16), ((n, 1024), BF16)]
    out_specs = [tr(512), tr(512), row(aug), tr(512), tr(aug), tr(N_HEADS), tr(N_HEADS), tr(N_HEADS),
                 row(512), row(1024), row(1024)]
    return pl.pallas_call(
        functools.partial(_prompt_in_kernel, tiles_per_seq=tps),
        out_shape=[jax.ShapeDtypeStruct(s, t) for s, t in outs],
        grid=(n // tm,),
        in_specs=[row(d), mod, mod, _full(wa.shape), _full(wb.shape), _full(wf.shape), _full(bf.shape),
                  _full(glg.shape), _full(glb.shape), _full(ws.shape), _full(bs.shape), _full(tri.shape)],
        out_specs=out_specs,
        scratch_shapes=[pltpu.VMEM((1, LANES), F32), pltpu.VMEM((tm, 512), F32)],
        compiler_params=_params("arbitrary"),
        name="prompt_in",
    )(x, sc, sh, wa, wb, wf, bf, glg, glb, ws, bs, tri)


ATTN_TILE = 512
QK_AHEAD = 2


def _fox_prompt_kernel(qt_ref, kaug_ref, vtaug_ref, cumt_ref, reft_ref, o_ref, m_ref, acc_ref):
    t = ATTN_TILE
    qi = pl.program_id(1)
    q0 = pl.multiple_of(qi * t, t)
    m_ref[...] = jnp.full_like(m_ref, NEG_INF)
    acc_ref[...] = jnp.zeros_like(acc_ref)
    ones3 = jnp.where(lax.broadcasted_iota(jnp.int32, (HEAD_DIM, t), 0) < 3, 1.0, 0.0).astype(BF16)
    qaug = [jnp.concatenate([qt_ref[h * HEAD_DIM:(h + 1) * HEAD_DIM, :], ones3], axis=0) for h in range(N_HEADS)]
    cb = [cumt_ref[h:h + 1, pl.ds(q0, t)] for h in range(N_HEADS)]

    def run(tiles):
        starts = [pl.multiple_of(j * t, t) for j, _ in tiles]
        rounds = [(i, h) for i in range(len(tiles)) for h in range(N_HEADS)]

        def scores(i, h):
            return jnp.dot(kaug_ref[pl.ds(starts[i], t), h * LANES:(h + 1) * LANES], qaug[h],
                           preferred_element_type=F32)

        raw = [scores(*r) for r in rounds[:QK_AHEAD]]
        for n, (i, h) in enumerate(rounds):
            if n + QK_AHEAD < len(rounds):
                raw.append(scores(*rounds[n + QK_AHEAD]))
            ks, grp = starts[i], slice(h * LANES, (h + 1) * LANES)
            st = raw[n]
            if tiles[i][1]:
                st = jnp.where(lax.broadcasted_iota(jnp.int32, (t, t), 0) <= lax.broadcasted_iota(jnp.int32, (t, t), 1),
                               st, NEG_INF)
            ref_j = jnp.concatenate([reft_ref[h:h + 1, pl.ds(ks, LANES)]] * (t // LANES), axis=1)
            crow = cb[h] - ref_j
            m_old = m_ref[h]
            m_new = jnp.maximum(m_old, jnp.max(st, axis=0, keepdims=True) + crow)
            pt = jnp.exp2(st - (m_new - crow)).astype(BF16)
            acc_ref[h] = acc_ref[h] * jnp.exp2(m_old - m_new) + jnp.dot(
                vtaug_ref[grp, pl.ds(ks, t)], pt, preferred_element_type=F32)
            m_ref[h] = m_new

    lax.fori_loop(0, qi // 2, lambda jj, c: (run([(2 * jj, False), (2 * jj + 1, False)]), c)[1], 0)

    @pl.when(qi % 2 == 1)
    def _():
        run([(qi - 1, False), (qi, True)])

    @pl.when(qi % 2 == 0)
    def _():
        run([(qi, True)])

    outs = []
    for h in range(N_HEADS):
        acc = acc_ref[h]
        outs.append(acc[:HEAD_DIM] / acc[HEAD_DIM:HEAD_DIM + 1])
    o_ref[...] = jnp.concatenate(outs, axis=0).T.astype(o_ref.dtype)


def _fox_prompt(qt, kaug, vtaug, cumt, reft, *, seq):
    nb = qt.shape[0]
    t = ATTN_TILE
    nq = seq // t
    aug = N_HEADS * LANES
    per_b = lambda r: pl.BlockSpec((None, r, seq), lambda b, i: (b, 0, 0))
    return pl.pallas_call(
        _fox_prompt_kernel,
        out_shape=jax.ShapeDtypeStruct((nb * seq, D_ATTN), BF16),
        grid=(nb, nq),
        in_specs=[pl.BlockSpec((None, D_ATTN, t), lambda b, i: (b, 0, i)),
                  pl.BlockSpec((seq, aug), lambda b, i: (b, 0)),
                  per_b(aug), per_b(N_HEADS), per_b(N_HEADS)],
        out_specs=pl.BlockSpec((t, D_ATTN), lambda b, i: (b * nq + i, 0)),
        scratch_shapes=[pltpu.VMEM((N_HEADS, 1, t), F32), pltpu.VMEM((N_HEADS, LANES, t), F32)],
        compiler_params=_params("parallel", "arbitrary"),
        name="fox_prompt",
    )(qt, kaug, vtaug, cumt, reft)


def _route(logits):
    lane = lax.broadcasted_iota(jnp.int32, logits.shape, 1)
    big = jnp.int32(LANES)
    is_g = (lane >= N_EXPERTS) & (lane < N_EXPERTS + N_GROUPS)
    gl = jnp.where(is_g, logits, NEG_INF)
    gmax = jnp.max(gl, axis=-1, keepdims=True)
    gi = jnp.min(jnp.where(gl == gmax, lane, big), axis=-1, keepdims=True) - N_EXPERTS
    pg_top = 1.0 / jnp.sum(jnp.exp(gl - gmax), axis=-1, keepdims=True)
    in_g = (lane >= gi * EXPERTS_PER_GROUP) & (lane < (gi + 1) * EXPERTS_PER_GROUP)
    el = jnp.where(in_g, logits, NEG_INF)
    m1 = jnp.max(el, axis=-1, keepdims=True)
    i1 = jnp.min(jnp.where(el == m1, lane, big), axis=-1, keepdims=True)
    el2 = jnp.where(lane == i1, NEG_INF, el)
    m2 = jnp.max(el2, axis=-1, keepdims=True)
    i2 = jnp.min(jnp.where(el2 == m2, lane, big), axis=-1, keepdims=True)
    e2 = jnp.exp(m2 - m1)
    w1 = pg_top / (1.0 + e2)
    return jnp.where(lane == i1, w1, jnp.where(lane == i2, w1 * e2, 0.0)), gi


ROUTE_CHUNK = 32
SORT_BLOCK = 512
SORTED_ROWS = 768
ROUTE_TILE = 512
EXPERT_SUB = 256


def _group_ranks(gi, ls_ref):
    tm = gi.shape[0]
    lane = lax.broadcasted_iota(jnp.int32, (tm, LANES), 1)
    onehot = jnp.where(lane == gi, 1.0, 0.0)
    return onehot, jnp.dot(ls_ref[...], onehot.astype(BF16), preferred_element_type=F32)


def _sort_block(h2b, comb, gi, onehot, before, xs_ref, cs_ref, pos_ref, cnt_ref):
    tm = h2b.shape[0]
    lane = lax.broadcasted_iota(jnp.int32, (tm, LANES), 1)
    tot = before[tm - 1:tm] + onehot[tm - 1:tm]
    cnt_ref[0] = tot
    padded = jnp.floor((tot + (ROUTE_CHUNK - 1)) * (1.0 / ROUTE_CHUNK)) * ROUTE_CHUNK
    lane1 = lax.broadcasted_iota(jnp.int32, (1, LANES), 1)
    start = jnp.zeros_like(padded)
    for s in range(1, N_GROUPS):
        start = start + jnp.where(lane1 >= s, pltpu.roll(padded, s, 1), 0.0)
    pos = jnp.sum(onehot * (start + before), axis=-1, keepdims=True)
    pos_ref[...] = pos
    pos_row = jnp.broadcast_to(pos, (tm, LANES)).T[0:1]
    rows = lax.broadcasted_iota(jnp.int32, (SORTED_ROWS, tm), 0).astype(F32)
    perm = jnp.where(rows == pos_row, 1.0, 0.0).astype(BF16)
    xs_ref[...] = jnp.dot(perm, h2b, preferred_element_type=F32).astype(BF16)
    rel = jnp.zeros_like(comb)
    for g in range(N_GROUPS):
        moved = comb if g == 0 else pltpu.roll(comb, LANES - g * EXPERTS_PER_GROUP, 1)
        rel = rel + jnp.where(gi == g, moved, 0.0)
    rel = jnp.where(lane < EXPERTS_PER_GROUP, rel, 0.0)
    hi = rel.astype(BF16)
    lo = (rel - hi.astype(F32)).astype(BF16)
    cs_ref[...] = (jnp.dot(perm, hi, preferred_element_type=F32) + jnp.dot(perm, lo, preferred_element_type=F32))


def _merge_kernel(a_ref, gm_ref, sga_ref, sgg_ref, x_ref, g1_ref, sc2_ref, sh2_ref, wao_ref, wgo_ref, wo_ref,
                  l1g_ref, l1b_ref, wr_ref, br_ref, *rest, alpha, precise, sort):
    if sort:
        ls_ref, x1_ref, xs_ref, cs_ref, pos_ref, cnt_ref, h2_s, comb_s, gi_s = rest

        @pl.when(pl.program_id(0) == 0)
        def _():
            h2_s[...] = jnp.zeros_like(h2_s)
            comb_s[...] = jnp.zeros_like(comb_s)
            gi_s[...] = jnp.zeros_like(gi_s)

        h2_prev, comb_prev, gi_prev = h2_s[...], comb_s[...], gi_s[...]
        onehot, before = _group_ranks(gi_prev, ls_ref)
    else:
        x1_ref, h2_ref, comb_ref = rest
    da = _dot(a_ref[...], wao_ref[...], precise)
    dg = _dot(gm_ref[...], wgo_ref[...], precise)
    t = sga_ref[...].astype(F32) * da + sgg_ref[...].astype(F32) * dg
    m = _dot(t, wo_ref[...], precise)
    if sort:
        _sort_block(h2_prev, comb_prev, gi_prev, onehot, before, xs_ref, cs_ref, pos_ref, cnt_ref)
    x1 = _ln(alpha * x_ref[...] + (1.0 + _mod(g1_ref)) * m, l1g_ref[...], l1b_ref[...])
    h2 = x1 * (1.0 + _mod(sc2_ref)) + _mod(sh2_ref)
    comb, gi = _route(_dot(h2, wr_ref[...], precise) + br_ref[...])
    if sort:
        h2_s[...] = h2.astype(BF16)
        comb_s[...] = comb
        gi_s[...] = gi
    else:
        h2_ref[...] = h2.astype(h2_ref.dtype)
        comb_ref[...] = comb
    x1_ref[...] = x1


def _merge(a, gm, sga, sgg, x, g1, sc2, sh2, wao, wgo, wo, l1g, l1b, wr, br, *, rows_per_mod, tm, alpha, precise,
           sort):
    n, d = x.shape
    nblk = n // tm
    cur = (lambda i: jnp.minimum(i, nblk - 1)) if sort else (lambda i: i)
    row = lambda c: pl.BlockSpec((tm, c), lambda i: (cur(i), 0))
    if g1.ndim == 3:
        mod = pl.BlockSpec((1, 1, d), lambda i: (cur(i) * tm // rows_per_mod, 0, 0))
    else:
        mod = row(d)
    ins = [a, gm, sga, sgg, x, g1, sc2, sh2, wao, wgo, wo, l1g, l1b, wr, br]
    in_specs = [row(D_ATTN), row(gm.shape[1]), row(d), row(d), row(d), mod, mod, mod, _full(wao.shape),
                _full(wgo.shape), _full(wo.shape), _full(l1g.shape), _full(l1b.shape), _full(wr.shape),
                _full(br.shape)]
    scratch = []
    if sort:
        assert tm == SORT_BLOCK
        ls = jnp.tril(jnp.ones((tm, tm), BF16), -1)
        ins.append(ls)
        in_specs.append(_full(ls.shape))
        prev = lambda i: jnp.maximum(i - 1, 0)
        srow = lambda c: pl.BlockSpec((SORTED_ROWS, c), lambda i: (prev(i), 0))
        out_shape = [((n, d), F32), ((nblk * SORTED_ROWS, d), BF16), ((nblk * SORTED_ROWS, LANES), F32),
                     ((n, 1), F32), ((nblk, 1, LANES), F32)]
        out_specs = [row(d), srow(d), srow(LANES), pl.BlockSpec((tm, 1), lambda i: (prev(i), 0)),
                     pl.BlockSpec((1, 1, LANES), lambda i: (prev(i), 0, 0))]
        scratch = [pltpu.VMEM((tm, d), BF16), pltpu.VMEM((tm, LANES), F32), pltpu.VMEM((tm, 1), jnp.int32)]
    else:
        out_shape = [((n, d), F32), ((n, d), BF16), ((n, LANES), F32)]
        out_specs = [row(d), row(d), row(LANES)]
    return pl.pallas_call(
        functools.partial(_merge_kernel, alpha=alpha, precise=precise, sort=sort),
        out_shape=[jax.ShapeDtypeStruct(s, t) for s, t in out_shape],
        grid=(nblk + 1 if sort else nblk,),
        in_specs=in_specs,
        out_specs=out_specs,
        scratch_shapes=scratch,
        compiler_params=_params("arbitrary"),
        name="merge_precise" if precise else "merge",
    )(*ins)


def _route_tables(counts, nblk):
    ch, cpb, cpt = ROUTE_CHUNK, SORTED_ROWS // ROUTE_CHUNK, ROUTE_TILE // ROUTE_CHUNK
    max_tiles = (nblk * (SORT_BLOCK + N_GROUPS * (ch - 1))) // ROUTE_TILE + N_GROUPS + 1
    cnt = counts[:, 0, :N_GROUPS].astype(jnp.int32)
    nch = (cnt + ch - 1) // ch
    loc = jnp.cumsum(nch, axis=1) - nch
    earlier = jnp.cumsum(nch, axis=0) - nch
    tiles_g = (nch.sum(0) + cpt - 1) // cpt
    tile_off = jnp.cumsum(tiles_g) - tiles_g
    n_tiles = tiles_g.sum()
    k = jnp.arange(cpb, dtype=jnp.int32)[None, None, :]
    used = (k >= loc[:, :, None]) & (k < (loc + nch)[:, :, None])
    glob = (tile_off * cpt)[None, :, None] + earlier[:, :, None] + k - loc[:, :, None]
    dst = jnp.sum(jnp.where(used, glob, 0), axis=1)
    blk_chunk = jnp.arange(nblk, dtype=jnp.int32)[:, None] * cpb + k[0]
    scatter_to = jnp.where(used.any(axis=1), dst, max_tiles * cpt)
    src = jnp.full((max_tiles * cpt,), cpb - 1, jnp.int32).at[scatter_to.reshape(-1)].set(
        blk_chunk.reshape(-1), mode="drop")
    t = jnp.arange(max_tiles, dtype=jnp.int32)
    tile_group = jnp.minimum(jnp.sum(t[:, None] >= (tile_off + tiles_g)[None, :], axis=1), N_GROUPS - 1)
    return src, dst.reshape(-1), tile_group.astype(jnp.int32), n_tiles.reshape(1).astype(jnp.int32), max_tiles


def _experts_kernel(src_ref, grp_ref, nt_ref, *refs):
    del src_ref
    cpt = ROUTE_TILE // ROUTE_CHUNK
    xr, cr = refs[:cpt], refs[cpt:2 * cpt]
    wg_ref, wu_ref, wd_ref, o_ref, wg_s, wu_s, wd_s = refs[2 * cpt:]
    t = pl.program_id(0)
    live = t < nt_ref[0]
    de = wg_ref.shape[3]

    @pl.when((t == 0) | (grp_ref[t] != grp_ref[jnp.maximum(t - 1, 0)]))
    def _():
        for e in range(EXPERTS_PER_GROUP):
            wg_s[:, e * de:(e + 1) * de] = wg_ref[0, e].astype(BF16)
            wu_s[:, e * de:(e + 1) * de] = wu_ref[0, e].astype(BF16)
            wd_s[e * de:(e + 1) * de, :] = wd_ref[0, e].astype(BF16)

    @pl.when(live)
    def _():
        per = EXPERT_SUB // ROUTE_CHUNK
        halves = range(ROUTE_TILE // EXPERT_SUB)
        xs = [jnp.concatenate([r[...] for r in xr[i * per:(i + 1) * per]], axis=0) for i in halves]
        gate_up = [(jnp.dot(x, wg_s[...], preferred_element_type=F32), jnp.dot(x, wu_s[...], preferred_element_type=F32))
                   for x in xs]
        for i, (hg, hu) in enumerate(gate_up):
            c = jnp.concatenate([r[...] for r in cr[i * per:(i + 1) * per]], axis=0)
            a = jnp.concatenate([(hg[:, e * de:(e + 1) * de] * jax.nn.sigmoid(hg[:, e * de:(e + 1) * de])
                                  * hu[:, e * de:(e + 1) * de] * c[:, e:e + 1]).astype(BF16)
                                 for e in range(EXPERTS_PER_GROUP)], axis=1)
            o_ref[i * EXPERT_SUB:(i + 1) * EXPERT_SUB, :] = jnp.dot(
                a, wd_s[...], preferred_element_type=F32).astype(o_ref.dtype)

    @pl.when(jnp.logical_not(live))
    def _():
        o_ref[...] = jnp.zeros_like(o_ref)


def _experts(src, tile_group, n_tiles, max_tiles, xs, cs, wg, wu, wd):
    d = xs.shape[1]
    de = wg.shape[2]
    cpt = ROUTE_TILE // ROUTE_CHUNK
    grouped = lambda w: w.reshape((N_GROUPS, EXPERTS_PER_GROUP) + w.shape[1:])

    def chunk(width):
        def one(r):
            return pl.BlockSpec((ROUTE_CHUNK, width), lambda t, src, grp, nt: (src[t * cpt + r], 0))
        return [one(r) for r in range(cpt)]

    wspec = lambda w: pl.BlockSpec((1, EXPERTS_PER_GROUP) + w.shape[1:], lambda t, src, grp, nt: (grp[t], 0, 0, 0))
    wide = EXPERTS_PER_GROUP * de
    grid_spec = pltpu.PrefetchScalarGridSpec(
        num_scalar_prefetch=3,
        grid=(max_tiles,),
        in_specs=chunk(d) + chunk(LANES) + [wspec(wg), wspec(wu), wspec(wd)],
        out_specs=pl.BlockSpec((ROUTE_TILE, d), lambda t, src, grp, nt: (t, 0)),
        scratch_shapes=[pltpu.VMEM((d, wide), BF16), pltpu.VMEM((d, wide), BF16), pltpu.VMEM((wide, d), BF16)],
    )
    return pl.pallas_call(
        _experts_kernel,
        out_shape=jax.ShapeDtypeStruct((max_tiles * ROUTE_TILE, d), BF16),
        grid_spec=grid_spec,
        compiler_params=_params("arbitrary"),
        name="experts",
    )(src, tile_group, n_tiles, *([xs] * cpt), *([cs] * cpt), grouped(wg), grouped(wu), grouped(wd))


def _unsort_kernel(dst_ref, *refs, alpha):
    del dst_ref
    cpb = SORTED_ROWS // ROUTE_CHUNK
    yr = refs[:cpb]
    pos_ref, x1_ref, g2_ref, l2g_ref, l2b_ref, o_ref = refs[cpb:]
    ys = jnp.concatenate([r[...] for r in yr], axis=0)
    tm = pos_ref.shape[0]
    cols = lax.broadcasted_iota(jnp.int32, (tm, SORTED_ROWS), 1).astype(F32)
    perm_t = jnp.where(cols == pos_ref[...], 1.0, 0.0).astype(BF16)
    f = jnp.dot(perm_t, ys, preferred_element_type=F32)
    o_ref[...] = _ln(alpha * x1_ref[...] + (1.0 + _mod(g2_ref)) * f, l2g_ref[...], l2b_ref[...])


def _unsort(dst, y, pos, x1, g2, l2g, l2b, *, rows_per_mod, alpha):
    n, d = x1.shape
    tm = SORT_BLOCK
    cpb = SORTED_ROWS // ROUTE_CHUNK

    def chunk(r):
        return pl.BlockSpec((ROUTE_CHUNK, d), lambda i, dst: (dst[i * cpb + r], 0))

    row = lambda c: pl.BlockSpec((tm, c), lambda i, dst: (i, 0))
    full = lambda s: pl.BlockSpec(s, lambda i, dst: (0,) * len(s))
    grid_spec = pltpu.PrefetchScalarGridSpec(
        num_scalar_prefetch=1,
        grid=(n // tm,),
        in_specs=[chunk(r) for r in range(cpb)] + [
            row(1), row(d), pl.BlockSpec((1, 1, d), lambda i, dst: (i * tm // rows_per_mod, 0, 0)),
            full(l2g.shape), full(l2b.shape)],
        out_specs=row(d),
    )
    return pl.pallas_call(
        functools.partial(_unsort_kernel, alpha=alpha),
        out_shape=jax.ShapeDtypeStruct((n, d), F32),
        grid_spec=grid_spec,
        compiler_params=_params("parallel"),
        name="unsort",
    )(dst, *([y] * cpb), pos, x1, g2, l2g, l2b)


def _moe_kernel(h2_ref, comb_ref, x1_ref, g2_ref, wg_ref, wu_ref, wd_ref, l2g_ref, l2b_ref, o_ref, *, alpha):
    e = pl.program_id(0)

    @pl.when(e == 0)
    def _():
        o_ref[...] = jnp.zeros_like(o_ref)

    h2 = h2_ref[...]
    comb = comb_ref[...]
    lane = lax.broadcasted_iota(jnp.int32, comb.shape, 1)
    w = jnp.sum(jnp.where(lane == e, comb, 0.0), axis=-1, keepdims=True)
    hg = jnp.dot(h2, wg_ref[0].astype(BF16), preferred_element_type=F32)
    hu = jnp.dot(h2, wu_ref[0].astype(BF16), preferred_element_type=F32)
    a = hg * jax.nn.sigmoid(hg) * hu * w
    o_ref[...] += jnp.dot(a.astype(BF16), wd_ref[0].astype(BF16), preferred_element_type=F32)

    @pl.when(e == pl.num_programs(0) - 1)
    def _():
        o_ref[...] = _ln(alpha * x1_ref[...] + (1.0 + g2_ref[...]) * o_ref[...], l2g_ref[...], l2b_ref[...])


def _moe(h2, comb, x1, g2, wg, wu, wd, l2g, l2b, *, alpha):
    n, d = x1.shape
    ne, _, de = wg.shape
    return pl.pallas_call(
        functools.partial(_moe_kernel, alpha=alpha),
        out_shape=jax.ShapeDtypeStruct((n, d), F32),
        grid=(ne,),
        in_specs=[_full((n, d)), _full((n, LANES)), _full((n, d)), _full((n, d)),
                  pl.BlockSpec((1, d, de), lambda e: (e, 0, 0)), pl.BlockSpec((1, d, de), lambda e: (e, 0, 0)),
                  pl.BlockSpec((1, de, d), lambda e: (e, 0, 0)), _full(l2g.shape), _full(l2b.shape)],
        out_specs=_full((n, d)),
        compiler_params=_params("arbitrary"),
        name="moe",
    )(h2, comb, x1, g2, wg, wu, wd, l2g, l2b)


def _sample_proj_kernel(x_ref, sc_ref, sh_ref, wt_ref, z_ref):
    h = x_ref[...] * (1.0 + sc_ref[...]) + sh_ref[...]
    z_ref[...] = lax.dot_general(h, wt_ref[...], (((1,), (1,)), ((), ())), precision=HIGHEST,
                                 preferred_element_type=F32)


def _sample_proj(x, sc, sh, wt):
    n, d = x.shape
    return pl.pallas_call(
        _sample_proj_kernel,
        out_shape=jax.ShapeDtypeStruct((n, wt.shape[0]), F32),
        compiler_params=pltpu.CompilerParams(vmem_limit_bytes=VMEM_LIMIT_BYTES),
        name="sample_proj",
    )(x, sc, sh, wt)


def _sample_mix_kernel(z_ref, bf_ref, glg_ref, glb_ref, mg_ref, bs_ref, mc_ref,
                       q_ref, k_ref, v_ref, logf_ref, cn_ref, gv_ref, gm_ref, sga_ref, sgg_ref):
    z = z_ref[...]
    sec = lambda off, width: z[:, off:off + width]
    q, k, v, logf, u, gv, sga, sgg = _mixer_epilogue(
        sec(_OFF_Q, 512), sec(_OFF_K, 512), sec(_OFF_V, 512), sec(_OFF_F, LANES), sec(_OFF_U, 512),
        sec(_OFF_GV, 512), sec(_OFF_GA, 1024), sec(_OFF_GG, 1024), bf_ref[...], glg_ref[...], glb_ref[...])
    q_ref[...] = q
    k_ref[...] = k
    v_ref[...] = v
    logf_ref[...] = logf
    cn_ref[...] = _dot(mc_ref[...], logf, True)
    gv_ref[...] = gv
    sga_ref[...] = sga
    sgg_ref[...] = sgg
    gw = gv.shape[1] // GMLP_GROUPS
    sp = jnp.concatenate([_dot(mg_ref[g], gv[:, g * gw:(g + 1) * gw], True) for g in range(GMLP_GROUPS)], axis=-1)
    gm_ref[...] = u * (sp + bs_ref[...])


def _sample_mix(z, bf, glg, glb, mg, bs, mc):
    n = z.shape[0]
    shapes = [(n, 512)] * 3 + [(n, LANES)] * 2 + [(n, 512)] * 2 + [(n, 1024)] * 2
    return pl.pallas_call(
        _sample_mix_kernel,
        out_shape=[jax.ShapeDtypeStruct(s, F32) for s in shapes],
        compiler_params=pltpu.CompilerParams(vmem_limit_bytes=VMEM_LIMIT_BYTES),
        name="sample_mix",
    )(z, bf, glg, glb, mg, bs, mc)


PAGES_PER_STEP = 16


def _fox_sample_kernel(pt_ref, kt_hbm, vt_hbm, lf_hbm, qbd_ref, cn_ref, kn_ref, vn_ref, cnb_ref, o_ref,
                       kbuf, vbuf, lbuf, sem, m_ref, l_ref, acc_ref, later_ref, *, n_chunks, n_new):
    P = PAGES_PER_STEP
    b, c = pl.program_id(0), pl.program_id(1)
    step = b * n_chunks + c
    n_steps = pl.num_programs(0) * n_chunks
    slot = lax.rem(step, 2)

    def page_copies(seq, chunk, sl, lookup):
        out = []
        for r in range(P):
            page = pt_ref[seq, (n_chunks - 1 - chunk) * P + r] if lookup else 0
            out += [pltpu.make_async_copy(kt_hbm.at[page], kbuf.at[sl, r], sem.at[sl]),
                    pltpu.make_async_copy(vt_hbm.at[page], vbuf.at[sl, r], sem.at[sl]),
                    pltpu.make_async_copy(lf_hbm.at[page], lbuf.at[sl, r], sem.at[sl])]
        return out

    def start_all(cps):
        for i, cp in enumerate(cps):
            cp.start(priority=1 if i % 3 == 1 else 0)

    @pl.when(step == 0)
    def _():
        start_all(page_copies(b, c, slot, True))

    @pl.when(step + 1 < n_steps)
    def _():
        nxt = step + 1
        start_all(page_copies(nxt // n_chunks, lax.rem(nxt, n_chunks), 1 - slot, True))

    @pl.when(c == 0)
    def _():
        m_ref[...] = jnp.full_like(m_ref, NEG_INF)
        l_ref[...] = jnp.zeros_like(l_ref)
        acc_ref[...] = jnp.zeros_like(acc_ref)
        later_ref[...] = jnp.zeros_like(later_ref)

    qbd = qbd_ref[0]
    cn = cn_ref[0]
    nt = (((1,), (1,)), ((), ()))

    def update(s, vs):
        m_old = m_ref[...]
        m_new = m_old
        for sb in s:
            m_new = jnp.maximum(m_new, jnp.max(sb, axis=-1, keepdims=True))
        alpha = jnp.exp(m_old - m_new)
        l_new = l_ref[...] * alpha
        acc = acc_ref[...] * alpha
        for sb, vb in zip(s, vs):
            p = jnp.exp(sb - m_new)
            l_new = l_new + jnp.sum(p, axis=-1, keepdims=True)
            acc = acc + lax.dot_general(p, vb, nt, preferred_element_type=F32)
        m_ref[...] = m_new
        l_ref[...] = l_new
        acc_ref[...] = acc

    for cp in page_copies(b, c, slot, False):
        cp.wait()

    lane = lax.broadcasted_iota(jnp.int32, (N_HEADS, PAGE_SIZE), 1)
    later = later_ref[...]
    bias = [None] * P
    for r in reversed(range(P)):
        y = lbuf[slot, r]
        for k in range(7):
            sh = 1 << k
            y = y + jnp.where(lane < PAGE_SIZE - sh, pltpu.roll(y, PAGE_SIZE - sh, 1), 0.0)
        bias[r] = jnp.where(lane < PAGE_SIZE - 1, pltpu.roll(y, PAGE_SIZE - 1, 1), 0.0) + later
        later = later + y[:, 0:1]
    later_ref[...] = later

    scores = []
    for r in range(P):
        st = jnp.dot(qbd, kbuf[slot, r].reshape(D_ATTN, PAGE_SIZE), preferred_element_type=F32)
        scores.append(st + jnp.concatenate([bias[r]] * n_new, axis=0) + cn)
    update(scores, [vbuf[slot, r].reshape(D_ATTN, PAGE_SIZE) for r in range(P)])

    @pl.when(c == n_chunks - 1)
    def _():
        st = jnp.dot(qbd, kn_ref[0], preferred_element_type=F32) + cn - cnb_ref[0]
        col = lax.broadcasted_iota(jnp.int32, st.shape, 1)
        row = lax.broadcasted_iota(jnp.int32, st.shape, 0)
        update([jnp.where(col <= row // N_HEADS, st, NEG_INF)], [vn_ref[0]])
        full = acc_ref[...] / l_ref[...]
        lane_head = lax.broadcasted_iota(jnp.int32, full.shape, 1) // HEAD_DIM
        row_head = lax.broadcasted_iota(jnp.int32, full.shape, 0) % N_HEADS
        own = jnp.where(lane_head == row_head, full, 0.0)
        rows = [jnp.sum(own[q * N_HEADS:(q + 1) * N_HEADS], axis=0, keepdims=True) for q in range(n_new)]
        o_ref[0] = jnp.concatenate(rows + [jnp.zeros((o_ref.shape[1] - n_new, D_ATTN), F32)], axis=0)


def _fox_sample(page_table, kt_pool, vt_pool, logf_pool, qbd, cn, kn, vn, cnb, *, n_new):
    nb, n_pages = page_table.shape
    P = PAGES_PER_STEP
    n_chunks = n_pages // P
    nrow = n_new * N_HEADS
    per_b = lambda s: pl.BlockSpec((1,) + s, lambda b, c, pt: (b,) + (0,) * len(s))
    hbm = pl.BlockSpec(memory_space=pl.ANY)
    kv_page = (N_HEADS, HEAD_DIM, PAGE_SIZE)
    grid_spec = pltpu.PrefetchScalarGridSpec(
        num_scalar_prefetch=1,
        grid=(nb, n_chunks),
        in_specs=[hbm, hbm, hbm, per_b((nrow, D_ATTN)), per_b((nrow, 1)), per_b((D_ATTN, LANES)),
                  per_b((D_ATTN, LANES)), per_b((nrow, LANES))],
        out_specs=per_b((8, D_ATTN)),
        scratch_shapes=[pltpu.VMEM((2, P) + kv_page, F32), pltpu.VMEM((2, P) + kv_page, F32),
                        pltpu.VMEM((2, P, N_HEADS, PAGE_SIZE), F32), pltpu.SemaphoreType.DMA((2,)),
                        pltpu.VMEM((nrow, 1), F32), pltpu.VMEM((nrow, 1), F32), pltpu.VMEM((nrow, D_ATTN), F32),
                        pltpu.VMEM((N_HEADS, PAGE_SIZE), F32)],
    )
    return pl.pallas_call(
        functools.partial(_fox_sample_kernel, n_chunks=n_chunks, n_new=n_new),
        out_shape=jax.ShapeDtypeStruct((nb, 8, D_ATTN), F32),
        grid_spec=grid_spec,
        compiler_params=_params("arbitrary", "arbitrary"),
        name="fox_sample",
    )(page_table, kt_pool, vt_pool, logf_pool, qbd, cn, kn, vn, cnb)


def _split_cols(z):
    f0 = 3 * D_ATTN
    return z[:, :f0], z[:, f0 + N_HEADS:], jnp.pad(z[:, f0:f0 + N_HEADS], ((0, 0), (0, LANES - N_HEADS)))


def kernel(x_prompt, x_sample, c_prompt, c_sample, cache_k, cache_v, cache_logf, page_table, w_ada, b_ada, w_in,
           b_f, gmlp_ln_g, gmlp_ln_b, w_s, b_s, w_attn_out, w_gmlp_out, w_o, ln1_g, ln1_b, w_group_router,
           b_group_router, w_expert_router, b_expert_router, w_exp_gate, w_exp_up, w_exp_down, ln2_g, ln2_b):
    depth = w_ada.shape[0]
    assert depth == 1
    nbp, seq, d = x_prompt.shape
    nbs, n_new, _ = x_sample.shape
    alpha = (2.0 * depth) ** 0.25
    n_pool = cache_k.shape[1]
    d_gmlp = gmlp_ln_g.shape[1]
    gw = d_gmlp // GMLP_GROUPS

    wa_b, wb_b, wf_b = _split_cols(w_in[0].astype(BF16))
    bf =jnp.pad(b_f[0], (0, LANES - N_HEADS))[None]
    glg, glb = gmlp_ln_g[0][None], gmlp_ln_b[0][None]
    tril = jnp.tril(jnp.ones((CHUNK, CHUNK), F32))
    ws = jnp.where(tril > 0, w_s[0], 0.0)
    bs_tile = jnp.repeat(b_s[0].T, gw, axis=1)
    w_r = jnp.pad(jnp.concatenate([w_expert_router[0], w_group_router[0]], axis=1),
                  ((0, 0), (0, LANES - N_EXPERTS - N_GROUPS)))
    b_r = jnp.pad(jnp.concatenate([b_expert_router[0], b_group_router[0]]), (0, LANES - N_EXPERTS - N_GROUPS))[None]
    l1g, l1b, l2g, l2b = ln1_g[0][None], ln1_b[0][None], ln2_g[0][None], ln2_b[0][None]

    nc = nbp + nbs
    c_all = jnp.pad(jnp.concatenate([c_prompt, c_sample], axis=0), ((0, -nc % 8), (0, 0)))
    mod = _ada(c_all, w_ada[0], b_ada[0][None])
    modp = [m[:, None, :] for m in jnp.split(mod[:nbp], 6, axis=-1)]
    mods = [m[:, None, :] for m in jnp.split(mod[nbp:nc], 6, axis=-1)]

    xp = x_prompt.reshape(nbp * seq, d)
    qt, kt32, kaug, vt32, vtaug, logft, cumt, reft, gm, sga, sgg = _prompt_in(
        xp, modp[1], modp[0], wa_b, wb_b, wf_b, bf, glg, glb, ws.astype(BF16), bs_tile, tril, seq=seq, tm=ATTN_TILE)
    a = _fox_prompt(qt, kaug, vtaug, cumt, reft, seq=seq)
    x1, xs_sorted, cs_sorted, pos, counts = _merge(
        a, gm, sga, sgg, xp, modp[2], modp[4], modp[3], w_attn_out[0].astype(BF16), w_gmlp_out[0].astype(BF16),
        w_o[0].astype(BF16), l1g, l1b, w_r.astype(BF16), b_r, rows_per_mod=seq, tm=SORT_BLOCK, alpha=alpha,
        precise=False, sort=True)
    src, dst, tile_group, n_tiles, max_tiles = _route_tables(counts, nbp * seq // SORT_BLOCK)
    y_sorted = _experts(src, tile_group, n_tiles, max_tiles, xs_sorted, cs_sorted, w_exp_gate[0], w_exp_up[0],
                        w_exp_down[0])
    yp = _unsort(dst, y_sorted, pos, x1, modp[5], l2g, l2b, rows_per_mod=seq, alpha=alpha)

    ns = nbs * n_new
    xs = x_sample.reshape(ns, d)
    rep = lambda m: jnp.repeat(m[:, 0, :], n_new, axis=0)
    z = jnp.concatenate(_split_cols(_sample_proj(xs, rep(mods[1]), rep(mods[0]), w_in[0].T)), axis=1)
    eye_b = jnp.eye(nbs, dtype=F32)
    mg = jnp.stack([jnp.kron(eye_b, ws[g, :n_new, :n_new]) for g in range(GMLP_GROUPS)])
    bs_rows = jnp.tile(bs_tile[:n_new], (nbs, 1))
    mc = jnp.kron(eye_b, tril[:n_new, :n_new])
    qs, ks, vs, logf_s, cn, gv_s, gm_s, sga_s, sgg_s = _sample_mix(z, bf, glg, glb, mg, bs_rows, mc)

    kt_pool = cache_k[0].transpose(0, 2, 3, 1)
    vt_pool = cache_v[0].transpose(0, 2, 3, 1)
    logf_pool = cache_logf[0].transpose(0, 2, 1)
    nrow = n_new * N_HEADS
    q4 = qs.reshape(nbs, n_new, N_HEADS, HEAD_DIM)
    qbd = jnp.einsum("bqhd,hg->bqhgd", q4, jnp.eye(N_HEADS, dtype=F32)).reshape(nbs, nrow, D_ATTN)
    cn3 = cn[:, :N_HEADS].reshape(nbs, n_new, N_HEADS)
    cn_col = cn3.reshape(nbs, nrow, 1)
    cnb = jnp.pad(jnp.tile(cn3.transpose(0, 2, 1), (1, n_new, 1)), ((0, 0), (0, 0), (0, LANES - n_new)))
    new_t = lambda t: jnp.pad(t.reshape(nbs, n_new, D_ATTN).transpose(0, 2, 1), ((0, 0), (0, 0), (0, LANES - n_new)))
    a_s = _fox_sample(page_table, kt_pool, vt_pool, logf_pool, qbd, cn_col, new_t(ks), new_t(vs), cnb, n_new=n_new)
    a_s = a_s[:, :n_new].reshape(ns, D_ATTN)
    x1s, h2s, comb_s = _merge(a_s, gm_s, sga_s, sgg_s, xs, rep(mods[2]), rep(mods[4]), rep(mods[3]),
                              w_attn_out[0], w_gmlp_out[0], w_o[0], l1g, l1b, w_r, b_r,
                              rows_per_mod=ns, tm=ns, alpha=alpha, precise=True, sort=False)
    ys = _moe(h2s, comb_s, x1s, rep(mods[5]), w_exp_gate[0], w_exp_up[0], w_exp_down[0], l2g, l2b, alpha=alpha)

    hs = (N_HEADS, HEAD_DIM)
    untr = lambda t: t.reshape(1, nbp, *hs, seq).transpose(0, 1, 4, 2, 3)
    return (yp.reshape(nbp, seq, d), ys.reshape(nbs, n_new, d), untr(kt32), untr(vt32),
            logft.reshape(1, nbp, N_HEADS, seq).transpose(0, 1, 3, 2),
            ks.reshape(1, nbs, n_new, *hs), vs.reshape(1, nbs, n_new, *hs),
            logf_s[:, :N_HEADS].reshape(1, nbs, n_new, N_HEADS), gv_s.reshape(1, nbs, n_new, d_gmlp))
```

```python
import functools

import numpy as np
import jax
import jax.numpy as jnp
from jax import lax
from jax.experimental import pallas as pl
from jax.experimental.pallas import tpu as pltpu

F32 = jnp.float32
BF16 = jnp.bfloat16
HIGHEST = lax.Precision.HIGHEST

N_HEADS = 8
HEAD_DIM = 64
D_ATTN = N_HEADS * HEAD_DIM
PAGE_SIZE = 128
CHUNK = 128
GMLP_GROUPS = 4
N_GROUPS = 4
EXPERTS_PER_GROUP = 4
N_EXPERTS = N_GROUPS * EXPERTS_PER_GROUP
LN_EPS = 1e-5
LANES = 128
NEG_INF = float("-inf")
LOG2E = 1.4426950408889634

VMEM_LIMIT_BYTES = 56 * 1024 * 1024


def _params(*sem):
    return pltpu.CompilerParams(dimension_semantics=sem, vmem_limit_bytes=VMEM_LIMIT_BYTES)


def _full(shape):
    n = len(shape)
    return pl.BlockSpec(shape, lambda *_: (0,) * n)


def _ln(x, g, b):
    mu = jnp.mean(x, axis=-1, keepdims=True)
    xc = x - mu
    var = jnp.mean(xc * xc, axis=-1, keepdims=True)
    return xc * lax.rsqrt(var + LN_EPS) * g + b


def _mod(ref):
    return ref[0] if len(ref.shape) == 3 else ref[...]


def _dot(a, b, precise):
    if precise:
        return jnp.dot(a, b, precision=HIGHEST, preferred_element_type=F32)
    return jnp.dot(a.astype(BF16), b.astype(BF16), preferred_element_type=F32)


def _ada_kernel(c_ref, w_ref, b_ref, o_ref):
    c = c_ref[...]
    o_ref[...] = _dot(c * jax.nn.sigmoid(c), w_ref[...], True) + b_ref[...]


def _ada(c, w, b, *, tn=1024):
    n, d = c.shape
    dout = w.shape[1]
    return pl.pallas_call(
        _ada_kernel,
        out_shape=jax.ShapeDtypeStruct((n, dout), F32),
        grid=(dout // tn,),
        in_specs=[_full((n, d)), pl.BlockSpec((d, tn), lambda j: (0, j)), pl.BlockSpec((1, tn), lambda j: (0, j))],
        out_specs=pl.BlockSpec((n, tn), lambda j: (0, j)),
        compiler_params=_params("parallel"),
        name="ada",
    )(c, w, b)


_OFF_Q, _OFF_K, _OFF_V, _OFF_U, _OFF_GV = 0, 512, 1024, 1536, 2048
_OFF_GA, _OFF_GG, _OFF_F, _W_COLS = 2560, 3584, 4608, 4736


def _forget(zf, bf):
    return jax.nn.log_sigmoid(zf + bf)


def _gate_value(zgv, glg, glb):
    return _ln(jax.nn.gelu(zgv), glg, glb)


def _mixer_epilogue(zq, zk, zv, zf, zu, zgv, zga, zgg, bf, glg, glb):
    q = zq * (HEAD_DIM ** -0.5)
    return (q, zk, zv, _forget(zf, bf), jax.nn.gelu(zu), _gate_value(zgv, glg, glb), jax.nn.sigmoid(zga),
            jax.nn.sigmoid(zgg))


def _prompt_in_kernel(x_ref, sc_ref, sh_ref, wa_ref, wb_ref, wf_ref, bf_ref, glg_ref, glb_ref, ws_ref, bs_ref, tri_ref,
                      qt_ref, kt32_ref, kaug_ref, vt32_ref, vtaug_ref, logft_ref, cumt_ref, reft_ref, gm_ref,
                      sga_ref, sgg_ref, carry_ref, sp_ref, *, tiles_per_seq):
    tm = x_ref.shape[0]

    @pl.when(pl.program_id(0) % tiles_per_seq == 0)
    def _():
        carry_ref[...] = jnp.zeros_like(carry_ref)

    h = (x_ref[...] * (1.0 + sc_ref[0]) + sh_ref[0]).astype(BF16)

    def proj(off, width):
        ref, base = (wa_ref, 0) if off < _OFF_U else (wb_ref, _OFF_U) if off < _OFF_F else (wf_ref, _OFF_F)
        return jnp.dot(h, ref[:, off - base:off - base + width], preferred_element_type=F32)

    logf = _forget(proj(_OFF_F, LANES), bf_ref[...])
    gv = _gate_value(proj(_OFF_GV, 512), glg_ref[...], glb_ref[...])
    u = jax.nn.gelu(proj(_OFF_U, 512))
    q = proj(_OFF_Q, 512) * (HEAD_DIM ** -0.5 * LOG2E)
    k = proj(_OFF_K, 512)

    carry_in = carry_ref[...]
    carry = carry_in
    parts = []
    for n in range(tm // CHUNK):
        c = _dot(tri_ref[...], logf[n * CHUNK:(n + 1) * CHUNK], True) + carry
        parts.append(c)
        carry = c[CHUNK - 1:CHUNK]
    carry_ref[...] = carry
    cum = jnp.concatenate(parts, axis=0)

    gvb = gv.astype(BF16)
    gw = gv.shape[1] // GMLP_GROUPS
    for n in range(tm // CHUNK):
        for g in range(GMLP_GROUPS):
            sp_ref[n * CHUNK:(n + 1) * CHUNK, g * gw:(g + 1) * gw] = jnp.dot(
                ws_ref[g], gvb[n * CHUNK:(n + 1) * CHUNK, g * gw:(g + 1) * gw], preferred_element_type=F32)

    v = proj(_OFF_V, 512)
    sga_ref[...] = jax.nn.sigmoid(proj(_OFF_GA, 1024)).astype(BF16)
    sgg_ref[...] = jax.nn.sigmoid(proj(_OFF_GG, 1024)).astype(BF16)

    qt_ref[0] = q.T.astype(BF16)
    kt32_ref[0] = k.T
    vt = v.T
    vt32_ref[0] = vt
    logft_ref[0] = logf.T[:N_HEADS]
    cumt_ref[0] = (cum * LOG2E).T[:N_HEADS]
    reft_ref[0] = (jnp.broadcast_to(carry_in, cum.shape) * LOG2E).T[:N_HEADS]
    bs = jnp.concatenate([bs_ref[...]] * (tm // CHUNK), axis=0)
    gm_ref[...] = (u * (sp_ref[...] + bs)).astype(BF16)

    neg = (carry_in - cum) * LOG2E
    hi = neg.astype(BF16).astype(F32)
    mid = (neg - hi).astype(BF16).astype(F32)
    lo = (neg - hi - mid).astype(BF16).astype(F32)
    lane = lax.broadcasted_iota(jnp.int32, (tm, LANES), 1)
    for hd in range(N_HEADS):
        src = k[:, (hd // 2) * LANES:(hd // 2 + 1) * LANES]
        if hd % 2:
            src = pltpu.roll(src, HEAD_DIM, 1)
        blk = jnp.where(lane < HEAD_DIM, src, 0.0)
        for i, piece in enumerate((hi, mid, lo)):
            blk = jnp.where(lane == HEAD_DIM + i, piece[:, hd:hd + 1], blk)
        kaug_ref[:, hd * LANES:(hd + 1) * LANES] = blk.astype(BF16)

    ones_rows = jnp.where(lax.broadcasted_iota(jnp.int32, (HEAD_DIM, tm), 0) == 0, 1.0, 0.0)
    for hd in range(N_HEADS):
        grp = jnp.concatenate([vt[hd * HEAD_DIM:(hd + 1) * HEAD_DIM], ones_rows], axis=0)
        vtaug_ref[0, hd * LANES:(hd + 1) * LANES, :] = grp.astype(BF16)


def _prompt_in(x, sc, sh, wa, wb, wf, bf, glg, glb, ws, bs, tri, *, seq, tm):
    n, d = x.shape
    nb = n // seq
    tps = seq // tm
    row = lambda c: pl.BlockSpec((tm, c), lambda i: (i, 0))
    mod = pl.BlockSpec((1, 1, d), lambda i: (i // tps, 0, 0))
    tr = lambda r: pl.BlockSpec((1, r, tm), lambda i: (i // tps, 0, i % tps))
    aug = N_HEADS * LANES
    outs = [((nb, 512, seq), BF16), ((nb, 512, seq), F32), ((n, aug), BF16), ((nb, 512, seq), F32),
            ((nb, aug, seq), BF16), ((nb, N_HEADS, seq), F32), ((nb, N_HEADS, seq), F32),
            ((nb, N_HEADS, seq), F32), ((n, 512), BF16), ((n, 1024), BF16), ((n, 1024), BF16)]
    out_specs = [tr(512), tr(512), row(aug), tr(512), tr(aug), tr(N_HEADS), tr(N_HEADS), tr(N_HEADS),
                 row(512), row(1024), row(1024)]
    return pl.pallas_call(
        functools.partial(_prompt_in_kernel, tiles_per_seq=tps),
        out_shape=[jax.ShapeDtypeStruct(s, t) for s, t in outs],
        grid=(n // tm,),
        in_specs=[row(d), mod, mod, _full(wa.shape), _full(wb.shape), _full(wf.shape), _full(bf.shape),
                  _full(glg.shape), _full(glb.shape), _full(ws.shape), _full(bs.shape), _full(tri.shape)],
        out_specs=out_specs,
        scratch_shapes=[pltpu.VMEM((1, LANES), F32), pltpu.VMEM((tm, 512), F32)],
        compiler_params=_params("arbitrary"),
        name="prompt_in",
    )(x, sc, sh, wa, wb, wf, bf, glg, glb, ws, bs, tri)


ATTN_TILE = 512
QK_AHEAD = 2


def _fox_prompt_kernel(qt_ref, kaug_ref, vtaug_ref, cumt_ref, reft_ref, o_ref, m_ref, acc_ref):
    t = ATTN_TILE
    qi = pl.program_id(1)
    q0 = pl.multiple_of(qi * t, t)
    m_ref[...] = jnp.full_like(m_ref, NEG_INF)
    acc_ref[...] = jnp.zeros_like(acc_ref)
    ones3 = jnp.where(lax.broadcasted_iota(jnp.int32, (HEAD_DIM, t), 0) < 3, 1.0, 0.0).astype(BF16)
    qaug = [jnp.concatenate([qt_ref[h * HEAD_DIM:(h + 1) * HEAD_DIM, :], ones3], axis=0) for h in range(N_HEADS)]
    cb = [cumt_ref[h:h + 1, pl.ds(q0, t)] for h in range(N_HEADS)]

    def run(pieces):
        rounds = [(pc, h) for pc in pieces for h in range(N_HEADS)]

        def scores(pc, h):
            ks, kl, ql, _ = pc
            return jnp.dot(kaug_ref[pl.ds(ks, kl), h * LANES:(h + 1) * LANES], qaug[h][:, ql:],
                           preferred_element_type=F32)

        raw = [scores(*r) for r in rounds[:QK_AHEAD]]
        for n, ((ks, kl, ql, diag), h) in enumerate(rounds):
            if n + QK_AHEAD < len(rounds):
                raw.append(scores(*rounds[n + QK_AHEAD]))
            grp = slice(h * LANES, (h + 1) * LANES)
            st = raw[n]
            if diag:
                st = jnp.where(lax.broadcasted_iota(jnp.int32, st.shape, 0) <= lax.broadcasted_iota(jnp.int32, st.shape, 1),
                               st, NEG_INF)
            ref_j = jnp.concatenate([reft_ref[h:h + 1, pl.ds(ks, LANES)]] * ((t - ql) // LANES), axis=1)
            crow = cb[h][:, ql:] - ref_j
            m_old = m_ref[h, :, ql:]
            m_new = jnp.maximum(m_old, jnp.max(st, axis=0, keepdims=True) + crow)
            pt = jnp.exp2(st - (m_new - crow)).astype(BF16)
            acc_ref[h, :, ql:] = acc_ref[h, :, ql:] * jnp.exp2(m_old - m_new) + jnp.dot(
                vtaug_ref[grp, pl.ds(ks, kl)], pt, preferred_element_type=F32)
            m_ref[h, :, ql:] = m_new

    def full(j):
        return (pl.multiple_of(j * t, t), t, 0, False)

    def diagonal():
        half = t // 2
        return [(q0, half, 0, True), (pl.multiple_of(q0 + half, half), half, half, True)]

    lax.fori_loop(0, qi // 2, lambda jj, c: (run([full(2 * jj), full(2 * jj + 1)]), c)[1], 0)

    @pl.when(qi % 2 == 1)
    def _():
        run([full(qi - 1)] + diagonal())

    @pl.when(qi % 2 == 0)
    def _():
        run(diagonal())

    outs = []
    for h in range(N_HEADS):
        acc = acc_ref[h]
        outs.append(acc[:HEAD_DIM] / acc[HEAD_DIM:HEAD_DIM + 1])
    o_ref[...] = jnp.concatenate(outs, axis=0).T.astype(o_ref.dtype)


def _fox_prompt(qt, kaug, vtaug, cumt, reft, *, seq):
    nb = qt.shape[0]
    t = ATTN_TILE
    nq = seq // t
    aug = N_HEADS * LANES
    per_b = lambda r: pl.BlockSpec((None, r, seq), lambda b, i: (b, 0, 0))
    return pl.pallas_call(
        _fox_prompt_kernel,
        out_shape=jax.ShapeDtypeStruct((nb * seq, D_ATTN), BF16),
        grid=(nb, nq),
        in_specs=[pl.BlockSpec((None, D_ATTN, t), lambda b, i: (b, 0, i)),
                  pl.BlockSpec((seq, aug), lambda b, i: (b, 0)),
                  per_b(aug), per_b(N_HEADS), per_b(N_HEADS)],
        out_specs=pl.BlockSpec((t, D_ATTN), lambda b, i: (b * nq + i, 0)),
        scratch_shapes=[pltpu.VMEM((N_HEADS, 1, t), F32), pltpu.VMEM((N_HEADS, LANES, t), F32)],
        compiler_params=_params("parallel", "arbitrary"),
        name="fox_prompt",
    )(qt, kaug, vtaug, cumt, reft)


def _route(logits):
    lane = lax.broadcasted_iota(jnp.int32, logits.shape, 1)
    big = jnp.int32(LANES)
    is_g = (lane >= N_EXPERTS) & (lane < N_EXPERTS + N_GROUPS)
    gl = jnp.where(is_g, logits, NEG_INF)
    gmax = jnp.max(gl, axis=-1, keepdims=True)
    gi = jnp.min(jnp.where(gl == gmax, lane, big), axis=-1, keepdims=True) - N_EXPERTS
    pg_top = 1.0 / jnp.sum(jnp.exp(gl - gmax), axis=-1, keepdims=True)
    in_g = (lane >= gi * EXPERTS_PER_GROUP) & (lane < (gi + 1) * EXPERTS_PER_GROUP)
    el = jnp.where(in_g, logits, NEG_INF)
    m1 = jnp.max(el, axis=-1, keepdims=True)
    i1 = jnp.min(jnp.where(el == m1, lane, big), axis=-1, keepdims=True)
    el2 = jnp.where(lane == i1, NEG_INF, el)
    m2 = jnp.max(el2, axis=-1, keepdims=True)
    i2 = jnp.min(jnp.where(el2 == m2, lane, big), axis=-1, keepdims=True)
    e2 = jnp.exp(m2 - m1)
    w1 = pg_top / (1.0 + e2)
    return jnp.where(lane == i1, w1, jnp.where(lane == i2, w1 * e2, 0.0)), gi


ROUTE_CHUNK = 32
SORT_BLOCK = 512
SORTED_ROWS = 768
ROUTE_TILE = 512
EXPERT_SUB = 256


def _group_ranks(gi, ls_ref):
    tm = gi.shape[0]
    lane = lax.broadcasted_iota(jnp.int32, (tm, LANES), 1)
    onehot = jnp.where(lane == gi, 1.0, 0.0)
    return onehot, jnp.dot(ls_ref[...], onehot.astype(BF16), preferred_element_type=F32)


def _sort_block(h2b, comb, gi, onehot, before, xs_ref, cs_ref, pos_ref, cnt_ref):
    tm = h2b.shape[0]
    lane = lax.broadcasted_iota(jnp.int32, (tm, LANES), 1)
    tot = before[tm - 1:tm] + onehot[tm - 1:tm]
    cnt_ref[0] = tot
    padded = jnp.floor((tot + (ROUTE_CHUNK - 1)) * (1.0 / ROUTE_CHUNK)) * ROUTE_CHUNK
    lane1 = lax.broadcasted_iota(jnp.int32, (1, LANES), 1)
    start = jnp.zeros_like(padded)
    for s in range(1, N_GROUPS):
        start = start + jnp.where(lane1 >= s, pltpu.roll(padded, s, 1), 0.0)
    pos = jnp.sum(onehot * (start + before), axis=-1, keepdims=True)
    pos_ref[...] = pos
    pos_row = jnp.broadcast_to(pos, (tm, LANES)).T[0:1]
    rows = lax.broadcasted_iota(jnp.int32, (SORTED_ROWS, tm), 0).astype(F32)
    perm = jnp.where(rows == pos_row, 1.0, 0.0).astype(BF16)
    xs_ref[...] = jnp.dot(perm, h2b, preferred_element_type=F32).astype(BF16)
    rel = jnp.zeros_like(comb)
    for g in range(N_GROUPS):
        moved = comb if g == 0 else pltpu.roll(comb, LANES - g * EXPERTS_PER_GROUP, 1)
        rel = rel + jnp.where(gi == g, moved, 0.0)
    rel = jnp.where(lane < EXPERTS_PER_GROUP, rel, 0.0)
    hi = rel.astype(BF16)
    lo = (rel - hi.astype(F32)).astype(BF16)
    cs_ref[...] = (jnp.dot(perm, hi, preferred_element_type=F32) + jnp.dot(perm, lo, preferred_element_type=F32))


def _merge_kernel(a_ref, gm_ref, sga_ref, sgg_ref, x_ref, g1_ref, sc2_ref, sh2_ref, wao_ref, wgo_ref, wo_ref,
                  l1g_ref, l1b_ref, wr_ref, br_ref, *rest, alpha, precise, sort):
    if sort:
        ls_ref, x1_ref, xs_ref, cs_ref, pos_ref, cnt_ref, h2_s, comb_s, gi_s = rest

        @pl.when(pl.program_id(0) == 0)
        def _():
            h2_s[...] = jnp.zeros_like(h2_s)
            comb_s[...] = jnp.zeros_like(comb_s)
            gi_s[...] = jnp.zeros_like(gi_s)

        h2_prev, comb_prev, gi_prev = h2_s[...], comb_s[...], gi_s[...]
        onehot, before = _group_ranks(gi_prev, ls_ref)
    else:
        x1_ref, h2_ref, comb_ref = rest
    da = _dot(a_ref[...], wao_ref[...], precise)
    dg = _dot(gm_ref[...], wgo_ref[...], precise)
    t = sga_ref[...].astype(F32) * da + sgg_ref[...].astype(F32) * dg
    m = _dot(t, wo_ref[...], precise)
    if sort:
        _sort_block(h2_prev, comb_prev, gi_prev, onehot, before, xs_ref, cs_ref, pos_ref, cnt_ref)
    x1 = _ln(alpha * x_ref[...] + (1.0 + _mod(g1_ref)) * m, l1g_ref[...], l1b_ref[...])
    h2 = x1 * (1.0 + _mod(sc2_ref)) + _mod(sh2_ref)
    comb, gi = _route(_dot(h2, wr_ref[...], precise) + br_ref[...])
    if sort:
        h2_s[...] = h2.astype(BF16)
        comb_s[...] = comb
        gi_s[...] = gi
    else:
        h2_ref[...] = h2.astype(h2_ref.dtype)
        comb_ref[...] = comb
    x1_ref[...] = x1


def _merge(a, gm, sga, sgg, x, g1, sc2, sh2, wao, wgo, wo, l1g, l1b, wr, br, *, rows_per_mod, tm, alpha, precise,
           sort):
    n, d = x.shape
    nblk = n // tm
    cur = (lambda i: jnp.minimum(i, nblk - 1)) if sort else (lambda i: i)
    row = lambda c: pl.BlockSpec((tm, c), lambda i: (cur(i), 0))
    if g1.ndim == 3:
        mod = pl.BlockSpec((1, 1, d), lambda i: (cur(i) * tm // rows_per_mod, 0, 0))
    else:
        mod = row(d)
    ins = [a, gm, sga, sgg, x, g1, sc2, sh2, wao, wgo, wo, l1g, l1b, wr, br]
    in_specs = [row(D_ATTN), row(gm.shape[1]), row(d), row(d), row(d), mod, mod, mod, _full(wao.shape),
                _full(wgo.shape), _full(wo.shape), _full(l1g.shape), _full(l1b.shape), _full(wr.shape),
                _full(br.shape)]
    scratch = []
    if sort:
        assert tm == SORT_BLOCK
        ls = jnp.tril(jnp.ones((tm, tm), BF16), -1)
        ins.append(ls)
        in_specs.append(_full(ls.shape))
        prev = lambda i: jnp.maximum(i - 1, 0)
        srow = lambda c: pl.BlockSpec((SORTED_ROWS, c), lambda i: (prev(i), 0))
        out_shape = [((n, d), F32), ((nblk * SORTED_ROWS, d), BF16), ((nblk * SORTED_ROWS, LANES), F32),
                     ((n, 1), F32), ((nblk, 1, LANES), F32)]
        out_specs = [row(d), srow(d), srow(LANES), pl.BlockSpec((tm, 1), lambda i: (prev(i), 0)),
                     pl.BlockSpec((1, 1, LANES), lambda i: (prev(i), 0, 0))]
        scratch = [pltpu.VMEM((tm, d), BF16), pltpu.VMEM((tm, LANES), F32), pltpu.VMEM((tm, 1), jnp.int32)]
    else:
        out_shape = [((n, d), F32), ((n, d), BF16), ((n, LANES), F32)]
        out_specs = [row(d), row(d), row(LANES)]
    return pl.pallas_call(
        functools.partial(_merge_kernel, alpha=alpha, precise=precise, sort=sort),
        out_shape=[jax.ShapeDtypeStruct(s, t) for s, t in out_shape],
        grid=(nblk + 1 if sort else nblk,),
        in_specs=in_specs,
        out_specs=out_specs,
        scratch_shapes=scratch,
        compiler_params=_params("arbitrary"),
        name="merge_precise" if precise else "merge",
    )(*ins)


def _route_tables(counts, nblk):
    ch, cpb, cpt = ROUTE_CHUNK, SORTED_ROWS // ROUTE_CHUNK, ROUTE_TILE // ROUTE_CHUNK
    max_tiles = (nblk * (SORT_BLOCK + N_GROUPS * (ch - 1))) // ROUTE_TILE + N_GROUPS + 1
    cnt = counts[:, 0, :N_GROUPS].astype(jnp.int32)
    nch = (cnt + ch - 1) // ch
    loc = jnp.cumsum(nch, axis=1) - nch
    earlier = jnp.cumsum(nch, axis=0) - nch
    tiles_g = (nch.sum(0) + cpt - 1) // cpt
    tile_off = jnp.cumsum(tiles_g) - tiles_g
    n_tiles = tiles_g.sum()
    k = jnp.arange(cpb, dtype=jnp.int32)[None, None, :]
    used = (k >= loc[:, :, None]) & (k < (loc + nch)[:, :, None])
    glob = (tile_off * cpt)[None, :, None] + earlier[:, :, None] + k - loc[:, :, None]
    dst = jnp.sum(jnp.where(used, glob, 0), axis=1)
    blk_chunk = jnp.arange(nblk, dtype=jnp.int32)[:, None] * cpb + k[0]
    scatter_to = jnp.where(used.any(axis=1), dst, max_tiles * cpt)
    src = jnp.full((max_tiles * cpt,), cpb - 1, jnp.int32).at[scatter_to.reshape(-1)].set(
        blk_chunk.reshape(-1), mode="drop")
    t = jnp.arange(max_tiles, dtype=jnp.int32)
    tile_group = jnp.minimum(jnp.sum(t[:, None] >= (tile_off + tiles_g)[None, :], axis=1), N_GROUPS - 1)
    return src, dst.reshape(-1), tile_group.astype(jnp.int32), n_tiles.reshape(1).astype(jnp.int32), max_tiles


def _experts_kernel(src_ref, grp_ref, nt_ref, *refs):
    del src_ref
    cpt = ROUTE_TILE // ROUTE_CHUNK
    xr, cr = refs[:cpt], refs[cpt:2 * cpt]
    wg_ref, wu_ref, wd_ref, o_ref, wg_s, wu_s, wd_s = refs[2 * cpt:]
    t = pl.program_id(0)
    live = t < nt_ref[0]
    de = wg_ref.shape[3]

    @pl.when((t == 0) | (grp_ref[t] != grp_ref[jnp.maximum(t - 1, 0)]))
    def _():
        for e in range(EXPERTS_PER_GROUP):
            wg_s[:, e * de:(e + 1) * de] = wg_ref[0, e].astype(BF16)
            wu_s[:, e * de:(e + 1) * de] = wu_ref[0, e].astype(BF16)
            wd_s[e * de:(e + 1) * de, :] = wd_ref[0, e].astype(BF16)

    @pl.when(live)
    def _():
        per = EXPERT_SUB // ROUTE_CHUNK
        halves = range(ROUTE_TILE // EXPERT_SUB)
        xs = [jnp.concatenate([r[...] for r in xr[i * per:(i + 1) * per]], axis=0) for i in halves]
        gate_up = [(jnp.dot(x, wg_s[...], preferred_element_type=F32), jnp.dot(x, wu_s[...], preferred_element_type=F32))
                   for x in xs]
        for i, (hg, hu) in enumerate(gate_up):
            c = jnp.concatenate([r[...] for r in cr[i * per:(i + 1) * per]], axis=0)
            a = jnp.concatenate([(hg[:, e * de:(e + 1) * de] * jax.nn.sigmoid(hg[:, e * de:(e + 1) * de])
                                  * hu[:, e * de:(e + 1) * de] * c[:, e:e + 1]).astype(BF16)
                                 for e in range(EXPERTS_PER_GROUP)], axis=1)
            o_ref[i * EXPERT_SUB:(i + 1) * EXPERT_SUB, :] = jnp.dot(
                a, wd_s[...], preferred_element_type=F32).astype(o_ref.dtype)

    @pl.when(jnp.logical_not(live))
    def _():
        o_ref[...] = jnp.zeros_like(o_ref)


def _experts(src, tile_group, n_tiles, max_tiles, xs, cs, wg, wu, wd):
    d = xs.shape[1]
    de = wg.shape[2]
    cpt = ROUTE_TILE // ROUTE_CHUNK
    grouped = lambda w: w.reshape((N_GROUPS, EXPERTS_PER_GROUP) + w.shape[1:])

    def chunk(width):
        def one(r):
            return pl.BlockSpec((ROUTE_CHUNK, width), lambda t, src, grp, nt: (src[t * cpt + r], 0))
        return [one(r) for r in range(cpt)]

    wspec = lambda w: pl.BlockSpec((1, EXPERTS_PER_GROUP) + w.shape[1:], lambda t, src, grp, nt: (grp[t], 0, 0, 0))
    wide = EXPERTS_PER_GROUP * de
    grid_spec = pltpu.PrefetchScalarGridSpec(
        num_scalar_prefetch=3,
        grid=(max_tiles,),
        in_specs=chunk(d) + chunk(LANES) + [wspec(wg), wspec(wu), wspec(wd)],
        out_specs=pl.BlockSpec((ROUTE_TILE, d), lambda t, src, grp, nt: (t, 0)),
        scratch_shapes=[pltpu.VMEM((d, wide), BF16), pltpu.VMEM((d, wide), BF16), pltpu.VMEM((wide, d), BF16)],
    )
    return pl.pallas_call(
        _experts_kernel,
        out_shape=jax.ShapeDtypeStruct((max_tiles * ROUTE_TILE, d), BF16),
        grid_spec=grid_spec,
        compiler_params=_params("arbitrary"),
        name="experts",
    )(src, tile_group, n_tiles, *([xs] * cpt), *([cs] * cpt), grouped(wg), grouped(wu), grouped(wd))


def _unsort_kernel(dst_ref, *refs, alpha):
    del dst_ref
    cpb = SORTED_ROWS // ROUTE_CHUNK
    yr = refs[:cpb]
    pos_ref, x1_ref, g2_ref, l2g_ref, l2b_ref, o_ref = refs[cpb:]
    ys = jnp.concatenate([r[...] for r in yr], axis=0)
    tm = pos_ref.shape[0]
    cols = lax.broadcasted_iota(jnp.int32, (tm, SORTED_ROWS), 1).astype(F32)
    perm_t = jnp.where(cols == pos_ref[...], 1.0, 0.0).astype(BF16)
    f = jnp.dot(perm_t, ys, preferred_element_type=F32)
    o_ref[...] = _ln(alpha * x1_ref[...] + (1.0 + _mod(g2_ref)) * f, l2g_ref[...], l2b_ref[...])


def _unsort(dst, y, pos, x1, g2, l2g, l2b, *, rows_per_mod, alpha):
    n, d = x1.shape
    tm = SORT_BLOCK
    cpb = SORTED_ROWS // ROUTE_CHUNK

    def chunk(r):
        return pl.BlockSpec((ROUTE_CHUNK, d), lambda i, dst: (dst[i * cpb + r], 0))

    row = lambda c: pl.BlockSpec((tm, c), lambda i, dst: (i, 0))
    full = lambda s: pl.BlockSpec(s, lambda i, dst: (0,) * len(s))
    grid_spec = pltpu.PrefetchScalarGridSpec(
        num_scalar_prefetch=1,
        grid=(n // tm,),
        in_specs=[chunk(r) for r in range(cpb)] + [
            row(1), row(d), pl.BlockSpec((1, 1, d), lambda i, dst: (i * tm // rows_per_mod, 0, 0)),
            full(l2g.shape), full(l2b.shape)],
        out_specs=row(d),
    )
    return pl.pallas_call(
        functools.partial(_unsort_kernel, alpha=alpha),
        out_shape=jax.ShapeDtypeStruct((n, d), F32),
        grid_spec=grid_spec,
        compiler_params=_params("parallel"),
        name="unsort",
    )(dst, *([y] * cpb), pos, x1, g2, l2g, l2b)


def _moe_kernel(h2_ref, comb_ref, x1_ref, g2_ref, wg_ref, wu_ref, wd_ref, l2g_ref, l2b_ref, o_ref, *, alpha):
    e = pl.program_id(0)

    @pl.when(e == 0)
    def _():
        o_ref[...] = jnp.zeros_like(o_ref)

    h2 = h2_ref[...]
    comb = comb_ref[...]
    lane = lax.broadcasted_iota(jnp.int32, comb.shape, 1)
    w = jnp.sum(jnp.where(lane == e, comb, 0.0), axis=-1, keepdims=True)
    hg = jnp.dot(h2, wg_ref[0].astype(BF16), preferred_element_type=F32)
    hu = jnp.dot(h2, wu_ref[0].astype(BF16), preferred_element_type=F32)
    a = hg * jax.nn.sigmoid(hg) * hu * w
    o_ref[...] += jnp.dot(a.astype(BF16), wd_ref[0].astype(BF16), preferred_element_type=F32)

    @pl.when(e == pl.num_programs(0) - 1)
    def _():
        o_ref[...] = _ln(alpha * x1_ref[...] + (1.0 + g2_ref[...]) * o_ref[...], l2g_ref[...], l2b_ref[...])


def _moe(h2, comb, x1, g2, wg, wu, wd, l2g, l2b, *, alpha):
    n, d = x1.shape
    ne, _, de = wg.shape
    return pl.pallas_call(
        functools.partial(_moe_kernel, alpha=alpha),
        out_shape=jax.ShapeDtypeStruct((n, d), F32),
        grid=(ne,),
        in_specs=[_full((n, d)), _full((n, LANES)), _full((n, d)), _full((n, d)),
                  pl.BlockSpec((1, d, de), lambda e: (e, 0, 0)), pl.BlockSpec((1, d, de), lambda e: (e, 0, 0)),
                  pl.BlockSpec((1, de, d), lambda e: (e, 0, 0)), _full(l2g.shape), _full(l2b.shape)],
        out_specs=_full((n, d)),
        compiler_params=_params("arbitrary"),
        name="moe",
    )(h2, comb, x1, g2, wg, wu, wd, l2g, l2b)


def _sample_proj_kernel(x_ref, sc_ref, sh_ref, wt_ref, z_ref):
    h = x_ref[...] * (1.0 + sc_ref[...]) + sh_ref[...]
    z_ref[...] = lax.dot_general(h, wt_ref[...], (((1,), (1,)), ((), ())), precision=HIGHEST,
                                 preferred_element_type=F32)


def _sample_proj(x, sc, sh, wt):
    n, d = x.shape
    return pl.pallas_call(
        _sample_proj_kernel,
        out_shape=jax.ShapeDtypeStruct((n, wt.shape[0]), F32),
        compiler_params=pltpu.CompilerParams(vmem_limit_bytes=VMEM_LIMIT_BYTES),
        name="sample_proj",
    )(x, sc, sh, wt)


def _sample_mix_kernel(z_ref, bf_ref, glg_ref, glb_ref, mg_ref, bs_ref, mc_ref,
                       q_ref, k_ref, v_ref, logf_ref, cn_ref, gv_ref, gm_ref, sga_ref, sgg_ref):
    z = z_ref[...]
    sec = lambda off, width: z[:, off:off + width]
    q, k, v, logf, u, gv, sga, sgg = _mixer_epilogue(
        sec(_OFF_Q, 512), sec(_OFF_K, 512), sec(_OFF_V, 512), sec(_OFF_F, LANES), sec(_OFF_U, 512),
        sec(_OFF_GV, 512), sec(_OFF_GA, 1024), sec(_OFF_GG, 1024), bf_ref[...], glg_ref[...], glb_ref[...])
    q_ref[...] = q
    k_ref[...] = k
    v_ref[...] = v
    logf_ref[...] = logf
    cn_ref[...] = _dot(mc_ref[...], logf, True)
    gv_ref[...] = gv
    sga_ref[...] = sga
    sgg_ref[...] = sgg
    gw = gv.shape[1] // GMLP_GROUPS
    sp = jnp.concatenate([_dot(mg_ref[g], gv[:, g * gw:(g + 1) * gw], True) for g in range(GMLP_GROUPS)], axis=-1)
    gm_ref[...] = u * (sp + bs_ref[...])


def _sample_mix(z, bf, glg, glb, mg, bs, mc):
    n = z.shape[0]
    shapes = [(n, 512)] * 3 + [(n, LANES)] * 2 + [(n, 512)] * 2 + [(n, 1024)] * 2
    return pl.pallas_call(
        _sample_mix_kernel,
        out_shape=[jax.ShapeDtypeStruct(s, F32) for s in shapes],
        compiler_params=pltpu.CompilerParams(vmem_limit_bytes=VMEM_LIMIT_BYTES),
        name="sample_mix",
    )(z, bf, glg, glb, mg, bs, mc)


PAGES_PER_STEP = 32


def _fox_sample_kernel(pt_ref, kt_hbm, vt_hbm, lf_hbm, qbd_ref, cn_ref, kn_ref, vn_ref, cnb_ref, o_ref,
                       kbuf, vbuf, lbuf, sem, m_ref, l_ref, acc_ref, later_ref, *, n_chunks, n_new):
    P = PAGES_PER_STEP
    b, c = pl.program_id(0), pl.program_id(1)
    step = b * n_chunks + c
    n_steps = pl.num_programs(0) * n_chunks
    slot = lax.rem(step, 2)

    def page_copies(seq, chunk, sl, lookup):
        out = []
        for r in range(P):
            page = pt_ref[seq, (n_chunks - 1 - chunk) * P + r] if lookup else 0
            out += [pltpu.make_async_copy(kt_hbm.at[page], kbuf.at[sl, r], sem.at[sl]),
                    pltpu.make_async_copy(vt_hbm.at[page], vbuf.at[sl, r], sem.at[sl]),
                    pltpu.make_async_copy(lf_hbm.at[page], lbuf.at[sl, r], sem.at[sl])]
        return out

    def start_all(cps):
        for i, cp in enumerate(cps):
            cp.start(priority=1 if i % 3 == 1 else 0)

    @pl.when(step == 0)
    def _():
        start_all(page_copies(b, c, slot, True))

    @pl.when(step + 1 < n_steps)
    def _():
        nxt = step + 1
        start_all(page_copies(nxt // n_chunks, lax.rem(nxt, n_chunks), 1 - slot, True))

    @pl.when(c == 0)
    def _():
        m_ref[...] = jnp.full_like(m_ref, NEG_INF)
        l_ref[...] = jnp.zeros_like(l_ref)
        acc_ref[...] = jnp.zeros_like(acc_ref)
        later_ref[...] = jnp.zeros_like(later_ref)

    qbd = qbd_ref[0]
    cn = cn_ref[0]
    nt = (((1,), (1,)), ((), ()))

    def update(s, vs):
        m_old = m_ref[...]
        m_new = m_old
        for sb in s:
            m_new = jnp.maximum(m_new, jnp.max(sb, axis=-1, keepdims=True))
        alpha = jnp.exp(m_old - m_new)
        l_new = l_ref[...] * alpha
        acc = acc_ref[...] * alpha
        for sb, vb in zip(s, vs):
            p = jnp.exp(sb - m_new)
            l_new = l_new + jnp.sum(p, axis=-1, keepdims=True)
            acc = acc + lax.dot_general(p, vb, nt, preferred_element_type=F32)
        m_ref[...] = m_new
        l_ref[...] = l_new
        acc_ref[...] = acc

    for cp in page_copies(b, c, slot, False):
        cp.wait()

    lane = lax.broadcasted_iota(jnp.int32, (N_HEADS, PAGE_SIZE), 1)
    later = later_ref[...]
    bias = [None] * P
    for r in reversed(range(P)):
        y = lbuf[slot, r]
        for k in range(7):
            sh = 1 << k
            y = y + jnp.where(lane < PAGE_SIZE - sh, pltpu.roll(y, PAGE_SIZE - sh, 1), 0.0)
        bias[r] = jnp.where(lane < PAGE_SIZE - 1, pltpu.roll(y, PAGE_SIZE - 1, 1), 0.0) + later
        later = later + y[:, 0:1]
    later_ref[...] = later

    scores = []
    for r in range(P):
        st = jnp.dot(qbd, kbuf[slot, r].reshape(D_ATTN, PAGE_SIZE), preferred_element_type=F32)
        scores.append(st + jnp.concatenate([bias[r]] * n_new, axis=0) + cn)
    update(scores, [vbuf[slot, r].reshape(D_ATTN, PAGE_SIZE) for r in range(P)])

    @pl.when(c == n_chunks - 1)
    def _():
        st = jnp.dot(qbd, kn_ref[0], preferred_element_type=F32) + cn - cnb_ref[0]
        col = lax.broadcasted_iota(jnp.int32, st.shape, 1)
        row = lax.broadcasted_iota(jnp.int32, st.shape, 0)
        update([jnp.where(col <= row // N_HEADS, st, NEG_INF)], [vn_ref[0]])
        full = acc_ref[...] / l_ref[...]
        lane_head = lax.broadcasted_iota(jnp.int32, full.shape, 1) // HEAD_DIM
        row_head = lax.broadcasted_iota(jnp.int32, full.shape, 0) % N_HEADS
        own = jnp.where(lane_head == row_head, full, 0.0)
        rows = [jnp.sum(own[q * N_HEADS:(q + 1) * N_HEADS], axis=0, keepdims=True) for q in range(n_new)]
        o_ref[0] = jnp.concatenate(rows + [jnp.zeros((o_ref.shape[1] - n_new, D_ATTN), F32)], axis=0)


def _fox_sample(page_table, kt_pool, vt_pool, logf_pool, qbd, cn, kn, vn, cnb, *, n_new):
    nb, n_pages = page_table.shape
    P = PAGES_PER_STEP
    n_chunks = n_pages // P
    nrow = n_new * N_HEADS
    per_b = lambda s: pl.BlockSpec((1,) + s, lambda b, c, pt: (b,) + (0,) * len(s))
    hbm = pl.BlockSpec(memory_space=pl.ANY)
    kv_page = (N_HEADS, HEAD_DIM, PAGE_SIZE)
    grid_spec = pltpu.PrefetchScalarGridSpec(
        num_scalar_prefetch=1,
        grid=(nb, n_chunks),
        in_specs=[hbm, hbm, hbm, per_b((nrow, D_ATTN)), per_b((nrow, 1)), per_b((D_ATTN, LANES)),
                  per_b((D_ATTN, LANES)), per_b((nrow, LANES))],
        out_specs=per_b((8, D_ATTN)),
        scratch_shapes=[pltpu.VMEM((2, P) + kv_page, F32), pltpu.VMEM((2, P) + kv_page, F32),
                        pltpu.VMEM((2, P, N_HEADS, PAGE_SIZE), F32), pltpu.SemaphoreType.DMA((2,)),
                        pltpu.VMEM((nrow, 1), F32), pltpu.VMEM((nrow, 1), F32), pltpu.VMEM((nrow, D_ATTN), F32),
                        pltpu.VMEM((N_HEADS, PAGE_SIZE), F32)],
    )
    return pl.pallas_call(
        functools.partial(_fox_sample_kernel, n_chunks=n_chunks, n_new=n_new),
        out_shape=jax.ShapeDtypeStruct((nb, 8, D_ATTN), F32),
        grid_spec=grid_spec,
        compiler_params=_params("arbitrary", "arbitrary"),
        name="fox_sample",
    )(page_table, kt_pool, vt_pool, logf_pool, qbd, cn, kn, vn, cnb)


def _split_cols(z):
    f0 = 3 * D_ATTN
    return z[:, :f0], z[:, f0 + N_HEADS:], jnp.pad(z[:, f0:f0 + N_HEADS], ((0, 0), (0, LANES - N_HEADS)))


def kernel(x_prompt, x_sample, c_prompt, c_sample, cache_k, cache_v, cache_logf, page_table, w_ada, b_ada, w_in,
           b_f, gmlp_ln_g, gmlp_ln_b, w_s, b_s, w_attn_out, w_gmlp_out, w_o, ln1_g, ln1_b, w_group_router,
           b_group_router, w_expert_router, b_expert_router, w_exp_gate, w_exp_up, w_exp_down, ln2_g, ln2_b):
    depth = w_ada.shape[0]
    assert depth == 1
    nbp, seq, d = x_prompt.shape
    nbs, n_new, _ = x_sample.shape
    alpha = (2.0 * depth) ** 0.25
    n_pool = cache_k.shape[1]
    d_gmlp = gmlp_ln_g.shape[1]
    gw = d_gmlp // GMLP_GROUPS

    wa_b, wb_b, wf_b = _split_cols(w_in[0].astype(BF16))
    bf =jnp.pad(b_f[0], (0, LANES - N_HEADS))[None]
    glg, glb = gmlp_ln_g[0][None], gmlp_ln_b[0][None]
    tril = jnp.tril(jnp.ones((CHUNK, CHUNK), F32))
    ws = jnp.where(tril > 0, w_s[0], 0.0)
    bs_tile = jnp.repeat(b_s[0].T, gw, axis=1)
    w_r = jnp.pad(jnp.concatenate([w_expert_router[0], w_group_router[0]], axis=1),
                  ((0, 0), (0, LANES - N_EXPERTS - N_GROUPS)))
    b_r = jnp.pad(jnp.concatenate([b_expert_router[0], b_group_router[0]]), (0, LANES - N_EXPERTS - N_GROUPS))[None]
    l1g, l1b, l2g, l2b = ln1_g[0][None], ln1_b[0][None], ln2_g[0][None], ln2_b[0][None]

    nc = nbp + nbs
    c_all = jnp.pad(jnp.concatenate([c_prompt, c_sample], axis=0), ((0, -nc % 8), (0, 0)))
    mod = _ada(c_all, w_ada[0], b_ada[0][None])
    modp = [m[:, None, :] for m in jnp.split(mod[:nbp], 6, axis=-1)]
    mods = [m[:, None, :] for m in jnp.split(mod[nbp:nc], 6, axis=-1)]

    xp = x_prompt.reshape(nbp * seq, d)
    qt, kt32, kaug, vt32, vtaug, logft, cumt, reft, gm, sga, sgg = _prompt_in(
        xp, modp[1], modp[0], wa_b, wb_b, wf_b, bf, glg, glb, ws.astype(BF16), bs_tile, tril, seq=seq, tm=ATTN_TILE)
    a = _fox_prompt(qt, kaug, vtaug, cumt, reft, seq=seq)
    x1, xs_sorted, cs_sorted, pos, counts = _merge(
        a, gm, sga, sgg, xp, modp[2], modp[4], modp[3], w_attn_out[0].astype(BF16), w_gmlp_out[0].astype(BF16),
        w_o[0].astype(BF16), l1g, l1b, w_r.astype(BF16), b_r, rows_per_mod=seq, tm=SORT_BLOCK, alpha=alpha,
        precise=False, sort=True)
    src, dst, tile_group, n_tiles, max_tiles = _route_tables(counts, nbp * seq // SORT_BLOCK)
    y_sorted = _experts(src, tile_group, n_tiles, max_tiles, xs_sorted, cs_sorted, w_exp_gate[0], w_exp_up[0],
                        w_exp_down[0])
    yp = _unsort(dst, y_sorted, pos, x1, modp[5], l2g, l2b, rows_per_mod=seq, alpha=alpha)

    ns = nbs * n_new
    xs = x_sample.reshape(ns, d)
    rep = lambda m: jnp.repeat(m[:, 0, :], n_new, axis=0)
    z = jnp.concatenate(_split_cols(_sample_proj(xs, rep(mods[1]), rep(mods[0]), w_in[0].T)), axis=1)
    eye_b = jnp.eye(nbs, dtype=F32)
    mg = jnp.stack([jnp.kron(eye_b, ws[g, :n_new, :n_new]) for g in range(GMLP_GROUPS)])
    bs_rows = jnp.tile(bs_tile[:n_new], (nbs, 1))
    mc = jnp.kron(eye_b, tril[:n_new, :n_new])
    qs, ks, vs, logf_s, cn, gv_s, gm_s, sga_s, sgg_s = _sample_mix(z, bf, glg, glb, mg, bs_rows, mc)

    kt_pool = cache_k[0].transpose(0, 2, 3, 1)
    vt_pool = cache_v[0].transpose(0, 2, 3, 1)
    logf_pool = cache_logf[0].transpose(0, 2, 1)
    nrow = n_new * N_HEADS
    q4 = qs.reshape(nbs, n_new, N_HEADS, HEAD_DIM)
    qbd = jnp.einsum("bqhd,hg->bqhgd", q4, jnp.eye(N_HEADS, dtype=F32)).reshape(nbs, nrow, D_ATTN)
    cn3 = cn[:, :N_HEADS].reshape(nbs, n_new, N_HEADS)
    cn_col = cn3.reshape(nbs, nrow, 1)
    cnb = jnp.pad(jnp.tile(cn3.transpose(0, 2, 1), (1, n_new, 1)), ((0, 0), (0, 0), (0, LANES - n_new)))
    new_t = lambda t: jnp.pad(t.reshape(nbs, n_new, D_ATTN).transpose(0, 2, 1), ((0, 0), (0, 0), (0, LANES - n_new)))
    a_s = _fox_sample(page_table, kt_pool, vt_pool, logf_pool, qbd, cn_col, new_t(ks), new_t(vs), cnb, n_new=n_new)
    a_s = a_s[:, :n_new].reshape(ns, D_ATTN)
    x1s, h2s, comb_s = _merge(a_s, gm_s, sga_s, sgg_s, xs, rep(mods[2]), rep(mods[4]), rep(mods[3]),
                              w_attn_out[0], w_gmlp_out[0], w_o[0], l1g, l1b, w_r, b_r,
                              rows_per_mod=ns, tm=ns, alpha=alpha, precise=True, sort=False)
    ys = _moe(h2s, comb_s, x1s, rep(mods[5]), w_exp_gate[0], w_exp_up[0], w_exp_down[0], l2g, l2b, alpha=alpha)

    hs = (N_HEADS, HEAD_DIM)
    untr = lambda t: t.reshape(1, nbp, *hs, seq).transpose(0, 1, 4, 2, 3)
    return (yp.reshape(nbp, seq, d), ys.reshape(nbs, n_new, d), untr(kt32), untr(vt32),
            logft.reshape(1, nbp, N_HEADS, seq).transpose(0, 1, 3, 2),
            ks.reshape(1, nbs, n_new, *hs), vs.reshape(1, nbs, n_new, *hs),
            logf_s[:, :N_HEADS].reshape(1, nbs, n_new, N_HEADS), gv_s.reshape(1, nbs, n_new, d_gmlp))
```

```python
import functools

import numpy as np
import jax
import jax.numpy as jnp
from jax import lax
from jax.experimental import pallas as pl
from jax.experimental.pallas import tpu as pltpu

F32 = jnp.float32
BF16 = jnp.bfloat16
HIGHEST = lax.Precision.HIGHEST

N_HEADS = 8
HEAD_DIM = 64
D_ATTN = N_HEADS * HEAD_DIM
PAGE_SIZE = 128
CHUNK = 128
GMLP_GROUPS = 4
N_GROUPS = 4
EXPERTS_PER_GROUP = 4
N_EXPERTS = N_GROUPS * EXPERTS_PER_GROUP
LN_EPS = 1e-5
LANES = 128
NEG_INF = float("-inf")
LOG2E = 1.4426950408889634

VMEM_LIMIT_BYTES = 56 * 1024 * 1024


def _params(*sem):
    return pltpu.CompilerParams(dimension_semantics=sem, vmem_limit_bytes=VMEM_LIMIT_BYTES)


def _full(shape):
    n = len(shape)
    return pl.BlockSpec(shape, lambda *_: (0,) * n)


def _ln(x, g, b):
    mu = jnp.mean(x, axis=-1, keepdims=True)
    xc = x - mu
    var = jnp.mean(xc * xc, axis=-1, keepdims=True)
    return xc * lax.rsqrt(var + LN_EPS) * g + b


def _mod(ref):
    return ref[0] if len(ref.shape) == 3 else ref[...]


def _dot(a, b, precise):
    if precise:
        return jnp.dot(a, b, precision=HIGHEST, preferred_element_type=F32)
    return jnp.dot(a.astype(BF16), b.astype(BF16), preferred_element_type=F32)


def _ada_kernel(c_ref, w_ref, b_ref, o_ref):
    c = c_ref[...]
    o_ref[...] = _dot(c * jax.nn.sigmoid(c), w_ref[...], True) + b_ref[...]


def _ada(c, w, b, *, tn=1024):
    n, d = c.shape
    dout = w.shape[1]
    return pl.pallas_call(
        _ada_kernel,
        out_shape=jax.ShapeDtypeStruct((n, dout), F32),
        grid=(dout // tn,),
        in_specs=[_full((n, d)), pl.BlockSpec((d, tn), lambda j: (0, j)), pl.BlockSpec((1, tn), lambda j: (0, j))],
        out_specs=pl.BlockSpec((n, tn), lambda j: (0, j)),
        compiler_params=_params("parallel"),
        name="ada",
    )(c, w, b)


_OFF_Q, _OFF_K, _OFF_V, _OFF_U, _OFF_GV = 0, 512, 1024, 1536, 2048
_OFF_GA, _OFF_GG, _OFF_F, _W_COLS = 2560, 3584, 4608, 4736


def _forget(zf, bf):
    return jax.nn.log_sigmoid(zf + bf)


def _gate_value(zgv, glg, glb):
    return _ln(jax.nn.gelu(zgv), glg, glb)


def _mixer_epilogue(zq, zk, zv, zf, zu, zgv, zga, zgg, bf, glg, glb):
    q = zq * (HEAD_DIM ** -0.5)
    return (q, zk, zv, _forget(zf, bf), jax.nn.gelu(zu), _gate_value(zgv, glg, glb), jax.nn.sigmoid(zga),
            jax.nn.sigmoid(zgg))


def _prompt_in_kernel(x_ref, sc_ref, sh_ref, wa_ref, wb_ref, wf_ref, bf_ref, glg_ref, glb_ref, ws_ref, bs_ref, tri_ref,
                      qt_ref, kt32_ref, kaug_ref, vt32_ref, vtaug_ref, logft_ref, cumt_ref, reft_ref, gm_ref,
                      sga_ref, sgg_ref, carry_ref, sp_ref, *, tiles_per_seq):
    tm = x_ref.shape[0]

    @pl.when(pl.program_id(0) % tiles_per_seq == 0)
    def _():
        carry_ref[...] = jnp.zeros_like(carry_ref)

    h = (x_ref[...] * (1.0 + sc_ref[0]) + sh_ref[0]).astype(BF16)

    def proj(off, width):
        ref, base = (wa_ref, 0) if off < _OFF_U else (wb_ref, _OFF_U) if off < _OFF_F else (wf_ref, _OFF_F)
        return jnp.dot(h, ref[:, off - base:off - base + width], preferred_element_type=F32)

    logf = _forget(proj(_OFF_F, LANES), bf_ref[...])
    gv = _gate_value(proj(_OFF_GV, 512), glg_ref[...], glb_ref[...])
    u = jax.nn.gelu(proj(_OFF_U, 512))
    q = proj(_OFF_Q, 512) * (HEAD_DIM ** -0.5 * LOG2E)
    k = proj(_OFF_K, 512)

    carry_in = carry_ref[...]
    carry = carry_in
    parts = []
    for n in range(tm // CHUNK):
        c = _dot(tri_ref[...], logf[n * CHUNK:(n + 1) * CHUNK], True) + carry
        parts.append(c)
        carry = c[CHUNK - 1:CHUNK]
    carry_ref[...] = carry
    cum = jnp.concatenate(parts, axis=0)

    gvb = gv.astype(BF16)
    gw = gv.shape[1] // GMLP_GROUPS
    for n in range(tm // CHUNK):
        for g in range(GMLP_GROUPS):
            sp_ref[n * CHUNK:(n + 1) * CHUNK, g * gw:(g + 1) * gw] = jnp.dot(
                ws_ref[g], gvb[n * CHUNK:(n + 1) * CHUNK, g * gw:(g + 1) * gw], preferred_element_type=F32)

    v = proj(_OFF_V, 512)
    sga_ref[...] = jax.nn.sigmoid(proj(_OFF_GA, 1024)).astype(BF16)
    sgg_ref[...] = jax.nn.sigmoid(proj(_OFF_GG, 1024)).astype(BF16)

    qt_ref[0] = q.T.astype(BF16)
    kt32_ref[0] = k.T
    vt = v.T
    vt32_ref[0] = vt
    logft_ref[0] = logf.T[:N_HEADS]
    cumt_ref[0] = (cum * LOG2E).T[:N_HEADS]
    reft_ref[0] = (jnp.broadcast_to(carry_in, cum.shape) * LOG2E).T[:N_HEADS]
    bs = jnp.concatenate([bs_ref[...]] * (tm // CHUNK), axis=0)
    gm_ref[...] = (u * (sp_ref[...] + bs)).astype(BF16)

    neg = (carry_in - cum) * LOG2E
    hi = neg.astype(BF16).astype(F32)
    mid = (neg - hi).astype(BF16).astype(F32)
    lo = (neg - hi - mid).astype(BF16).astype(F32)
    lane = lax.broadcasted_iota(jnp.int32, (tm, LANES), 1)
    for hd in range(N_HEADS):
        src = k[:, (hd // 2) * LANES:(hd // 2 + 1) * LANES]
        if hd % 2:
            src = pltpu.roll(src, HEAD_DIM, 1)
        blk = jnp.where(lane < HEAD_DIM, src, 0.0)
        for i, piece in enumerate((hi, mid, lo)):
            blk = jnp.where(lane == HEAD_DIM + i, piece[:, hd:hd + 1], blk)
        kaug_ref[:, hd * LANES:(hd + 1) * LANES] = blk.astype(BF16)

    ones_rows = jnp.where(lax.broadcasted_iota(jnp.int32, (HEAD_DIM, tm), 0) == 0, 1.0, 0.0)
    for hd in range(N_HEADS):
        grp = jnp.concatenate([vt[hd * HEAD_DIM:(hd + 1) * HEAD_DIM], ones_rows], axis=0)
        vtaug_ref[0, hd * LANES:(hd + 1) * LANES, :] = grp.astype(BF16)


def _prompt_in(x, sc, sh, wa, wb, wf, bf, glg, glb, ws, bs, tri, *, seq, tm):
    n, d = x.shape
    nb = n // seq
    tps = seq // tm
    row = lambda c: pl.BlockSpec((tm, c), lambda i: (i, 0))
    mod = pl.BlockSpec((1, 1, d), lambda i: (i // tps, 0, 0))
    tr = lambda r: pl.BlockSpec((1, r, tm), lambda i: (i // tps, 0, i % tps))
    aug = N_HEADS * LANES
    outs = [((nb, 512, seq), BF16), ((nb, 512, seq), F32), ((n, aug), BF16), ((nb, 512, seq), F32),
            ((nb, aug, seq), BF16), ((nb, N_HEADS, seq), F32), ((nb, N_HEADS, seq), F32),
            ((nb, N_HEADS, seq), F32), ((n, 512), BF16), ((n, 1024), BF16), ((n, 1024), BF16)]
    out_specs = [tr(512), tr(512), row(aug), tr(512), tr(aug), tr(N_HEADS), tr(N_HEADS), tr(N_HEADS),
                 row(512), row(1024), row(1024)]
    return pl.pallas_call(
        functools.partial(_prompt_in_kernel, tiles_per_seq=tps),
        out_shape=[jax.ShapeDtypeStruct(s, t) for s, t in outs],
        grid=(n // tm,),
        in_specs=[row(d), mod, mod, _full(wa.shape), _full(wb.shape), _full(wf.shape), _full(bf.shape),
                  _full(glg.shape), _full(glb.shape), _full(ws.shape), _full(bs.shape), _full(tri.shape)],
        out_specs=out_specs,
        scratch_shapes=[pltpu.VMEM((1, LANES), F32), pltpu.VMEM((tm, 512), F32)],
        compiler_params=_params("arbitrary"),
        name="prompt_in",
    )(x, sc, sh, wa, wb, wf, bf, glg, glb, ws, bs, tri)


ATTN_TILE = 512
QK_AHEAD = 2


def _fox_prompt_kernel(qt_ref, kaug_ref, vtaug_ref, cumt_ref, reft_ref, o_ref, m_ref, acc_ref):
    t = ATTN_TILE
    qi = pl.program_id(1)
    q0 = pl.multiple_of(qi * t, t)
    m_ref[...] = jnp.full_like(m_ref, NEG_INF)
    acc_ref[...] = jnp.zeros_like(acc_ref)
    ones3 = jnp.where(lax.broadcasted_iota(jnp.int32, (HEAD_DIM, t), 0) < 3, 1.0, 0.0).astype(BF16)
    qaug = [jnp.concatenate([qt_ref[h * HEAD_DIM:(h + 1) * HEAD_DIM, :], ones3], axis=0) for h in range(N_HEADS)]
    cb = [cumt_ref[h:h + 1, pl.ds(q0, t)] for h in range(N_HEADS)]

    def run(pieces):
        rounds = [(pc, h) for pc in pieces for h in range(N_HEADS)]

        def scores(pc, h):
            ks, kl, ql, _ = pc
            return jnp.dot(kaug_ref[pl.ds(ks, kl), h * LANES:(h + 1) * LANES], qaug[h][:, ql:],
                           preferred_element_type=F32)

        raw = [scores(*r) for r in rounds[:QK_AHEAD]]
        for n, ((ks, kl, ql, diag), h) in enumerate(rounds):
            if n + QK_AHEAD < len(rounds):
                raw.append(scores(*rounds[n + QK_AHEAD]))
            grp = slice(h * LANES, (h + 1) * LANES)
            st = raw[n]
            if diag:
                st = jnp.where(lax.broadcasted_iota(jnp.int32, st.shape, 0) <= lax.broadcasted_iota(jnp.int32, st.shape, 1),
                               st, NEG_INF)
            ref_j = jnp.concatenate([reft_ref[h:h + 1, pl.ds(ks, LANES)]] * ((t - ql) // LANES), axis=1)
            crow = cb[h][:, ql:] - ref_j
            m_old = m_ref[h, :, ql:]
            m_new = jnp.maximum(m_old, jnp.max(st, axis=0, keepdims=True) + crow)
            pt = jnp.exp2(st - (m_new - crow)).astype(BF16)
            acc_ref[h, :, ql:] = acc_ref[h, :, ql:] * jnp.exp2(m_old - m_new) + jnp.dot(
                vtaug_ref[grp, pl.ds(ks, kl)], pt, preferred_element_type=F32)
            m_ref[h, :, ql:] = m_new

    def full(j):
        return (pl.multiple_of(j * t, t), t, 0, False)

    def diagonal():
        half = t // 2
        return [(q0, half, 0, True), (pl.multiple_of(q0 + half, half), half, half, True)]

    lax.fori_loop(0, qi // 2, lambda jj, c: (run([full(2 * jj), full(2 * jj + 1)]), c)[1], 0)

    @pl.when(qi % 2 == 1)
    def _():
        run([full(qi - 1)] + diagonal())

    @pl.when(qi % 2 == 0)
    def _():
        run(diagonal())

    outs = []
    for h in range(N_HEADS):
        acc = acc_ref[h]
        outs.append(acc[:HEAD_DIM] / acc[HEAD_DIM:HEAD_DIM + 1])
    o_ref[...] = jnp.concatenate(outs, axis=0).T.astype(o_ref.dtype)


def _fox_prompt(qt, kaug, vtaug, cumt, reft, *, seq):
    nb = qt.shape[0]
    t = ATTN_TILE
    nq = seq // t
    aug = N_HEADS * LANES
    per_b = lambda r: pl.BlockSpec((None, r, seq), lambda b, i: (b, 0, 0))
    return pl.pallas_call(
        _fox_prompt_kernel,
        out_shape=jax.ShapeDtypeStruct((nb * seq, D_ATTN), BF16),
        grid=(nb, nq),
        in_specs=[pl.BlockSpec((None, D_ATTN, t), lambda b, i: (b, 0, i)),
                  pl.BlockSpec((seq, aug), lambda b, i: (b, 0)),
                  per_b(aug), per_b(N_HEADS), per_b(N_HEADS)],
        out_specs=pl.BlockSpec((t, D_ATTN), lambda b, i: (b * nq + i, 0)),
        scratch_shapes=[pltpu.VMEM((N_HEADS, 1, t), F32), pltpu.VMEM((N_HEADS, LANES, t), F32)],
        compiler_params=_params("parallel", "arbitrary"),
        name="fox_prompt",
    )(qt, kaug, vtaug, cumt, reft)


def _route(logits):
    lane = lax.broadcasted_iota(jnp.int32, logits.shape, 1)
    big = jnp.int32(LANES)
    is_g = (lane >= N_EXPERTS) & (lane < N_EXPERTS + N_GROUPS)
    gl = jnp.where(is_g, logits, NEG_INF)
    gmax = jnp.max(gl, axis=-1, keepdims=True)
    gi = jnp.min(jnp.where(gl == gmax, lane, big), axis=-1, keepdims=True) - N_EXPERTS
    pg_top = 1.0 / jnp.sum(jnp.exp(gl - gmax), axis=-1, keepdims=True)
    in_g = (lane >= gi * EXPERTS_PER_GROUP) & (lane < (gi + 1) * EXPERTS_PER_GROUP)
    el = jnp.where(in_g, logits, NEG_INF)
    m1 = jnp.max(el, axis=-1, keepdims=True)
    i1 = jnp.min(jnp.where(el == m1, lane, big), axis=-1, keepdims=True)
    el2 = jnp.where(lane == i1, NEG_INF, el)
    m2 = jnp.max(el2, axis=-1, keepdims=True)
    i2 = jnp.min(jnp.where(el2 == m2, lane, big), axis=-1, keepdims=True)
    e2 = jnp.exp(m2 - m1)
    w1 = pg_top / (1.0 + e2)
    return jnp.where(lane == i1, w1, jnp.where(lane == i2, w1 * e2, 0.0)), gi


ROUTE_CHUNK = 32
SORT_BLOCK = 512
SORTED_ROWS = 768
ROUTE_TILE = 512
EXPERT_SUB = 256


def _group_ranks(gi, ls_ref):
    tm = gi.shape[0]
    lane = lax.broadcasted_iota(jnp.int32, (tm, LANES), 1)
    onehot = jnp.where(lane == gi, 1.0, 0.0)
    return onehot, jnp.dot(ls_ref[...], onehot.astype(BF16), preferred_element_type=F32)


def _sort_block(h2b, comb, gi, onehot, before, xs_ref, cs_ref, pos_ref, cnt_ref):
    tm = h2b.shape[0]
    lane = lax.broadcasted_iota(jnp.int32, (tm, LANES), 1)
    tot = before[tm - 1:tm] + onehot[tm - 1:tm]
    cnt_ref[0] = tot
    padded = jnp.floor((tot + (ROUTE_CHUNK - 1)) * (1.0 / ROUTE_CHUNK)) * ROUTE_CHUNK
    lane1 = lax.broadcasted_iota(jnp.int32, (1, LANES), 1)
    start = jnp.zeros_like(padded)
    for s in range(1, N_GROUPS):
        start = start + jnp.where(lane1 >= s, pltpu.roll(padded, s, 1), 0.0)
    pos = jnp.sum(onehot * (start + before), axis=-1, keepdims=True)
    pos_ref[...] = pos
    pos_row = jnp.broadcast_to(pos, (tm, LANES)).T[0:1]
    rows = lax.broadcasted_iota(jnp.int32, (SORTED_ROWS, tm), 0).astype(F32)
    perm = jnp.where(rows == pos_row, 1.0, 0.0).astype(BF16)
    xs_ref[...] = jnp.dot(perm, h2b, preferred_element_type=F32).astype(BF16)
    rel = jnp.zeros_like(comb)
    for g in range(N_GROUPS):
        moved = comb if g == 0 else pltpu.roll(comb, LANES - g * EXPERTS_PER_GROUP, 1)
        rel = rel + jnp.where(gi == g, moved, 0.0)
    rel = jnp.where(lane < EXPERTS_PER_GROUP, rel, 0.0)
    hi = rel.astype(BF16)
    lo = (rel - hi.astype(F32)).astype(BF16)
    cs_ref[...] = (jnp.dot(perm, hi, preferred_element_type=F32) + jnp.dot(perm, lo, preferred_element_type=F32))


def _merge_kernel(a_ref, gm_ref, sga_ref, sgg_ref, x_ref, g1_ref, sc2_ref, sh2_ref, wao_ref, wgo_ref, wo_ref,
                  l1g_ref, l1b_ref, wr_ref, br_ref, *rest, alpha, precise, sort):
    if sort:
        ls_ref, x1_ref, xs_ref, cs_ref, pos_ref, cnt_ref, h2_s, comb_s, gi_s = rest

        @pl.when(pl.program_id(0) == 0)
        def _():
            h2_s[...] = jnp.zeros_like(h2_s)
            comb_s[...] = jnp.zeros_like(comb_s)
            gi_s[...] = jnp.zeros_like(gi_s)

        h2_prev, comb_prev, gi_prev = h2_s[...], comb_s[...], gi_s[...]
        onehot, before = _group_ranks(gi_prev, ls_ref)
    else:
        x1_ref, h2_ref, comb_ref = rest
    da = _dot(a_ref[...], wao_ref[...], precise)
    dg = _dot(gm_ref[...], wgo_ref[...], precise)
    t = sga_ref[...].astype(F32) * da + sgg_ref[...].astype(F32) * dg
    m = _dot(t, wo_ref[...], precise)
    if sort:
        _sort_block(h2_prev, comb_prev, gi_prev, onehot, before, xs_ref, cs_ref, pos_ref, cnt_ref)
    x1 = _ln(alpha * x_ref[...] + (1.0 + _mod(g1_ref)) * m, l1g_ref[...], l1b_ref[...])
    h2 = x1 * (1.0 + _mod(sc2_ref)) + _mod(sh2_ref)
    comb, gi = _route(_dot(h2, wr_ref[...], precise) + br_ref[...])
    if sort:
        h2_s[...] = h2.astype(BF16)
        comb_s[...] = comb
        gi_s[...] = gi
    else:
        h2_ref[...] = h2.astype(h2_ref.dtype)
        comb_ref[...] = comb
    x1_ref[...] = x1


def _merge(a, gm, sga, sgg, x, g1, sc2, sh2, wao, wgo, wo, l1g, l1b, wr, br, *, rows_per_mod, tm, alpha, precise,
           sort):
    n, d = x.shape
    nblk = n // tm
    cur = (lambda i: jnp.minimum(i, nblk - 1)) if sort else (lambda i: i)
    row = lambda c: pl.BlockSpec((tm, c), lambda i: (cur(i), 0))
    if g1.ndim == 3:
        mod = pl.BlockSpec((1, 1, d), lambda i: (cur(i) * tm // rows_per_mod, 0, 0))
    else:
        mod = row(d)
    ins = [a, gm, sga, sgg, x, g1, sc2, sh2, wao, wgo, wo, l1g, l1b, wr, br]
    in_specs = [row(D_ATTN), row(gm.shape[1]), row(d), row(d), row(d), mod, mod, mod, _full(wao.shape),
                _full(wgo.shape), _full(wo.shape), _full(l1g.shape), _full(l1b.shape), _full(wr.shape),
                _full(br.shape)]
    scratch = []
    if sort:
        assert tm == SORT_BLOCK
        ls = jnp.tril(jnp.ones((tm, tm), BF16), -1)
        ins.append(ls)
        in_specs.append(_full(ls.shape))
        prev = lambda i: jnp.maximum(i - 1, 0)
        srow = lambda c: pl.BlockSpec((SORTED_ROWS, c), lambda i: (prev(i), 0))
        out_shape = [((n, d), F32), ((nblk * SORTED_ROWS, d), BF16), ((nblk * SORTED_ROWS, LANES), F32),
                     ((n, 1), F32), ((nblk, 1, LANES), F32)]
        out_specs = [row(d), srow(d), srow(LANES), pl.BlockSpec((tm, 1), lambda i: (prev(i), 0)),
                     pl.BlockSpec((1, 1, LANES), lambda i: (prev(i), 0, 0))]
        scratch = [pltpu.VMEM((tm, d), BF16), pltpu.VMEM((tm, LANES), F32), pltpu.VMEM((tm, 1), jnp.int32)]
    else:
        out_shape = [((n, d), F32), ((n, d), BF16), ((n, LANES), F32)]
        out_specs = [row(d), row(d), row(LANES)]
    return pl.pallas_call(
        functools.partial(_merge_kernel, alpha=alpha, precise=precise, sort=sort),
        out_shape=[jax.ShapeDtypeStruct(s, t) for s, t in out_shape],
        grid=(nblk + 1 if sort else nblk,),
        in_specs=in_specs,
        out_specs=out_specs,
        scratch_shapes=scratch,
        compiler_params=_params("arbitrary"),
        name="merge_precise" if precise else "merge",
    )(*ins)


def _route_tables(counts, nblk):
    ch, cpb, cpt = ROUTE_CHUNK, SORTED_ROWS // ROUTE_CHUNK, ROUTE_TILE // ROUTE_CHUNK
    max_tiles = (nblk * (SORT_BLOCK + N_GROUPS * (ch - 1))) // ROUTE_TILE + N_GROUPS + 1
    cnt = counts[:, 0, :N_GROUPS].astype(jnp.int32)
    nch = (cnt + ch - 1) // ch
    loc = jnp.cumsum(nch, axis=1) - nch
    earlier = jnp.cumsum(nch, axis=0) - nch
    tiles_g = (nch.sum(0) + cpt - 1) // cpt
    tile_off = jnp.cumsum(tiles_g) - tiles_g
    n_tiles = tiles_g.sum()
    k = jnp.arange(cpb, dtype=jnp.int32)[None, None, :]
    used = (k >= loc[:, :, None]) & (k < (loc + nch)[:, :, None])
    glob = (tile_off * cpt)[None, :, None] + earlier[:, :, None] + k - loc[:, :, None]
    dst = jnp.sum(jnp.where(used, glob, 0), axis=1)
    blk_chunk = jnp.arange(nblk, dtype=jnp.int32)[:, None] * cpb + k[0]
    scatter_to = jnp.where(used.any(axis=1), dst, max_tiles * cpt)
    src = jnp.full((max_tiles * cpt,), cpb - 1, jnp.int32).at[scatter_to.reshape(-1)].set(
        blk_chunk.reshape(-1), mode="drop")
    t = jnp.arange(max_tiles, dtype=jnp.int32)
    tile_group = jnp.minimum(jnp.sum(t[:, None] >= (tile_off + tiles_g)[None, :], axis=1), N_GROUPS - 1)
    return src, dst.reshape(-1), tile_group.astype(jnp.int32), n_tiles.reshape(1).astype(jnp.int32), max_tiles


def _experts_kernel(src_ref, grp_ref, nt_ref, *refs):
    del src_ref
    cpt = ROUTE_TILE // ROUTE_CHUNK
    xr, cr = refs[:cpt], refs[cpt:2 * cpt]
    wg_ref, wu_ref, wd_ref, o_ref, wg_s, wu_s, wd_s = refs[2 * cpt:]
    t = pl.program_id(0)
    live = t < nt_ref[0]
    de = wg_ref.shape[3]

    @pl.when((t == 0) | (grp_ref[t] != grp_ref[jnp.maximum(t - 1, 0)]))
    def _():
        for e in range(EXPERTS_PER_GROUP):
            wg_s[:, e * de:(e + 1) * de] = wg_ref[0, e].astype(BF16)
            wu_s[:, e * de:(e + 1) * de] = wu_ref[0, e].astype(BF16)
            wd_s[e * de:(e + 1) * de, :] = wd_ref[0, e].astype(BF16)

    @pl.when(live)
    def _():
        per = EXPERT_SUB // ROUTE_CHUNK
        halves = range(ROUTE_TILE // EXPERT_SUB)
        xs = [jnp.concatenate([r[...] for r in xr[i * per:(i + 1) * per]], axis=0) for i in halves]
        gate_up = [(jnp.dot(x, wg_s[...], preferred_element_type=F32), jnp.dot(x, wu_s[...], preferred_element_type=F32))
                   for x in xs]
        for i, (hg, hu) in enumerate(gate_up):
            c = jnp.concatenate([r[...] for r in cr[i * per:(i + 1) * per]], axis=0)
            a = jnp.concatenate([(hg[:, e * de:(e + 1) * de] * jax.nn.sigmoid(hg[:, e * de:(e + 1) * de])
                                  * hu[:, e * de:(e + 1) * de] * c[:, e:e + 1]).astype(BF16)
                                 for e in range(EXPERTS_PER_GROUP)], axis=1)
            o_ref[i * EXPERT_SUB:(i + 1) * EXPERT_SUB, :] = jnp.dot(
                a, wd_s[...], preferred_element_type=F32).astype(o_ref.dtype)

    @pl.when(jnp.logical_not(live))
    def _():
        o_ref[...] = jnp.zeros_like(o_ref)


def _experts(src, tile_group, n_tiles, max_tiles, xs, cs, wg, wu, wd):
    d = xs.shape[1]
    de = wg.shape[2]
    cpt = ROUTE_TILE // ROUTE_CHUNK
    grouped = lambda w: w.reshape((N_GROUPS, EXPERTS_PER_GROUP) + w.shape[1:])

    def chunk(width):
        def one(r):
            return pl.BlockSpec((ROUTE_CHUNK, width), lambda t, src, grp, nt: (src[t * cpt + r], 0))
        return [one(r) for r in range(cpt)]

    wspec = lambda w: pl.BlockSpec((1, EXPERTS_PER_GROUP) + w.shape[1:], lambda t, src, grp, nt: (grp[t], 0, 0, 0))
    wide = EXPERTS_PER_GROUP * de
    grid_spec = pltpu.PrefetchScalarGridSpec(
        num_scalar_prefetch=3,
        grid=(max_tiles,),
        in_specs=chunk(d) + chunk(LANES) + [wspec(wg), wspec(wu), wspec(wd)],
        out_specs=pl.BlockSpec((ROUTE_TILE, d), lambda t, src, grp, nt: (t, 0)),
        scratch_shapes=[pltpu.VMEM((d, wide), BF16), pltpu.VMEM((d, wide), BF16), pltpu.VMEM((wide, d), BF16)],
    )
    return pl.pallas_call(
        _experts_kernel,
        out_shape=jax.ShapeDtypeStruct((max_tiles * ROUTE_TILE, d), BF16),
        grid_spec=grid_spec,
        compiler_params=_params("arbitrary"),
        name="experts",
    )(src, tile_group, n_tiles, *([xs] * cpt), *([cs] * cpt), grouped(wg), grouped(wu), grouped(wd))


def _unsort_kernel(dst_ref, *refs, alpha):
    del dst_ref
    cpb = SORTED_ROWS // ROUTE_CHUNK
    yr = refs[:cpb]
    pos_ref, x1_ref, g2_ref, l2g_ref, l2b_ref, o_ref = refs[cpb:]
    ys = jnp.concatenate([r[...] for r in yr], axis=0)
    tm = pos_ref.shape[0]
    cols = lax.broadcasted_iota(jnp.int32, (tm, SORTED_ROWS), 1).astype(F32)
    perm_t = jnp.where(cols == pos_ref[...], 1.0, 0.0).astype(BF16)
    f = jnp.dot(perm_t, ys, preferred_element_type=F32)
    o_ref[...] = _ln(alpha * x1_ref[...] + (1.0 + _mod(g2_ref)) * f, l2g_ref[...], l2b_ref[...])


def _unsort(dst, y, pos, x1, g2, l2g, l2b, *, rows_per_mod, alpha):
    n, d = x1.shape
    tm = SORT_BLOCK
    cpb = SORTED_ROWS // ROUTE_CHUNK

    def chunk(r):
        return pl.BlockSpec((ROUTE_CHUNK, d), lambda i, dst: (dst[i * cpb + r], 0))

    row = lambda c: pl.BlockSpec((tm, c), lambda i, dst: (i, 0))
    full = lambda s: pl.BlockSpec(s, lambda i, dst: (0,) * len(s))
    grid_spec = pltpu.PrefetchScalarGridSpec(
        num_scalar_prefetch=1,
        grid=(n // tm,),
        in_specs=[chunk(r) for r in range(cpb)] + [
            row(1), row(d), pl.BlockSpec((1, 1, d), lambda i, dst: (i * tm // rows_per_mod, 0, 0)),
            full(l2g.shape), full(l2b.shape)],
        out_specs=row(d),
    )
    return pl.pallas_call(
        functools.partial(_unsort_kernel, alpha=alpha),
        out_shape=jax.ShapeDtypeStruct((n, d), F32),
        grid_spec=grid_spec,
        compiler_params=_params("parallel"),
        name="unsort",
    )(dst, *([y] * cpb), pos, x1, g2, l2g, l2b)


def _moe_kernel(h2_ref, comb_ref, x1_ref, g2_ref, wg_ref, wu_ref, wd_ref, l2g_ref, l2b_ref, o_ref, *, alpha):
    e = pl.program_id(0)

    @pl.when(e == 0)
    def _():
        o_ref[...] = jnp.zeros_like(o_ref)

    h2 = h2_ref[...]
    comb = comb_ref[...]
    lane = lax.broadcasted_iota(jnp.int32, comb.shape, 1)
    w = jnp.sum(jnp.where(lane == e, comb, 0.0), axis=-1, keepdims=True)
    hg = jnp.dot(h2, wg_ref[0].astype(BF16), preferred_element_type=F32)
    hu = jnp.dot(h2, wu_ref[0].astype(BF16), preferred_element_type=F32)
    a = hg * jax.nn.sigmoid(hg) * hu * w
    o_ref[...] += jnp.dot(a.astype(BF16), wd_ref[0].astype(BF16), preferred_element_type=F32)

    @pl.when(e == pl.num_programs(0) - 1)
    def _():
        o_ref[...] = _ln(alpha * x1_ref[...] + (1.0 + g2_ref[...]) * o_ref[...], l2g_ref[...], l2b_ref[...])


def _moe(h2, comb, x1, g2, wg, wu, wd, l2g, l2b, *, alpha):
    n, d = x1.shape
    ne, _, de = wg.shape
    return pl.pallas_call(
        functools.partial(_moe_kernel, alpha=alpha),
        out_shape=jax.ShapeDtypeStruct((n, d), F32),
        grid=(ne,),
        in_specs=[_full((n, d)), _full((n, LANES)), _full((n, d)), _full((n, d)),
                  pl.BlockSpec((1, d, de), lambda e: (e, 0, 0)), pl.BlockSpec((1, d, de), lambda e: (e, 0, 0)),
                  pl.BlockSpec((1, de, d), lambda e: (e, 0, 0)), _full(l2g.shape), _full(l2b.shape)],
        out_specs=_full((n, d)),
        compiler_params=_params("arbitrary"),
        name="moe",
    )(h2, comb, x1, g2, wg, wu, wd, l2g, l2b)


def _sample_proj_kernel(x_ref, sc_ref, sh_ref, wt_ref, z_ref):
    h = x_ref[...] * (1.0 + sc_ref[...]) + sh_ref[...]
    z_ref[...] = lax.dot_general(h, wt_ref[...], (((1,), (1,)), ((), ())), precision=HIGHEST,
                                 preferred_element_type=F32)


def _sample_proj(x, sc, sh, wt):
    n, d = x.shape
    return pl.pallas_call(
        _sample_proj_kernel,
        out_shape=jax.ShapeDtypeStruct((n, wt.shape[0]), F32),
        compiler_params=pltpu.CompilerParams(vmem_limit_bytes=VMEM_LIMIT_BYTES),
        name="sample_proj",
    )(x, sc, sh, wt)


def _sample_mix_kernel(z_ref, bf_ref, glg_ref, glb_ref, mg_ref, bs_ref, mc_ref,
                       q_ref, k_ref, v_ref, logf_ref, cn_ref, gv_ref, gm_ref, sga_ref, sgg_ref):
    z = z_ref[...]
    sec = lambda off, width: z[:, off:off + width]
    q, k, v, logf, u, gv, sga, sgg = _mixer_epilogue(
        sec(_OFF_Q, 512), sec(_OFF_K, 512), sec(_OFF_V, 512), sec(_OFF_F, LANES), sec(_OFF_U, 512),
        sec(_OFF_GV, 512), sec(_OFF_GA, 1024), sec(_OFF_GG, 1024), bf_ref[...], glg_ref[...], glb_ref[...])
    q_ref[...] = q
    k_ref[...] = k
    v_ref[...] = v
    logf_ref[...] = logf
    cn_ref[...] = _dot(mc_ref[...], logf, True)
    gv_ref[...] = gv
    sga_ref[...] = sga
    sgg_ref[...] = sgg
    gw = gv.shape[1] // GMLP_GROUPS
    sp = jnp.concatenate([_dot(mg_ref[g], gv[:, g * gw:(g + 1) * gw], True) for g in range(GMLP_GROUPS)], axis=-1)
    gm_ref[...] = u * (sp + bs_ref[...])


def _sample_mix(z, bf, glg, glb, mg, bs, mc):
    n = z.shape[0]
    shapes = [(n, 512)] * 3 + [(n, LANES)] * 2 + [(n, 512)] * 2 + [(n, 1024)] * 2
    return pl.pallas_call(
        _sample_mix_kernel,
        out_shape=[jax.ShapeDtypeStruct(s, F32) for s in shapes],
        compiler_params=pltpu.CompilerParams(vmem_limit_bytes=VMEM_LIMIT_BYTES),
        name="sample_mix",
    )(z, bf, glg, glb, mg, bs, mc)


PAGES_PER_STEP = 8


def _fox_sample_kernel(pt_ref, kt_hbm, vt_hbm, lf_hbm, qbd_ref, cn_ref, kn_ref, vn_ref, cnb_ref, o_ref,
                       kbuf, vbuf, lbuf, sem, m_ref, l_ref, acc_ref, later_ref, *, n_chunks, n_new):
    P = PAGES_PER_STEP
    b, c = pl.program_id(0), pl.program_id(1)
    step = b * n_chunks + c
    n_steps = pl.num_programs(0) * n_chunks
    slot = lax.rem(step, 2)

    def page_copies(seq, chunk, sl, lookup):
        out = []
        for r in range(P):
            page = pt_ref[seq, (n_chunks - 1 - chunk) * P + r] if lookup else 0
            out += [pltpu.make_async_copy(kt_hbm.at[page], kbuf.at[sl, r], sem.at[sl]),
                    pltpu.make_async_copy(vt_hbm.at[page], vbuf.at[sl, r], sem.at[sl]),
                    pltpu.make_async_copy(lf_hbm.at[page], lbuf.at[sl, r], sem.at[sl])]
        return out

    def start_all(cps):
        for i, cp in enumerate(cps):
            cp.start(priority=1 if i % 3 == 1 else 0)

    @pl.when(step == 0)
    def _():
        start_all(page_copies(b, c, slot, True))

    @pl.when(step + 1 < n_steps)
    def _():
        nxt = step + 1
        start_all(page_copies(nxt // n_chunks, lax.rem(nxt, n_chunks), 1 - slot, True))

    @pl.when(c == 0)
    def _():
        m_ref[...] = jnp.full_like(m_ref, NEG_INF)
        l_ref[...] = jnp.zeros_like(l_ref)
        acc_ref[...] = jnp.zeros_like(acc_ref)
        later_ref[...] = jnp.zeros_like(later_ref)

    qbd = qbd_ref[0]
    cn = cn_ref[0]
    nt = (((1,), (1,)), ((), ()))

    def update(s, vs):
        m_old = m_ref[...]
        m_new = m_old
        for sb in s:
            m_new = jnp.maximum(m_new, jnp.max(sb, axis=-1, keepdims=True))
        alpha = jnp.exp(m_old - m_new)
        l_new = l_ref[...] * alpha
        acc = acc_ref[...] * alpha
        for sb, vb in zip(s, vs):
            p = jnp.exp(sb - m_new)
            l_new = l_new + jnp.sum(p, axis=-1, keepdims=True)
            acc = acc + lax.dot_general(p, vb, nt, preferred_element_type=F32)
        m_ref[...] = m_new
        l_ref[...] = l_new
        acc_ref[...] = acc

    for cp in page_copies(b, c, slot, False):
        cp.wait()

    lane = lax.broadcasted_iota(jnp.int32, (N_HEADS, PAGE_SIZE), 1)
    later = later_ref[...]
    bias = [None] * P
    for r in reversed(range(P)):
        y = lbuf[slot, r]
        for k in range(7):
            sh = 1 << k
            y = y + jnp.where(lane < PAGE_SIZE - sh, pltpu.roll(y, PAGE_SIZE - sh, 1), 0.0)
        bias[r] = jnp.where(lane < PAGE_SIZE - 1, pltpu.roll(y, PAGE_SIZE - 1, 1), 0.0) + later
        later = later + y[:, 0:1]
    later_ref[...] = later

    scores = []
    for r in range(P):
        st = jnp.dot(qbd, kbuf[slot, r].reshape(D_ATTN, PAGE_SIZE), preferred_element_type=F32)
        scores.append(st + jnp.concatenate([bias[r]] * n_new, axis=0) + cn)
    update(scores, [vbuf[slot, r].reshape(D_ATTN, PAGE_SIZE) for r in range(P)])

    @pl.when(c == n_chunks - 1)
    def _():
        st = jnp.dot(qbd, kn_ref[0], preferred_element_type=F32) + cn - cnb_ref[0]
        col = lax.broadcasted_iota(jnp.int32, st.shape, 1)
        row = lax.broadcasted_iota(jnp.int32, st.shape, 0)
        update([jnp.where(col <= row // N_HEADS, st, NEG_INF)], [vn_ref[0]])
        full = acc_ref[...] / l_ref[...]
        lane_head = lax.broadcasted_iota(jnp.int32, full.shape, 1) // HEAD_DIM
        row_head = lax.broadcasted_iota(jnp.int32, full.shape, 0) % N_HEADS
        own = jnp.where(lane_head == row_head, full, 0.0)
        rows = [jnp.sum(own[q * N_HEADS:(q + 1) * N_HEADS], axis=0, keepdims=True) for q in range(n_new)]
        o_ref[0] = jnp.concatenate(rows + [jnp.zeros((o_ref.shape[1] - n_new, D_ATTN), F32)], axis=0)


def _fox_sample(page_table, kt_pool, vt_pool, logf_pool, qbd, cn, kn, vn, cnb, *, n_new):
    nb, n_pages = page_table.shape
    P = PAGES_PER_STEP
    n_chunks = n_pages // P
    nrow = n_new * N_HEADS
    per_b = lambda s: pl.BlockSpec((1,) + s, lambda b, c, pt: (b,) + (0,) * len(s))
    hbm = pl.BlockSpec(memory_space=pl.ANY)
    kv_page = (N_HEADS, HEAD_DIM, PAGE_SIZE)
    grid_spec = pltpu.PrefetchScalarGridSpec(
        num_scalar_prefetch=1,
        grid=(nb, n_chunks),
        in_specs=[hbm, hbm, hbm, per_b((nrow, D_ATTN)), per_b((nrow, 1)), per_b((D_ATTN, LANES)),
                  per_b((D_ATTN, LANES)), per_b((nrow, LANES))],
        out_specs=per_b((8, D_ATTN)),
        scratch_shapes=[pltpu.VMEM((2, P) + kv_page, F32), pltpu.VMEM((2, P) + kv_page, F32),
                        pltpu.VMEM((2, P, N_HEADS, PAGE_SIZE), F32), pltpu.SemaphoreType.DMA((2,)),
                        pltpu.VMEM((nrow, 1), F32), pltpu.VMEM((nrow, 1), F32), pltpu.VMEM((nrow, D_ATTN), F32),
                        pltpu.VMEM((N_HEADS, PAGE_SIZE), F32)],
    )
    return pl.pallas_call(
        functools.partial(_fox_sample_kernel, n_chunks=n_chunks, n_new=n_new),
        out_shape=jax.ShapeDtypeStruct((nb, 8, D_ATTN), F32),
        grid_spec=grid_spec,
        compiler_params=_params("arbitrary", "arbitrary"),
        name="fox_sample",
    )(page_table, kt_pool, vt_pool, logf_pool, qbd, cn, kn, vn, cnb)


def _split_cols(z):
    f0 = 3 * D_ATTN
    return z[:, :f0], z[:, f0 + N_HEADS:], jnp.pad(z[:, f0:f0 + N_HEADS], ((0, 0), (0, LANES - N_HEADS)))


def kernel(x_prompt, x_sample, c_prompt, c_sample, cache_k, cache_v, cache_logf, page_table, w_ada, b_ada, w_in,
           b_f, gmlp_ln_g, gmlp_ln_b, w_s, b_s, w_attn_out, w_gmlp_out, w_o, ln1_g, ln1_b, w_group_router,
           b_group_router, w_expert_router, b_expert_router, w_exp_gate, w_exp_up, w_exp_down, ln2_g, ln2_b):
    depth = w_ada.shape[0]
    assert depth == 1
    nbp, seq, d = x_prompt.shape
    nbs, n_new, _ = x_sample.shape
    alpha = (2.0 * depth) ** 0.25
    n_pool = cache_k.shape[1]
    d_gmlp = gmlp_ln_g.shape[1]
    gw = d_gmlp // GMLP_GROUPS

    wa_b, wb_b, wf_b = _split_cols(w_in[0].astype(BF16))
    bf =jnp.pad(b_f[0], (0, LANES - N_HEADS))[None]
    glg, glb = gmlp_ln_g[0][None], gmlp_ln_b[0][None]
    tril = jnp.tril(jnp.ones((CHUNK, CHUNK), F32))
    ws = jnp.where(tril > 0, w_s[0], 0.0)
    bs_tile = jnp.repeat(b_s[0].T, gw, axis=1)
    w_r = jnp.pad(jnp.concatenate([w_expert_router[0], w_group_router[0]], axis=1),
                  ((0, 0), (0, LANES - N_EXPERTS - N_GROUPS)))
    b_r = jnp.pad(jnp.concatenate([b_expert_router[0], b_group_router[0]]), (0, LANES - N_EXPERTS - N_GROUPS))[None]
    l1g, l1b, l2g, l2b = ln1_g[0][None], ln1_b[0][None], ln2_g[0][None], ln2_b[0][None]

    nc = nbp + nbs
    c_all = jnp.pad(jnp.concatenate([c_prompt, c_sample], axis=0), ((0, -nc % 8), (0, 0)))
    mod = _ada(c_all, w_ada[0], b_ada[0][None])
    modp = [m[:, None, :] for m in jnp.split(mod[:nbp], 6, axis=-1)]
    mods = [m[:, None, :] for m in jnp.split(mod[nbp:nc], 6, axis=-1)]

    xp = x_prompt.reshape(nbp * seq, d)
    qt, kt32, kaug, vt32, vtaug, logft, cumt, reft, gm, sga, sgg = _prompt_in(
        xp, modp[1], modp[0], wa_b, wb_b, wf_b, bf, glg, glb, ws.astype(BF16), bs_tile, tril, seq=seq, tm=ATTN_TILE)
    a = _fox_prompt(qt, kaug, vtaug, cumt, reft, seq=seq)
    x1, xs_sorted, cs_sorted, pos, counts = _merge(
        a, gm, sga, sgg, xp, modp[2], modp[4], modp[3], w_attn_out[0].astype(BF16), w_gmlp_out[0].astype(BF16),
        w_o[0].astype(BF16), l1g, l1b, w_r.astype(BF16), b_r, rows_per_mod=seq, tm=SORT_BLOCK, alpha=alpha,
        precise=False, sort=True)
    src, dst, tile_group, n_tiles, max_tiles = _route_tables(counts, nbp * seq // SORT_BLOCK)
    y_sorted = _experts(src, tile_group, n_tiles, max_tiles, xs_sorted, cs_sorted, w_exp_gate[0], w_exp_up[0],
                        w_exp_down[0])
    yp = _unsort(dst, y_sorted, pos, x1, modp[5], l2g, l2b, rows_per_mod=seq, alpha=alpha)

    ns = nbs * n_new
    xs = x_sample.reshape(ns, d)
    rep = lambda m: jnp.repeat(m[:, 0, :], n_new, axis=0)
    z = jnp.concatenate(_split_cols(_sample_proj(xs, rep(mods[1]), rep(mods[0]), w_in[0].T)), axis=1)
    eye_b = jnp.eye(nbs, dtype=F32)
    mg = jnp.stack([jnp.kron(eye_b, ws[g, :n_new, :n_new]) for g in range(GMLP_GROUPS)])
    bs_rows = jnp.tile(bs_tile[:n_new], (nbs, 1))
    mc = jnp.kron(eye_b, tril[:n_new, :n_new])
    qs, ks, vs, logf_s, cn, gv_s, gm_s, sga_s, sgg_s = _sample_mix(z, bf, glg, glb, mg, bs_rows, mc)

    kt_pool = cache_k[0].transpose(0, 2, 3, 1)
    vt_pool = cache_v[0].transpose(0, 2, 3, 1)
    logf_pool = cache_logf[0].transpose(0, 2, 1)
    nrow = n_new * N_HEADS
    q4 = qs.reshape(nbs, n_new, N_HEADS, HEAD_DIM)
    qbd = jnp.einsum("bqhd,hg->bqhgd", q4, jnp.eye(N_HEADS, dtype=F32)).reshape(nbs, nrow, D_ATTN)
    cn3 = cn[:, :N_HEADS].reshape(nbs, n_new, N_HEADS)
    cn_col = cn3.reshape(nbs, nrow, 1)
    cnb = jnp.pad(jnp.tile(cn3.transpose(0, 2, 1), (1, n_new, 1)), ((0, 0), (0, 0), (0, LANES - n_new)))
    new_t = lambda t: jnp.pad(t.reshape(nbs, n_new, D_ATTN).transpose(0, 2, 1), ((0, 0), (0, 0), (0, LANES - n_new)))
    a_s = _fox_sample(page_table, kt_pool, vt_pool, logf_pool, qbd, cn_col, new_t(ks), new_t(vs), cnb, n_new=n_new)
    a_s = a_s[:, :n_new].reshape(ns, D_ATTN)
    x1s, h2s, comb_s = _merge(a_s, gm_s, sga_s, sgg_s, xs, rep(mods[2]), rep(mods[4]), rep(mods[3]),
                              w_attn_out[0], w_gmlp_out[0], w_o[0], l1g, l1b, w_r, b_r,
                              rows_per_mod=ns, tm=ns, alpha=alpha, precise=True, sort=False)
    ys = _moe(h2s, comb_s, x1s, rep(mods[5]), w_exp_gate[0], w_exp_up[0], w_exp_down[0], l2g, l2b, alpha=alpha)

    hs = (N_HEADS, HEAD_DIM)
    untr = lambda t: t.reshape(1, nbp, *hs, seq).transpose(0, 1, 4, 2, 3)
    return (yp.reshape(nbp, seq, d), ys.reshape(nbs, n_new, d), untr(kt32), untr(vt32),
            logft.reshape(1, nbp, N_HEADS, seq).transpose(0, 1, 3, 2),
            ks.reshape(1, nbs, n_new, *hs), vs.reshape(1, nbs, n_new, *hs),
            logf_s[:, :N_HEADS].reshape(1, nbs, n_new, N_HEADS), gv_s.reshape(1, nbs, n_new, d_gmlp))
```

```python
import functools

import numpy as np
import jax
import jax.numpy as jnp
from jax import lax
from jax.experimental import pallas as pl
from jax.experimental.pallas import tpu as pltpu

F32 = jnp.float32
BF16 = jnp.bfloat16
HIGHEST = lax.Precision.HIGHEST

N_HEADS = 8
HEAD_DIM = 64
D_ATTN = N_HEADS * HEAD_DIM
PAGE_SIZE = 128
CHUNK = 128
GMLP_GROUPS = 4
N_GROUPS = 4
EXPERTS_PER_GROUP = 4
N_EXPERTS = N_GROUPS * EXPERTS_PER_GROUP
LN_EPS = 1e-5
LANES = 128
NEG_INF = float("-inf")
LOG2E = 1.4426950408889634

VMEM_LIMIT_BYTES = 56 * 1024 * 1024


def _params(*sem):
    return pltpu.CompilerParams(dimension_semantics=sem, vmem_limit_bytes=VMEM_LIMIT_BYTES)


def _full(shape):
    n = len(shape)
    return pl.BlockSpec(shape, lambda *_: (0,) * n)


def _ln(x, g, b):
    mu = jnp.mean(x, axis=-1, keepdims=True)
    xc = x - mu
    var = jnp.mean(xc * xc, axis=-1, keepdims=True)
    return xc * lax.rsqrt(var + LN_EPS) * g + b


def _mod(ref):
    return ref[0] if len(ref.shape) == 3 else ref[...]


def _dot(a, b, precise):
    if precise:
        return jnp.dot(a, b, precision=HIGHEST, preferred_element_type=F32)
    return jnp.dot(a.astype(BF16), b.astype(BF16), preferred_element_type=F32)


def _ada_kernel(c_ref, w_ref, b_ref, o_ref):
    c = c_ref[...]
    o_ref[...] = _dot(c * jax.nn.sigmoid(c), w_ref[...], True) + b_ref[...]


def _ada(c, w, b, *, tn=1024):
    n, d = c.shape
    dout = w.shape[1]
    return pl.pallas_call(
        _ada_kernel,
        out_shape=jax.ShapeDtypeStruct((n, dout), F32),
        grid=(dout // tn,),
        in_specs=[_full((n, d)), pl.BlockSpec((d, tn), lambda j: (0, j)), pl.BlockSpec((1, tn), lambda j: (0, j))],
        out_specs=pl.BlockSpec((n, tn), lambda j: (0, j)),
        compiler_params=_params("parallel"),
        name="ada",
    )(c, w, b)


_OFF_Q, _OFF_K, _OFF_V, _OFF_U, _OFF_GV = 0, 512, 1024, 1536, 2048
_OFF_GA, _OFF_GG, _OFF_F, _W_COLS = 2560, 3584, 4608, 4736


def _forget(zf, bf):
    return jax.nn.log_sigmoid(zf + bf)


def _gate_value(zgv, glg, glb):
    return _ln(jax.nn.gelu(zgv), glg, glb)


def _mixer_epilogue(zq, zk, zv, zf, zu, zgv, zga, zgg, bf, glg, glb):
    q = zq * (HEAD_DIM ** -0.5)
    return (q, zk, zv, _forget(zf, bf), jax.nn.gelu(zu), _gate_value(zgv, glg, glb), jax.nn.sigmoid(zga),
            jax.nn.sigmoid(zgg))


def _prompt_in_kernel(x_ref, sc_ref, sh_ref, wa_ref, wb_ref, wf_ref, bf_ref, glg_ref, glb_ref, ws_ref, bs_ref, tri_ref,
                      qt_ref, kt32_ref, kaug_ref, vt32_ref, vtaug_ref, logft_ref, cumt_ref, reft_ref, gm_ref,
                      sga_ref, sgg_ref, carry_ref, sp_ref, *, tiles_per_seq):
    tm = x_ref.shape[0]

    @pl.when(pl.program_id(0) % tiles_per_seq == 0)
    def _():
        carry_ref[...] = jnp.zeros_like(carry_ref)

    h = (x_ref[...] * (1.0 + sc_ref[0]) + sh_ref[0]).astype(BF16)

    def proj(off, width):
        ref, base = (wa_ref, 0) if off < _OFF_U else (wb_ref, _OFF_U) if off < _OFF_F else (wf_ref, _OFF_F)
        return jnp.dot(h, ref[:, off - base:off - base + width], preferred_element_type=F32)

    logf = _forget(proj(_OFF_F, LANES), bf_ref[...])
    gv = _gate_value(proj(_OFF_GV, 512), glg_ref[...], glb_ref[...])
    u = jax.nn.gelu(proj(_OFF_U, 512))
    q = proj(_OFF_Q, 512) * (HEAD_DIM ** -0.5 * LOG2E)
    k = proj(_OFF_K, 512)

    carry_in = carry_ref[...]
    carry = carry_in
    parts = []
    for n in range(tm // CHUNK):
        c = _dot(tri_ref[...], logf[n * CHUNK:(n + 1) * CHUNK], True) + carry
        parts.append(c)
        carry = c[CHUNK - 1:CHUNK]
    carry_ref[...] = carry
    cum = jnp.concatenate(parts, axis=0)

    gvb = gv.astype(BF16)
    gw = gv.shape[1] // GMLP_GROUPS
    for n in range(tm // CHUNK):
        for g in range(GMLP_GROUPS):
            sp_ref[n * CHUNK:(n + 1) * CHUNK, g * gw:(g + 1) * gw] = jnp.dot(
                ws_ref[g], gvb[n * CHUNK:(n + 1) * CHUNK, g * gw:(g + 1) * gw], preferred_element_type=F32)

    v = proj(_OFF_V, 512)
    sga_ref[...] = jax.nn.sigmoid(proj(_OFF_GA, 1024)).astype(BF16)
    sgg_ref[...] = jax.nn.sigmoid(proj(_OFF_GG, 1024)).astype(BF16)

    qt_ref[0] = q.T.astype(BF16)
    kt32_ref[0] = k.T
    vt = v.T
    vt32_ref[0] = vt
    logft_ref[0] = logf.T[:N_HEADS]
    cumt_ref[0] = (cum * LOG2E).T[:N_HEADS]
    reft_ref[0] = (jnp.broadcast_to(carry_in, cum.shape) * LOG2E).T[:N_HEADS]
    bs = jnp.concatenate([bs_ref[...]] * (tm // CHUNK), axis=0)
    gm_ref[...] = (u * (sp_ref[...] + bs)).astype(BF16)

    neg = (carry_in - cum) * LOG2E
    hi = neg.astype(BF16).astype(F32)
    mid = (neg - hi).astype(BF16).astype(F32)
    lo = (neg - hi - mid).astype(BF16).astype(F32)
    lane = lax.broadcasted_iota(jnp.int32, (tm, LANES), 1)
    for hd in range(N_HEADS):
        src = k[:, (hd // 2) * LANES:(hd // 2 + 1) * LANES]
        if hd % 2:
            src = pltpu.roll(src, HEAD_DIM, 1)
        blk = jnp.where(lane < HEAD_DIM, src, 0.0)
        for i, piece in enumerate((hi, mid, lo)):
            blk = jnp.where(lane == HEAD_DIM + i, piece[:, hd:hd + 1], blk)
        kaug_ref[:, hd * LANES:(hd + 1) * LANES] = blk.astype(BF16)

    ones_rows = jnp.where(lax.broadcasted_iota(jnp.int32, (HEAD_DIM, tm), 0) == 0, 1.0, 0.0)
    for hd in range(N_HEADS):
        grp = jnp.concatenate([vt[hd * HEAD_DIM:(hd + 1) * HEAD_DIM], ones_rows], axis=0)
        vtaug_ref[0, hd * LANES:(hd + 1) * LANES, :] = grp.astype(BF16)


def _prompt_in(x, sc, sh, wa, wb, wf, bf, glg, glb, ws, bs, tri, *, seq, tm):
    n, d = x.shape
    nb = n // seq
    tps = seq // tm
    row = lambda c: pl.BlockSpec((tm, c), lambda i: (i, 0))
    mod = pl.BlockSpec((1, 1, d), lambda i: (i // tps, 0, 0))
    tr = lambda r: pl.BlockSpec((1, r, tm), lambda i: (i // tps, 0, i % tps))
    aug = N_HEADS * LANES
    outs = [((nb, 512, seq), BF16), ((nb, 512, seq), F32), ((n, aug), BF16), ((nb, 512, seq), F32),
            ((nb, aug, seq), BF16), ((nb, N_HEADS, seq), F32), ((nb, N_HEADS, seq), F32),
            ((nb, N_HEADS, seq), F32), ((n, 512), BF16), ((n, 1024), BF16), ((n, 1024), BF16)]
    out_specs = [tr(512), tr(512), row(aug), tr(512), tr(aug), tr(N_HEADS), tr(N_HEADS), tr(N_HEADS),
                 row(512), row(1024), row(1024)]
    return pl.pallas_call(
        functools.partial(_prompt_in_kernel, tiles_per_seq=tps),
        out_shape=[jax.ShapeDtypeStruct(s, t) for s, t in outs],
        grid=(n // tm,),
        in_specs=[row(d), mod, mod, _full(wa.shape), _full(wb.shape), _full(wf.shape), _full(bf.shape),
                  _full(glg.shape), _full(glb.shape), _full(ws.shape), _full(bs.shape), _full(tri.shape)],
        out_specs=out_specs,
        scratch_shapes=[pltpu.VMEM((1, LANES), F32), pltpu.VMEM((tm, 512), F32)],
        compiler_params=_params("arbitrary"),
        name="prompt_in",
    )(x, sc, sh, wa, wb, wf, bf, glg, glb, ws, bs, tri)


ATTN_TILE = 512
QK_AHEAD = 2


def _fox_prompt_kernel(qt_ref, kaug_ref, vtaug_ref, cumt_ref, reft_ref, o_ref, m_ref, acc_ref):
    t = ATTN_TILE
    qi = pl.program_id(1)
    q0 = pl.multiple_of(qi * t, t)
    m_ref[...] = jnp.full_like(m_ref, NEG_INF)
    acc_ref[...] = jnp.zeros_like(acc_ref)
    ones3 = jnp.where(lax.broadcasted_iota(jnp.int32, (HEAD_DIM, t), 0) < 3, 1.0, 0.0).astype(BF16)
    qaug = [jnp.concatenate([qt_ref[h * HEAD_DIM:(h + 1) * HEAD_DIM, :], ones3], axis=0) for h in range(N_HEADS)]
    cb = [cumt_ref[h:h + 1, pl.ds(q0, t)] for h in range(N_HEADS)]

    def run(pieces):
        rounds = [(pc, h) for pc in pieces for h in range(N_HEADS)]

        def scores(pc, h):
            ks, kl, ql, _ = pc
            return jnp.dot(kaug_ref[pl.ds(ks, kl), h * LANES:(h + 1) * LANES], qaug[h][:, ql:],
                           preferred_element_type=F32)

        raw = [scores(*r) for r in rounds[:QK_AHEAD]]
        for n, ((ks, kl, ql, diag), h) in enumerate(rounds):
            if n + QK_AHEAD < len(rounds):
                raw.append(scores(*rounds[n + QK_AHEAD]))
            grp = slice(h * LANES, (h + 1) * LANES)
            st = raw[n]
            if diag:
                st = jnp.where(lax.broadcasted_iota(jnp.int32, st.shape, 0) <= lax.broadcasted_iota(jnp.int32, st.shape, 1),
                               st, NEG_INF)
            ref_j = jnp.concatenate([reft_ref[h:h + 1, pl.ds(ks, LANES)]] * ((t - ql) // LANES), axis=1)
            crow = cb[h][:, ql:] - ref_j
            m_old = m_ref[h, :, ql:]
            m_new = jnp.maximum(m_old, jnp.max(st, axis=0, keepdims=True) + crow)
            pt = jnp.exp2(st - (m_new - crow)).astype(BF16)
            acc_ref[h, :, ql:] = acc_ref[h, :, ql:] * jnp.exp2(m_old - m_new) + jnp.dot(
                vtaug_ref[grp, pl.ds(ks, kl)], pt, preferred_element_type=F32)
            m_ref[h, :, ql:] = m_new

    def full(j):
        return (pl.multiple_of(j * t, t), t, 0, False)

    def diagonal():
        half = t // 2
        return [(q0, half, 0, True), (pl.multiple_of(q0 + half, half), half, half, True)]

    lax.fori_loop(0, qi // 2, lambda jj, c: (run([full(2 * jj), full(2 * jj + 1)]), c)[1], 0)

    @pl.when(qi % 2 == 1)
    def _():
        run([full(qi - 1)] + diagonal())

    @pl.when(qi % 2 == 0)
    def _():
        run(diagonal())

    outs = []
    for h in range(N_HEADS):
        acc = acc_ref[h]
        outs.append(acc[:HEAD_DIM] / acc[HEAD_DIM:HEAD_DIM + 1])
    o_ref[...] = jnp.concatenate(outs, axis=0).T.astype(o_ref.dtype)


def _fox_prompt(qt, kaug, vtaug, cumt, reft, *, seq):
    nb = qt.shape[0]
    t = ATTN_TILE
    nq = seq // t
    aug = N_HEADS * LANES
    per_b = lambda r: pl.BlockSpec((None, r, seq), lambda b, i: (b, 0, 0))
    return pl.pallas_call(
        _fox_prompt_kernel,
        out_shape=jax.ShapeDtypeStruct((nb * seq, D_ATTN), BF16),
        grid=(nb, nq),
        in_specs=[pl.BlockSpec((None, D_ATTN, t), lambda b, i: (b, 0, i)),
                  pl.BlockSpec((seq, aug), lambda b, i: (b, 0)),
                  per_b(aug), per_b(N_HEADS), per_b(N_HEADS)],
        out_specs=pl.BlockSpec((t, D_ATTN), lambda b, i: (b * nq + i, 0)),
        scratch_shapes=[pltpu.VMEM((N_HEADS, 1, t), F32), pltpu.VMEM((N_HEADS, LANES, t), F32)],
        compiler_params=_params("parallel", "arbitrary"),
        name="fox_prompt",
    )(qt, kaug, vtaug, cumt, reft)


def _route(logits):
    lane = lax.broadcasted_iota(jnp.int32, logits.shape, 1)
    big = jnp.int32(LANES)
    is_g = (lane >= N_EXPERTS) & (lane < N_EXPERTS + N_GROUPS)
    gl = jnp.where(is_g, logits, NEG_INF)
    gmax = jnp.max(gl, axis=-1, keepdims=True)
    gi = jnp.min(jnp.where(gl == gmax, lane, big), axis=-1, keepdims=True) - N_EXPERTS
    pg_top = 1.0 / jnp.sum(jnp.exp(gl - gmax), axis=-1, keepdims=True)
    in_g = (lane >= gi * EXPERTS_PER_GROUP) & (lane < (gi + 1) * EXPERTS_PER_GROUP)
    el = jnp.where(in_g, logits, NEG_INF)
    m1 = jnp.max(el, axis=-1, keepdims=True)
    i1 = jnp.min(jnp.where(el == m1, lane, big), axis=-1, keepdims=True)
    el2 = jnp.where(lane == i1, NEG_INF, el)
    m2 = jnp.max(el2, axis=-1, keepdims=True)
    i2 = jnp.min(jnp.where(el2 == m2, lane, big), axis=-1, keepdims=True)
    e2 = jnp.exp(m2 - m1)
    w1 = pg_top / (1.0 + e2)
    return jnp.where(lane == i1, w1, jnp.where(lane == i2, w1 * e2, 0.0)), gi


ROUTE_CHUNK = 32
SORT_BLOCK = 512
SORTED_ROWS = 768
ROUTE_TILE = 512
EXPERT_SUB = 256


def _group_ranks(gi, ls_ref):
    tm = gi.shape[0]
    lane = lax.broadcasted_iota(jnp.int32, (tm, LANES), 1)
    onehot = jnp.where(lane == gi, 1.0, 0.0)
    return onehot, jnp.dot(ls_ref[...], onehot.astype(BF16), preferred_element_type=F32)


def _sort_block(h2b, comb, gi, onehot, before, xs_ref, cs_ref, pos_ref, cnt_ref):
    tm = h2b.shape[0]
    lane = lax.broadcasted_iota(jnp.int32, (tm, LANES), 1)
    tot = before[tm - 1:tm] + onehot[tm - 1:tm]
    cnt_ref[0] = tot
    padded = jnp.floor((tot + (ROUTE_CHUNK - 1)) * (1.0 / ROUTE_CHUNK)) * ROUTE_CHUNK
    lane1 = lax.broadcasted_iota(jnp.int32, (1, LANES), 1)
    start = jnp.zeros_like(padded)
    for s in range(1, N_GROUPS):
        start = start + jnp.where(lane1 >= s, pltpu.roll(padded, s, 1), 0.0)
    pos = jnp.sum(onehot * (start + before), axis=-1, keepdims=True)
    pos_ref[...] = pos
    pos_row = jnp.broadcast_to(pos, (tm, LANES)).T[0:1]
    rows = lax.broadcasted_iota(jnp.int32, (SORTED_ROWS, tm), 0).astype(F32)
    perm = jnp.where(rows == pos_row, 1.0, 0.0).astype(BF16)
    xs_ref[...] = jnp.dot(perm, h2b, preferred_element_type=F32).astype(BF16)
    rel = jnp.zeros_like(comb)
    for g in range(N_GROUPS):
        moved = comb if g == 0 else pltpu.roll(comb, LANES - g * EXPERTS_PER_GROUP, 1)
        rel = rel + jnp.where(gi == g, moved, 0.0)
    rel = jnp.where(lane < EXPERTS_PER_GROUP, rel, 0.0)
    hi = rel.astype(BF16)
    lo = (rel - hi.astype(F32)).astype(BF16)
    cs_ref[...] = (jnp.dot(perm, hi, preferred_element_type=F32) + jnp.dot(perm, lo, preferred_element_type=F32))


def _merge_kernel(a_ref, gm_ref, sga_ref, sgg_ref, x_ref, g1_ref, sc2_ref, sh2_ref, wao_ref, wgo_ref, wo_ref,
                  l1g_ref, l1b_ref, wr_ref, br_ref, *rest, alpha, precise, sort):
    if sort:
        ls_ref, x1_ref, xs_ref, cs_ref, pos_ref, cnt_ref, h2_s, comb_s, gi_s = rest

        @pl.when(pl.program_id(0) == 0)
        def _():
            h2_s[...] = jnp.zeros_like(h2_s)
            comb_s[...] = jnp.zeros_like(comb_s)
            gi_s[...] = jnp.zeros_like(gi_s)

        h2_prev, comb_prev, gi_prev = h2_s[...], comb_s[...], gi_s[...]
        onehot, before = _group_ranks(gi_prev, ls_ref)
    else:
        x1_ref, h2_ref, comb_ref = rest
    da = _dot(a_ref[...], wao_ref[...], precise)
    dg = _dot(gm_ref[...], wgo_ref[...], precise)
    t = sga_ref[...].astype(F32) * da + sgg_ref[...].astype(F32) * dg
    m = _dot(t, wo_ref[...], precise)
    if sort:
        _sort_block(h2_prev, comb_prev, gi_prev, onehot, before, xs_ref, cs_ref, pos_ref, cnt_ref)
    x1 = _ln(alpha * x_ref[...] + (1.0 + _mod(g1_ref)) * m, l1g_ref[...], l1b_ref[...])
    h2 = x1 * (1.0 + _mod(sc2_ref)) + _mod(sh2_ref)
    comb, gi = _route(_dot(h2, wr_ref[...], precise) + br_ref[...])
    if sort:
        h2_s[...] = h2.astype(BF16)
        comb_s[...] = comb
        gi_s[...] = gi
    else:
        h2_ref[...] = h2.astype(h2_ref.dtype)
        comb_ref[...] = comb
    x1_ref[...] = x1


def _merge(a, gm, sga, sgg, x, g1, sc2, sh2, wao, wgo, wo, l1g, l1b, wr, br, *, rows_per_mod, tm, alpha, precise,
           sort):
    n, d = x.shape
    nblk = n // tm
    cur = (lambda i: jnp.minimum(i, nblk - 1)) if sort else (lambda i: i)
    row = lambda c: pl.BlockSpec((tm, c), lambda i: (cur(i), 0))
    if g1.ndim == 3:
        mod = pl.BlockSpec((1, 1, d), lambda i: (cur(i) * tm // rows_per_mod, 0, 0))
    else:
        mod = row(d)
    ins = [a, gm, sga, sgg, x, g1, sc2, sh2, wao, wgo, wo, l1g, l1b, wr, br]
    in_specs = [row(D_ATTN), row(gm.shape[1]), row(d), row(d), row(d), mod, mod, mod, _full(wao.shape),
                _full(wgo.shape), _full(wo.shape), _full(l1g.shape), _full(l1b.shape), _full(wr.shape),
                _full(br.shape)]
    scratch = []
    if sort:
        assert tm == SORT_BLOCK
        ls = jnp.tril(jnp.ones((tm, tm), BF16), -1)
        ins.append(ls)
        in_specs.append(_full(ls.shape))
        prev = lambda i: jnp.maximum(i - 1, 0)
        srow = lambda c: pl.BlockSpec((SORTED_ROWS, c), lambda i: (prev(i), 0))
        out_shape = [((n, d), F32), ((nblk * SORTED_ROWS, d), BF16), ((nblk * SORTED_ROWS, LANES), F32),
                     ((n, 1), F32), ((nblk, 1, LANES), F32)]
        out_specs = [row(d), srow(d), srow(LANES), pl.BlockSpec((tm, 1), lambda i: (prev(i), 0)),
                     pl.BlockSpec((1, 1, LANES), lambda i: (prev(i), 0, 0))]
        scratch = [pltpu.VMEM((tm, d), BF16), pltpu.VMEM((tm, LANES), F32), pltpu.VMEM((tm, 1), jnp.int32)]
    else:
        out_shape = [((n, d), F32), ((n, d), BF16), ((n, LANES), F32)]
        out_specs = [row(d), row(d), row(LANES)]
    return pl.pallas_call(
        functools.partial(_merge_kernel, alpha=alpha, precise=precise, sort=sort),
        out_shape=[jax.ShapeDtypeStruct(s, t) for s, t in out_shape],
        grid=(nblk + 1 if sort else nblk,),
        in_specs=in_specs,
        out_specs=out_specs,
        scratch_shapes=scratch,
        compiler_params=_params("arbitrary"),
        name="merge_precise" if precise else "merge",
    )(*ins)


def _route_tables(counts, nblk):
    ch, cpb, cpt = ROUTE_CHUNK, SORTED_ROWS // ROUTE_CHUNK, ROUTE_TILE // ROUTE_CHUNK
    max_tiles = (nblk * (SORT_BLOCK + N_GROUPS * (ch - 1))) // ROUTE_TILE + N_GROUPS + 1
    cnt = counts[:, 0, :N_GROUPS].astype(jnp.int32)
    nch = (cnt + ch - 1) // ch
    loc = jnp.cumsum(nch, axis=1) - nch
    earlier = jnp.cumsum(nch, axis=0) - nch
    tiles_g = (nch.sum(0) + cpt - 1) // cpt
    tile_off = jnp.cumsum(tiles_g) - tiles_g
    n_tiles = tiles_g.sum()
    k = jnp.arange(cpb, dtype=jnp.int32)[None, None, :]
    used = (k >= loc[:, :, None]) & (k < (loc + nch)[:, :, None])
    glob = (tile_off * cpt)[None, :, None] + earlier[:, :, None] + k - loc[:, :, None]
    dst = jnp.sum(jnp.where(used, glob, 0), axis=1)
    blk_chunk = jnp.arange(nblk, dtype=jnp.int32)[:, None] * cpb + k[0]
    scatter_to = jnp.where(used.any(axis=1), dst, max_tiles * cpt)
    src = jnp.full((max_tiles * cpt,), cpb - 1, jnp.int32).at[scatter_to.reshape(-1)].set(
        blk_chunk.reshape(-1), mode="drop")
    t = jnp.arange(max_tiles, dtype=jnp.int32)
    tile_group = jnp.minimum(jnp.sum(t[:, None] >= (tile_off + tiles_g)[None, :], axis=1), N_GROUPS - 1)
    return src, dst.reshape(-1), tile_group.astype(jnp.int32), n_tiles.reshape(1).astype(jnp.int32), max_tiles


def _experts_kernel(src_ref, grp_ref, nt_ref, *refs):
    del src_ref
    cpt = ROUTE_TILE // ROUTE_CHUNK
    xr, cr = refs[:cpt], refs[cpt:2 * cpt]
    wg_ref, wu_ref, wd_ref, o_ref, wg_s, wu_s, wd_s = refs[2 * cpt:]
    t = pl.program_id(0)
    live = t < nt_ref[0]
    de = wg_ref.shape[3]

    @pl.when((t == 0) | (grp_ref[t] != grp_ref[jnp.maximum(t - 1, 0)]))
    def _():
        for e in range(EXPERTS_PER_GROUP):
            wg_s[:, e * de:(e + 1) * de] = wg_ref[0, e].astype(BF16)
            wu_s[:, e * de:(e + 1) * de] = wu_ref[0, e].astype(BF16)
            wd_s[e * de:(e + 1) * de, :] = wd_ref[0, e].astype(BF16)

    @pl.when(live)
    def _():
        per = EXPERT_SUB // ROUTE_CHUNK
        halves = range(ROUTE_TILE // EXPERT_SUB)
        xs = [jnp.concatenate([r[...] for r in xr[i * per:(i + 1) * per]], axis=0) for i in halves]
        gate_up = [(jnp.dot(x, wg_s[...], preferred_element_type=F32), jnp.dot(x, wu_s[...], preferred_element_type=F32))
                   for x in xs]
        for i, (hg, hu) in enumerate(gate_up):
            c = jnp.concatenate([r[...] for r in cr[i * per:(i + 1) * per]], axis=0)
            a = jnp.concatenate([(hg[:, e * de:(e + 1) * de] * jax.nn.sigmoid(hg[:, e * de:(e + 1) * de])
                                  * hu[:, e * de:(e + 1) * de] * c[:, e:e + 1]).astype(BF16)
                                 for e in range(EXPERTS_PER_GROUP)], axis=1)
            o_ref[i * EXPERT_SUB:(i + 1) * EXPERT_SUB, :] = jnp.dot(
                a, wd_s[...], preferred_element_type=F32).astype(o_ref.dtype)

    @pl.when(jnp.logical_not(live))
    def _():
        o_ref[...] = jnp.zeros_like(o_ref)


def _experts(src, tile_group, n_tiles, max_tiles, xs, cs, wg, wu, wd):
    d = xs.shape[1]
    de = wg.shape[2]
    cpt = ROUTE_TILE // ROUTE_CHUNK
    grouped = lambda w: w.reshape((N_GROUPS, EXPERTS_PER_GROUP) + w.shape[1:])

    def chunk(width):
        def one(r):
            return pl.BlockSpec((ROUTE_CHUNK, width), lambda t, src, grp, nt: (src[t * cpt + r], 0))
        return [one(r) for r in range(cpt)]

    wspec = lambda w: pl.BlockSpec((1, EXPERTS_PER_GROUP) + w.shape[1:], lambda t, src, grp, nt: (grp[t], 0, 0, 0))
    wide = EXPERTS_PER_GROUP * de
    grid_spec = pltpu.PrefetchScalarGridSpec(
        num_scalar_prefetch=3,
        grid=(max_tiles,),
        in_specs=chunk(d) + chunk(LANES) + [wspec(wg), wspec(wu), wspec(wd)],
        out_specs=pl.BlockSpec((ROUTE_TILE, d), lambda t, src, grp, nt: (t, 0)),
        scratch_shapes=[pltpu.VMEM((d, wide), BF16), pltpu.VMEM((d, wide), BF16), pltpu.VMEM((wide, d), BF16)],
    )
    return pl.pallas_call(
        _experts_kernel,
        out_shape=jax.ShapeDtypeStruct((max_tiles * ROUTE_TILE, d), BF16),
        grid_spec=grid_spec,
        compiler_params=_params("arbitrary"),
        name="experts",
    )(src, tile_group, n_tiles, *([xs] * cpt), *([cs] * cpt), grouped(wg), grouped(wu), grouped(wd))


def _unsort_kernel(dst_ref, *refs, alpha):
    del dst_ref
    cpb = SORTED_ROWS // ROUTE_CHUNK
    yr = refs[:cpb]
    pos_ref, x1_ref, g2_ref, l2g_ref, l2b_ref, o_ref = refs[cpb:]
    ys = jnp.concatenate([r[...] for r in yr], axis=0)
    tm = pos_ref.shape[0]
    cols = lax.broadcasted_iota(jnp.int32, (tm, SORTED_ROWS), 1).astype(F32)
    perm_t = jnp.where(cols == pos_ref[...], 1.0, 0.0).astype(BF16)
    f = jnp.dot(perm_t, ys, preferred_element_type=F32)
    o_ref[...] = _ln(alpha * x1_ref[...] + (1.0 + _mod(g2_ref)) * f, l2g_ref[...], l2b_ref[...])


def _unsort(dst, y, pos, x1, g2, l2g, l2b, *, rows_per_mod, alpha):
    n, d = x1.shape
    tm = SORT_BLOCK
    cpb = SORTED_ROWS // ROUTE_CHUNK

    def chunk(r):
        return pl.BlockSpec((ROUTE_CHUNK, d), lambda i, dst: (dst[i * cpb + r], 0))

    row = lambda c: pl.BlockSpec((tm, c), lambda i, dst: (i, 0))
    full = lambda s: pl.BlockSpec(s, lambda i, dst: (0,) * len(s))
    grid_spec = pltpu.PrefetchScalarGridSpec(
        num_scalar_prefetch=1,
        grid=(n // tm,),
        in_specs=[chunk(r) for r in range(cpb)] + [
            row(1), row(d), pl.BlockSpec((1, 1, d), lambda i, dst: (i * tm // rows_per_mod, 0, 0)),
            full(l2g.shape), full(l2b.shape)],
        out_specs=row(d),
    )
    return pl.pallas_call(
        functools.partial(_unsort_kernel, alpha=alpha),
        out_shape=jax.ShapeDtypeStruct((n, d), F32),
        grid_spec=grid_spec,
        compiler_params=_params("parallel"),
        name="unsort",
    )(dst, *([y] * cpb), pos, x1, g2, l2g, l2b)


def _moe_kernel(h2_ref, comb_ref, x1_ref, g2_ref, wg_ref, wu_ref, wd_ref, l2g_ref, l2b_ref, o_ref, *, alpha):
    e = pl.program_id(0)

    @pl.when(e == 0)
    def _():
        o_ref[...] = jnp.zeros_like(o_ref)

    h2 = h2_ref[...]
    comb = comb_ref[...]
    lane = lax.broadcasted_iota(jnp.int32, comb.shape, 1)
    w = jnp.sum(jnp.where(lane == e, comb, 0.0), axis=-1, keepdims=True)
    hg = jnp.dot(h2, wg_ref[0].astype(BF16), preferred_element_type=F32)
    hu = jnp.dot(h2, wu_ref[0].astype(BF16), preferred_element_type=F32)
    a = hg * jax.nn.sigmoid(hg) * hu * w
    o_ref[...] += jnp.dot(a.astype(BF16), wd_ref[0].astype(BF16), preferred_element_type=F32)

    @pl.when(e == pl.num_programs(0) - 1)
    def _():
        o_ref[...] = _ln(alpha * x1_ref[...] + (1.0 + g2_ref[...]) * o_ref[...], l2g_ref[...], l2b_ref[...])


def _moe(h2, comb, x1, g2, wg, wu, wd, l2g, l2b, *, alpha):
    n, d = x1.shape
    ne, _, de = wg.shape
    return pl.pallas_call(
        functools.partial(_moe_kernel, alpha=alpha),
        out_shape=jax.ShapeDtypeStruct((n, d), F32),
        grid=(ne,),
        in_specs=[_full((n, d)), _full((n, LANES)), _full((n, d)), _full((n, d)),
                  pl.BlockSpec((1, d, de), lambda e: (e, 0, 0)), pl.BlockSpec((1, d, de), lambda e: (e, 0, 0)),
                  pl.BlockSpec((1, de, d), lambda e: (e, 0, 0)), _full(l2g.shape), _full(l2b.shape)],
        out_specs=_full((n, d)),
        compiler_params=_params("arbitrary"),
        name="moe",
    )(h2, comb, x1, g2, wg, wu, wd, l2g, l2b)


def _sample_proj_kernel(x_ref, sc_ref, sh_ref, wt_ref, z_ref):
    h = x_ref[...] * (1.0 + sc_ref[...]) + sh_ref[...]
    z_ref[...] = lax.dot_general(h, wt_ref[...], (((1,), (1,)), ((), ())), precision=HIGHEST,
                                 preferred_element_type=F32)


def _sample_proj(x, sc, sh, wt):
    n, d = x.shape
    return pl.pallas_call(
        _sample_proj_kernel,
        out_shape=jax.ShapeDtypeStruct((n, wt.shape[0]), F32),
        compiler_params=pltpu.CompilerParams(vmem_limit_bytes=VMEM_LIMIT_BYTES),
        name="sample_proj",
    )(x, sc, sh, wt)


def _sample_mix_kernel(z_ref, bf_ref, glg_ref, glb_ref, mg_ref, bs_ref, mc_ref,
                       q_ref, k_ref, v_ref, logf_ref, cn_ref, gv_ref, gm_ref, sga_ref, sgg_ref):
    z = z_ref[...]
    sec = lambda off, width: z[:, off:off + width]
    q, k, v, logf, u, gv, sga, sgg = _mixer_epilogue(
        sec(_OFF_Q, 512), sec(_OFF_K, 512), sec(_OFF_V, 512), sec(_OFF_F, LANES), sec(_OFF_U, 512),
        sec(_OFF_GV, 512), sec(_OFF_GA, 1024), sec(_OFF_GG, 1024), bf_ref[...], glg_ref[...], glb_ref[...])
    q_ref[...] = q
    k_ref[...] = k
    v_ref[...] = v
    logf_ref[...] = logf
    cn_ref[...] = _dot(mc_ref[...], logf, True)
    gv_ref[...] = gv
    sga_ref[...] = sga
    sgg_ref[...] = sgg
    gw = gv.shape[1] // GMLP_GROUPS
    sp = jnp.concatenate([_dot(mg_ref[g], gv[:, g * gw:(g + 1) * gw], True) for g in range(GMLP_GROUPS)], axis=-1)
    gm_ref[...] = u * (sp + bs_ref[...])


def _sample_mix(z, bf, glg, glb, mg, bs, mc):
    n = z.shape[0]
    shapes = [(n, 512)] * 3 + [(n, LANES)] * 2 + [(n, 512)] * 2 + [(n, 1024)] * 2
    return pl.pallas_call(
        _sample_mix_kernel,
        out_shape=[jax.ShapeDtypeStruct(s, F32) for s in shapes],
        compiler_params=pltpu.CompilerParams(vmem_limit_bytes=VMEM_LIMIT_BYTES),
        name="sample_mix",
    )(z, bf, glg, glb, mg, bs, mc)


PAGES_PER_STEP = 16


def _fox_sample_kernel(pt_ref, kt_hbm, vt_hbm, lf_hbm, qbd_ref, cn_ref, kn_ref, vn_ref, cnb_ref, o_ref,
                       kbuf, vbuf, lbuf, sem, m_ref, l_ref, acc_ref, later_ref, *, n_chunks, n_new):
    P = PAGES_PER_STEP
    b, c = pl.program_id(0), pl.program_id(1)
    step = b * n_chunks + c
    n_steps = pl.num_programs(0) * n_chunks
    slot = lax.rem(step, 2)

    def page_copies(seq, chunk, sl, lookup):
        out = []
        for r in range(P):
            page = pt_ref[seq, (n_chunks - 1 - chunk) * P + r] if lookup else 0
            out += [pltpu.make_async_copy(kt_hbm.at[page], kbuf.at[sl, r], sem.at[sl]),
                    pltpu.make_async_copy(vt_hbm.at[page], vbuf.at[sl, r], sem.at[sl]),
                    pltpu.make_async_copy(lf_hbm.at[page], lbuf.at[sl, r], sem.at[sl])]
        return out

    def start_all(cps):
        for i, cp in enumerate(cps):
            cp.start(priority=1 if i % 3 == 1 else 0)

    @pl.when(step == 0)
    def _():
        start_all(page_copies(b, c, slot, True))

    @pl.when(step + 1 < n_steps)
    def _():
        nxt = step + 1
        start_all(page_copies(nxt // n_chunks, lax.rem(nxt, n_chunks), 1 - slot, True))

    @pl.when(c == 0)
    def _():
        m_ref[...] = jnp.full_like(m_ref, NEG_INF)
        l_ref[...] = jnp.zeros_like(l_ref)
        acc_ref[...] = jnp.zeros_like(acc_ref)
        later_ref[...] = jnp.zeros_like(later_ref)

    qbd = qbd_ref[0]
    cn = cn_ref[0]
    nt = (((1,), (1,)), ((), ()))

    def update(s, vs):
        m_old = m_ref[...]
        m_new = m_old
        for sb in s:
            m_new = jnp.maximum(m_new, jnp.max(sb, axis=-1, keepdims=True))
        alpha = jnp.exp(m_old - m_new)
        l_new = l_ref[...] * alpha
        acc = acc_ref[...] * alpha
        for sb, vb in zip(s, vs):
            p = jnp.exp(sb - m_new)
            l_new = l_new + jnp.sum(p, axis=-1, keepdims=True)
            acc = acc + lax.dot_general(p, vb, nt, preferred_element_type=F32)
        m_ref[...] = m_new
        l_ref[...] = l_new
        acc_ref[...] = acc

    for cp in page_copies(b, c, slot, False):
        cp.wait()

    lane = lax.broadcasted_iota(jnp.int32, (N_HEADS, PAGE_SIZE), 1)
    later = later_ref[...]
    bias = [None] * P
    for r in reversed(range(P)):
        y = lbuf[slot, r]
        for k in range(7):
            sh = 1 << k
            y = y + jnp.where(lane < PAGE_SIZE - sh, pltpu.roll(y, PAGE_SIZE - sh, 1), 0.0)
        bias[r] = jnp.where(lane < PAGE_SIZE - 1, pltpu.roll(y, PAGE_SIZE - 1, 1), 0.0) + later
        later = later + y[:, 0:1]
    later_ref[...] = later

    scores = []
    for r in range(P):
        st = jnp.dot(qbd, kbuf[slot, r].reshape(D_ATTN, PAGE_SIZE), preferred_element_type=F32)
        scores.append(st + jnp.concatenate([bias[r]] * n_new, axis=0) + cn)
    update(scores, [vbuf[slot, r].reshape(D_ATTN, PAGE_SIZE) for r in range(P)])

    @pl.when(c == n_chunks - 1)
    def _():
        st = jnp.dot(qbd, kn_ref[0], preferred_element_type=F32) + cn - cnb_ref[0]
        col = lax.broadcasted_iota(jnp.int32, st.shape, 1)
        row = lax.broadcasted_iota(jnp.int32, st.shape, 0)
        update([jnp.where(col <= row // N_HEADS, st, NEG_INF)], [vn_ref[0]])
        full = acc_ref[...] / l_ref[...]
        lane_head = lax.broadcasted_iota(jnp.int32, full.shape, 1) // HEAD_DIM
        row_head = lax.broadcasted_iota(jnp.int32, full.shape, 0) % N_HEADS
        own = jnp.where(lane_head == row_head, full, 0.0)
        rows = [jnp.sum(own[q * N_HEADS:(q + 1) * N_HEADS], axis=0, keepdims=True) for q in range(n_new)]
        o_ref[0] = jnp.concatenate(rows + [jnp.zeros((o_ref.shape[1] - n_new, D_ATTN), F32)], axis=0)


def _fox_sample(page_table, kt_pool, vt_pool, logf_pool, qbd, cn, kn, vn, cnb, *, n_new):
    nb, n_pages = page_table.shape
    P = PAGES_PER_STEP
    n_chunks = n_pages // P
    nrow = n_new * N_HEADS
    per_b = lambda s: pl.BlockSpec((1,) + s, lambda b, c, pt: (b,) + (0,) * len(s))
    hbm = pl.BlockSpec(memory_space=pl.ANY)
    kv_page = (N_HEADS, HEAD_DIM, PAGE_SIZE)
    grid_spec = pltpu.PrefetchScalarGridSpec(
        num_scalar_prefetch=1,
        grid=(nb, n_chunks),
        in_specs=[hbm, hbm, hbm, per_b((nrow, D_ATTN)), per_b((nrow, 1)), per_b((D_ATTN, LANES)),
                  per_b((D_ATTN, LANES)), per_b((nrow, LANES))],
        out_specs=per_b((8, D_ATTN)),
        scratch_shapes=[pltpu.VMEM((2, P) + kv_page, F32), pltpu.VMEM((2, P) + kv_page, F32),
                        pltpu.VMEM((2, P, N_HEADS, PAGE_SIZE), F32), pltpu.SemaphoreType.DMA((2,)),
                        pltpu.VMEM((nrow, 1), F32), pltpu.VMEM((nrow, 1), F32), pltpu.VMEM((nrow, D_ATTN), F32),
                        pltpu.VMEM((N_HEADS, PAGE_SIZE), F32)],
    )
    return pl.pallas_call(
        functools.partial(_fox_sample_kernel, n_chunks=n_chunks, n_new=n_new),
        out_shape=jax.ShapeDtypeStruct((nb, 8, D_ATTN), F32),
        grid_spec=grid_spec,
        compiler_params=_params("arbitrary", "arbitrary"),
        name="fox_sample",
    )(page_table, kt_pool, vt_pool, logf_pool, qbd, cn, kn, vn, cnb)


def _split_cols(z):
    f0 = 3 * D_ATTN
    return z[:, :f0], z[:, f0 + N_HEADS:], jnp.pad(z[:, f0:f0 + N_HEADS], ((0, 0), (0, LANES - N_HEADS)))


def kernel(x_prompt, x_sample, c_prompt, c_sample, cache_k, cache_v, cache_logf, page_table, w_ada, b_ada, w_in,
           b_f, gmlp_ln_g, gmlp_ln_b, w_s, b_s, w_attn_out, w_gmlp_out, w_o, ln1_g, ln1_b, w_group_router,
           b_group_router, w_expert_router, b_expert_router, w_exp_gate, w_exp_up, w_exp_down, ln2_g, ln2_b):
    depth = w_ada.shape[0]
    assert depth == 1
    nbp, seq, d = x_prompt.shape
    nbs, n_new, _ = x_sample.shape
    alpha = (2.0 * depth) ** 0.25
    n_pool = cache_k.shape[1]
    d_gmlp = gmlp_ln_g.shape[1]
    gw = d_gmlp // GMLP_GROUPS

    wa_b, wb_b, wf_b = _split_cols(w_in[0].astype(BF16))
    bf =jnp.pad(b_f[0], (0, LANES - N_HEADS))[None]
    glg, glb = gmlp_ln_g[0][None], gmlp_ln_b[0][None]
    tril = jnp.tril(jnp.ones((CHUNK, CHUNK), F32))
    ws = jnp.where(tril > 0, w_s[0], 0.0)
    bs_tile = jnp.repeat(b_s[0].T, gw, axis=1)
    w_r = jnp.pad(jnp.concatenate([w_expert_router[0], w_group_router[0]], axis=1),
                  ((0, 0), (0, LANES - N_EXPERTS - N_GROUPS)))
    b_r = jnp.pad(jnp.concatenate([b_expert_router[0], b_group_router[0]]), (0, LANES - N_EXPERTS - N_GROUPS))[None]
    l1g, l1b, l2g, l2b = ln1_g[0][None], ln1_b[0][None], ln2_g[0][None], ln2_b[0][None]

    nc = nbp + nbs
    c_all = jnp.pad(jnp.concatenate([c_prompt, c_sample], axis=0), ((0, -nc % 8), (0, 0)))
    mod = _ada(c_all, w_ada[0], b_ada[0][None])
    modp = [m[:, None, :] for m in jnp.split(mod[:nbp], 6, axis=-1)]
    mods = [m[:, None, :] for m in jnp.split(mod[nbp:nc], 6, axis=-1)]

    xp = x_prompt.reshape(nbp * seq, d)
    qt, kt32, kaug, vt32, vtaug, logft, cumt, reft, gm, sga, sgg = _prompt_in(
        xp, modp[1], modp[0], wa_b, wb_b, wf_b, bf, glg, glb, ws.astype(BF16), bs_tile, tril, seq=seq, tm=ATTN_TILE)
    a = _fox_prompt(qt, kaug, vtaug, cumt, reft, seq=seq)
    x1, xs_sorted, cs_sorted, pos, counts = _merge(
        a, gm, sga, sgg, xp, modp[2], modp[4], modp[3], w_attn_out[0].astype(BF16), w_gmlp_out[0].astype(BF16),
        w_o[0].astype(BF16), l1g, l1b, w_r.astype(BF16), b_r, rows_per_mod=seq, tm=SORT_BLOCK, alpha=alpha,
        precise=False, sort=True)
    src, dst, tile_group, n_tiles, max_tiles = _route_tables(counts, nbp * seq // SORT_BLOCK)
    y_sorted = _experts(src, tile_group, n_tiles, max_tiles, xs_sorted, cs_sorted, w_exp_gate[0], w_exp_up[0],
                        w_exp_down[0])
    yp = _unsort(dst, y_sorted, pos, x1, modp[5], l2g, l2b, rows_per_mod=seq, alpha=alpha)

    ns = nbs * n_new
    xs = x_sample.reshape(ns, d)
    rep = lambda m: jnp.repeat(m[:, 0, :], n_new, axis=0)
    z = jnp.concatenate(_split_cols(_sample_proj(xs, rep(mods[1]), rep(mods[0]), w_in[0].T)), axis=1)
    eye_b = jnp.eye(nbs, dtype=F32)
    mg = jnp.stack([jnp.kron(eye_b, ws[g, :n_new, :n_new]) for g in range(GMLP_GROUPS)])
    bs_rows = jnp.tile(bs_tile[:n_new], (nbs, 1))
    mc = jnp.kron(eye_b, tril[:n_new, :n_new])
    qs, ks, vs, logf_s, cn, gv_s, gm_s, sga_s, sgg_s = _sample_mix(z, bf, glg, glb, mg, bs_rows, mc)

    kt_pool = cache_k[0].transpose(0, 2, 3, 1)
    vt_pool = cache_v[0].transpose(0, 2, 3, 1)
    logf_pool = cache_logf[0].transpose(0, 2, 1)
    nrow = n_new * N_HEADS
    q4 = qs.reshape(nbs, n_new, N_HEADS, HEAD_DIM)
    qbd = jnp.einsum("bqhd,hg->bqhgd", q4, jnp.eye(N_HEADS, dtype=F32)).reshape(nbs, nrow, D_ATTN)
    cn3 = cn[:, :N_HEADS].reshape(nbs, n_new, N_HEADS)
    cn_col = cn3.reshape(nbs, nrow, 1)
    cnb = jnp.pad(jnp.tile(cn3.transpose(0, 2, 1), (1, n_new, 1)), ((0, 0), (0, 0), (0, LANES - n_new)))
    new_t = lambda t: jnp.pad(t.reshape(nbs, n_new, D_ATTN).transpose(0, 2, 1), ((0, 0), (0, 0), (0, LANES - n_new)))
    a_s = _fox_sample(page_table, kt_pool, vt_pool, logf_pool, qbd, cn_col, new_t(ks), new_t(vs), cnb, n_new=n_new)
    a_s = a_s[:, :n_new].reshape(ns, D_ATTN)
    x1s, h2s, comb_s = _merge(a_s, gm_s, sga_s, sgg_s, xs, rep(mods[2]), rep(mods[4]), rep(mods[3]),
                              w_attn_out[0], w_gmlp_out[0], w_o[0], l1g, l1b, w_r, b_r,
                              rows_per_mod=ns, tm=ns, alpha=alpha, precise=True, sort=False)
    ys = _moe(h2s, comb_s, x1s, rep(mods[5]), w_exp_gate[0], w_exp_up[0], w_exp_down[0], l2g, l2b, alpha=alpha)

    hs = (N_HEADS, HEAD_DIM)
    untr = lambda t: t.reshape(1, nbp, *hs, seq).transpose(0, 1, 4, 2, 3)
    return (yp.reshape(nbp, seq, d), ys.reshape(nbs, n_new, d), untr(kt32), untr(vt32),
            logft.reshape(1, nbp, N_HEADS, seq).transpose(0, 1, 3, 2),
            ks.reshape(1, nbs, n_new, *hs), vs.reshape(1, nbs, n_new, *hs),
            logf_s[:, :N_HEADS].reshape(1, nbs, n_new, N_HEADS), gv_s.reshape(1, nbs, n_new, d_gmlp))
```

```python
import functools

import numpy as np
import jax
import jax.numpy as jnp
from jax import lax
from jax.experimental import pallas as pl
from jax.experimental.pallas import tpu as pltpu

F32 = jnp.float32
BF16 = jnp.bfloat16
HIGHEST = lax.Precision.HIGHEST

N_HEADS = 8
HEAD_DIM = 64
D_ATTN = N_HEADS * HEAD_DIM
PAGE_SIZE = 128
CHUNK = 128
GMLP_GROUPS = 4
N_GROUPS = 4
EXPERTS_PER_GROUP = 4
N_EXPERTS = N_GROUPS * EXPERTS_PER_GROUP
LN_EPS = 1e-5
LANES = 128
NEG_INF = float("-inf")
LOG2E = 1.4426950408889634

VMEM_LIMIT_BYTES = 56 * 1024 * 1024


def _params(*sem):
    return pltpu.CompilerParams(dimension_semantics=sem, vmem_limit_bytes=VMEM_LIMIT_BYTES)


def _full(shape):
    n = len(shape)
    return pl.BlockSpec(shape, lambda *_: (0,) * n)


def _ln(x, g, b):
    mu = jnp.mean(x, axis=-1, keepdims=True)
    xc = x - mu
    var = jnp.mean(xc * xc, axis=-1, keepdims=True)
    return xc * lax.rsqrt(var + LN_EPS) * g + b


def _mod(ref):
    return ref[0] if len(ref.shape) == 3 else ref[...]


def _dot(a, b, precise):
    if precise:
        return jnp.dot(a, b, precision=HIGHEST, preferred_element_type=F32)
    return jnp.dot(a.astype(BF16), b.astype(BF16), preferred_element_type=F32)


def _ada_kernel(c_ref, w_ref, b_ref, o_ref):
    c = c_ref[...]
    o_ref[...] = _dot(c * jax.nn.sigmoid(c), w_ref[...], True) + b_ref[...]


def _ada(c, w, b, *, tn=1024):
    n, d = c.shape
    dout = w.shape[1]
    return pl.pallas_call(
        _ada_kernel,
        out_shape=jax.ShapeDtypeStruct((n, dout), F32),
        grid=(dout // tn,),
        in_specs=[_full((n, d)), pl.BlockSpec((d, tn), lambda j: (0, j)), pl.BlockSpec((1, tn), lambda j: (0, j))],
        out_specs=pl.BlockSpec((n, tn), lambda j: (0, j)),
        compiler_params=_params("parallel"),
        name="ada",
    )(c, w, b)


_OFF_Q, _OFF_K, _OFF_V, _OFF_U, _OFF_GV = 0, 512, 1024, 1536, 2048
_OFF_GA, _OFF_GG, _OFF_F, _W_COLS = 2560, 3584, 4608, 4736


def _forget(zf, bf):
    return jax.nn.log_sigmoid(zf + bf)


def _gate_value(zgv, glg, glb):
    return _ln(jax.nn.gelu(zgv), glg, glb)


def _mixer_epilogue(zq, zk, zv, zf, zu, zgv, zga, zgg, bf, glg, glb):
    q = zq * (HEAD_DIM ** -0.5)
    return (q, zk, zv, _forget(zf, bf), jax.nn.gelu(zu), _gate_value(zgv, glg, glb), jax.nn.sigmoid(zga),
            jax.nn.sigmoid(zgg))


def _prompt_in_kernel(x_ref, sc_ref, sh_ref, wa_ref, wb_ref, wf_ref, bf_ref, glg_ref, glb_ref, ws_ref, bs_ref, tri_ref,
                      qt_ref, kt32_ref, kaug_ref, vt32_ref, vtaug_ref, logft_ref, cumt_ref, reft_ref, gm_ref,
                      sga_ref, sgg_ref, carry_ref, sp_ref, *, tiles_per_seq):
    tm = x_ref.shape[0]

    @pl.when(pl.program_id(0) % tiles_per_seq == 0)
    def _():
        carry_ref[...] = jnp.zeros_like(carry_ref)

    h = (x_ref[...] * (1.0 + sc_ref[0]) + sh_ref[0]).astype(BF16)

    def proj(off, width):
        ref, base = (wa_ref, 0) if off < _OFF_U else (wb_ref, _OFF_U) if off < _OFF_F else (wf_ref, _OFF_F)
        return jnp.dot(h, ref[:, off - base:off - base + width], preferred_element_type=F32)

    logf = _forget(proj(_OFF_F, LANES), bf_ref[...])
    gv = _gate_value(proj(_OFF_GV, 512), glg_ref[...], glb_ref[...])
    u = jax.nn.gelu(proj(_OFF_U, 512))
    q = proj(_OFF_Q, 512) * (HEAD_DIM ** -0.5 * LOG2E)
    k = proj(_OFF_K, 512)

    carry_in = carry_ref[...]
    carry = carry_in
    parts = []
    for n in range(tm // CHUNK):
        c = _dot(tri_ref[...], logf[n * CHUNK:(n + 1) * CHUNK], True) + carry
        parts.append(c)
        carry = c[CHUNK - 1:CHUNK]
    carry_ref[...] = carry
    cum = jnp.concatenate(parts, axis=0)

    gvb = gv.astype(BF16)
    gw = gv.shape[1] // GMLP_GROUPS
    for n in range(tm // CHUNK):
        for g in range(GMLP_GROUPS):
            sp_ref[n * CHUNK:(n + 1) * CHUNK, g * gw:(g + 1) * gw] = jnp.dot(
                ws_ref[g], gvb[n * CHUNK:(n + 1) * CHUNK, g * gw:(g + 1) * gw], preferred_element_type=F32)

    v = proj(_OFF_V, 512)
    sga_ref[...] = jax.nn.sigmoid(proj(_OFF_GA, 1024)).astype(BF16)
    sgg_ref[...] = jax.nn.sigmoid(proj(_OFF_GG, 1024)).astype(BF16)

    qt_ref[0] = q.T.astype(BF16)
    kt32_ref[0] = k.T
    vt = v.T
    vt32_ref[0] = vt
    logft_ref[0] = logf.T[:N_HEADS]
    cumt_ref[0] = (cum * LOG2E).T[:N_HEADS]
    reft_ref[0] = (jnp.broadcast_to(carry_in, cum.shape) * LOG2E).T[:N_HEADS]
    bs = jnp.concatenate([bs_ref[...]] * (tm // CHUNK), axis=0)
    gm_ref[...] = (u * (sp_ref[...] + bs)).astype(BF16)

    neg = (carry_in - cum) * LOG2E
    hi = neg.astype(BF16).astype(F32)
    mid = (neg - hi).astype(BF16).astype(F32)
    lo = (neg - hi - mid).astype(BF16).astype(F32)
    lane = lax.broadcasted_iota(jnp.int32, (tm, LANES), 1)
    for hd in range(N_HEADS):
        src = k[:, (hd // 2) * LANES:(hd // 2 + 1) * LANES]
        if hd % 2:
            src = pltpu.roll(src, HEAD_DIM, 1)
        blk = jnp.where(lane < HEAD_DIM, src, 0.0)
        for i, piece in enumerate((hi, mid, lo)):
            blk = jnp.where(lane == HEAD_DIM + i, piece[:, hd:hd + 1], blk)
        kaug_ref[:, hd * LANES:(hd + 1) * LANES] = blk.astype(BF16)

    ones_rows = jnp.where(lax.broadcasted_iota(jnp.int32, (HEAD_DIM, tm), 0) == 0, 1.0, 0.0)
    for hd in range(N_HEADS):
        grp = jnp.concatenate([vt[hd * HEAD_DIM:(hd + 1) * HEAD_DIM], ones_rows], axis=0)
        vtaug_ref[0, hd * LANES:(hd + 1) * LANES, :] = grp.astype(BF16)


def _prompt_in(x, sc, sh, wa, wb, wf, bf, glg, glb, ws, bs, tri, *, seq, tm):
    n, d = x.shape
    nb = n // seq
    tps = seq // tm
    row = lambda c: pl.BlockSpec((tm, c), lambda i: (i, 0))
    mod = pl.BlockSpec((1, 1, d), lambda i: (i // tps, 0, 0))
    tr = lambda r: pl.BlockSpec((1, r, tm), lambda i: (i // tps, 0, i % tps))
    aug = N_HEADS * LANES
    outs = [((nb, 512, seq), BF16), ((nb, 512, seq), F32), ((n, aug), BF16), ((nb, 512, seq), F32),
            ((nb, aug, seq), BF16), ((nb, N_HEADS, seq), F32), ((nb, N_HEADS, seq), F32),
            ((nb, N_HEADS, seq), F32), ((n, 512), BF16), ((n, 1024), BF16), ((n, 1024), BF16)]
    out_specs = [tr(512), tr(512), row(aug), tr(512), tr(aug), tr(N_HEADS), tr(N_HEADS), tr(N_HEADS),
                 row(512), row(1024), row(1024)]
    return pl.pallas_call(
        functools.partial(_prompt_in_kernel, tiles_per_seq=tps),
        out_shape=[jax.ShapeDtypeStruct(s, t) for s, t in outs],
        grid=(n // tm,),
        in_specs=[row(d), mod, mod, _full(wa.shape), _full(wb.shape), _full(wf.shape), _full(bf.shape),
                  _full(glg.shape), _full(glb.shape), _full(ws.shape), _full(bs.shape), _full(tri.shape)],
        out_specs=out_specs,
        scratch_shapes=[pltpu.VMEM((1, LANES), F32), pltpu.VMEM((tm, 512), F32)],
        compiler_params=_params("arbitrary"),
        name="prompt_in",
    )(x, sc, sh, wa, wb, wf, bf, glg, glb, ws, bs, tri)


ATTN_TILE = 512
QK_AHEAD = 2


def _fox_prompt_kernel(qt_ref, kaug_ref, vtaug_ref, cumt_ref, reft_ref, o_ref, m_ref, acc_ref):
    t = ATTN_TILE
    qi = pl.program_id(1)
    q0 = pl.multiple_of(qi * t, t)
    m_ref[...] = jnp.full_like(m_ref, NEG_INF)
    acc_ref[...] = jnp.zeros_like(acc_ref)
    ones3 = jnp.where(lax.broadcasted_iota(jnp.int32, (HEAD_DIM, t), 0) < 3, 1.0, 0.0).astype(BF16)
    qaug = [jnp.concatenate([qt_ref[h * HEAD_DIM:(h + 1) * HEAD_DIM, :], ones3], axis=0) for h in range(N_HEADS)]
    cb = [cumt_ref[h:h + 1, pl.ds(q0, t)] for h in range(N_HEADS)]

    def run(pieces):
        rounds = [(pc, h) for pc in pieces for h in range(N_HEADS)]

        def scores(pc, h):
            ks, kl, ql, _ = pc
            return jnp.dot(kaug_ref[pl.ds(ks, kl), h * LANES:(h + 1) * LANES], qaug[h][:, ql:],
                           preferred_element_type=F32)

        raw = [scores(*r) for r in rounds[:QK_AHEAD]]
        for n, ((ks, kl, ql, diag), h) in enumerate(rounds):
            if n + QK_AHEAD < len(rounds):
                raw.append(scores(*rounds[n + QK_AHEAD]))
            grp = slice(h * LANES, (h + 1) * LANES)
            st = raw[n]
            if diag:
                st = jnp.where(lax.broadcasted_iota(jnp.int32, st.shape, 0) <= lax.broadcasted_iota(jnp.int32, st.shape, 1),
                               st, NEG_INF)
            ref_j = jnp.concatenate([reft_ref[h:h + 1, pl.ds(ks, LANES)]] * ((t - ql) // LANES), axis=1)
            crow = cb[h][:, ql:] - ref_j
            m_old = m_ref[h, :, ql:]
            m_new = jnp.maximum(m_old, jnp.max(st, axis=0, keepdims=True) + crow)
            pt = jnp.exp2(st - (m_new - crow)).astype(BF16)
            acc_ref[h, :, ql:] = acc_ref[h, :, ql:] * jnp.exp2(m_old - m_new) + jnp.dot(
                vtaug_ref[grp, pl.ds(ks, kl)], pt, preferred_element_type=F32)
            m_ref[h, :, ql:] = m_new

    def full(j):
        return (pl.multiple_of(j * t, t), t, 0, False)

    def diagonal():
        half = t // 2
        return [(q0, half, 0, True), (pl.multiple_of(q0 + half, half), half, half, True)]

    lax.fori_loop(0, qi // 2, lambda jj, c: (run([full(2 * jj), full(2 * jj + 1)]), c)[1], 0)

    @pl.when(qi % 2 == 1)
    def _():
        run([full(qi - 1)] + diagonal())

    @pl.when(qi % 2 == 0)
    def _():
        run(diagonal())

    outs = []
    for h in range(N_HEADS):
        acc = acc_ref[h]
        outs.append(acc[:HEAD_DIM] / acc[HEAD_DIM:HEAD_DIM + 1])
    o_ref[...] = jnp.concatenate(outs, axis=0).T.astype(o_ref.dtype)


def _fox_prompt(qt, kaug, vtaug, cumt, reft, *, seq):
    nb = qt.shape[0]
    t = ATTN_TILE
    nq = seq // t
    aug = N_HEADS * LANES
    per_b = lambda r: pl.BlockSpec((None, r, seq), lambda b, i: (b, 0, 0))
    return pl.pallas_call(
        _fox_prompt_kernel,
        out_shape=jax.ShapeDtypeStruct((nb * seq, D_ATTN), BF16),
        grid=(nb, nq),
        in_specs=[pl.BlockSpec((None, D_ATTN, t), lambda b, i: (b, 0, i)),
                  pl.BlockSpec((seq, aug), lambda b, i: (b, 0)),
                  per_b(aug), per_b(N_HEADS), per_b(N_HEADS)],
        out_specs=pl.BlockSpec((t, D_ATTN), lambda b, i: (b * nq + i, 0)),
        scratch_shapes=[pltpu.VMEM((N_HEADS, 1, t), F32), pltpu.VMEM((N_HEADS, LANES, t), F32)],
        compiler_params=_params("parallel", "arbitrary"),
        name="fox_prompt",
    )(qt, kaug, vtaug, cumt, reft)


def _route(logits):
    lane = lax.broadcasted_iota(jnp.int32, logits.shape, 1)
    big = jnp.int32(LANES)
    is_g = (lane >= N_EXPERTS) & (lane < N_EXPERTS + N_GROUPS)
    gl = jnp.where(is_g, logits, NEG_INF)
    gmax = jnp.max(gl, axis=-1, keepdims=True)
    gi = jnp.min(jnp.where(gl == gmax, lane, big), axis=-1, keepdims=True) - N_EXPERTS
    pg_top = 1.0 / jnp.sum(jnp.exp(gl - gmax), axis=-1, keepdims=True)
    in_g = (lane >= gi * EXPERTS_PER_GROUP) & (lane < (gi + 1) * EXPERTS_PER_GROUP)
    el = jnp.where(in_g, logits, NEG_INF)
    m1 = jnp.max(el, axis=-1, keepdims=True)
    i1 = jnp.min(jnp.where(el == m1, lane, big), axis=-1, keepdims=True)
    el2 = jnp.where(lane == i1, NEG_INF, el)
    m2 = jnp.max(el2, axis=-1, keepdims=True)
    i2 = jnp.min(jnp.where(el2 == m2, lane, big), axis=-1, keepdims=True)
    e2 = jnp.exp(m2 - m1)
    w1 = pg_top / (1.0 + e2)
    return jnp.where(lane == i1, w1, jnp.where(lane == i2, w1 * e2, 0.0)), gi


ROUTE_CHUNK = 32
SORT_BLOCK = 512
SORTED_ROWS = 768
ROUTE_TILE = 512
EXPERT_SUB = 256


def _group_ranks(gi, ls_ref):
    tm = gi.shape[0]
    lane = lax.broadcasted_iota(jnp.int32, (tm, LANES), 1)
    onehot = jnp.where(lane == gi, 1.0, 0.0)
    return onehot, jnp.dot(ls_ref[...], onehot.astype(BF16), preferred_element_type=F32)


def _sort_block(h2b, comb, gi, onehot, before, xs_ref, cs_ref, pos_ref, cnt_ref):
    tm = h2b.shape[0]
    lane = lax.broadcasted_iota(jnp.int32, (tm, LANES), 1)
    tot = before[tm - 1:tm] + onehot[tm - 1:tm]
    cnt_ref[0] = tot
    padded = jnp.floor((tot + (ROUTE_CHUNK - 1)) * (1.0 / ROUTE_CHUNK)) * ROUTE_CHUNK
    lane1 = lax.broadcasted_iota(jnp.int32, (1, LANES), 1)
    start = jnp.zeros_like(padded)
    for s in range(1, N_GROUPS):
        start = start + jnp.where(lane1 >= s, pltpu.roll(padded, s, 1), 0.0)
    pos = jnp.sum(onehot * (start + before), axis=-1, keepdims=True)
    pos_ref[...] = pos
    pos_row = jnp.broadcast_to(pos, (tm, LANES)).T[0:1]
    rows = lax.broadcasted_iota(jnp.int32, (SORTED_ROWS, tm), 0).astype(F32)
    perm = jnp.where(rows == pos_row, 1.0, 0.0).astype(BF16)
    xs_ref[...] = jnp.dot(perm, h2b, preferred_element_type=F32).astype(BF16)
    rel = jnp.zeros_like(comb)
    for g in range(N_GROUPS):
        moved = comb if g == 0 else pltpu.roll(comb, LANES - g * EXPERTS_PER_GROUP, 1)
        rel = rel + jnp.where(gi == g, moved, 0.0)
    rel = jnp.where(lane < EXPERTS_PER_GROUP, rel, 0.0)
    hi = rel.astype(BF16)
    lo = (rel - hi.astype(F32)).astype(BF16)
    cs_ref[...] = (jnp.dot(perm, hi, preferred_element_type=F32) + jnp.dot(perm, lo, preferred_element_type=F32))


def _merge_kernel(a_ref, gm_ref, sga_ref, sgg_ref, x_ref, g1_ref, sc2_ref, sh2_ref, wao_ref, wgo_ref, wo_ref,
                  l1g_ref, l1b_ref, wr_ref, br_ref, *rest, alpha, precise, sort):
    if sort:
        assert not precise
        ls_ref, x1_ref, xs_ref, cs_ref, pos_ref, cnt_ref, h2_s = rest

        @pl.when(pl.program_id(0) == 0)
        def _():
            h2_s[...] = jnp.zeros_like(h2_s)

        h2_prev = h2_s[...]
        logits_prev = _dot(h2_prev, wr_ref[...], False) + br_ref[...]
    else:
        x1_ref, h2_ref, comb_ref = rest
    da = _dot(a_ref[...], wao_ref[...], precise)
    dg = _dot(gm_ref[...], wgo_ref[...], precise)
    t = sga_ref[...].astype(F32) * da + sgg_ref[...].astype(F32) * dg
    m = _dot(t, wo_ref[...], precise)
    if sort:
        comb_prev, gi_prev = _route(logits_prev)
        onehot, before = _group_ranks(gi_prev, ls_ref)
        _sort_block(h2_prev, comb_prev, gi_prev, onehot, before, xs_ref, cs_ref, pos_ref, cnt_ref)
    x1 = _ln(alpha * x_ref[...] + (1.0 + _mod(g1_ref)) * m, l1g_ref[...], l1b_ref[...])
    h2 = x1 * (1.0 + _mod(sc2_ref)) + _mod(sh2_ref)
    if sort:
        h2_s[...] = h2.astype(BF16)
    else:
        comb, _ = _route(_dot(h2, wr_ref[...], precise) + br_ref[...])
        h2_ref[...] = h2.astype(h2_ref.dtype)
        comb_ref[...] = comb
    x1_ref[...] = x1


def _merge(a, gm, sga, sgg, x, g1, sc2, sh2, wao, wgo, wo, l1g, l1b, wr, br, *, rows_per_mod, tm, alpha, precise,
           sort):
    n, d = x.shape
    nblk = n // tm
    cur = (lambda i: jnp.minimum(i, nblk - 1)) if sort else (lambda i: i)
    row = lambda c: pl.BlockSpec((tm, c), lambda i: (cur(i), 0))
    if g1.ndim == 3:
        mod = pl.BlockSpec((1, 1, d), lambda i: (cur(i) * tm // rows_per_mod, 0, 0))
    else:
        mod = row(d)
    ins = [a, gm, sga, sgg, x, g1, sc2, sh2, wao, wgo, wo, l1g, l1b, wr, br]
    in_specs = [row(D_ATTN), row(gm.shape[1]), row(d), row(d), row(d), mod, mod, mod, _full(wao.shape),
                _full(wgo.shape), _full(wo.shape), _full(l1g.shape), _full(l1b.shape), _full(wr.shape),
                _full(br.shape)]
    scratch = []
    if sort:
        assert tm == SORT_BLOCK
        ls = jnp.tril(jnp.ones((tm, tm), BF16), -1)
        ins.append(ls)
        in_specs.append(_full(ls.shape))
        prev = lambda i: jnp.maximum(i - 1, 0)
        srow = lambda c: pl.BlockSpec((SORTED_ROWS, c), lambda i: (prev(i), 0))
        out_shape = [((n, d), F32), ((nblk * SORTED_ROWS, d), BF16), ((nblk * SORTED_ROWS, LANES), F32),
                     ((n, 1), F32), ((nblk, 1, LANES), F32)]
        out_specs = [row(d), srow(d), srow(LANES), pl.BlockSpec((tm, 1), lambda i: (prev(i), 0)),
                     pl.BlockSpec((1, 1, LANES), lambda i: (prev(i), 0, 0))]
        scratch = [pltpu.VMEM((tm, d), BF16)]
    else:
        out_shape = [((n, d), F32), ((n, d), BF16), ((n, LANES), F32)]
        out_specs = [row(d), row(d), row(LANES)]
    return pl.pallas_call(
        functools.partial(_merge_kernel, alpha=alpha, precise=precise, sort=sort),
        out_shape=[jax.ShapeDtypeStruct(s, t) for s, t in out_shape],
        grid=(nblk + 1 if sort else nblk,),
        in_specs=in_specs,
        out_specs=out_specs,
        scratch_shapes=scratch,
        compiler_params=_params("arbitrary"),
        name="merge_precise" if precise else "merge",
    )(*ins)


def _route_tables(counts, nblk):
    ch, cpb, cpt = ROUTE_CHUNK, SORTED_ROWS // ROUTE_CHUNK, ROUTE_TILE // ROUTE_CHUNK
    max_tiles = (nblk * (SORT_BLOCK + N_GROUPS * (ch - 1))) // ROUTE_TILE + N_GROUPS + 1
    cnt = counts[:, 0, :N_GROUPS].astype(jnp.int32)
    nch = (cnt + ch - 1) // ch
    loc = jnp.cumsum(nch, axis=1) - nch
    earlier = jnp.cumsum(nch, axis=0) - nch
    tiles_g = (nch.sum(0) + cpt - 1) // cpt
    tile_off = jnp.cumsum(tiles_g) - tiles_g
    n_tiles = tiles_g.sum()
    k = jnp.arange(cpb, dtype=jnp.int32)[None, None, :]
    used = (k >= loc[:, :, None]) & (k < (loc + nch)[:, :, None])
    glob = (tile_off * cpt)[None, :, None] + earlier[:, :, None] + k - loc[:, :, None]
    dst = jnp.sum(jnp.where(used, glob, 0), axis=1)
    blk_chunk = jnp.arange(nblk, dtype=jnp.int32)[:, None] * cpb + k[0]
    scatter_to = jnp.where(used.any(axis=1), dst, max_tiles * cpt)
    src = jnp.full((max_tiles * cpt,), cpb - 1, jnp.int32).at[scatter_to.reshape(-1)].set(
        blk_chunk.reshape(-1), mode="drop")
    t = jnp.arange(max_tiles, dtype=jnp.int32)
    tile_group = jnp.minimum(jnp.sum(t[:, None] >= (tile_off + tiles_g)[None, :], axis=1), N_GROUPS - 1)
    return src, dst.reshape(-1), tile_group.astype(jnp.int32), n_tiles.reshape(1).astype(jnp.int32), max_tiles


def _experts_kernel(src_ref, grp_ref, nt_ref, *refs):
    del src_ref
    cpt = ROUTE_TILE // ROUTE_CHUNK
    xr, cr = refs[:cpt], refs[cpt:2 * cpt]
    wg_ref, wu_ref, wd_ref, o_ref, wg_s, wu_s, wd_s = refs[2 * cpt:]
    t = pl.program_id(0)
    live = t < nt_ref[0]
    de = wg_ref.shape[3]

    @pl.when((t == 0) | (grp_ref[t] != grp_ref[jnp.maximum(t - 1, 0)]))
    def _():
        for e in range(EXPERTS_PER_GROUP):
            wg_s[:, e * de:(e + 1) * de] = wg_ref[0, e].astype(BF16)
            wu_s[:, e * de:(e + 1) * de] = wu_ref[0, e].astype(BF16)
            wd_s[e * de:(e + 1) * de, :] = wd_ref[0, e].astype(BF16)

    @pl.when(live)
    def _():
        per = EXPERT_SUB // ROUTE_CHUNK
        halves = range(ROUTE_TILE // EXPERT_SUB)
        xs = [jnp.concatenate([r[...] for r in xr[i * per:(i + 1) * per]], axis=0) for i in halves]
        gate_up = [(jnp.dot(x, wg_s[...], preferred_element_type=F32), jnp.dot(x, wu_s[...], preferred_element_type=F32))
                   for x in xs]
        for i, (hg, hu) in enumerate(gate_up):
            c = jnp.concatenate([r[...] for r in cr[i * per:(i + 1) * per]], axis=0)
            a = jnp.concatenate([(hg[:, e * de:(e + 1) * de] * jax.nn.sigmoid(hg[:, e * de:(e + 1) * de])
                                  * hu[:, e * de:(e + 1) * de] * c[:, e:e + 1]).astype(BF16)
                                 for e in range(EXPERTS_PER_GROUP)], axis=1)
            o_ref[i * EXPERT_SUB:(i + 1) * EXPERT_SUB, :] = jnp.dot(
                a, wd_s[...], preferred_element_type=F32).astype(o_ref.dtype)

    @pl.when(jnp.logical_not(live))
    def _():
        o_ref[...] = jnp.zeros_like(o_ref)


def _experts(src, tile_group, n_tiles, max_tiles, xs, cs, wg, wu, wd):
    d = xs.shape[1]
    de = wg.shape[2]
    cpt = ROUTE_TILE // ROUTE_CHUNK
    grouped = lambda w: w.reshape((N_GROUPS, EXPERTS_PER_GROUP) + w.shape[1:])

    def chunk(width):
        def one(r):
            return pl.BlockSpec((ROUTE_CHUNK, width), lambda t, src, grp, nt: (src[t * cpt + r], 0))
        return [one(r) for r in range(cpt)]

    wspec = lambda w: pl.BlockSpec((1, EXPERTS_PER_GROUP) + w.shape[1:], lambda t, src, grp, nt: (grp[t], 0, 0, 0))
    wide = EXPERTS_PER_GROUP * de
    grid_spec = pltpu.PrefetchScalarGridSpec(
        num_scalar_prefetch=3,
        grid=(max_tiles,),
        in_specs=chunk(d) + chunk(LANES) + [wspec(wg), wspec(wu), wspec(wd)],
        out_specs=pl.BlockSpec((ROUTE_TILE, d), lambda t, src, grp, nt: (t, 0)),
        scratch_shapes=[pltpu.VMEM((d, wide), BF16), pltpu.VMEM((d, wide), BF16), pltpu.VMEM((wide, d), BF16)],
    )
    return pl.pallas_call(
        _experts_kernel,
        out_shape=jax.ShapeDtypeStruct((max_tiles * ROUTE_TILE, d), BF16),
        grid_spec=grid_spec,
        compiler_params=_params("arbitrary"),
        name="experts",
    )(src, tile_group, n_tiles, *([xs] * cpt), *([cs] * cpt), grouped(wg), grouped(wu), grouped(wd))


def _unsort_kernel(dst_ref, *refs, alpha):
    del dst_ref
    cpb = SORTED_ROWS // ROUTE_CHUNK
    yr = refs[:cpb]
    pos_ref, x1_ref, g2_ref, l2g_ref, l2b_ref, o_ref = refs[cpb:]
    ys = jnp.concatenate([r[...] for r in yr], axis=0)
    tm = pos_ref.shape[0]
    cols = lax.broadcasted_iota(jnp.int32, (tm, SORTED_ROWS), 1).astype(F32)
    perm_t = jnp.where(cols == pos_ref[...], 1.0, 0.0).astype(BF16)
    f = jnp.dot(perm_t, ys, preferred_element_type=F32)
    o_ref[...] = _ln(alpha * x1_ref[...] + (1.0 + _mod(g2_ref)) * f, l2g_ref[...], l2b_ref[...])


def _unsort(dst, y, pos, x1, g2, l2g, l2b, *, rows_per_mod, alpha):
    n, d = x1.shape
    tm = SORT_BLOCK
    cpb = SORTED_ROWS // ROUTE_CHUNK

    def chunk(r):
        return pl.BlockSpec((ROUTE_CHUNK, d), lambda i, dst: (dst[i * cpb + r], 0))

    row = lambda c: pl.BlockSpec((tm, c), lambda i, dst: (i, 0))
    full = lambda s: pl.BlockSpec(s, lambda i, dst: (0,) * len(s))
    grid_spec = pltpu.PrefetchScalarGridSpec(
        num_scalar_prefetch=1,
        grid=(n // tm,),
        in_specs=[chunk(r) for r in range(cpb)] + [
            row(1), row(d), pl.BlockSpec((1, 1, d), lambda i, dst: (i * tm // rows_per_mod, 0, 0)),
            full(l2g.shape), full(l2b.shape)],
        out_specs=row(d),
    )
    return pl.pallas_call(
        functools.partial(_unsort_kernel, alpha=alpha),
        out_shape=jax.ShapeDtypeStruct((n, d), F32),
        grid_spec=grid_spec,
        compiler_params=_params("parallel"),
        name="unsort",
    )(dst, *([y] * cpb), pos, x1, g2, l2g, l2b)


def _moe_kernel(h2_ref, comb_ref, x1_ref, g2_ref, wg_ref, wu_ref, wd_ref, l2g_ref, l2b_ref, o_ref, *, alpha):
    e = pl.program_id(0)

    @pl.when(e == 0)
    def _():
        o_ref[...] = jnp.zeros_like(o_ref)

    h2 = h2_ref[...]
    comb = comb_ref[...]
    lane = lax.broadcasted_iota(jnp.int32, comb.shape, 1)
    w = jnp.sum(jnp.where(lane == e, comb, 0.0), axis=-1, keepdims=True)
    hg = jnp.dot(h2, wg_ref[0].astype(BF16), preferred_element_type=F32)
    hu = jnp.dot(h2, wu_ref[0].astype(BF16), preferred_element_type=F32)
    a = hg * jax.nn.sigmoid(hg) * hu * w
    o_ref[...] += jnp.dot(a.astype(BF16), wd_ref[0].astype(BF16), preferred_element_type=F32)

    @pl.when(e == pl.num_programs(0) - 1)
    def _():
        o_ref[...] = _ln(alpha * x1_ref[...] + (1.0 + g2_ref[...]) * o_ref[...], l2g_ref[...], l2b_ref[...])


def _moe(h2, comb, x1, g2, wg, wu, wd, l2g, l2b, *, alpha):
    n, d = x1.shape
    ne, _, de = wg.shape
    return pl.pallas_call(
        functools.partial(_moe_kernel, alpha=alpha),
        out_shape=jax.ShapeDtypeStruct((n, d), F32),
        grid=(ne,),
        in_specs=[_full((n, d)), _full((n, LANES)), _full((n, d)), _full((n, d)),
                  pl.BlockSpec((1, d, de), lambda e: (e, 0, 0)), pl.BlockSpec((1, d, de), lambda e: (e, 0, 0)),
                  pl.BlockSpec((1, de, d), lambda e: (e, 0, 0)), _full(l2g.shape), _full(l2b.shape)],
        out_specs=_full((n, d)),
        compiler_params=_params("arbitrary"),
        name="moe",
    )(h2, comb, x1, g2, wg, wu, wd, l2g, l2b)


def _sample_proj_kernel(x_ref, sc_ref, sh_ref, wt_ref, z_ref):
    h = x_ref[...] * (1.0 + sc_ref[...]) + sh_ref[...]
    z_ref[...] = lax.dot_general(h, wt_ref[...], (((1,), (1,)), ((), ())), precision=HIGHEST,
                                 preferred_element_type=F32)


def _sample_proj(x, sc, sh, wt):
    n, d = x.shape
    return pl.pallas_call(
        _sample_proj_kernel,
        out_shape=jax.ShapeDtypeStruct((n, wt.shape[0]), F32),
        compiler_params=pltpu.CompilerParams(vmem_limit_bytes=VMEM_LIMIT_BYTES),
        name="sample_proj",
    )(x, sc, sh, wt)


def _sample_mix_kernel(z_ref, bf_ref, glg_ref, glb_ref, mg_ref, bs_ref, mc_ref,
                       q_ref, k_ref, v_ref, logf_ref, cn_ref, gv_ref, gm_ref, sga_ref, sgg_ref):
    z = z_ref[...]
    sec = lambda off, width: z[:, off:off + width]
    q, k, v, logf, u, gv, sga, sgg = _mixer_epilogue(
        sec(_OFF_Q, 512), sec(_OFF_K, 512), sec(_OFF_V, 512), sec(_OFF_F, LANES), sec(_OFF_U, 512),
        sec(_OFF_GV, 512), sec(_OFF_GA, 1024), sec(_OFF_GG, 1024), bf_ref[...], glg_ref[...], glb_ref[...])
    q_ref[...] = q
    k_ref[...] = k
    v_ref[...] = v
    logf_ref[...] = logf
    cn_ref[...] = _dot(mc_ref[...], logf, True)
    gv_ref[...] = gv
    sga_ref[...] = sga
    sgg_ref[...] = sgg
    gw = gv.shape[1] // GMLP_GROUPS
    sp = jnp.concatenate([_dot(mg_ref[g], gv[:, g * gw:(g + 1) * gw], True) for g in range(GMLP_GROUPS)], axis=-1)
    gm_ref[...] = u * (sp + bs_ref[...])


def _sample_mix(z, bf, glg, glb, mg, bs, mc):
    n = z.shape[0]
    shapes = [(n, 512)] * 3 + [(n, LANES)] * 2 + [(n, 512)] * 2 + [(n, 1024)] * 2
    return pl.pallas_call(
        _sample_mix_kernel,
        out_shape=[jax.ShapeDtypeStruct(s, F32) for s in shapes],
        compiler_params=pltpu.CompilerParams(vmem_limit_bytes=VMEM_LIMIT_BYTES),
        name="sample_mix",
    )(z, bf, glg, glb, mg, bs, mc)


PAGES_PER_STEP = 16


def _fox_sample_kernel(pt_ref, kt_hbm, vt_hbm, lf_hbm, qbd_ref, cn_ref, kn_ref, vn_ref, cnb_ref, o_ref,
                       kbuf, vbuf, lbuf, sem, m_ref, l_ref, acc_ref, later_ref, *, n_chunks, n_new):
    P = PAGES_PER_STEP
    b, c = pl.program_id(0), pl.program_id(1)
    step = b * n_chunks + c
    n_steps = pl.num_programs(0) * n_chunks
    slot = lax.rem(step, 2)

    def page_copies(seq, chunk, sl, lookup):
        out = []
        for r in range(P):
            page = pt_ref[seq, (n_chunks - 1 - chunk) * P + r] if lookup else 0
            out += [pltpu.make_async_copy(kt_hbm.at[page], kbuf.at[sl, r], sem.at[sl]),
                    pltpu.make_async_copy(vt_hbm.at[page], vbuf.at[sl, r], sem.at[sl]),
                    pltpu.make_async_copy(lf_hbm.at[page], lbuf.at[sl, r], sem.at[sl])]
        return out

    def start_all(cps):
        for i, cp in enumerate(cps):
            cp.start(priority=1 if i % 3 == 1 else 0)

    @pl.when(step == 0)
    def _():
        start_all(page_copies(b, c, slot, True))

    @pl.when(step + 1 < n_steps)
    def _():
        nxt = step + 1
        start_all(page_copies(nxt // n_chunks, lax.rem(nxt, n_chunks), 1 - slot, True))

    @pl.when(c == 0)
    def _():
        m_ref[...] = jnp.full_like(m_ref, NEG_INF)
        l_ref[...] = jnp.zeros_like(l_ref)
        acc_ref[...] = jnp.zeros_like(acc_ref)
        later_ref[...] = jnp.zeros_like(later_ref)

    qbd = qbd_ref[0]
    cn = cn_ref[0]
    nt = (((1,), (1,)), ((), ()))

    def update(s, vs):
        m_old = m_ref[...]
        m_new = m_old
        for sb in s:
            m_new = jnp.maximum(m_new, jnp.max(sb, axis=-1, keepdims=True))
        alpha = jnp.exp(m_old - m_new)
        l_new = l_ref[...] * alpha
        acc = acc_ref[...] * alpha
        for sb, vb in zip(s, vs):
            p = jnp.exp(sb - m_new)
            l_new = l_new + jnp.sum(p, axis=-1, keepdims=True)
            acc = acc + lax.dot_general(p, vb, nt, preferred_element_type=F32)
        m_ref[...] = m_new
        l_ref[...] = l_new
        acc_ref[...] = acc

    for cp in page_copies(b, c, slot, False):
        cp.wait()

    lane = lax.broadcasted_iota(jnp.int32, (N_HEADS, PAGE_SIZE), 1)
    later = later_ref[...]
    bias = [None] * P
    for r in reversed(range(P)):
        y = lbuf[slot, r]
        for k in range(7):
            sh = 1 << k
            y = y + jnp.where(lane < PAGE_SIZE - sh, pltpu.roll(y, PAGE_SIZE - sh, 1), 0.0)
        bias[r] = jnp.where(lane < PAGE_SIZE - 1, pltpu.roll(y, PAGE_SIZE - 1, 1), 0.0) + later
        later = later + y[:, 0:1]
    later_ref[...] = later

    scores = []
    for r in range(P):
        st = jnp.dot(qbd, kbuf[slot, r].reshape(D_ATTN, PAGE_SIZE), preferred_element_type=F32)
        scores.append(st + jnp.concatenate([bias[r]] * n_new, axis=0) + cn)
    update(scores, [vbuf[slot, r].reshape(D_ATTN, PAGE_SIZE) for r in range(P)])

    @pl.when(c == n_chunks - 1)
    def _():
        st = jnp.dot(qbd, kn_ref[0], preferred_element_type=F32) + cn - cnb_ref[0]
        col = lax.broadcasted_iota(jnp.int32, st.shape, 1)
        row = lax.broadcasted_iota(jnp.int32, st.shape, 0)
        update([jnp.where(col <= row // N_HEADS, st, NEG_INF)], [vn_ref[0]])
        full = acc_ref[...] / l_ref[...]
        lane_head = lax.broadcasted_iota(jnp.int32, full.shape, 1) // HEAD_DIM
        row_head = lax.broadcasted_iota(jnp.int32, full.shape, 0) % N_HEADS
        own = jnp.where(lane_head == row_head, full, 0.0)
        rows = [jnp.sum(own[q * N_HEADS:(q + 1) * N_HEADS], axis=0, keepdims=True) for q in range(n_new)]
        o_ref[0] = jnp.concatenate(rows + [jnp.zeros((o_ref.shape[1] - n_new, D_ATTN), F32)], axis=0)


def _fox_sample(page_table, kt_pool, vt_pool, logf_pool, qbd, cn, kn, vn, cnb, *, n_new):
    nb, n_pages = page_table.shape
    P = PAGES_PER_STEP
    n_chunks = n_pages // P
    nrow = n_new * N_HEADS
    per_b = lambda s: pl.BlockSpec((1,) + s, lambda b, c, pt: (b,) + (0,) * len(s))
    hbm = pl.BlockSpec(memory_space=pl.ANY)
    kv_page = (N_HEADS, HEAD_DIM, PAGE_SIZE)
    grid_spec = pltpu.PrefetchScalarGridSpec(
        num_scalar_prefetch=1,
        grid=(nb, n_chunks),
        in_specs=[hbm, hbm, hbm, per_b((nrow, D_ATTN)), per_b((nrow, 1)), per_b((D_ATTN, LANES)),
                  per_b((D_ATTN, LANES)), per_b((nrow, LANES))],
        out_specs=per_b((8, D_ATTN)),
        scratch_shapes=[pltpu.VMEM((2, P) + kv_page, F32), pltpu.VMEM((2, P) + kv_page, F32),
                        pltpu.VMEM((2, P, N_HEADS, PAGE_SIZE), F32), pltpu.SemaphoreType.DMA((2,)),
                        pltpu.VMEM((nrow, 1), F32), pltpu.VMEM((nrow, 1), F32), pltpu.VMEM((nrow, D_ATTN), F32),
                        pltpu.VMEM((N_HEADS, PAGE_SIZE), F32)],
    )
    return pl.pallas_call(
        functools.partial(_fox_sample_kernel, n_chunks=n_chunks, n_new=n_new),
        out_shape=jax.ShapeDtypeStruct((nb, 8, D_ATTN), F32),
        grid_spec=grid_spec,
        compiler_params=_params("arbitrary", "arbitrary"),
        name="fox_sample",
    )(page_table, kt_pool, vt_pool, logf_pool, qbd, cn, kn, vn, cnb)


def _split_cols(z):
    f0 = 3 * D_ATTN
    return z[:, :f0], z[:, f0 + N_HEADS:], jnp.pad(z[:, f0:f0 + N_HEADS], ((0, 0), (0, LANES - N_HEADS)))


def kernel(x_prompt, x_sample, c_prompt, c_sample, cache_k, cache_v, cache_logf, page_table, w_ada, b_ada, w_in,
           b_f, gmlp_ln_g, gmlp_ln_b, w_s, b_s, w_attn_out, w_gmlp_out, w_o, ln1_g, ln1_b, w_group_router,
           b_group_router, w_expert_router, b_expert_router, w_exp_gate, w_exp_up, w_exp_down, ln2_g, ln2_b):
    depth = w_ada.shape[0]
    assert depth == 1
    nbp, seq, d = x_prompt.shape
    nbs, n_new, _ = x_sample.shape
    alpha = (2.0 * depth) ** 0.25
    n_pool = cache_k.shape[1]
    d_gmlp = gmlp_ln_g.shape[1]
    gw = d_gmlp // GMLP_GROUPS

    wa_b, wb_b, wf_b = _split_cols(w_in[0].astype(BF16))
    bf =jnp.pad(b_f[0], (0, LANES - N_HEADS))[None]
    glg, glb = gmlp_ln_g[0][None], gmlp_ln_b[0][None]
    tril = jnp.tril(jnp.ones((CHUNK, CHUNK), F32))
    ws = jnp.where(tril > 0, w_s[0], 0.0)
    bs_tile = jnp.repeat(b_s[0].T, gw, axis=1)
    w_r = jnp.pad(jnp.concatenate([w_expert_router[0], w_group_router[0]], axis=1),
                  ((0, 0), (0, LANES - N_EXPERTS - N_GROUPS)))
    b_r = jnp.pad(jnp.concatenate([b_expert_router[0], b_group_router[0]]), (0, LANES - N_EXPERTS - N_GROUPS))[None]
    l1g, l1b, l2g, l2b = ln1_g[0][None], ln1_b[0][None], ln2_g[0][None], ln2_b[0][None]

    nc = nbp + nbs
    c_all = jnp.pad(jnp.concatenate([c_prompt, c_sample], axis=0), ((0, -nc % 8), (0, 0)))
    mod = _ada(c_all, w_ada[0], b_ada[0][None])
    modp = [m[:, None, :] for m in jnp.split(mod[:nbp], 6, axis=-1)]
    mods = [m[:, None, :] for m in jnp.split(mod[nbp:nc], 6, axis=-1)]

    xp = x_prompt.reshape(nbp * seq, d)
    qt, kt32, kaug, vt32, vtaug, logft, cumt, reft, gm, sga, sgg = _prompt_in(
        xp, modp[1], modp[0], wa_b, wb_b, wf_b, bf, glg, glb, ws.astype(BF16), bs_tile, tril, seq=seq, tm=ATTN_TILE)
    a = _fox_prompt(qt, kaug, vtaug, cumt, reft, seq=seq)
    x1, xs_sorted, cs_sorted, pos, counts = _merge(
        a, gm, sga, sgg, xp, modp[2], modp[4], modp[3], w_attn_out[0].astype(BF16), w_gmlp_out[0].astype(BF16),
        w_o[0].astype(BF16), l1g, l1b, w_r.astype(BF16), b_r, rows_per_mod=seq, tm=SORT_BLOCK, alpha=alpha,
        precise=False, sort=True)
    src, dst, tile_group, n_tiles, max_tiles = _route_tables(counts, nbp * seq // SORT_BLOCK)
    y_sorted = _experts(src, tile_group, n_tiles, max_tiles, xs_sorted, cs_sorted, w_exp_gate[0], w_exp_up[0],
                        w_exp_down[0])
    yp = _unsort(dst, y_sorted, pos, x1, modp[5], l2g, l2b, rows_per_mod=seq, alpha=alpha)

    ns = nbs * n_new
    xs = x_sample.reshape(ns, d)
    rep = lambda m: jnp.repeat(m[:, 0, :], n_new, axis=0)
    z = jnp.concatenate(_split_cols(_sample_proj(xs, rep(mods[1]), rep(mods[0]), w_in[0].T)), axis=1)
    eye_b = jnp.eye(nbs, dtype=F32)
    mg = jnp.stack([jnp.kron(eye_b, ws[g, :n_new, :n_new]) for g in range(GMLP_GROUPS)])
    bs_rows = jnp.tile(bs_tile[:n_new], (nbs, 1))
    mc = jnp.kron(eye_b, tril[:n_new, :n_new])
    qs, ks, vs, logf_s, cn, gv_s, gm_s, sga_s, sgg_s = _sample_mix(z, bf, glg, glb, mg, bs_rows, mc)

    kt_pool = cache_k[0].transpose(0, 2, 3, 1)
    vt_pool = cache_v[0].transpose(0, 2, 3, 1)
    logf_pool = cache_logf[0].transpose(0, 2, 1)
    nrow = n_new * N_HEADS
    q4 = qs.reshape(nbs, n_new, N_HEADS, HEAD_DIM)
    qbd = jnp.einsum("bqhd,hg->bqhgd", q4, jnp.eye(N_HEADS, dtype=F32)).reshape(nbs, nrow, D_ATTN)
    cn3 = cn[:, :N_HEADS].reshape(nbs, n_new, N_HEADS)
    cn_col = cn3.reshape(nbs, nrow, 1)
    cnb = jnp.pad(jnp.tile(cn3.transpose(0, 2, 1), (1, n_new, 1)), ((0, 0), (0, 0), (0, LANES - n_new)))
    new_t = lambda t: jnp.pad(t.reshape(nbs, n_new, D_ATTN).transpose(0, 2, 1), ((0, 0), (0, 0), (0, LANES - n_new)))
    a_s = _fox_sample(page_table, kt_pool, vt_pool, logf_pool, qbd, cn_col, new_t(ks), new_t(vs), cnb, n_new=n_new)
    a_s = a_s[:, :n_new].reshape(ns, D_ATTN)
    x1s, h2s, comb_s = _merge(a_s, gm_s, sga_s, sgg_s, xs, rep(mods[2]), rep(mods[4]), rep(mods[3]),
                              w_attn_out[0], w_gmlp_out[0], w_o[0], l1g, l1b, w_r, b_r,
                              rows_per_mod=ns, tm=ns, alpha=alpha, precise=True, sort=False)
    ys = _moe(h2s, comb_s, x1s, rep(mods[5]), w_exp_gate[0], w_exp_up[0], w_exp_down[0], l2g, l2b, alpha=alpha)

    hs = (N_HEADS, HEAD_DIM)
    untr = lambda t: t.reshape(1, nbp, *hs, seq).transpose(0, 1, 4, 2, 3)
    return (yp.reshape(nbp, seq, d), ys.reshape(nbs, n_new, d), untr(kt32), untr(vt32),
            logft.reshape(1, nbp, N_HEADS, seq).transpose(0, 1, 3, 2),
            ks.reshape(1, nbs, n_new, *hs), vs.reshape(1, nbs, n_new, *hs),
            logf_s[:, :N_HEADS].reshape(1, nbs, n_new, N_HEADS), gv_s.reshape(1, nbs, n_new, d_gmlp))
```

```python
import functools

import numpy as np
import jax
import jax.numpy as jnp
from jax import lax
from jax.experimental import pallas as pl
from jax.experimental.pallas import tpu as pltpu

F32 = jnp.float32
BF16 = jnp.bfloat16
HIGHEST = lax.Precision.HIGHEST

N_HEADS = 8
HEAD_DIM = 64
D_ATTN = N_HEADS * HEAD_DIM
PAGE_SIZE = 128
CHUNK = 128
GMLP_GROUPS = 4
N_GROUPS = 4
EXPERTS_PER_GROUP = 4
N_EXPERTS = N_GROUPS * EXPERTS_PER_GROUP
LN_EPS = 1e-5
LANES = 128
NEG_INF = float("-inf")
LOG2E = 1.4426950408889634

VMEM_LIMIT_BYTES = 56 * 1024 * 1024


def _params(*sem):
    return pltpu.CompilerParams(dimension_semantics=sem, vmem_limit_bytes=VMEM_LIMIT_BYTES)


def _full(shape):
    n = len(shape)
    return pl.BlockSpec(shape, lambda *_: (0,) * n)


def _ln(x, g, b):
    mu = jnp.mean(x, axis=-1, keepdims=True)
    xc = x - mu
    var = jnp.mean(xc * xc, axis=-1, keepdims=True)
    return xc * lax.rsqrt(var + LN_EPS) * g + b


def _mod(ref):
    return ref[0] if len(ref.shape) == 3 else ref[...]


def _dot(a, b, precise):
    if precise:
        return jnp.dot(a, b, precision=HIGHEST, preferred_element_type=F32)
    return jnp.dot(a.astype(BF16), b.astype(BF16), preferred_element_type=F32)


def _ada_kernel(c_ref, w_ref, b_ref, o_ref):
    c = c_ref[...]
    o_ref[...] = _dot(c * jax.nn.sigmoid(c), w_ref[...], True) + b_ref[...]


def _ada(c, w, b, *, tn=1024):
    n, d = c.shape
    dout = w.shape[1]
    return pl.pallas_call(
        _ada_kernel,
        out_shape=jax.ShapeDtypeStruct((n, dout), F32),
        grid=(dout // tn,),
        in_specs=[_full((n, d)), pl.BlockSpec((d, tn), lambda j: (0, j)), pl.BlockSpec((1, tn), lambda j: (0, j))],
        out_specs=pl.BlockSpec((n, tn), lambda j: (0, j)),
        compiler_params=_params("parallel"),
        name="ada",
    )(c, w, b)


_OFF_Q, _OFF_K, _OFF_V, _OFF_U, _OFF_GV = 0, 512, 1024, 1536, 2048
_OFF_GA, _OFF_GG, _OFF_F, _W_COLS = 2560, 3584, 4608, 4736


def _forget(zf, bf):
    return jax.nn.log_sigmoid(zf + bf)


def _gate_value(zgv, glg, glb):
    return _ln(jax.nn.gelu(zgv), glg, glb)


def _mixer_epilogue(zq, zk, zv, zf, zu, zgv, zga, zgg, bf, glg, glb):
    q = zq * (HEAD_DIM ** -0.5)
    return (q, zk, zv, _forget(zf, bf), jax.nn.gelu(zu), _gate_value(zgv, glg, glb), jax.nn.sigmoid(zga),
            jax.nn.sigmoid(zgg))


def _prompt_in_kernel(x_ref, sc_ref, sh_ref, wa_ref, wb_ref, wf_ref, bf_ref, glg_ref, glb_ref, ws_ref, bs_ref, tri_ref,
                      qt_ref, kt32_ref, kaug_ref, vt32_ref, vtaug_ref, logft_ref, cumt_ref, reft_ref, gm_ref,
                      sga_ref, sgg_ref, carry_ref, sp_ref, *, tiles_per_seq):
    tm = x_ref.shape[0]

    @pl.when(pl.program_id(0) % tiles_per_seq == 0)
    def _():
        carry_ref[...] = jnp.zeros_like(carry_ref)

    h = (x_ref[...] * (1.0 + sc_ref[0]) + sh_ref[0]).astype(BF16)

    def proj(off, width):
        ref, base = (wa_ref, 0) if off < _OFF_U else (wb_ref, _OFF_U) if off < _OFF_F else (wf_ref, _OFF_F)
        return jnp.dot(h, ref[:, off - base:off - base + width], preferred_element_type=F32)

    logf = _forget(proj(_OFF_F, LANES), bf_ref[...])
    gv = _gate_value(proj(_OFF_GV, 512), glg_ref[...], glb_ref[...])
    u = jax.nn.gelu(proj(_OFF_U, 512))
    q = proj(_OFF_Q, 512) * (HEAD_DIM ** -0.5 * LOG2E)
    k = proj(_OFF_K, 512)

    carry_in = carry_ref[...]
    carry = carry_in
    parts = []
    for n in range(tm // CHUNK):
        c = _dot(tri_ref[...], logf[n * CHUNK:(n + 1) * CHUNK], True) + carry
        parts.append(c)
        carry = c[CHUNK - 1:CHUNK]
    carry_ref[...] = carry
    cum = jnp.concatenate(parts, axis=0)

    gvb = gv.astype(BF16)
    gw = gv.shape[1] // GMLP_GROUPS
    for n in range(tm // CHUNK):
        for g in range(GMLP_GROUPS):
            sp_ref[n * CHUNK:(n + 1) * CHUNK, g * gw:(g + 1) * gw] = jnp.dot(
                ws_ref[g], gvb[n * CHUNK:(n + 1) * CHUNK, g * gw:(g + 1) * gw], preferred_element_type=F32)

    v = proj(_OFF_V, 512)
    sga_ref[...] = jax.nn.sigmoid(proj(_OFF_GA, 1024)).astype(BF16)
    sgg_ref[...] = jax.nn.sigmoid(proj(_OFF_GG, 1024)).astype(BF16)

    qt_ref[0] = q.T.astype(BF16)
    kt32_ref[0] = k.T
    vt = v.T
    vt32_ref[0] = vt
    logft_ref[0] = logf.T[:N_HEADS]
    cumt_ref[0] = (cum * LOG2E).T[:N_HEADS]
    reft_ref[0] = (jnp.broadcast_to(carry_in, cum.shape) * LOG2E).T[:N_HEADS]
    bs = jnp.concatenate([bs_ref[...]] * (tm // CHUNK), axis=0)
    gm_ref[...] = (u * (sp_ref[...] + bs)).astype(BF16)

    neg = (carry_in - cum) * LOG2E
    hi = neg.astype(BF16).astype(F32)
    mid = (neg - hi).astype(BF16).astype(F32)
    lo = (neg - hi - mid).astype(BF16).astype(F32)
    lane = lax.broadcasted_iota(jnp.int32, (tm, LANES), 1)
    for hd in range(N_HEADS):
        src = k[:, (hd // 2) * LANES:(hd // 2 + 1) * LANES]
        if hd % 2:
            src = pltpu.roll(src, HEAD_DIM, 1)
        blk = jnp.where(lane < HEAD_DIM, src, 0.0)
        for i, piece in enumerate((hi, mid, lo)):
            blk = jnp.where(lane == HEAD_DIM + i, piece[:, hd:hd + 1], blk)
        kaug_ref[:, hd * LANES:(hd + 1) * LANES] = blk.astype(BF16)

    ones_rows = jnp.where(lax.broadcasted_iota(jnp.int32, (HEAD_DIM, tm), 0) == 0, 1.0, 0.0)
    for hd in range(N_HEADS):
        grp = jnp.concatenate([vt[hd * HEAD_DIM:(hd + 1) * HEAD_DIM], ones_rows], axis=0)
        vtaug_ref[0, hd * LANES:(hd + 1) * LANES, :] = grp.astype(BF16)


def _prompt_in(x, sc, sh, wa, wb, wf, bf, glg, glb, ws, bs, tri, *, seq, tm):
    n, d = x.shape
    nb = n // seq
    tps = seq // tm
    row = lambda c: pl.BlockSpec((tm, c), lambda i: (i, 0))
    mod = pl.BlockSpec((1, 1, d), lambda i: (i // tps, 0, 0))
    tr = lambda r: pl.BlockSpec((1, r, tm), lambda i: (i // tps, 0, i % tps))
    aug = N_HEADS * LANES
    outs = [((nb, 512, seq), BF16), ((nb, 512, seq), F32), ((n, aug), BF16), ((nb, 512, seq), F32),
            ((nb, aug, seq), BF16), ((nb, N_HEADS, seq), F32), ((nb, N_HEADS, seq), F32),
            ((nb, N_HEADS, seq), F32), ((n, 512), BF16), ((n, 1024), BF16), ((n, 1024), BF16)]
    out_specs = [tr(512), tr(512), row(aug), tr(512), tr(aug), tr(N_HEADS), tr(N_HEADS), tr(N_HEADS),
                 row(512), row(1024), row(1024)]
    return pl.pallas_call(
        functools.partial(_prompt_in_kernel, tiles_per_seq=tps),
        out_shape=[jax.ShapeDtypeStruct(s, t) for s, t in outs],
        grid=(n // tm,),
        in_specs=[row(d), mod, mod, _full(wa.shape), _full(wb.shape), _full(wf.shape), _full(bf.shape),
                  _full(glg.shape), _full(glb.shape), _full(ws.shape), _full(bs.shape), _full(tri.shape)],
        out_specs=out_specs,
        scratch_shapes=[pltpu.VMEM((1, LANES), F32), pltpu.VMEM((tm, 512), F32)],
        compiler_params=_params("arbitrary"),
        name="prompt_in",
    )(x, sc, sh, wa, wb, wf, bf, glg, glb, ws, bs, tri)


ATTN_TILE = 512
QK_AHEAD = 2


def _fox_prompt_kernel(qt_ref, kaug_ref, vtaug_ref, cumt_ref, reft_ref, o_ref, m_ref, acc_ref):
    t = ATTN_TILE
    qi = pl.program_id(1)
    q0 = pl.multiple_of(qi * t, t)
    m_ref[...] = jnp.full_like(m_ref, NEG_INF)
    acc_ref[...] = jnp.zeros_like(acc_ref)
    ones3 = jnp.where(lax.broadcasted_iota(jnp.int32, (HEAD_DIM, t), 0) < 3, 1.0, 0.0).astype(BF16)
    qaug = [jnp.concatenate([qt_ref[h * HEAD_DIM:(h + 1) * HEAD_DIM, :], ones3], axis=0) for h in range(N_HEADS)]
    cb = [cumt_ref[h:h + 1, pl.ds(q0, t)] for h in range(N_HEADS)]

    def run(pieces):
        rounds = [(pc, h) for pc in pieces for h in range(N_HEADS)]

        def scores(pc, h):
            ks, kl, ql, _ = pc
            return jnp.dot(kaug_ref[pl.ds(ks, kl), h * LANES:(h + 1) * LANES], qaug[h][:, ql:],
                           preferred_element_type=F32)

        raw = [scores(*r) for r in rounds[:QK_AHEAD]]
        for n, ((ks, kl, ql, diag), h) in enumerate(rounds):
            if n + QK_AHEAD < len(rounds):
                raw.append(scores(*rounds[n + QK_AHEAD]))
            grp = slice(h * LANES, (h + 1) * LANES)
            st = raw[n]
            if diag:
                st = jnp.where(lax.broadcasted_iota(jnp.int32, st.shape, 0) <= lax.broadcasted_iota(jnp.int32, st.shape, 1),
                               st, NEG_INF)
            ref_j = jnp.concatenate([reft_ref[h:h + 1, pl.ds(ks, LANES)]] * ((t - ql) // LANES), axis=1)
            crow = cb[h][:, ql:] - ref_j
            m_old = m_ref[h, :, ql:]
            m_new = jnp.maximum(m_old, jnp.max(st, axis=0, keepdims=True) + crow)
            pt = jnp.exp2(st - (m_new - crow)).astype(BF16)
            acc_ref[h, :, ql:] = acc_ref[h, :, ql:] * jnp.exp2(m_old - m_new) + jnp.dot(
                vtaug_ref[grp, pl.ds(ks, kl)], pt, preferred_element_type=F32)
            m_ref[h, :, ql:] = m_new

    def full(j):
        return (pl.multiple_of(j * t, t), t, 0, False)

    def diagonal():
        half = t // 2
        return [(q0, half, 0, True), (pl.multiple_of(q0 + half, half), half, half, True)]

    lax.fori_loop(0, qi // 2, lambda jj, c: (run([full(2 * jj), full(2 * jj + 1)]), c)[1], 0)

    @pl.when(qi % 2 == 1)
    def _():
        run([full(qi - 1)] + diagonal())

    @pl.when(qi % 2 == 0)
    def _():
        run(diagonal())

    outs = []
    for h in range(N_HEADS):
        acc = acc_ref[h]
        outs.append(acc[:HEAD_DIM] / acc[HEAD_DIM:HEAD_DIM + 1])
    o_ref[...] = jnp.concatenate(outs, axis=0).T.astype(o_ref.dtype)


def _fox_prompt(qt, kaug, vtaug, cumt, reft, *, seq):
    nb = qt.shape[0]
    t = ATTN_TILE
    nq = seq // t
    aug = N_HEADS * LANES
    per_b = lambda r: pl.BlockSpec((None, r, seq), lambda b, i: (b, 0, 0))
    return pl.pallas_call(
        _fox_prompt_kernel,
        out_shape=jax.ShapeDtypeStruct((nb * seq, D_ATTN), BF16),
        grid=(nb, nq),
        in_specs=[pl.BlockSpec((None, D_ATTN, t), lambda b, i: (b, 0, i)),
                  pl.BlockSpec((seq, aug), lambda b, i: (b, 0)),
                  per_b(aug), per_b(N_HEADS), per_b(N_HEADS)],
        out_specs=pl.BlockSpec((t, D_ATTN), lambda b, i: (b * nq + i, 0)),
        scratch_shapes=[pltpu.VMEM((N_HEADS, 1, t), F32), pltpu.VMEM((N_HEADS, LANES, t), F32)],
        compiler_params=_params("parallel", "arbitrary"),
        name="fox_prompt",
    )(qt, kaug, vtaug, cumt, reft)


def _route(logits):
    lane = lax.broadcasted_iota(jnp.int32, logits.shape, 1)
    big = jnp.int32(LANES)
    is_g = (lane >= N_EXPERTS) & (lane < N_EXPERTS + N_GROUPS)
    gl = jnp.where(is_g, logits, NEG_INF)
    gmax = jnp.max(gl, axis=-1, keepdims=True)
    gi = jnp.min(jnp.where(gl == gmax, lane, big), axis=-1, keepdims=True) - N_EXPERTS
    pg_top = 1.0 / jnp.sum(jnp.exp(gl - gmax), axis=-1, keepdims=True)
    in_g = (lane >= gi * EXPERTS_PER_GROUP) & (lane < (gi + 1) * EXPERTS_PER_GROUP)
    el = jnp.where(in_g, logits, NEG_INF)
    m1 = jnp.max(el, axis=-1, keepdims=True)
    i1 = jnp.min(jnp.where(el == m1, lane, big), axis=-1, keepdims=True)
    el2 = jnp.where(lane == i1, NEG_INF, el)
    m2 = jnp.max(el2, axis=-1, keepdims=True)
    i2 = jnp.min(jnp.where(el2 == m2, lane, big), axis=-1, keepdims=True)
    e2 = jnp.exp(m2 - m1)
    w1 = pg_top / (1.0 + e2)
    return jnp.where(lane == i1, w1, jnp.where(lane == i2, w1 * e2, 0.0)), gi


ROUTE_CHUNK = 32
SORT_BLOCK = 512
SORTED_ROWS = 768
ROUTE_TILE = 512
EXPERT_SUB = 256


def _group_ranks(gi, ls_ref):
    tm = gi.shape[0]
    lane = lax.broadcasted_iota(jnp.int32, (tm, LANES), 1)
    onehot = jnp.where(lane == gi, 1.0, 0.0)
    return onehot, jnp.dot(ls_ref[...], onehot.astype(BF16), preferred_element_type=F32)


def _sort_block(h2b, comb, gi, onehot, before, xs_ref, cs_ref, pos_ref, cnt_ref):
    tm = h2b.shape[0]
    lane = lax.broadcasted_iota(jnp.int32, (tm, LANES), 1)
    tot = before[tm - 1:tm] + onehot[tm - 1:tm]
    cnt_ref[0] = tot
    padded = jnp.floor((tot + (ROUTE_CHUNK - 1)) * (1.0 / ROUTE_CHUNK)) * ROUTE_CHUNK
    lane1 = lax.broadcasted_iota(jnp.int32, (1, LANES), 1)
    start = jnp.zeros_like(padded)
    for s in range(1, N_GROUPS):
        start = start + jnp.where(lane1 >= s, pltpu.roll(padded, s, 1), 0.0)
    pos = jnp.sum(onehot * (start + before), axis=-1, keepdims=True)
    pos_ref[...] = pos
    pos_row = jnp.broadcast_to(pos, (tm, LANES)).T[0:1]
    rows = lax.broadcasted_iota(jnp.int32, (SORTED_ROWS, tm), 0).astype(F32)
    perm = jnp.where(rows == pos_row, 1.0, 0.0).astype(BF16)
    xs_ref[...] = jnp.dot(perm, h2b, preferred_element_type=F32).astype(BF16)
    rel = jnp.zeros_like(comb)
    for g in range(N_GROUPS):
        moved = comb if g == 0 else pltpu.roll(comb, LANES - g * EXPERTS_PER_GROUP, 1)
        rel = rel + jnp.where(gi == g, moved, 0.0)
    rel = jnp.where(lane < EXPERTS_PER_GROUP, rel, 0.0)
    hi = rel.astype(BF16)
    lo = (rel - hi.astype(F32)).astype(BF16)
    cs_ref[...] = (jnp.dot(perm, hi, preferred_element_type=F32) + jnp.dot(perm, lo, preferred_element_type=F32))


def _merge_kernel(a_ref, gm_ref, sga_ref, sgg_ref, x_ref, g1_ref, sc2_ref, sh2_ref, wao_ref, wgo_ref, wo_ref,
                  l1g_ref, l1b_ref, wr_ref, br_ref, *rest, alpha, precise, sort):
    if sort:
        assert not precise
        ls_ref, x1_ref, xs_ref, cs_ref, pos_ref, cnt_ref, h2_s = rest

        @pl.when(pl.program_id(0) == 0)
        def _():
            h2_s[...] = jnp.zeros_like(h2_s)

        h2_prev = h2_s[...]
        logits_prev = _dot(h2_prev, wr_ref[...], False) + br_ref[...]
    else:
        x1_ref, h2_ref, comb_ref = rest
    da = _dot(a_ref[...], wao_ref[...], precise)
    dg = _dot(gm_ref[...], wgo_ref[...], precise)
    t = sga_ref[...].astype(F32) * da + sgg_ref[...].astype(F32) * dg
    m = _dot(t, wo_ref[...], precise)
    if sort:
        comb_prev, gi_prev = _route(logits_prev)
        onehot, before = _group_ranks(gi_prev, ls_ref)
        _sort_block(h2_prev, comb_prev, gi_prev, onehot, before, xs_ref, cs_ref, pos_ref, cnt_ref)
    x1 = _ln(alpha * x_ref[...] + (1.0 + _mod(g1_ref)) * m, l1g_ref[...], l1b_ref[...])
    h2 = x1 * (1.0 + _mod(sc2_ref)) + _mod(sh2_ref)
    if sort:
        h2_s[...] = h2.astype(BF16)
    else:
        comb, _ = _route(_dot(h2, wr_ref[...], precise) + br_ref[...])
        h2_ref[...] = h2.astype(h2_ref.dtype)
        comb_ref[...] = comb
    x1_ref[...] = x1


def _merge(a, gm, sga, sgg, x, g1, sc2, sh2, wao, wgo, wo, l1g, l1b, wr, br, *, rows_per_mod, tm, alpha, precise,
           sort):
    n, d = x.shape
    nblk = n // tm
    cur = (lambda i: jnp.minimum(i, nblk - 1)) if sort else (lambda i: i)
    row = lambda c: pl.BlockSpec((tm, c), lambda i: (cur(i), 0))
    if g1.ndim == 3:
        mod = pl.BlockSpec((1, 1, d), lambda i: (cur(i) * tm // rows_per_mod, 0, 0))
    else:
        mod = row(d)
    ins = [a, gm, sga, sgg, x, g1, sc2, sh2, wao, wgo, wo, l1g, l1b, wr, br]
    in_specs = [row(D_ATTN), row(gm.shape[1]), row(d), row(d), row(d), mod, mod, mod, _full(wao.shape),
                _full(wgo.shape), _full(wo.shape), _full(l1g.shape), _full(l1b.shape), _full(wr.shape),
                _full(br.shape)]
    scratch = []
    if sort:
        assert tm == SORT_BLOCK
        ls = jnp.tril(jnp.ones((tm, tm), BF16), -1)
        ins.append(ls)
        in_specs.append(_full(ls.shape))
        prev = lambda i: jnp.maximum(i - 1, 0)
        srow = lambda c: pl.BlockSpec((SORTED_ROWS, c), lambda i: (prev(i), 0))
        out_shape = [((n, d), F32), ((nblk * SORTED_ROWS, d), BF16), ((nblk * SORTED_ROWS, LANES), F32),
                     ((n, 1), F32), ((nblk, 1, LANES), F32)]
        out_specs = [row(d), srow(d), srow(LANES), pl.BlockSpec((tm, 1), lambda i: (prev(i), 0)),
                     pl.BlockSpec((1, 1, LANES), lambda i: (prev(i), 0, 0))]
        scratch = [pltpu.VMEM((tm, d), BF16)]
    else:
        out_shape = [((n, d), F32), ((n, d), BF16), ((n, LANES), F32)]
        out_specs = [row(d), row(d), row(LANES)]
    return pl.pallas_call(
        functools.partial(_merge_kernel, alpha=alpha, precise=precise, sort=sort),
        out_shape=[jax.ShapeDtypeStruct(s, t) for s, t in out_shape],
        grid=(nblk + 1 if sort else nblk,),
        in_specs=in_specs,
        out_specs=out_specs,
        scratch_shapes=scratch,
        compiler_params=_params("arbitrary"),
        name="merge_precise" if precise else "merge",
    )(*ins)


def _route_tables(counts, nblk):
    ch, cpb, cpt = ROUTE_CHUNK, SORTED_ROWS // ROUTE_CHUNK, ROUTE_TILE // ROUTE_CHUNK
    max_tiles = (nblk * (SORT_BLOCK + N_GROUPS * (ch - 1))) // ROUTE_TILE + N_GROUPS + 1
    cnt = counts[:, 0, :N_GROUPS].astype(jnp.int32)
    nch = (cnt + ch - 1) // ch
    loc = jnp.cumsum(nch, axis=1) - nch
    earlier = jnp.cumsum(nch, axis=0) - nch
    tiles_g = (nch.sum(0) + cpt - 1) // cpt
    tile_off = jnp.cumsum(tiles_g) - tiles_g
    n_tiles = tiles_g.sum()
    k = jnp.arange(cpb, dtype=jnp.int32)[None, None, :]
    used = (k >= loc[:, :, None]) & (k < (loc + nch)[:, :, None])
    glob = (tile_off * cpt)[None, :, None] + earlier[:, :, None] + k - loc[:, :, None]
    dst = jnp.sum(jnp.where(used, glob, 0), axis=1)
    blk_chunk = jnp.arange(nblk, dtype=jnp.int32)[:, None] * cpb + k[0]
    scatter_to = jnp.where(used.any(axis=1), dst, max_tiles * cpt)
    src = jnp.full((max_tiles * cpt,), cpb - 1, jnp.int32).at[scatter_to.reshape(-1)].set(
        blk_chunk.reshape(-1), mode="drop")
    t = jnp.arange(max_tiles, dtype=jnp.int32)
    tile_group = jnp.minimum(jnp.sum(t[:, None] >= (tile_off + tiles_g)[None, :], axis=1), N_GROUPS - 1)
    return src, dst.reshape(-1), tile_group.astype(jnp.int32), n_tiles.reshape(1).astype(jnp.int32), max_tiles


def _experts_kernel(src_ref, grp_ref, nt_ref, *refs):
    del src_ref
    cpt = ROUTE_TILE // ROUTE_CHUNK
    xr, cr = refs[:cpt], refs[cpt:2 * cpt]
    wg_ref, wu_ref, wd_ref, o_ref, wg_s, wu_s, wd_s = refs[2 * cpt:]
    t = pl.program_id(0)
    live = t < nt_ref[0]
    de = wg_ref.shape[3]

    @pl.when((t == 0) | (grp_ref[t] != grp_ref[jnp.maximum(t - 1, 0)]))
    def _():
        for e in range(EXPERTS_PER_GROUP):
            wg_s[:, e * de:(e + 1) * de] = wg_ref[0, e].astype(BF16)
            wu_s[:, e * de:(e + 1) * de] = wu_ref[0, e].astype(BF16)
            wd_s[e * de:(e + 1) * de, :] = wd_ref[0, e].astype(BF16)

    @pl.when(live)
    def _():
        per = EXPERT_SUB // ROUTE_CHUNK
        halves = range(ROUTE_TILE // EXPERT_SUB)
        xs = [jnp.concatenate([r[...] for r in xr[i * per:(i + 1) * per]], axis=0) for i in halves]
        gate_up = [(jnp.dot(x, wg_s[...], preferred_element_type=F32), jnp.dot(x, wu_s[...], preferred_element_type=F32))
                   for x in xs]
        for i, (hg, hu) in enumerate(gate_up):
            c = jnp.concatenate([r[...] for r in cr[i * per:(i + 1) * per]], axis=0)
            a = jnp.concatenate([(hg[:, e * de:(e + 1) * de] * jax.nn.sigmoid(hg[:, e * de:(e + 1) * de])
                                  * hu[:, e * de:(e + 1) * de] * c[:, e:e + 1]).astype(BF16)
                                 for e in range(EXPERTS_PER_GROUP)], axis=1)
            o_ref[i * EXPERT_SUB:(i + 1) * EXPERT_SUB, :] = jnp.dot(
                a, wd_s[...], preferred_element_type=F32).astype(o_ref.dtype)

    @pl.when(jnp.logical_not(live))
    def _():
        o_ref[...] = jnp.zeros_like(o_ref)


def _experts(src, tile_group, n_tiles, max_tiles, xs, cs, wg, wu, wd):
    d = xs.shape[1]
    de = wg.shape[2]
    cpt = ROUTE_TILE // ROUTE_CHUNK
    grouped = lambda w: w.reshape((N_GROUPS, EXPERTS_PER_GROUP) + w.shape[1:])

    def chunk(width):
        def one(r):
            return pl.BlockSpec((ROUTE_CHUNK, width), lambda t, src, grp, nt: (src[t * cpt + r], 0))
        return [one(r) for r in range(cpt)]

    wspec = lambda w: pl.BlockSpec((1, EXPERTS_PER_GROUP) + w.shape[1:], lambda t, src, grp, nt: (grp[t], 0, 0, 0))
    wide = EXPERTS_PER_GROUP * de
    grid_spec = pltpu.PrefetchScalarGridSpec(
        num_scalar_prefetch=3,
        grid=(max_tiles,),
        in_specs=chunk(d) + chunk(LANES) + [wspec(wg), wspec(wu), wspec(wd)],
        out_specs=pl.BlockSpec((ROUTE_TILE, d), lambda t, src, grp, nt: (t, 0)),
        scratch_shapes=[pltpu.VMEM((d, wide), BF16), pltpu.VMEM((d, wide), BF16), pltpu.VMEM((wide, d), BF16)],
    )
    return pl.pallas_call(
        _experts_kernel,
        out_shape=jax.ShapeDtypeStruct((max_tiles * ROUTE_TILE, d), BF16),
        grid_spec=grid_spec,
        compiler_params=_params("arbitrary"),
        name="experts",
    )(src, tile_group, n_tiles, *([xs] * cpt), *([cs] * cpt), grouped(wg), grouped(wu), grouped(wd))


def _unsort_kernel(dst_ref, *refs, alpha):
    del dst_ref
    cpb = SORTED_ROWS // ROUTE_CHUNK
    yr = refs[:cpb]
    pos_ref, x1_ref, g2_ref, l2g_ref, l2b_ref, o_ref = refs[cpb:]
    ys = jnp.concatenate([r[...] for r in yr], axis=0)
    tm = pos_ref.shape[0]
    cols = lax.broadcasted_iota(jnp.int32, (tm, SORTED_ROWS), 1).astype(F32)
    perm_t = jnp.where(cols == pos_ref[...], 1.0, 0.0).astype(BF16)
    f = jnp.dot(perm_t, ys, preferred_element_type=F32)
    o_ref[...] = _ln(alpha * x1_ref[...] + (1.0 + _mod(g2_ref)) * f, l2g_ref[...], l2b_ref[...])


def _unsort(dst, y, pos, x1, g2, l2g, l2b, *, rows_per_mod, alpha):
    n, d = x1.shape
    tm = SORT_BLOCK
    cpb = SORTED_ROWS // ROUTE_CHUNK

    def chunk(r):
        return pl.BlockSpec((ROUTE_CHUNK, d), lambda i, dst: (dst[i * cpb + r], 0))

    row = lambda c: pl.BlockSpec((tm, c), lambda i, dst: (i, 0))
    full = lambda s: pl.BlockSpec(s, lambda i, dst: (0,) * len(s))
    grid_spec = pltpu.PrefetchScalarGridSpec(
        num_scalar_prefetch=1,
        grid=(n // tm,),
        in_specs=[chunk(r) for r in range(cpb)] + [
            row(1), row(d), pl.BlockSpec((1, 1, d), lambda i, dst: (i * tm // rows_per_mod, 0, 0)),
            full(l2g.shape), full(l2b.shape)],
        out_specs=row(d),
    )
    return pl.pallas_call(
        functools.partial(_unsort_kernel, alpha=alpha),
        out_shape=jax.ShapeDtypeStruct((n, d), F32),
        grid_spec=grid_spec,
        compiler_params=_params("parallel"),
        name="unsort",
    )(dst, *([y] * cpb), pos, x1, g2, l2g, l2b)


def _moe_kernel(h2_ref, comb_ref, x1_ref, g2_ref, wg_ref, wu_ref, wd_ref, l2g_ref, l2b_ref, o_ref, *, alpha):
    e = pl.program_id(0)

    @pl.when(e == 0)
    def _():
        o_ref[...] = jnp.zeros_like(o_ref)

    h2 = h2_ref[...]
    comb = comb_ref[...]
    lane = lax.broadcasted_iota(jnp.int32, comb.shape, 1)
    w = jnp.sum(jnp.where(lane == e, comb, 0.0), axis=-1, keepdims=True)
    hg = jnp.dot(h2, wg_ref[0].astype(BF16), preferred_element_type=F32)
    hu = jnp.dot(h2, wu_ref[0].astype(BF16), preferred_element_type=F32)
    a = hg * jax.nn.sigmoid(hg) * hu * w
    o_ref[...] += jnp.dot(a.astype(BF16), wd_ref[0].astype(BF16), preferred_element_type=F32)

    @pl.when(e == pl.num_programs(0) - 1)
    def _():
        o_ref[...] = _ln(alpha * x1_ref[...] + (1.0 + g2_ref[...]) * o_ref[...], l2g_ref[...], l2b_ref[...])


def _moe(h2, comb, x1, g2, wg, wu, wd, l2g, l2b, *, alpha):
    n, d = x1.shape
    ne, _, de = wg.shape
    return pl.pallas_call(
        functools.partial(_moe_kernel, alpha=alpha),
        out_shape=jax.ShapeDtypeStruct((n, d), F32),
        grid=(ne,),
        in_specs=[_full((n, d)), _full((n, LANES)), _full((n, d)), _full((n, d)),
                  pl.BlockSpec((1, d, de), lambda e: (e, 0, 0)), pl.BlockSpec((1, d, de), lambda e: (e, 0, 0)),
                  pl.BlockSpec((1, de, d), lambda e: (e, 0, 0)), _full(l2g.shape), _full(l2b.shape)],
        out_specs=_full((n, d)),
        compiler_params=_params("arbitrary"),
        name="moe",
    )(h2, comb, x1, g2, wg, wu, wd, l2g, l2b)


def _sample_proj_kernel(x_ref, sc_ref, sh_ref, wt_ref, z_ref):
    h = x_ref[...] * (1.0 + sc_ref[...]) + sh_ref[...]
    z_ref[...] = lax.dot_general(h, wt_ref[...], (((1,), (1,)), ((), ())), precision=HIGHEST,
                                 preferred_element_type=F32)


def _sample_proj(x, sc, sh, wt):
    n, d = x.shape
    return pl.pallas_call(
        _sample_proj_kernel,
        out_shape=jax.ShapeDtypeStruct((n, wt.shape[0]), F32),
        compiler_params=pltpu.CompilerParams(vmem_limit_bytes=VMEM_LIMIT_BYTES),
        name="sample_proj",
    )(x, sc, sh, wt)


def _sample_mix_kernel(z_ref, bf_ref, glg_ref, glb_ref, mg_ref, bs_ref, mc_ref,
                       q_ref, k_ref, v_ref, logf_ref, cn_ref, gv_ref, gm_ref, sga_ref, sgg_ref):
    z = z_ref[...]
    sec = lambda off, width: z[:, off:off + width]
    q, k, v, logf, u, gv, sga, sgg = _mixer_epilogue(
        sec(_OFF_Q, 512), sec(_OFF_K, 512), sec(_OFF_V, 512), sec(_OFF_F, LANES), sec(_OFF_U, 512),
        sec(_OFF_GV, 512), sec(_OFF_GA, 1024), sec(_OFF_GG, 1024), bf_ref[...], glg_ref[...], glb_ref[...])
    q_ref[...] = q
    k_ref[...] = k
    v_ref[...] = v
    logf_ref[...] = logf
    cn_ref[...] = _dot(mc_ref[...], logf, True)
    gv_ref[...] = gv
    sga_ref[...] = sga
    sgg_ref[...] = sgg
    gw = gv.shape[1] // GMLP_GROUPS
    sp = jnp.concatenate([_dot(mg_ref[g], gv[:, g * gw:(g + 1) * gw], True) for g in range(GMLP_GROUPS)], axis=-1)
    gm_ref[...] = u * (sp + bs_ref[...])


def _sample_mix(z, bf, glg, glb, mg, bs, mc):
    n = z.shape[0]
    shapes = [(n, 512)] * 3 + [(n, LANES)] * 2 + [(n, 512)] * 2 + [(n, 1024)] * 2
    return pl.pallas_call(
        _sample_mix_kernel,
        out_shape=[jax.ShapeDtypeStruct(s, F32) for s in shapes],
        compiler_params=pltpu.CompilerParams(vmem_limit_bytes=VMEM_LIMIT_BYTES),
        name="sample_mix",
    )(z, bf, glg, glb, mg, bs, mc)


PAGES_PER_STEP = 16
PAGE_SLOTS = 3


def _fox_sample_kernel(pt_ref, kt_hbm, vt_hbm, lf_hbm, qbd_ref, cn_ref, kn_ref, vn_ref, cnb_ref, o_ref,
                       kbuf, vbuf, lbuf, sem, m_ref, l_ref, acc_ref, later_ref, *, n_chunks, n_new):
    P = PAGES_PER_STEP
    b, c = pl.program_id(0), pl.program_id(1)
    step = b * n_chunks + c
    n_steps = pl.num_programs(0) * n_chunks
    slot = lax.rem(step, PAGE_SLOTS)

    def page_copies(seq, chunk, sl, lookup):
        out = []
        for r in range(P):
            page = pt_ref[seq, (n_chunks - 1 - chunk) * P + r] if lookup else 0
            out += [pltpu.make_async_copy(kt_hbm.at[page], kbuf.at[sl, r], sem.at[sl]),
                    pltpu.make_async_copy(vt_hbm.at[page], vbuf.at[sl, r], sem.at[sl]),
                    pltpu.make_async_copy(lf_hbm.at[page], lbuf.at[sl, r], sem.at[sl])]
        return out

    def start_all(cps):
        for i, cp in enumerate(cps):
            cp.start(priority=1 if i % 3 == 1 else 0)

    def start_step(s):
        start_all(page_copies(s // n_chunks, lax.rem(s, n_chunks), lax.rem(s, PAGE_SLOTS), True))

    @pl.when(step == 0)
    def _():
        for s in range(PAGE_SLOTS - 1):
            @pl.when(s < n_steps)
            def _():
                start_step(jnp.int32(s))

    @pl.when(step + PAGE_SLOTS - 1 < n_steps)
    def _():
        start_step(step + PAGE_SLOTS - 1)

    @pl.when(c == 0)
    def _():
        m_ref[...] = jnp.full_like(m_ref, NEG_INF)
        l_ref[...] = jnp.zeros_like(l_ref)
        acc_ref[...] = jnp.zeros_like(acc_ref)
        later_ref[...] = jnp.zeros_like(later_ref)

    qbd = qbd_ref[0]
    cn = cn_ref[0]
    nt = (((1,), (1,)), ((), ()))

    def update(s, vs):
        m_old = m_ref[...]
        m_new = m_old
        for sb in s:
            m_new = jnp.maximum(m_new, jnp.max(sb, axis=-1, keepdims=True))
        alpha = jnp.exp(m_old - m_new)
        l_new = l_ref[...] * alpha
        acc = acc_ref[...] * alpha
        for sb, vb in zip(s, vs):
            p = jnp.exp(sb - m_new)
            l_new = l_new + jnp.sum(p, axis=-1, keepdims=True)
            acc = acc + lax.dot_general(p, vb, nt, preferred_element_type=F32)
        m_ref[...] = m_new
        l_ref[...] = l_new
        acc_ref[...] = acc

    for cp in page_copies(b, c, slot, False):
        cp.wait()

    lane = lax.broadcasted_iota(jnp.int32, (N_HEADS, PAGE_SIZE), 1)
    later = later_ref[...]
    bias = [None] * P
    for r in reversed(range(P)):
        y = lbuf[slot, r]
        for k in range(7):
            sh = 1 << k
            y = y + jnp.where(lane < PAGE_SIZE - sh, pltpu.roll(y, PAGE_SIZE - sh, 1), 0.0)
        bias[r] = jnp.where(lane < PAGE_SIZE - 1, pltpu.roll(y, PAGE_SIZE - 1, 1), 0.0) + later
        later = later + y[:, 0:1]
    later_ref[...] = later

    scores = []
    for r in range(P):
        st = jnp.dot(qbd, kbuf[slot, r].reshape(D_ATTN, PAGE_SIZE), preferred_element_type=F32)
        scores.append(st + jnp.concatenate([bias[r]] * n_new, axis=0) + cn)
    update(scores, [vbuf[slot, r].reshape(D_ATTN, PAGE_SIZE) for r in range(P)])

    @pl.when(c == n_chunks - 1)
    def _():
        st = jnp.dot(qbd, kn_ref[0], preferred_element_type=F32) + cn - cnb_ref[0]
        col = lax.broadcasted_iota(jnp.int32, st.shape, 1)
        row = lax.broadcasted_iota(jnp.int32, st.shape, 0)
        update([jnp.where(col <= row // N_HEADS, st, NEG_INF)], [vn_ref[0]])
        full = acc_ref[...] / l_ref[...]
        lane_head = lax.broadcasted_iota(jnp.int32, full.shape, 1) // HEAD_DIM
        row_head = lax.broadcasted_iota(jnp.int32, full.shape, 0) % N_HEADS
        own = jnp.where(lane_head == row_head, full, 0.0)
        rows = [jnp.sum(own[q * N_HEADS:(q + 1) * N_HEADS], axis=0, keepdims=True) for q in range(n_new)]
        o_ref[0] = jnp.concatenate(rows + [jnp.zeros((o_ref.shape[1] - n_new, D_ATTN), F32)], axis=0)


def _fox_sample(page_table, kt_pool, vt_pool, logf_pool, qbd, cn, kn, vn, cnb, *, n_new):
    nb, n_pages = page_table.shape
    P = PAGES_PER_STEP
    n_chunks = n_pages // P
    nrow = n_new * N_HEADS
    per_b = lambda s: pl.BlockSpec((1,) + s, lambda b, c, pt: (b,) + (0,) * len(s))
    hbm = pl.BlockSpec(memory_space=pl.ANY)
    kv_page = (N_HEADS, HEAD_DIM, PAGE_SIZE)
    grid_spec = pltpu.PrefetchScalarGridSpec(
        num_scalar_prefetch=1,
        grid=(nb, n_chunks),
        in_specs=[hbm, hbm, hbm, per_b((nrow, D_ATTN)), per_b((nrow, 1)), per_b((D_ATTN, LANES)),
                  per_b((D_ATTN, LANES)), per_b((nrow, LANES))],
        out_specs=per_b((8, D_ATTN)),
        scratch_shapes=[pltpu.VMEM((PAGE_SLOTS, P) + kv_page, F32), pltpu.VMEM((PAGE_SLOTS, P) + kv_page, F32),
                        pltpu.VMEM((PAGE_SLOTS, P, N_HEADS, PAGE_SIZE), F32), pltpu.SemaphoreType.DMA((PAGE_SLOTS,)),
                        pltpu.VMEM((nrow, 1), F32), pltpu.VMEM((nrow, 1), F32), pltpu.VMEM((nrow, D_ATTN), F32),
                        pltpu.VMEM((N_HEADS, PAGE_SIZE), F32)],
    )
    return pl.pallas_call(
        functools.partial(_fox_sample_kernel, n_chunks=n_chunks, n_new=n_new),
        out_shape=jax.ShapeDtypeStruct((nb, 8, D_ATTN), F32),
        grid_spec=grid_spec,
        compiler_params=_params("arbitrary", "arbitrary"),
        name="fox_sample",
    )(page_table, kt_pool, vt_pool, logf_pool, qbd, cn, kn, vn, cnb)


def _split_cols(z):
    f0 = 3 * D_ATTN
    return z[:, :f0], z[:, f0 + N_HEADS:], jnp.pad(z[:, f0:f0 + N_HEADS], ((0, 0), (0, LANES - N_HEADS)))


def kernel(x_prompt, x_sample, c_prompt, c_sample, cache_k, cache_v, cache_logf, page_table, w_ada, b_ada, w_in,
           b_f, gmlp_ln_g, gmlp_ln_b, w_s, b_s, w_attn_out, w_gmlp_out, w_o, ln1_g, ln1_b, w_group_router,
           b_group_router, w_expert_router, b_expert_router, w_exp_gate, w_exp_up, w_exp_down, ln2_g, ln2_b):
    depth = w_ada.shape[0]
    assert depth == 1
    nbp, seq, d = x_prompt.shape
    nbs, n_new, _ = x_sample.shape
    alpha = (2.0 * depth) ** 0.25
    n_pool = cache_k.shape[1]
    d_gmlp = gmlp_ln_g.shape[1]
    gw = d_gmlp // GMLP_GROUPS

    wa_b, wb_b, wf_b = _split_cols(w_in[0].astype(BF16))
    bf =jnp.pad(b_f[0], (0, LANES - N_HEADS))[None]
    glg, glb = gmlp_ln_g[0][None], gmlp_ln_b[0][None]
    tril = jnp.tril(jnp.ones((CHUNK, CHUNK), F32))
    ws = jnp.where(tril > 0, w_s[0], 0.0)
    bs_tile = jnp.repeat(b_s[0].T, gw, axis=1)
    w_r = jnp.pad(jnp.concatenate([w_expert_router[0], w_group_router[0]], axis=1),
                  ((0, 0), (0, LANES - N_EXPERTS - N_GROUPS)))
    b_r = jnp.pad(jnp.concatenate([b_expert_router[0], b_group_router[0]]), (0, LANES - N_EXPERTS - N_GROUPS))[None]
    l1g, l1b, l2g, l2b = ln1_g[0][None], ln1_b[0][None], ln2_g[0][None], ln2_b[0][None]

    nc = nbp + nbs
    c_all = jnp.pad(jnp.concatenate([c_prompt, c_sample], axis=0), ((0, -nc % 8), (0, 0)))
    mod = _ada(c_all, w_ada[0], b_ada[0][None])
    modp = [m[:, None, :] for m in jnp.split(mod[:nbp], 6, axis=-1)]
    mods = [m[:, None, :] for m in jnp.split(mod[nbp:nc], 6, axis=-1)]

    xp = x_prompt.reshape(nbp * seq, d)
    qt, kt32, kaug, vt32, vtaug, logft, cumt, reft, gm, sga, sgg = _prompt_in(
        xp, modp[1], modp[0], wa_b, wb_b, wf_b, bf, glg, glb, ws.astype(BF16), bs_tile, tril, seq=seq, tm=ATTN_TILE)
    a = _fox_prompt(qt, kaug, vtaug, cumt, reft, seq=seq)
    x1, xs_sorted, cs_sorted, pos, counts = _merge(
        a, gm, sga, sgg, xp, modp[2], modp[4], modp[3], w_attn_out[0].astype(BF16), w_gmlp_out[0].astype(BF16),
        w_o[0].astype(BF16), l1g, l1b, w_r.astype(BF16), b_r, rows_per_mod=seq, tm=SORT_BLOCK, alpha=alpha,
        precise=False, sort=True)
    src, dst, tile_group, n_tiles, max_tiles = _route_tables(counts, nbp * seq // SORT_BLOCK)
    y_sorted = _experts(src, tile_group, n_tiles, max_tiles, xs_sorted, cs_sorted, w_exp_gate[0], w_exp_up[0],
                        w_exp_down[0])
    yp = _unsort(dst, y_sorted, pos, x1, modp[5], l2g, l2b, rows_per_mod=seq, alpha=alpha)

    ns = nbs * n_new
    xs = x_sample.reshape(ns, d)
    rep = lambda m: jnp.repeat(m[:, 0, :], n_new, axis=0)
    z = jnp.concatenate(_split_cols(_sample_proj(xs, rep(mods[1]), rep(mods[0]), w_in[0].T)), axis=1)
    eye_b = jnp.eye(nbs, dtype=F32)
    mg = jnp.stack([jnp.kron(eye_b, ws[g, :n_new, :n_new]) for g in range(GMLP_GROUPS)])
    bs_rows = jnp.tile(bs_tile[:n_new], (nbs, 1))
    mc = jnp.kron(eye_b, tril[:n_new, :n_new])
    qs, ks, vs, logf_s, cn, gv_s, gm_s, sga_s, sgg_s = _sample_mix(z, bf, glg, glb, mg, bs_rows, mc)

    kt_pool = cache_k[0].transpose(0, 2, 3, 1)
    vt_pool = cache_v[0].transpose(0, 2, 3, 1)
    logf_pool = cache_logf[0].transpose(0, 2, 1)
    nrow = n_new * N_HEADS
    q4 = qs.reshape(nbs, n_new, N_HEADS, HEAD_DIM)
    qbd = jnp.einsum("bqhd,hg->bqhgd", q4, jnp.eye(N_HEADS, dtype=F32)).reshape(nbs, nrow, D_ATTN)
    cn3 = cn[:, :N_HEADS].reshape(nbs, n_new, N_HEADS)
    cn_col = cn3.reshape(nbs, nrow, 1)
    cnb = jnp.pad(jnp.tile(cn3.transpose(0, 2, 1), (1, n_new, 1)), ((0, 0), (0, 0), (0, LANES - n_new)))
    new_t = lambda t: jnp.pad(t.reshape(nbs, n_new, D_ATTN).transpose(0, 2, 1), ((0, 0), (0, 0), (0, LANES - n_new)))
    a_s = _fox_sample(page_table, kt_pool, vt_pool, logf_pool, qbd, cn_col, new_t(ks), new_t(vs), cnb, n_new=n_new)
    a_s = a_s[:, :n_new].reshape(ns, D_ATTN)
    x1s, h2s, comb_s = _merge(a_s, gm_s, sga_s, sgg_s, xs, rep(mods[2]), rep(mods[4]), rep(mods[3]),
                              w_attn_out[0], w_gmlp_out[0], w_o[0], l1g, l1b, w_r, b_r,
                              rows_per_mod=ns, tm=ns, alpha=alpha, precise=True, sort=False)
    ys = _moe(h2s, comb_s, x1s, rep(mods[5]), w_exp_gate[0], w_exp_up[0], w_exp_down[0], l2g, l2b, alpha=alpha)

    hs = (N_HEADS, HEAD_DIM)
    untr = lambda t: t.reshape(1, nbp, *hs, seq).transpose(0, 1, 4, 2, 3)
    return (yp.reshape(nbp, seq, d), ys.reshape(nbs, n_new, d), untr(kt32), untr(vt32),
            logft.reshape(1, nbp, N_HEADS, seq).transpose(0, 1, 3, 2),
            ks.reshape(1, nbs, n_new, *hs), vs.reshape(1, nbs, n_new, *hs),
            logf_s[:, :N_HEADS].reshape(1, nbs, n_new, N_HEADS), gv_s.reshape(1, nbs, n_new, d_gmlp))
```
